```python
import math
import jax, jax.numpy as jnp
from jax import lax
import numpy as np

D_MODEL = 2048
BATCH = 2
SEQ = 4096
DEPTH = 1

HEAD_DIM = 128
A_Q_HEADS = 8
A_KV_HEADS = 2
A_GROUP = A_Q_HEADS // A_KV_HEADS
A_HALF_WINDOW = 128
B_PATTERNS = ((128, 1), (512, 4), (2048, 16))
B_GROUPS = 3
B_HEADS_PER_GROUP = 4
B_HEADS = B_GROUPS * B_HEADS_PER_GROUP
B_HALF_SPAN = 64
N_BUCKETS = 32
MAX_DISTANCE = 1024
N_ATTN_HEADS = A_Q_HEADS + B_HEADS
PEER_HEADS = 8
PEER_NKEYS = 128
PEER_EXPERTS = PEER_NKEYS * PEER_NKEYS
PEER_QDIM = 256
PEER_TOPK = 16
PEER_TOKEN_BLOCK = 128
EPS = 1e-6
A_Q_W = A_Q_HEADS * HEAD_DIM
A_KV_W = A_KV_HEADS * HEAD_DIM
B_W = B_HEADS * HEAD_DIM
B_OUT_W = B_HEADS_PER_GROUP * HEAD_DIM
IN_WIDTH = A_Q_W + 2 * A_KV_W + 3 * B_W + 2 * D_MODEL

kernel_name = "hybrid_gated_swa_dilated_peer"


def rms_norm(x, g):
    xf = x.astype(jnp.float32)
    y = xf * lax.rsqrt(jnp.mean(xf * xf, axis=-1, keepdims=True) + EPS)
    return (y * g.astype(jnp.float32)).astype(x.dtype)


def t5_bucket(rel):
    half = N_BUCKETS // 2
    max_exact = half // 2
    ret = jnp.where(rel > 0, half, 0)
    n = jnp.abs(rel)
    nf = jnp.maximum(n, 1).astype(jnp.float32)
    large = max_exact + (jnp.log(nf / max_exact) / math.log(MAX_DISTANCE / max_exact) * (half - max_exact)).astype(jnp.int32)
    large = jnp.minimum(large, half - 1)
    return ret + jnp.where(n < max_exact, n, large)


def banded_attention(q, k, v, half_w, dist_scale, bias_table, sink, with_lse):
    dt = q.dtype
    n, hk, g, l, hd = q.shape
    blk = half_w
    nb = -(-l // blk)
    pad = nb * blk - l
    q = jnp.pad(q, ((0, 0), (0, 0), (0, 0), (0, pad), (0, 0)))
    kp = jnp.pad(k, ((0, 0), (0, 0), (blk, blk + pad), (0, 0))).reshape(n, hk, nb + 2, blk, hd)
    vp = jnp.pad(v, ((0, 0), (0, 0), (blk, blk + pad), (0, 0))).reshape(n, hk, nb + 2, blk, hd)
    kb = jnp.concatenate([kp[:, :, :-2], kp[:, :, 1:-1], kp[:, :, 2:]], axis=3)
    vb = jnp.concatenate([vp[:, :, :-2], vp[:, :, 1:-1], vp[:, :, 2:]], axis=3)
    qb = q.reshape(n, hk, g, nb, blk, hd)
    s = jnp.einsum('nhgbqd,nhbkd->nhgbqk', qb, kb, preferred_element_type=jnp.float32) * (hd ** -0.5)
    delta = jnp.arange(3 * blk)[None, :] - blk - jnp.arange(blk)[:, None]
    bias = bias_table[t5_bucket(delta * dist_scale)].astype(jnp.float32)
    bias = jnp.transpose(bias, (2, 3, 0, 1))[:, :, None]
    kpos = jnp.arange(nb)[:, None] * blk - blk + jnp.arange(3 * blk)[None, :]
    valid = (jnp.abs(delta) <= half_w)[None] & ((kpos >= 0) & (kpos < l))[:, None, :]
    s = jnp.where(valid, s + bias, -jnp.inf)
    m = jnp.max(s, axis=-1)
    if sink is not None:
        sk = sink.astype(jnp.float32)[None, :, :, None, None]
        m = jnp.maximum(m, sk)
    p = jnp.exp(s - m[..., None])
    denom = jnp.sum(p, axis=-1)
    if sink is not None:
        denom = denom + jnp.exp(sk - m)
    o = jnp.einsum('nhgbqk,nhbkd->nhgbqd', p, vb.astype(jnp.float32)) / denom[..., None]
    o = o.reshape(n, hk, g, nb * blk, hd)[:, :, :, :l].astype(dt)
    if with_lse:
        lse = (m + jnp.log(denom)).reshape(n, hk, g, nb * blk)[..., :l]
        return o, lse
    return o, None


def to_strided(t, d):
    b, s, h, hd = t.shape
    return t.reshape(b, s // d, d, h, hd).transpose(0, 2, 3, 1, 4).reshape(b * d, h, s // d, hd)


def from_strided(t, b, d):
    _, h, l, hd = t.shape
    return t.reshape(b, d, h, l, hd).transpose(0, 3, 1, 2, 4).reshape(b, l * d, h, hd)


def mixer_windowed_gqa(q, k, v, bias_a, sink):
    b, s, _ = q.shape
    qh = q.reshape(b, s, A_KV_HEADS, A_GROUP, HEAD_DIM).transpose(0, 2, 3, 1, 4)
    kh = k.reshape(b, s, A_KV_HEADS, HEAD_DIM).transpose(0, 2, 1, 3)
    vh = v.reshape(b, s, A_KV_HEADS, HEAD_DIM).transpose(0, 2, 1, 3)
    o, _ = banded_attention(qh, kh, vh, A_HALF_WINDOW, 1,
                            bias_a.reshape(N_BUCKETS, A_KV_HEADS, A_GROUP),
                            sink.reshape(A_KV_HEADS, A_GROUP), False)
    return o.transpose(0, 3, 1, 2, 4).reshape(b, s, A_Q_W)


def mixer_dilated(q, k, v, bias_b):
    b, s, _ = q.shape
    gw = B_HEADS_PER_GROUP * HEAD_DIM
    outs, lses = [], []
    for gi in range(B_GROUPS):
        d = B_PATTERNS[gi][1]
        sl = slice(gi * gw, (gi + 1) * gw)
        qg = to_strided(q[..., sl].reshape(b, s, B_HEADS_PER_GROUP, HEAD_DIM), d)[:, :, None]
        kg = to_strided(k[..., sl].reshape(b, s, B_HEADS_PER_GROUP, HEAD_DIM), d)
        vg = to_strided(v[..., sl].reshape(b, s, B_HEADS_PER_GROUP, HEAD_DIM), d)
        tb = bias_b[:, gi * B_HEADS_PER_GROUP:(gi + 1) * B_HEADS_PER_GROUP][:, :, None]
        o, lse = banded_attention(qg, kg, vg, B_HALF_SPAN, d, tb, None, True)
        outs.append(from_strided(o[:, :, 0], b, d).astype(jnp.float32))
        lses.append(lse[:, :, 0].reshape(b, d, B_HEADS_PER_GROUP, s // d).transpose(0, 3, 1, 2).reshape(b, s, B_HEADS_PER_GROUP))
    w = jax.nn.softmax(jnp.stack(lses, axis=0), axis=0)
    o = jnp.sum(w[..., None] * jnp.stack(outs, axis=0), axis=0)
    return o.reshape(b, s, B_OUT_W).astype(q.dtype)


def peer_ffn(h, w_pq, sub_keys, expert_u, expert_v):
    b, s, d = h.shape
    t = h.reshape(b * s, d)
    n_tok = b * s
    q = (t @ w_pq).reshape(n_tok, PEER_HEADS, 2, PEER_QDIM // 2)
    sc = jnp.einsum('thcd,hckd->thck', q, sub_keys, preferred_element_type=jnp.float32)
    top_s, top_i = lax.top_k(sc, PEER_TOPK)
    cand_s = (top_s[:, :, 0, :, None] + top_s[:, :, 1, None, :]).reshape(n_tok, PEER_HEADS, PEER_TOPK * PEER_TOPK)
    cand_i = (top_i[:, :, 0, :, None] * PEER_NKEYS + top_i[:, :, 1, None, :]).reshape(n_tok, PEER_HEADS, PEER_TOPK * PEER_TOPK)
    best_s, best_j = lax.top_k(cand_s, PEER_TOPK)
    expert_idx = jnp.take_along_axis(cand_i, best_j, axis=-1)
    gate = jax.nn.softmax(best_s, axis=-1)
    nblk = n_tok // PEER_TOKEN_BLOCK

    def block_fn(args):
        tb, ib, gb = args
        u = expert_u[ib]
        a = jax.nn.gelu(jnp.einsum('thkd,td->thk', u, tb, preferred_element_type=jnp.float32), approximate=False)
        vv = expert_v[ib]
        return jnp.einsum('thk,thkd->td', (gb * a).astype(vv.dtype), vv)

    out = lax.map(block_fn, (t.reshape(nblk, PEER_TOKEN_BLOCK, d),
                             expert_idx.reshape(nblk, PEER_TOKEN_BLOCK, PEER_HEADS, PEER_TOPK),
                             gate.reshape(nblk, PEER_TOKEN_BLOCK, PEER_HEADS, PEER_TOPK)))
    return out.reshape(b, s, d).astype(h.dtype)


def setup_inputs(seed: int = 0) -> dict:
    key = jax.random.key(seed)
    ks = jax.random.split(key, 16)
    f32 = jnp.float32
    nrm = lambda k, shape, scale: jax.random.normal(k, shape, f32) * scale
    return {
        "x": nrm(ks[0], (BATCH, SEQ, D_MODEL), 1.0),
        "rel_bias": nrm(ks[1], (N_BUCKETS, N_ATTN_HEADS), 0.5),
        "norm_mix_g": 1.0 + nrm(ks[2], (DEPTH, D_MODEL), 0.01),
        "w_in": nrm(ks[3], (DEPTH, D_MODEL, IN_WIDTH), D_MODEL ** -0.5),
        "sink_a": nrm(ks[4], (DEPTH, A_Q_HEADS), 0.5),
        "w_oa": nrm(ks[5], (DEPTH, A_Q_W, D_MODEL), A_Q_W ** -0.5),
        "w_ob": nrm(ks[6], (DEPTH, B_OUT_W, D_MODEL), B_OUT_W ** -0.5),
        "w_out": nrm(ks[7], (DEPTH, D_MODEL, D_MODEL), D_MODEL ** -0.5),
        "norm_ffn_g": 1.0 + nrm(ks[8], (DEPTH, D_MODEL), 0.01),
        "peer_wq": nrm(ks[9], (DEPTH, D_MODEL, PEER_HEADS * PEER_QDIM), D_MODEL ** -0.5),
        "peer_keys": nrm(ks[10], (DEPTH, PEER_HEADS, 2, PEER_NKEYS, PEER_QDIM // 2), (PEER_QDIM // 2) ** -0.5),
        "peer_u": nrm(ks[11], (DEPTH, PEER_EXPERTS, D_MODEL), D_MODEL ** -0.5),
        "peer_v": nrm(ks[12], (DEPTH, PEER_EXPERTS, D_MODEL), PEER_HEADS ** -0.5),
        "norm_final_g": 1.0 + nrm(ks[13], (D_MODEL,), 0.01),
    }


def reference(x, rel_bias, norm_mix_g, w_in, sink_a, w_oa, w_ob, w_out, norm_ffn_g,
              peer_wq, peer_keys, peer_u, peer_v, norm_final_g):
    splits = [A_Q_W, A_KV_W, A_KV_W, B_W, B_W, B_W, D_MODEL, D_MODEL]
    bounds = [int(c) for c in np.cumsum(splits)[:-1]]
    bias_a = rel_bias[:, :A_Q_HEADS]
    bias_b = rel_bias[:, A_Q_HEADS:]
    for layer in range(DEPTH):
        h = rms_norm(x, norm_mix_g[layer])
        proj = jnp.einsum('bsd,de->bse', h, w_in[layer])
        qa, ka, va, qb, kb, vb, ga, gb = jnp.split(proj, bounds, axis=-1)
        ya = mixer_windowed_gqa(qa, ka, va, bias_a, sink_a[layer]) @ w_oa[layer]
        yb = mixer_dilated(qb, kb, vb, bias_b) @ w_ob[layer]
        merged = jax.nn.sigmoid(ga) * ya + jax.nn.sigmoid(gb) * yb
        x = x + merged @ w_out[layer]
        h = rms_norm(x, norm_ffn_g[layer])
        x = x + peer_ffn(h, peer_wq[layer], peer_keys[layer], peer_u[layer], peer_v[layer])
    return rms_norm(x, norm_final_g)
```

```python
import functools
import math

import numpy as np
import jax
import jax.numpy as jnp
from jax import lax
from jax.experimental import pallas as pl
from jax.experimental.pallas import tpu as pltpu

F32 = jnp.float32
BF16 = jnp.bfloat16
I32 = jnp.int32

D_MODEL = 2048
BATCH = 2
SEQ = 4096
N_TOK = BATCH * SEQ
HEAD_DIM = 128
LANES = 128

A_Q_HEADS = 8
A_KV_HEADS = 2
A_GROUP = A_Q_HEADS // A_KV_HEADS
A_HALF_WINDOW = 128
B_DILATIONS = (1, 4, 16)
B_GROUPS = 3
B_HEADS_PER_GROUP = 4
B_HALF_SPAN = 64
N_BUCKETS = 32
MAX_DISTANCE = 1024
N_ATTN_HEADS = A_Q_HEADS + B_GROUPS * B_HEADS_PER_GROUP

PEER_HEADS = 8
PEER_NKEYS = 128
PEER_EXPERTS = PEER_NKEYS * PEER_NKEYS
PEER_QDIM = 256
PEER_TOPK = 16
PEER_PICKS = PEER_HEADS * PEER_TOPK
EPS = 1e-6

A_Q_W = A_Q_HEADS * HEAD_DIM
A_KV_W = A_KV_HEADS * HEAD_DIM
B_W = B_GROUPS * B_HEADS_PER_GROUP * HEAD_DIM
B_OUT_W = B_HEADS_PER_GROUP * HEAD_DIM
IN_WIDTH = A_Q_W + 2 * A_KV_W + 3 * B_W + 2 * D_MODEL
QA_CB = 0
KA_CB = A_Q_W // LANES
VA_CB = (A_Q_W + A_KV_W) // LANES
QB_CB = (A_Q_W + 2 * A_KV_W) // LANES
KB_CB = QB_CB + B_W // LANES
VB_CB = KB_CB + B_W // LANES
GA_OFF = A_Q_W + 2 * A_KV_W + 3 * B_W
GB_OFF = GA_OFF + D_MODEL
IN_CB = IN_WIDTH // LANES

NEG = -1e30
ATT_SCALE = HEAD_DIM ** -0.5
ATT_QT = 128

MIB = 1024 * 1024


def _params(sem, vmem_mib):
    return pltpu.CompilerParams(dimension_semantics=sem, vmem_limit_bytes=vmem_mib * MIB)


def _rms(x, g):
    return x * lax.rsqrt(jnp.mean(x * x, axis=-1, keepdims=True) + EPS) * g


def _inproj_kernel(x_ref, g_ref, w_ref, o_ref, h_scr):
    @pl.when(pl.program_id(1) == 0)
    def _():
        h_scr[...] = _rms(x_ref[...], g_ref[...]).astype(BF16)

    o_ref[...] = jnp.dot(h_scr[...], w_ref[...], preferred_element_type=F32).astype(o_ref.dtype)


def _inproj(x2, g, w, tm=512, tn=1024):
    n = w.shape[1]
    return pl.pallas_call(
        _inproj_kernel,
        out_shape=jax.ShapeDtypeStruct((N_TOK, n), BF16),
        grid=(N_TOK // tm, n // tn),
        in_specs=[
            pl.BlockSpec((tm, D_MODEL), lambda i, j: (i, 0)),
            pl.BlockSpec((1, D_MODEL), lambda i, j: (0, 0)),
            pl.BlockSpec((D_MODEL, tn), lambda i, j: (0, j)),
        ],
        out_specs=pl.BlockSpec((tm, tn), lambda i, j: (i, j)),
        scratch_shapes=[pltpu.VMEM((tm, D_MODEL), BF16)],
        compiler_params=_params(("parallel", "arbitrary"), 40),
        name="inproj",
    )(x2, g, w)


def _t5_bucket_np(rel):
    half = N_BUCKETS // 2
    max_exact = half // 2
    ret = np.where(rel > 0, half, 0)
    n = np.abs(rel)
    nf = np.maximum(n, 1).astype(np.float64)
    large = max_exact + (np.log(nf / max_exact) / math.log(MAX_DISTANCE / max_exact) * (half - max_exact)).astype(np.int64)
    large = np.minimum(large, half - 1)
    return (ret + np.where(n < max_exact, n, large)).astype(np.int32)


def _bucket_matrix(hw, dist_scale):
    w = ATT_QT + 2 * hw
    delta = (np.arange(w)[None, :] - hw) - np.arange(ATT_QT)[:, None]
    bkt = _t5_bucket_np(delta * dist_scale)
    return np.where(np.abs(delta) <= hw, bkt, -1).astype(np.int32)


def _attn_kernel(tab_ref, sink_ref, q_ref, k_ref, v_ref, bkt_ref, *rest,
                 seq, hw, head_base, head_axis, has_sink, with_lse):
    if with_lse:
        o_ref, lse_ref, kpad, vpad, bias_scr = rest
    else:
        o_ref, kpad, vpad, bias_scr = rest
    qt = ATT_QT
    win = qt + 2 * hw
    head = head_base + pl.program_id(head_axis)

    bkt = bkt_ref[...]
    bias = jnp.full((qt, win), NEG, F32)
    for b in range(N_BUCKETS):
        bias = jnp.where(bkt == b, tab_ref[b, head], bias)
    bias_scr[...] = bias

    zeros = jnp.zeros((hw, HEAD_DIM), BF16)
    for pad_ref, src_ref in ((kpad, k_ref), (vpad, v_ref)):
        pad_ref[0:hw, :] = zeros
        pad_ref[hw + seq:hw + seq + hw, :] = zeros
        pad_ref[hw:hw + seq, :] = src_ref[0]

    def body(t, carry):
        q0 = pl.multiple_of(t * qt, qt)
        q = q_ref[0, pl.ds(q0, qt), :]
        kw = kpad[pl.ds(q0, win), :]
        vw = vpad[pl.ds(q0, win), :]
        s = lax.dot_general(q, kw, (((1,), (1,)), ((), ())), preferred_element_type=F32)
        s = s * ATT_SCALE + bias_scr[...]
        kpos = q0 - hw + lax.broadcasted_iota(I32, (qt, win), 1)
        s = jnp.where((kpos >= 0) & (kpos < seq), s, NEG)
        m = jnp.max(s, axis=-1, keepdims=True)
        if has_sink:
            sk = sink_ref[head]
            m = jnp.maximum(m, sk)
        p = jnp.exp(s - m)
        den = jnp.sum(p, axis=-1, keepdims=True)
        if has_sink:
            den = den + jnp.exp(sk - m)
        o = jnp.dot(p.astype(BF16), vw, preferred_element_type=F32) / den
        o_ref[0, pl.ds(q0, qt), :] = o.astype(o_ref.dtype)
        if with_lse:
            lse_ref[0, pl.ds(q0, qt), :] = jnp.broadcast_to(m + jnp.log(den), (qt, HEAD_DIM))
        return carry

    lax.fori_loop(0, seq // qt, body, 0)


def _attention(proj, rel_bias, sink, *, dil, hw, q_cb, k_cb, v_cb, n_heads, kv_group,
               head_base, has_sink, with_lse):
    seq = SEQ // dil
    pv = proj.reshape(BATCH, seq, dil * IN_WIDTH)
    bkt = jnp.asarray(_bucket_matrix(hw, dil))
    win = ATT_QT + 2 * hw
    out_w = n_heads * HEAD_DIM
    blk = (1, seq, HEAD_DIM)
    o_spec = pl.BlockSpec(blk, lambda b, r, h: (b, 0, r * n_heads + h))
    out_shape = [jax.ShapeDtypeStruct((BATCH, seq, dil * out_w), BF16)]
    out_specs = [o_spec]
    if with_lse:
        out_shape.append(jax.ShapeDtypeStruct((BATCH, seq, dil * out_w), F32))
        out_specs.append(o_spec)
    kern = functools.partial(_attn_kernel, seq=seq, hw=hw, head_base=head_base, head_axis=2,
                             has_sink=has_sink, with_lse=with_lse)
    outs = pl.pallas_call(
        kern,
        out_shape=out_shape,
        grid=(BATCH, dil, n_heads),
        in_specs=[
            pl.BlockSpec(memory_space=pltpu.SMEM),
            pl.BlockSpec(memory_space=pltpu.SMEM),
            pl.BlockSpec(blk, lambda b, r, h: (b, 0, r * IN_CB + q_cb + h)),
            pl.BlockSpec(blk, lambda b, r, h: (b, 0, r * IN_CB + k_cb + h // kv_group)),
            pl.BlockSpec(blk, lambda b, r, h: (b, 0, r * IN_CB + v_cb + h // kv_group)),
            pl.BlockSpec((ATT_QT, win), lambda b, r, h: (0, 0)),
        ],
        out_specs=out_specs,
        scratch_shapes=[
            pltpu.VMEM((seq + 2 * hw, HEAD_DIM), BF16),
            pltpu.VMEM((seq + 2 * hw, HEAD_DIM), BF16),
            pltpu.VMEM((ATT_QT, win), F32),
        ],
        compiler_params=_params(("parallel", "parallel", "arbitrary"), 32),
        name=f"attn_d{dil}_h{head_base}",
    )(rel_bias, sink, pv, pv, pv, bkt)
    return [o.reshape(N_TOK, out_w) for o in outs]


def _outproj_kernel(oa_ref, o0_ref, o1_ref, o2_ref, l0_ref, l1_ref, l2_ref, ga_ref, gb_ref,
                    x_ref, woa_ref, wob_ref, wout_ref, gn_ref, x1_ref, h2_ref):
    l0, l1, l2 = l0_ref[...], l1_ref[...], l2_ref[...]
    mx = jnp.maximum(jnp.maximum(l0, l1), l2)
    e0, e1, e2 = jnp.exp(l0 - mx), jnp.exp(l1 - mx), jnp.exp(l2 - mx)
    den = e0 + e1 + e2
    ob = ((e0 / den) * o0_ref[...].astype(F32) + (e1 / den) * o1_ref[...].astype(F32)
          + (e2 / den) * o2_ref[...].astype(F32))
    ya = jnp.dot(oa_ref[...], woa_ref[...], preferred_element_type=F32)
    yb = jnp.dot(ob.astype(BF16), wob_ref[...], preferred_element_type=F32)
    merged = (jax.nn.sigmoid(ga_ref[...].astype(F32)) * ya
              + jax.nn.sigmoid(gb_ref[...].astype(F32)) * yb)
    x1 = x_ref[...] + jnp.dot(merged.astype(BF16), wout_ref[...], preferred_element_type=F32)
    x1_ref[...] = x1
    h2_ref[...] = _rms(x1, gn_ref[...]).astype(BF16)


def _outproj(oa, obs, lses, proj, x2, w_oa, w_ob, w_out, g_ffn, tm=256):
    row = lambda w: pl.BlockSpec((tm, w), lambda i: (i, 0))
    const = lambda shape: pl.BlockSpec(shape, lambda i: (0, 0), pipeline_mode=pl.Buffered(1))
    return pl.pallas_call(
        _outproj_kernel,
        out_shape=[jax.ShapeDtypeStruct((N_TOK, D_MODEL), F32),
                   jax.ShapeDtypeStruct((N_TOK, D_MODEL), BF16)],
        grid=(N_TOK // tm,),
        in_specs=[row(A_Q_W), row(B_OUT_W), row(B_OUT_W), row(B_OUT_W),
                  row(B_OUT_W), row(B_OUT_W), row(B_OUT_W),
                  pl.BlockSpec((tm, D_MODEL), lambda i: (i, GA_OFF // D_MODEL)),
                  pl.BlockSpec((tm, D_MODEL), lambda i: (i, GB_OFF // D_MODEL)),
                  row(D_MODEL),
                  const((A_Q_W, D_MODEL)), const((B_OUT_W, D_MODEL)), const((D_MODEL, D_MODEL)),
                  const((1, D_MODEL))],
        out_specs=[row(D_MODEL), row(D_MODEL)],
        compiler_params=_params(("parallel",), 48),
        name="outproj",
    )(oa, *obs, *lses, proj, proj, x2, w_oa, w_ob, w_out, g_ffn)


def _extract_top(s_scr, n_rows, tm, outs_ref, outv_ref, payload_ref=None):
    iota = lax.broadcasted_iota(I32, (n_rows, tm), 0)

    def body(k, carry):
        s = s_scr[...]
        m = jnp.max(s, axis=0, keepdims=True)
        idx = jnp.min(jnp.where(s == m, iota, n_rows), axis=0, keepdims=True)
        hit = iota == idx
        if payload_ref is None:
            val = idx
        else:
            val = jnp.max(jnp.where(hit, payload_ref[...], -1), axis=0, keepdims=True)
        outs_ref[pl.ds(k, 1), :] = m
        outv_ref[pl.ds(k, 1), :] = val
        s_scr[...] = jnp.where(hit, -jnp.inf, s)
        return carry

    lax.fori_loop(0, PEER_TOPK, body, 0)


def _route_kernel(h_ref, wq_ref, keys_ref, e_ref, g_ref,
                  q_scr, s_scr, c_scr, ci_scr, ts_scr, ti_scr, bs_scr, be_scr, et_scr, gt_scr, *, tm):
    k = PEER_TOPK
    q_scr[...] = jnp.dot(h_ref[...], wq_ref[...], preferred_element_type=F32).astype(BF16)
    for h in range(PEER_HEADS):
        for c in range(2):
            col = (h * 2 + c) * LANES
            s_scr[...] = lax.dot_general(keys_ref[h * 2 + c], q_scr[:, col:col + LANES],
                                         (((1,), (1,)), ((), ())), preferred_element_type=F32)
            _extract_top(s_scr, PEER_NKEYS, tm, ts_scr.at[c], ti_scr.at[c])
        for k1 in range(k):
            c_scr[k1 * k:(k1 + 1) * k, :] = ts_scr[0, k1:k1 + 1, :] + ts_scr[1]
            ci_scr[k1 * k:(k1 + 1) * k, :] = ti_scr[0, k1:k1 + 1, :] * PEER_NKEYS + ti_scr[1]
        _extract_top(c_scr, k * k, tm, bs_scr, be_scr, payload_ref=ci_scr)
        bs = bs_scr[...]
        ex = jnp.exp(bs - jnp.max(bs, axis=0, keepdims=True))
        et_scr[h * k:(h + 1) * k, :] = be_scr[...]
        gt_scr[h * k:(h + 1) * k, :] = ex / jnp.sum(ex, axis=0, keepdims=True)
    for i in range(tm // LANES):
        sl = slice(i * LANES, (i + 1) * LANES)
        e_ref[sl, :] = pltpu.bitcast(pltpu.bitcast(et_scr[:, sl], F32).T, I32)
        g_ref[sl, :] = gt_scr[:, sl].T


def _route(h2, wq, keys, tm=256):
    k = PEER_TOPK
    return pl.pallas_call(
        functools.partial(_route_kernel, tm=tm),
        out_shape=[jax.ShapeDtypeStruct((N_TOK, PEER_PICKS), I32),
                   jax.ShapeDtypeStruct((N_TOK, PEER_PICKS), F32)],
        grid=(N_TOK // tm,),
        in_specs=[pl.BlockSpec((tm, D_MODEL), lambda i: (i, 0)),
                  pl.BlockSpec((D_MODEL, PEER_HEADS * PEER_QDIM), lambda i: (0, 0),
                               pipeline_mode=pl.Buffered(1)),
                  pl.BlockSpec((PEER_HEADS * 2, PEER_NKEYS, PEER_QDIM // 2), lambda i: (0, 0, 0))],
        out_specs=[pl.BlockSpec((tm, PEER_PICKS), lambda i: (i, 0)),
                   pl.BlockSpec((tm, PEER_PICKS), lambda i: (i, 0))],
        scratch_shapes=[
            pltpu.VMEM((tm, PEER_HEADS * PEER_QDIM), BF16),
            pltpu.VMEM((PEER_NKEYS, tm), F32),
            pltpu.VMEM((k * k, tm), F32),
            pltpu.VMEM((k * k, tm), I32),
            pltpu.VMEM((2, k, tm), F32),
            pltpu.VMEM((2, k, tm), I32),
            pltpu.VMEM((k, tm), F32),
            pltpu.VMEM((k, tm), I32),
            pltpu.VMEM((PEER_PICKS, tm), I32),
            pltpu.VMEM((PEER_PICKS, tm), F32),
        ],
        compiler_params=_params(("parallel",), 32),
        name="peer_route",
    )(h2, wq, keys)


def _peer_up_kernel(h_ref, u_ref, e_ref, act_ref, *, te):
    j = pl.program_id(1)

    @pl.when(j == 0)
    def _():
        act_ref[...] = jnp.zeros_like(act_ref)

    dense = lax.dot_general(h_ref[...], u_ref[...], (((1,), (1,)), ((), ())),
                            preferred_element_type=F32)
    e = e_ref[...]
    row = e >> 7
    col = e & (PEER_NKEYS - 1)
    acc = act_ref[...]
    for q in range(te // LANES):
        got = jnp.take_along_axis(dense[:, q * LANES:(q + 1) * LANES], col, axis=1)
        acc = jnp.where(row == j * (te // LANES) + q, got, acc)
    act_ref[...] = acc


def _peer_up(h2, u, e, tb=512, te=512):
    return pl.pallas_call(
        functools.partial(_peer_up_kernel, te=te),
        out_shape=jax.ShapeDtypeStruct((N_TOK, PEER_PICKS), F32),
        grid=(N_TOK // tb, PEER_EXPERTS // te),
        in_specs=[pl.BlockSpec((tb, D_MODEL), lambda i, j: (i, 0)),
                  pl.BlockSpec((te, D_MODEL), lambda i, j: (j, 0)),
                  pl.BlockSpec((tb, PEER_PICKS), lambda i, j: (i, 0))],
        out_specs=pl.BlockSpec((tb, PEER_PICKS), lambda i, j: (i, 0)),
        compiler_params=_params(("parallel", "arbitrary"), 32),
        name="peer_up",
    )(h2, u, e)


def _peer_coef_kernel(e_ref, g_ref, act_ref, p_ref, row_scr, col_scr, w_scr, *, tp):
    e = e_ref[...]
    row_scr[...] = e >> 7
    col_scr[...] = e & (PEER_NKEYS - 1)
    act = act_ref[...]
    w_scr[...] = g_ref[...] * (0.5 * act * (1.0 + lax.erf(act * math.sqrt(0.5))))
    iota = lax.broadcasted_iota(I32, (PEER_NKEYS, PEER_PICKS), 0)

    def body(t, carry):
        rows = row_scr[pl.ds(t, 1), :]
        cols = col_scr[pl.ds(t, 1), :]
        w = w_scr[pl.ds(t, 1), :]
        left = jnp.where(iota == rows, 1.0, 0.0).astype(BF16)
        right = jnp.where(iota == cols, w, 0.0).astype(BF16)
        c = lax.dot_general(left, right, (((1,), (1,)), ((), ())), preferred_element_type=F32)
        p_ref[t] = c.astype(p_ref.dtype)
        return carry

    lax.fori_loop(0, tp, body, 0)


def _peer_coef(e, g, act, tp=64):
    tok = pl.BlockSpec((tp, PEER_PICKS), lambda i: (i, 0))
    return pl.pallas_call(
        functools.partial(_peer_coef_kernel, tp=tp),
        out_shape=jax.ShapeDtypeStruct((N_TOK, PEER_NKEYS, PEER_NKEYS), BF16),
        grid=(N_TOK // tp,),
        in_specs=[tok, tok, tok],
        out_specs=pl.BlockSpec((tp, PEER_NKEYS, PEER_NKEYS), lambda i: (i, 0, 0)),
        scratch_shapes=[pltpu.VMEM((tp, PEER_PICKS), I32),
                        pltpu.VMEM((tp, PEER_PICKS), I32),
                        pltpu.VMEM((tp, PEER_PICKS), F32)],
        compiler_params=_params(("parallel",), 32),
        name="peer_coef",
    )(e, g, act)


def _peer_down_kernel(p_ref, v_ref, x_ref, g_ref, o_ref):
    j = pl.program_id(1)

    @pl.when(j == 0)
    def _():
        o_ref[...] = x_ref[...]

    o_ref[...] += jnp.dot(p_ref[...], v_ref[...], preferred_element_type=F32)

    @pl.when(j == pl.num_programs(1) - 1)
    def _():
        o_ref[...] = _rms(o_ref[...], g_ref[...])


def _peer_down(coef, v, x1, g_final, tm=512, tk=512):
    return pl.pallas_call(
        _peer_down_kernel,
        out_shape=jax.ShapeDtypeStruct((N_TOK, D_MODEL), F32),
        grid=(N_TOK // tm, PEER_EXPERTS // tk),
        in_specs=[pl.BlockSpec((tm, tk), lambda i, j: (i, j)),
                  pl.BlockSpec((tk, D_MODEL), lambda i, j: (j, 0)),
                  pl.BlockSpec((tm, D_MODEL), lambda i, j: (i, 0)),
                  pl.BlockSpec((1, D_MODEL), lambda i, j: (0, 0))],
        out_specs=pl.BlockSpec((tm, D_MODEL), lambda i, j: (i, 0)),
        compiler_params=_params(("parallel", "arbitrary"), 40),
        name="peer_down",
    )(coef, v, x1, g_final)


def kernel(x, rel_bias, norm_mix_g, w_in, sink_a, w_oa, w_ob, w_out, norm_ffn_g,
           peer_wq, peer_keys, peer_u, peer_v, norm_final_g):
    assert x.shape == (BATCH, SEQ, D_MODEL) and w_in.shape[0] == 1
    x2 = x.reshape(N_TOK, D_MODEL)
    proj = _inproj(x2, norm_mix_g, w_in[0].astype(BF16))

    (oa,) = _attention(proj, rel_bias, sink_a[0], dil=1, hw=A_HALF_WINDOW,
                       q_cb=QA_CB, k_cb=KA_CB, v_cb=VA_CB, n_heads=A_Q_HEADS, kv_group=A_GROUP,
                       head_base=0, has_sink=True, with_lse=False)
    obs, lses = [], []
    for gi, dil in enumerate(B_DILATIONS):
        off = gi * B_HEADS_PER_GROUP
        o, lse = _attention(proj, rel_bias, sink_a[0], dil=dil, hw=B_HALF_SPAN,
                            q_cb=QB_CB + off, k_cb=KB_CB + off, v_cb=VB_CB + off,
                            n_heads=B_HEADS_PER_GROUP, kv_group=1,
                            head_base=A_Q_HEADS + off, has_sink=False, with_lse=True)
        obs.append(o)
        lses.append(lse)

    x1, h2 = _outproj(oa, obs, lses, proj, x2, w_oa[0].astype(BF16), w_ob[0].astype(BF16),
                      w_out[0].astype(BF16), norm_ffn_g)

    keys = peer_keys[0].reshape(PEER_HEADS * 2, PEER_NKEYS, PEER_QDIM // 2).astype(BF16)
    e, gate = _route(h2, peer_wq[0].astype(BF16), keys)
    act = _peer_up(h2, peer_u[0].astype(BF16), e)
    coef = _peer_coef(e, gate, act).reshape(N_TOK, PEER_EXPERTS)
    out = _peer_down(coef, peer_v[0].astype(BF16), x1, norm_final_g.reshape(1, D_MODEL))
    return out.reshape(BATCH, SEQ, D_MODEL)
```

```python
import functools
import math

import numpy as np
import jax
import jax.numpy as jnp
from jax import lax
from jax.experimental import pallas as pl
from jax.experimental.pallas import tpu as pltpu

F32 = jnp.float32
BF16 = jnp.bfloat16
I32 = jnp.int32

D_MODEL = 2048
BATCH = 2
SEQ = 4096
N_TOK = BATCH * SEQ
HEAD_DIM = 128
LANES = 128

A_Q_HEADS = 8
A_KV_HEADS = 2
A_GROUP = A_Q_HEADS // A_KV_HEADS
A_HALF_WINDOW = 128
B_DILATIONS = (1, 4, 16)
B_GROUPS = 3
B_HEADS_PER_GROUP = 4
B_HALF_SPAN = 64
N_BUCKETS = 32
MAX_DISTANCE = 1024
N_ATTN_HEADS = A_Q_HEADS + B_GROUPS * B_HEADS_PER_GROUP

PEER_HEADS = 8
PEER_NKEYS = 128
PEER_EXPERTS = PEER_NKEYS * PEER_NKEYS
PEER_QDIM = 256
PEER_TOPK = 16
PEER_PICKS = PEER_HEADS * PEER_TOPK
EPS = 1e-6

A_Q_W = A_Q_HEADS * HEAD_DIM
A_KV_W = A_KV_HEADS * HEAD_DIM
B_W = B_GROUPS * B_HEADS_PER_GROUP * HEAD_DIM
B_OUT_W = B_HEADS_PER_GROUP * HEAD_DIM
IN_WIDTH = A_Q_W + 2 * A_KV_W + 3 * B_W + 2 * D_MODEL
QA_CB = 0
KA_CB = A_Q_W // LANES
VA_CB = (A_Q_W + A_KV_W) // LANES
QB_CB = (A_Q_W + 2 * A_KV_W) // LANES
KB_CB = QB_CB + B_W // LANES
VB_CB = KB_CB + B_W // LANES
GA_OFF = A_Q_W + 2 * A_KV_W + 3 * B_W
GB_OFF = GA_OFF + D_MODEL
IN_CB = IN_WIDTH // LANES

NEG = -1e30
ATT_SCALE = HEAD_DIM ** -0.5
ATT_QT = 128

MIB = 1024 * 1024


def _params(sem, vmem_mib):
    return pltpu.CompilerParams(dimension_semantics=sem, vmem_limit_bytes=vmem_mib * MIB)


def _rms(x, g):
    return x * lax.rsqrt(jnp.mean(x * x, axis=-1, keepdims=True) + EPS) * g


def _inproj_kernel(x_ref, g_ref, w_ref, o_ref, h_scr):
    @pl.when(pl.program_id(1) == 0)
    def _():
        h_scr[...] = _rms(x_ref[...], g_ref[...]).astype(BF16)

    o_ref[...] = jnp.dot(h_scr[...], w_ref[...], preferred_element_type=F32).astype(o_ref.dtype)


def _inproj(x2, g, w, tm=512, tn=1024):
    n = w.shape[1]
    return pl.pallas_call(
        _inproj_kernel,
        out_shape=jax.ShapeDtypeStruct((N_TOK, n), BF16),
        grid=(N_TOK // tm, n // tn),
        in_specs=[
            pl.BlockSpec((tm, D_MODEL), lambda i, j: (i, 0)),
            pl.BlockSpec((1, D_MODEL), lambda i, j: (0, 0)),
            pl.BlockSpec((D_MODEL, tn), lambda i, j: (0, j)),
        ],
        out_specs=pl.BlockSpec((tm, tn), lambda i, j: (i, j)),
        scratch_shapes=[pltpu.VMEM((tm, D_MODEL), BF16)],
        compiler_params=_params(("parallel", "arbitrary"), 40),
        name="inproj",
    )(x2, g, w)


def _t5_bucket_np(rel):
    half = N_BUCKETS // 2
    max_exact = half // 2
    ret = np.where(rel > 0, half, 0)
    n = np.abs(rel)
    nf = np.maximum(n, 1).astype(np.float64)
    large = max_exact + (np.log(nf / max_exact) / math.log(MAX_DISTANCE / max_exact) * (half - max_exact)).astype(np.int64)
    large = np.minimum(large, half - 1)
    return (ret + np.where(n < max_exact, n, large)).astype(np.int32)


def _bucket_matrix(hw, dist_scale):
    w = ATT_QT + 2 * hw
    delta = (np.arange(w)[None, :] - hw) - np.arange(ATT_QT)[:, None]
    bkt = _t5_bucket_np(delta * dist_scale)
    return np.where(np.abs(delta) <= hw, bkt, -1).astype(np.int32)


def _attn_kernel(tab_ref, sink_ref, q_ref, k_ref, v_ref, bkt_ref, *rest,
                 dil, hw, head_base, has_sink, with_lse):
    n_out = 2 if with_lse else 1
    outs, scr = rest[:n_out], rest[n_out:]
    o_ref = outs[0]
    kpad, vpad, bias_scr = scr[:3]
    if dil > 1:
        q32, k32, v32, o32, qres = scr[3:8]
        if with_lse:
            l32 = scr[8]
    seq = SEQ // dil
    qt = ATT_QT
    win = qt + 2 * hw
    head = head_base + pl.program_id(1)

    bkt = bkt_ref[...]
    bias = jnp.full((qt, win), NEG, F32)
    for b in range(N_BUCKETS):
        bias = jnp.where(bkt == b, tab_ref[b, head], bias)
    bias_scr[...] = bias

    zeros = jnp.zeros((hw, HEAD_DIM), BF16)
    for pad_ref in (kpad, vpad):
        pad_ref[0:hw, :] = zeros
        pad_ref[hw + seq:hw + seq + hw, :] = zeros
    if dil == 1:
        kpad[hw:hw + seq, :] = k_ref[0]
        vpad[hw:hw + seq, :] = v_ref[0]
    else:
        q32[...] = q_ref[0].astype(F32)
        k32[...] = k_ref[0].astype(F32)
        v32[...] = v_ref[0].astype(F32)

    def residue(r, carry):
        if dil > 1:
            kpad[hw:hw + seq, :] = k32[pl.ds(r, seq, stride=dil), :].astype(BF16)
            vpad[hw:hw + seq, :] = v32[pl.ds(r, seq, stride=dil), :].astype(BF16)
            qres[...] = q32[pl.ds(r, seq, stride=dil), :].astype(BF16)

        def body(t, c):
            q0 = pl.multiple_of(t * qt, qt)
            if dil > 1:
                q = qres[pl.ds(q0, qt), :]
            else:
                q = q_ref[0, pl.ds(q0, qt), :]
            kw = kpad[pl.ds(q0, win), :]
            vw = vpad[pl.ds(q0, win), :]
            s = lax.dot_general(q, kw, (((1,), (1,)), ((), ())), preferred_element_type=F32)
            s = s * ATT_SCALE + bias_scr[...]
            kpos = q0 - hw + lax.broadcasted_iota(I32, (qt, win), 1)
            s = jnp.where((kpos >= 0) & (kpos < seq), s, NEG)
            m = jnp.max(s, axis=-1, keepdims=True)
            if has_sink:
                sk = sink_ref[head]
                m = jnp.maximum(m, sk)
            p = jnp.exp(s - m)
            den = jnp.sum(p, axis=-1, keepdims=True)
            if has_sink:
                den = den + jnp.exp(sk - m)
            o = jnp.dot(p.astype(BF16), vw, preferred_element_type=F32) / den
            if with_lse:
                lse = jnp.broadcast_to(m + jnp.log(den), (qt, HEAD_DIM))
            if dil > 1:
                o32[pl.ds(r + q0 * dil, qt, stride=dil), :] = o
                if with_lse:
                    l32[pl.ds(r + q0 * dil, qt, stride=dil), :] = lse
            else:
                o_ref[0, pl.ds(q0, qt), :] = o.astype(o_ref.dtype)
                if with_lse:
                    outs[1][0, pl.ds(q0, qt), :] = lse
            return c

        lax.fori_loop(0, seq // qt, body, 0)
        return carry

    if dil > 1:
        lax.fori_loop(0, dil, residue, 0)
        o_ref[0] = o32[...].astype(o_ref.dtype)
        if with_lse:
            outs[1][0] = l32[...]
    else:
        residue(0, 0)


def _attention(proj, rel_bias, sink, *, dil, hw, q_cb, k_cb, v_cb, n_heads, kv_group,
               head_base, has_sink, with_lse):
    seq = SEQ // dil
    pv = proj.reshape(BATCH, SEQ, IN_WIDTH)
    bkt = jnp.asarray(_bucket_matrix(hw, dil))
    win = ATT_QT + 2 * hw
    out_w = n_heads * HEAD_DIM
    blk = (1, SEQ, HEAD_DIM)
    o_spec = pl.BlockSpec(blk, lambda b, h: (b, 0, h))
    out_shape = [jax.ShapeDtypeStruct((BATCH, SEQ, out_w), BF16)]
    out_specs = [o_spec]
    if with_lse:
        out_shape.append(jax.ShapeDtypeStruct((BATCH, SEQ, out_w), F32))
        out_specs.append(o_spec)
    scratch = [pltpu.VMEM((seq + 2 * hw, HEAD_DIM), BF16),
               pltpu.VMEM((seq + 2 * hw, HEAD_DIM), BF16),
               pltpu.VMEM((ATT_QT, win), F32)]
    if dil > 1:
        scratch += [pltpu.VMEM((SEQ, HEAD_DIM), F32)] * 4 + [pltpu.VMEM((seq, HEAD_DIM), BF16)]
        if with_lse:
            scratch.append(pltpu.VMEM((SEQ, HEAD_DIM), F32))
    kern = functools.partial(_attn_kernel, dil=dil, hw=hw, head_base=head_base,
                             has_sink=has_sink, with_lse=with_lse)
    outs = pl.pallas_call(
        kern,
        out_shape=out_shape,
        grid=(BATCH, n_heads),
        in_specs=[
            pl.BlockSpec(memory_space=pltpu.SMEM),
            pl.BlockSpec(memory_space=pltpu.SMEM),
            pl.BlockSpec(blk, lambda b, h: (b, 0, q_cb + h)),
            pl.BlockSpec(blk, lambda b, h: (b, 0, k_cb + h // kv_group)),
            pl.BlockSpec(blk, lambda b, h: (b, 0, v_cb + h // kv_group)),
            pl.BlockSpec((ATT_QT, win), lambda b, h: (0, 0)),
        ],
        out_specs=out_specs,
        scratch_shapes=scratch,
        compiler_params=_params(("parallel", "arbitrary"), 40),
        name=f"attn_d{dil}_h{head_base}",
    )(rel_bias, sink, pv, pv, pv, bkt)
    return [o.reshape(N_TOK, out_w) for o in outs]


def _outproj_kernel(oa_ref, o0_ref, o1_ref, o2_ref, l0_ref, l1_ref, l2_ref, ga_ref, gb_ref,
                    x_ref, woa_ref, wob_ref, wout_ref, gn_ref, x1_ref, h2_ref):
    l0, l1, l2 = l0_ref[...], l1_ref[...], l2_ref[...]
    mx = jnp.maximum(jnp.maximum(l0, l1), l2)
    e0, e1, e2 = jnp.exp(l0 - mx), jnp.exp(l1 - mx), jnp.exp(l2 - mx)
    den = e0 + e1 + e2
    ob = ((e0 / den) * o0_ref[...].astype(F32) + (e1 / den) * o1_ref[...].astype(F32)
          + (e2 / den) * o2_ref[...].astype(F32))
    ya = jnp.dot(oa_ref[...], woa_ref[...], preferred_element_type=F32)
    yb = jnp.dot(ob.astype(BF16), wob_ref[...], preferred_element_type=F32)
    merged = (jax.nn.sigmoid(ga_ref[...].astype(F32)) * ya
              + jax.nn.sigmoid(gb_ref[...].astype(F32)) * yb)
    x1 = x_ref[...] + jnp.dot(merged.astype(BF16), wout_ref[...], preferred_element_type=F32)
    x1_ref[...] = x1
    h2_ref[...] = _rms(x1, gn_ref[...]).astype(BF16)


def _outproj(oa, obs, lses, proj, x2, w_oa, w_ob, w_out, g_ffn, tm=256):
    row = lambda w: pl.BlockSpec((tm, w), lambda i: (i, 0))
    const = lambda shape: pl.BlockSpec(shape, lambda i: (0, 0), pipeline_mode=pl.Buffered(1))
    return pl.pallas_call(
        _outproj_kernel,
        out_shape=[jax.ShapeDtypeStruct((N_TOK, D_MODEL), F32),
                   jax.ShapeDtypeStruct((N_TOK, D_MODEL), BF16)],
        grid=(N_TOK // tm,),
        in_specs=[row(A_Q_W), row(B_OUT_W), row(B_OUT_W), row(B_OUT_W),
                  row(B_OUT_W), row(B_OUT_W), row(B_OUT_W),
                  pl.BlockSpec((tm, D_MODEL), lambda i: (i, GA_OFF // D_MODEL)),
                  pl.BlockSpec((tm, D_MODEL), lambda i: (i, GB_OFF // D_MODEL)),
                  row(D_MODEL),
                  const((A_Q_W, D_MODEL)), const((B_OUT_W, D_MODEL)), const((D_MODEL, D_MODEL)),
                  const((1, D_MODEL))],
        out_specs=[row(D_MODEL), row(D_MODEL)],
        compiler_params=_params(("parallel",), 48),
        name="outproj",
    )(oa, *obs, *lses, proj, proj, x2, w_oa, w_ob, w_out, g_ffn)


def _extract_top(s_scr, n_rows, tm, outs_ref, outv_ref, payload_ref=None):
    iota = lax.broadcasted_iota(I32, (n_rows, tm), 0)

    def body(k, carry):
        s = s_scr[...]
        m = jnp.max(s, axis=0, keepdims=True)
        idx = jnp.min(jnp.where(s == m, iota, n_rows), axis=0, keepdims=True)
        hit = iota == idx
        if payload_ref is None:
            val = idx
        else:
            val = jnp.max(jnp.where(hit, payload_ref[...], -1), axis=0, keepdims=True)
        outs_ref[pl.ds(k, 1), :] = m
        outv_ref[pl.ds(k, 1), :] = val
        s_scr[...] = jnp.where(hit, -jnp.inf, s)
        return carry

    lax.fori_loop(0, PEER_TOPK, body, 0)


def _route_kernel(h_ref, wq_ref, keys_ref, e_ref, g_ref,
                  q_scr, s_scr, c_scr, ci_scr, ts_scr, ti_scr, bs_scr, be_scr, et_scr, gt_scr, *, tm):
    k = PEER_TOPK
    q_scr[...] = jnp.dot(h_ref[...], wq_ref[...], preferred_element_type=F32).astype(BF16)
    for h in range(PEER_HEADS):
        for c in range(2):
            col = (h * 2 + c) * LANES
            s_scr[...] = lax.dot_general(keys_ref[h * 2 + c], q_scr[:, col:col + LANES],
                                         (((1,), (1,)), ((), ())), preferred_element_type=F32)
            _extract_top(s_scr, PEER_NKEYS, tm, ts_scr.at[c], ti_scr.at[c])
        for k1 in range(k):
            c_scr[k1 * k:(k1 + 1) * k, :] = ts_scr[0, k1:k1 + 1, :] + ts_scr[1]
            ci_scr[k1 * k:(k1 + 1) * k, :] = ti_scr[0, k1:k1 + 1, :] * PEER_NKEYS + ti_scr[1]
        _extract_top(c_scr, k * k, tm, bs_scr, be_scr, payload_ref=ci_scr)
        bs = bs_scr[...]
        ex = jnp.exp(bs - jnp.max(bs, axis=0, keepdims=True))
        et_scr[h * k:(h + 1) * k, :] = be_scr[...]
        gt_scr[h * k:(h + 1) * k, :] = ex / jnp.sum(ex, axis=0, keepdims=True)
    for i in range(tm // LANES):
        sl = slice(i * LANES, (i + 1) * LANES)
        e_ref[sl, :] = pltpu.bitcast(pltpu.bitcast(et_scr[:, sl], F32).T, I32)
        g_ref[sl, :] = gt_scr[:, sl].T


def _route(h2, wq, keys, tm=256):
    k = PEER_TOPK
    return pl.pallas_call(
        functools.partial(_route_kernel, tm=tm),
        out_shape=[jax.ShapeDtypeStruct((N_TOK, PEER_PICKS), I32),
                   jax.ShapeDtypeStruct((N_TOK, PEER_PICKS), F32)],
        grid=(N_TOK // tm,),
        in_specs=[pl.BlockSpec((tm, D_MODEL), lambda i: (i, 0)),
                  pl.BlockSpec((D_MODEL, PEER_HEADS * PEER_QDIM), lambda i: (0, 0),
                               pipeline_mode=pl.Buffered(1)),
                  pl.BlockSpec((PEER_HEADS * 2, PEER_NKEYS, PEER_QDIM // 2), lambda i: (0, 0, 0))],
        out_specs=[pl.BlockSpec((tm, PEER_PICKS), lambda i: (i, 0)),
                   pl.BlockSpec((tm, PEER_PICKS), lambda i: (i, 0))],
        scratch_shapes=[
            pltpu.VMEM((tm, PEER_HEADS * PEER_QDIM), BF16),
            pltpu.VMEM((PEER_NKEYS, tm), F32),
            pltpu.VMEM((k * k, tm), F32),
            pltpu.VMEM((k * k, tm), I32),
            pltpu.VMEM((2, k, tm), F32),
            pltpu.VMEM((2, k, tm), I32),
            pltpu.VMEM((k, tm), F32),
            pltpu.VMEM((k, tm), I32),
            pltpu.VMEM((PEER_PICKS, tm), I32),
            pltpu.VMEM((PEER_PICKS, tm), F32),
        ],
        compiler_params=_params(("parallel",), 32),
        name="peer_route",
    )(h2, wq, keys)


def _peer_up_kernel(h_ref, u_ref, e_ref, act_ref, *, te):
    j = pl.program_id(1)

    @pl.when(j == 0)
    def _():
        act_ref[...] = jnp.zeros_like(act_ref)

    dense = lax.dot_general(h_ref[...], u_ref[...], (((1,), (1,)), ((), ())),
                            preferred_element_type=F32)
    e = e_ref[...]
    row = e >> 7
    col = e & (PEER_NKEYS - 1)
    acc = act_ref[...]
    for q in range(te // LANES):
        got = jnp.take_along_axis(dense[:, q * LANES:(q + 1) * LANES], col, axis=1)
        acc = jnp.where(row == j * (te // LANES) + q, got, acc)
    act_ref[...] = acc


def _peer_up(h2, u, e, tb=512, te=512):
    return pl.pallas_call(
        functools.partial(_peer_up_kernel, te=te),
        out_shape=jax.ShapeDtypeStruct((N_TOK, PEER_PICKS), F32),
        grid=(N_TOK // tb, PEER_EXPERTS // te),
        in_specs=[pl.BlockSpec((tb, D_MODEL), lambda i, j: (i, 0)),
                  pl.BlockSpec((te, D_MODEL), lambda i, j: (j, 0)),
                  pl.BlockSpec((tb, PEER_PICKS), lambda i, j: (i, 0))],
        out_specs=pl.BlockSpec((tb, PEER_PICKS), lambda i, j: (i, 0)),
        compiler_params=_params(("parallel", "arbitrary"), 32),
        name="peer_up",
    )(h2, u, e)


def _peer_coef_kernel(e_ref, g_ref, act_ref, p_ref, row_scr, col_scr, w_scr, c_scr, *, tp):
    e = e_ref[...]
    row_scr[...] = e >> 7
    col_scr[...] = e & (PEER_NKEYS - 1)
    act = act_ref[...]
    w_scr[...] = g_ref[...] * (0.5 * act * (1.0 + lax.erf(act * math.sqrt(0.5))))
    iota = lax.broadcasted_iota(I32, (PEER_NKEYS, PEER_PICKS), 0)

    def body(t, carry):
        rows = row_scr[pl.ds(t, 1), :]
        cols = col_scr[pl.ds(t, 1), :]
        w = w_scr[pl.ds(t, 1), :]
        left = jnp.where(iota == rows, 1.0, 0.0).astype(BF16)
        right = jnp.where(iota == cols, w, 0.0).astype(BF16)
        c_scr[pl.ds(pl.multiple_of(t * PEER_NKEYS, PEER_NKEYS), PEER_NKEYS), :] = lax.dot_general(
            left, right, (((1,), (1,)), ((), ())), preferred_element_type=F32)
        return carry

    lax.fori_loop(0, tp, body, 0, unroll=8)
    for k1 in range(PEER_NKEYS):
        p_ref[:, k1 * PEER_NKEYS:(k1 + 1) * PEER_NKEYS] = (
            c_scr[pl.ds(k1, tp, stride=PEER_NKEYS), :].astype(p_ref.dtype))


def _peer_coef(e, g, act, tp=64):
    tok = pl.BlockSpec((tp, PEER_PICKS), lambda i: (i, 0))
    return pl.pallas_call(
        functools.partial(_peer_coef_kernel, tp=tp),
        out_shape=jax.ShapeDtypeStruct((N_TOK, PEER_EXPERTS), BF16),
        grid=(N_TOK // tp,),
        in_specs=[tok, tok, tok],
        out_specs=pl.BlockSpec((tp, PEER_EXPERTS), lambda i: (i, 0)),
        scratch_shapes=[pltpu.VMEM((tp, PEER_PICKS), I32),
                        pltpu.VMEM((tp, PEER_PICKS), I32),
                        pltpu.VMEM((tp, PEER_PICKS), F32),
                        pltpu.VMEM((tp * PEER_NKEYS, PEER_NKEYS), F32)],
        compiler_params=_params(("parallel",), 32),
        name="peer_coef",
    )(e, g, act)


def _peer_down_kernel(p_ref, v_ref, x_ref, g_ref, o_ref):
    j = pl.program_id(1)

    @pl.when(j == 0)
    def _():
        o_ref[...] = x_ref[...]

    o_ref[...] += jnp.dot(p_ref[...], v_ref[...], preferred_element_type=F32)

    @pl.when(j == pl.num_programs(1) - 1)
    def _():
        o_ref[...] = _rms(o_ref[...], g_ref[...])


def _peer_down(coef, v, x1, g_final, tm=512, tk=512):
    return pl.pallas_call(
        _peer_down_kernel,
        out_shape=jax.ShapeDtypeStruct((N_TOK, D_MODEL), F32),
        grid=(N_TOK // tm, PEER_EXPERTS // tk),
        in_specs=[pl.BlockSpec((tm, tk), lambda i, j: (i, j)),
                  pl.BlockSpec((tk, D_MODEL), lambda i, j: (j, 0)),
                  pl.BlockSpec((tm, D_MODEL), lambda i, j: (i, 0)),
                  pl.BlockSpec((1, D_MODEL), lambda i, j: (0, 0))],
        out_specs=pl.BlockSpec((tm, D_MODEL), lambda i, j: (i, 0)),
        compiler_params=_params(("parallel", "arbitrary"), 40),
        name="peer_down",
    )(coef, v, x1, g_final)


def kernel(x, rel_bias, norm_mix_g, w_in, sink_a, w_oa, w_ob, w_out, norm_ffn_g,
           peer_wq, peer_keys, peer_u, peer_v, norm_final_g):
    assert x.shape == (BATCH, SEQ, D_MODEL) and w_in.shape[0] == 1
    x2 = x.reshape(N_TOK, D_MODEL)
    proj = _inproj(x2, norm_mix_g, w_in[0].astype(BF16))

    (oa,) = _attention(proj, rel_bias, sink_a[0], dil=1, hw=A_HALF_WINDOW,
                       q_cb=QA_CB, k_cb=KA_CB, v_cb=VA_CB, n_heads=A_Q_HEADS, kv_group=A_GROUP,
                       head_base=0, has_sink=True, with_lse=False)
    obs, lses = [], []
    for gi, dil in enumerate(B_DILATIONS):
        off = gi * B_HEADS_PER_GROUP
        o, lse = _attention(proj, rel_bias, sink_a[0], dil=dil, hw=B_HALF_SPAN,
                            q_cb=QB_CB + off, k_cb=KB_CB + off, v_cb=VB_CB + off,
                            n_heads=B_HEADS_PER_GROUP, kv_group=1,
                            head_base=A_Q_HEADS + off, has_sink=False, with_lse=True)
        obs.append(o)
        lses.append(lse)

    x1, h2 = _outproj(oa, obs, lses, proj, x2, w_oa[0].astype(BF16), w_ob[0].astype(BF16),
                      w_out[0].astype(BF16), norm_ffn_g)

    keys = peer_keys[0].reshape(PEER_HEADS * 2, PEER_NKEYS, PEER_QDIM // 2).astype(BF16)
    e, gate = _route(h2, peer_wq[0].astype(BF16), keys)
    act = _peer_up(h2, peer_u[0].astype(BF16), e)
    coef = _peer_coef(e, gate, act)
    out = _peer_down(coef, peer_v[0].astype(BF16), x1, norm_final_g.reshape(1, D_MODEL))
    return out.reshape(BATCH, SEQ, D_MODEL)
```

```python
import functools
import math

import numpy as np
import jax
import jax.numpy as jnp
from jax import lax
from jax.experimental import pallas as pl
from jax.experimental.pallas import tpu as pltpu

F32 = jnp.float32
BF16 = jnp.bfloat16
I32 = jnp.int32

D_MODEL = 2048
BATCH = 2
SEQ = 4096
N_TOK = BATCH * SEQ
HEAD_DIM = 128
LANES = 128

A_Q_HEADS = 8
A_KV_HEADS = 2
A_GROUP = A_Q_HEADS // A_KV_HEADS
A_HALF_WINDOW = 128
B_DILATIONS = (1, 4, 16)
B_GROUPS = 3
B_HEADS_PER_GROUP = 4
B_HALF_SPAN = 64
N_BUCKETS = 32
MAX_DISTANCE = 1024
N_ATTN_HEADS = A_Q_HEADS + B_GROUPS * B_HEADS_PER_GROUP

PEER_HEADS = 8
PEER_NKEYS = 128
PEER_EXPERTS = PEER_NKEYS * PEER_NKEYS
PEER_QDIM = 256
PEER_TOPK = 16
PEER_PICKS = PEER_HEADS * PEER_TOPK
EPS = 1e-6

A_Q_W = A_Q_HEADS * HEAD_DIM
A_KV_W = A_KV_HEADS * HEAD_DIM
B_W = B_GROUPS * B_HEADS_PER_GROUP * HEAD_DIM
B_OUT_W = B_HEADS_PER_GROUP * HEAD_DIM
IN_WIDTH = A_Q_W + 2 * A_KV_W + 3 * B_W + 2 * D_MODEL
QA_CB = 0
KA_CB = A_Q_W // LANES
VA_CB = (A_Q_W + A_KV_W) // LANES
QB_CB = (A_Q_W + 2 * A_KV_W) // LANES
KB_CB = QB_CB + B_W // LANES
VB_CB = KB_CB + B_W // LANES
GA_OFF = A_Q_W + 2 * A_KV_W + 3 * B_W
GB_OFF = GA_OFF + D_MODEL
IN_CB = IN_WIDTH // LANES

NEG = -1e30
ATT_SCALE = HEAD_DIM ** -0.5
ATT_QT = 128

MIB = 1024 * 1024


def _params(sem, vmem_mib):
    return pltpu.CompilerParams(dimension_semantics=sem, vmem_limit_bytes=vmem_mib * MIB)


def _rms(x, g):
    return x * lax.rsqrt(jnp.mean(x * x, axis=-1, keepdims=True) + EPS) * g


def _inproj_kernel(x_ref, g_ref, w_ref, o_ref, h_scr):
    @pl.when(pl.program_id(1) == 0)
    def _():
        h_scr[...] = _rms(x_ref[...], g_ref[...]).astype(BF16)

    o_ref[...] = jnp.dot(h_scr[...], w_ref[...], preferred_element_type=F32).astype(o_ref.dtype)


def _inproj(x2, g, w, tm=512, tn=1024):
    n = w.shape[1]
    return pl.pallas_call(
        _inproj_kernel,
        out_shape=jax.ShapeDtypeStruct((N_TOK, n), BF16),
        grid=(N_TOK // tm, n // tn),
        in_specs=[
            pl.BlockSpec((tm, D_MODEL), lambda i, j: (i, 0)),
            pl.BlockSpec((1, D_MODEL), lambda i, j: (0, 0)),
            pl.BlockSpec((D_MODEL, tn), lambda i, j: (0, j)),
        ],
        out_specs=pl.BlockSpec((tm, tn), lambda i, j: (i, j)),
        scratch_shapes=[pltpu.VMEM((tm, D_MODEL), BF16)],
        compiler_params=_params(("parallel", "arbitrary"), 40),
        name="inproj",
    )(x2, g, w)


def _t5_bucket_np(rel):
    half = N_BUCKETS // 2
    max_exact = half // 2
    ret = np.where(rel > 0, half, 0)
    n = np.abs(rel)
    nf = np.maximum(n, 1).astype(np.float64)
    large = max_exact + (np.log(nf / max_exact) / math.log(MAX_DISTANCE / max_exact) * (half - max_exact)).astype(np.int64)
    large = np.minimum(large, half - 1)
    return (ret + np.where(n < max_exact, n, large)).astype(np.int32)


def _bucket_matrix(hw, dist_scale):
    w = ATT_QT + 2 * hw
    delta = (np.arange(w)[None, :] - hw) - np.arange(ATT_QT)[:, None]
    bkt = _t5_bucket_np(delta * dist_scale)
    return np.where(np.abs(delta) <= hw, bkt, -1).astype(np.int32)


def _attn_kernel(tab_ref, sink_ref, q_ref, k_ref, v_ref, bkt_ref, *rest,
                 dil, hw, head_base, has_sink, with_lse):
    n_out = 2 if with_lse else 1
    outs, scr = rest[:n_out], rest[n_out:]
    o_ref = outs[0]
    kpad, vpad, bias_scr = scr[:3]
    if dil > 1:
        q32, k32, v32, o32, qres = scr[3:8]
        if with_lse:
            l32 = scr[8]
    seq = SEQ // dil
    qt = ATT_QT
    win = qt + 2 * hw
    head = head_base + pl.program_id(1)

    bkt = bkt_ref[...]
    bias = jnp.full((qt, win), NEG, F32)
    for b in range(N_BUCKETS):
        bias = jnp.where(bkt == b, tab_ref[b, head], bias)
    bias_scr[...] = bias

    zeros = jnp.zeros((hw, HEAD_DIM), BF16)
    for pad_ref in (kpad, vpad):
        pad_ref[0:hw, :] = zeros
        pad_ref[hw + seq:hw + seq + hw, :] = zeros
    if dil == 1:
        kpad[hw:hw + seq, :] = k_ref[0]
        vpad[hw:hw + seq, :] = v_ref[0]
    else:
        q32[...] = q_ref[0].astype(F32)
        k32[...] = k_ref[0].astype(F32)
        v32[...] = v_ref[0].astype(F32)

    def residue(r, carry):
        if dil > 1:
            kpad[hw:hw + seq, :] = k32[pl.ds(r, seq, stride=dil), :].astype(BF16)
            vpad[hw:hw + seq, :] = v32[pl.ds(r, seq, stride=dil), :].astype(BF16)
            qres[...] = q32[pl.ds(r, seq, stride=dil), :].astype(BF16)

        def body(t, c):
            q0 = pl.multiple_of(t * qt, qt)
            if dil > 1:
                q = qres[pl.ds(q0, qt), :]
            else:
                q = q_ref[0, pl.ds(q0, qt), :]
            kw = kpad[pl.ds(q0, win), :]
            vw = vpad[pl.ds(q0, win), :]
            s = lax.dot_general(q, kw, (((1,), (1,)), ((), ())), preferred_element_type=F32)
            s = s * ATT_SCALE + bias_scr[...]
            kpos = q0 - hw + lax.broadcasted_iota(I32, (qt, win), 1)
            s = jnp.where((kpos >= 0) & (kpos < seq), s, NEG)
            m = jnp.max(s, axis=-1, keepdims=True)
            if has_sink:
                sk = sink_ref[head]
                m = jnp.maximum(m, sk)
            p = jnp.exp(s - m)
            den = jnp.sum(p, axis=-1, keepdims=True)
            if has_sink:
                den = den + jnp.exp(sk - m)
            o = jnp.dot(p.astype(BF16), vw, preferred_element_type=F32) / den
            if with_lse:
                lse = jnp.broadcast_to(m + jnp.log(den), (qt, HEAD_DIM))
            if dil > 1:
                o32[pl.ds(r + q0 * dil, qt, stride=dil), :] = o
                if with_lse:
                    l32[pl.ds(r + q0 * dil, qt, stride=dil), :] = lse
            else:
                o_ref[0, pl.ds(q0, qt), :] = o.astype(o_ref.dtype)
                if with_lse:
                    outs[1][0, pl.ds(q0, qt), :] = lse
            return c

        lax.fori_loop(0, seq // qt, body, 0)
        return carry

    if dil > 1:
        lax.fori_loop(0, dil, residue, 0)
        o_ref[0] = o32[...].astype(o_ref.dtype)
        if with_lse:
            outs[1][0] = l32[...]
    else:
        residue(0, 0)


def _attention(proj, rel_bias, sink, *, dil, hw, q_cb, k_cb, v_cb, n_heads, kv_group,
               head_base, has_sink, with_lse):
    seq = SEQ // dil
    pv = proj.reshape(BATCH, SEQ, IN_WIDTH)
    bkt = jnp.asarray(_bucket_matrix(hw, dil))
    win = ATT_QT + 2 * hw
    out_w = n_heads * HEAD_DIM
    blk = (1, SEQ, HEAD_DIM)
    o_spec = pl.BlockSpec(blk, lambda b, h: (b, 0, h))
    out_shape = [jax.ShapeDtypeStruct((BATCH, SEQ, out_w), BF16)]
    out_specs = [o_spec]
    if with_lse:
        out_shape.append(jax.ShapeDtypeStruct((BATCH, SEQ, out_w), F32))
        out_specs.append(o_spec)
    scratch = [pltpu.VMEM((seq + 2 * hw, HEAD_DIM), BF16),
               pltpu.VMEM((seq + 2 * hw, HEAD_DIM), BF16),
               pltpu.VMEM((ATT_QT, win), F32)]
    if dil > 1:
        scratch += [pltpu.VMEM((SEQ, HEAD_DIM), F32)] * 4 + [pltpu.VMEM((seq, HEAD_DIM), BF16)]
        if with_lse:
            scratch.append(pltpu.VMEM((SEQ, HEAD_DIM), F32))
    kern = functools.partial(_attn_kernel, dil=dil, hw=hw, head_base=head_base,
                             has_sink=has_sink, with_lse=with_lse)
    outs = pl.pallas_call(
        kern,
        out_shape=out_shape,
        grid=(BATCH, n_heads),
        in_specs=[
            pl.BlockSpec(memory_space=pltpu.SMEM),
            pl.BlockSpec(memory_space=pltpu.SMEM),
            pl.BlockSpec(blk, lambda b, h: (b, 0, q_cb + h)),
            pl.BlockSpec(blk, lambda b, h: (b, 0, k_cb + h // kv_group)),
            pl.BlockSpec(blk, lambda b, h: (b, 0, v_cb + h // kv_group)),
            pl.BlockSpec((ATT_QT, win), lambda b, h: (0, 0)),
        ],
        out_specs=out_specs,
        scratch_shapes=scratch,
        compiler_params=_params(("parallel", "arbitrary"), 40),
        name=f"attn_d{dil}_h{head_base}",
    )(rel_bias, sink, pv, pv, pv, bkt)
    return [o.reshape(N_TOK, out_w) for o in outs]


def _outproj_kernel(oa_ref, o0_ref, o1_ref, o2_ref, l0_ref, l1_ref, l2_ref, ga_ref, gb_ref,
                    x_ref, woa_ref, wob_ref, wout_ref, gn_ref, x1_ref, h2_ref):
    l0, l1, l2 = l0_ref[...], l1_ref[...], l2_ref[...]
    mx = jnp.maximum(jnp.maximum(l0, l1), l2)
    e0, e1, e2 = jnp.exp(l0 - mx), jnp.exp(l1 - mx), jnp.exp(l2 - mx)
    den = e0 + e1 + e2
    ob = ((e0 / den) * o0_ref[...].astype(F32) + (e1 / den) * o1_ref[...].astype(F32)
          + (e2 / den) * o2_ref[...].astype(F32))
    ya = jnp.dot(oa_ref[...], woa_ref[...], preferred_element_type=F32)
    yb = jnp.dot(ob.astype(BF16), wob_ref[...], preferred_element_type=F32)
    merged = (jax.nn.sigmoid(ga_ref[...].astype(F32)) * ya
              + jax.nn.sigmoid(gb_ref[...].astype(F32)) * yb)
    x1 = x_ref[...] + jnp.dot(merged.astype(BF16), wout_ref[...], preferred_element_type=F32)
    x1_ref[...] = x1
    h2_ref[...] = _rms(x1, gn_ref[...]).astype(BF16)


def _outproj(oa, obs, lses, proj, x2, w_oa, w_ob, w_out, g_ffn, tm=256):
    row = lambda w: pl.BlockSpec((tm, w), lambda i: (i, 0))
    const = lambda shape: pl.BlockSpec(shape, lambda i: (0, 0), pipeline_mode=pl.Buffered(1))
    return pl.pallas_call(
        _outproj_kernel,
        out_shape=[jax.ShapeDtypeStruct((N_TOK, D_MODEL), F32),
                   jax.ShapeDtypeStruct((N_TOK, D_MODEL), BF16)],
        grid=(N_TOK // tm,),
        in_specs=[row(A_Q_W), row(B_OUT_W), row(B_OUT_W), row(B_OUT_W),
                  row(B_OUT_W), row(B_OUT_W), row(B_OUT_W),
                  pl.BlockSpec((tm, D_MODEL), lambda i: (i, GA_OFF // D_MODEL)),
                  pl.BlockSpec((tm, D_MODEL), lambda i: (i, GB_OFF // D_MODEL)),
                  row(D_MODEL),
                  const((A_Q_W, D_MODEL)), const((B_OUT_W, D_MODEL)), const((D_MODEL, D_MODEL)),
                  const((1, D_MODEL))],
        out_specs=[row(D_MODEL), row(D_MODEL)],
        compiler_params=_params(("parallel",), 48),
        name="outproj",
    )(oa, *obs, *lses, proj, proj, x2, w_oa, w_ob, w_out, g_ffn)


def _cand_flat_table():
    k = PEER_TOPK
    rows = list(range(k))
    for k1 in range(1, 8):
        rows += [k1 * k + k2 if (k1 + 1) * (k2 + 1) <= k else -1 for k2 in range(8)]
    rows += [k1 * k for k1 in range(8, k)]
    return np.broadcast_to(np.asarray(rows, np.float32)[:, None], (len(rows), LANES)).copy()


N_CAND_ROWS = 80


def _route_kernel(h_ref, wq_ref, keys_ref, flat_ref, e_ref, g_ref,
                  q_scr, s_scr, ts_scr, ti_scr, bs_scr, et_scr, gt_scr, *, tm):
    k = PEER_TOPK
    q_scr[...] = jnp.dot(h_ref[...], wq_ref[...], preferred_element_type=F32).astype(BF16)
    key_iota = lax.broadcasted_iota(I32, (PEER_NKEYS, LANES), 0).astype(F32)
    halves = [slice(i * LANES, (i + 1) * LANES) for i in range(tm // LANES)]

    def level1(hc, carry):
        col = pl.multiple_of(hc * LANES, LANES)
        s_scr[...] = lax.dot_general(keys_ref[hc], q_scr[:, pl.ds(col, LANES)],
                                     (((1,), (1,)), ((), ())), preferred_element_type=F32)
        for lanes in halves:
            s = s_scr[:, lanes]
            for kk in range(k):
                m = jnp.max(s, axis=0, keepdims=True)
                idx = jnp.min(jnp.where(s == m, key_iota, float(PEER_NKEYS)), axis=0, keepdims=True)
                ts_scr[hc, kk:kk + 1, lanes] = m
                ti_scr[hc, kk:kk + 1, lanes] = idx
                s = jnp.where(key_iota == idx, -jnp.inf, s)
        return carry

    lax.fori_loop(0, 2 * PEER_HEADS, level1, 0)

    flat = flat_ref[...]

    def level2(h, carry):
        a, b = 2 * h, 2 * h + 1
        for lanes in halves:
            s2 = ts_scr[b, :, lanes]
            i2 = ti_scr[b, :, lanes]
            cs = [ts_scr[a, 0:1, lanes] + s2]
            ci = [ti_scr[a, 0:1, lanes] * float(PEER_NKEYS) + i2]
            for k1 in range(1, 8):
                cs.append(ts_scr[a, k1:k1 + 1, lanes] + s2[0:8])
                ci.append(ti_scr[a, k1:k1 + 1, lanes] * float(PEER_NKEYS) + i2[0:8])
            cs.append(ts_scr[a, 8:k, lanes] + ts_scr[b, 0:1, lanes])
            ci.append(ti_scr[a, 8:k, lanes] * float(PEER_NKEYS) + ti_scr[b, 0:1, lanes])
            cand = jnp.where(flat >= 0, jnp.concatenate(cs, axis=0), -jnp.inf)
            cidx = jnp.concatenate(ci, axis=0)
            for kk in range(k):
                m = jnp.max(cand, axis=0, keepdims=True)
                idx = jnp.min(jnp.where(cand == m, flat, float(k * k)), axis=0, keepdims=True)
                hit = flat == idx
                bs_scr[kk:kk + 1, lanes] = m
                et_scr[h, kk:kk + 1, lanes] = jnp.max(jnp.where(hit, cidx, -1.0), axis=0, keepdims=True)
                cand = jnp.where(hit, -jnp.inf, cand)
            bs = bs_scr[:, lanes]
            ex = jnp.exp(bs - jnp.max(bs, axis=0, keepdims=True))
            gt_scr[h, :, lanes] = ex / jnp.sum(ex, axis=0, keepdims=True)
        return carry

    lax.fori_loop(0, PEER_HEADS, level2, 0)

    for lanes in halves:
        et = jnp.concatenate([et_scr[h, :, lanes] for h in range(PEER_HEADS)], axis=0)
        gt = jnp.concatenate([gt_scr[h, :, lanes] for h in range(PEER_HEADS)], axis=0)
        e_ref[lanes, :] = et.T.astype(I32)
        g_ref[lanes, :] = gt.T


def _route(h2, wq, keys, tm=256):
    k = PEER_TOPK
    return pl.pallas_call(
        functools.partial(_route_kernel, tm=tm),
        out_shape=[jax.ShapeDtypeStruct((N_TOK, PEER_PICKS), I32),
                   jax.ShapeDtypeStruct((N_TOK, PEER_PICKS), F32)],
        grid=(N_TOK // tm,),
        in_specs=[pl.BlockSpec((tm, D_MODEL), lambda i: (i, 0)),
                  pl.BlockSpec((D_MODEL, PEER_HEADS * PEER_QDIM), lambda i: (0, 0),
                               pipeline_mode=pl.Buffered(1)),
                  pl.BlockSpec((PEER_HEADS * 2, PEER_NKEYS, PEER_QDIM // 2), lambda i: (0, 0, 0)),
                  pl.BlockSpec((N_CAND_ROWS, LANES), lambda i: (0, 0))],
        out_specs=[pl.BlockSpec((tm, PEER_PICKS), lambda i: (i, 0)),
                   pl.BlockSpec((tm, PEER_PICKS), lambda i: (i, 0))],
        scratch_shapes=[
            pltpu.VMEM((tm, PEER_HEADS * PEER_QDIM), BF16),
            pltpu.VMEM((PEER_NKEYS, tm), F32),
            pltpu.VMEM((2 * PEER_HEADS, k, tm), F32),
            pltpu.VMEM((2 * PEER_HEADS, k, tm), F32),
            pltpu.VMEM((k, tm), F32),
            pltpu.VMEM((PEER_HEADS, k, tm), F32),
            pltpu.VMEM((PEER_HEADS, k, tm), F32),
        ],
        compiler_params=_params(("parallel",), 32),
        name="peer_route",
    )(h2, wq, keys, jnp.asarray(_cand_flat_table()))


UP_CHUNK = 256


def _peer_up_kernel(h_ref, u_ref, e_ref, act_ref, *, te):
    j = pl.program_id(1)

    @pl.when(j == 0)
    def _():
        act_ref[...] = jnp.zeros_like(act_ref)

    e = e_ref[...]
    row = e >> 7
    col = e & (PEER_NKEYS - 1)
    acc = act_ref[...]
    h = h_ref[...]
    for c in range(te // UP_CHUNK):
        dense = lax.dot_general(h, u_ref[c * UP_CHUNK:(c + 1) * UP_CHUNK, :], (((1,), (1,)), ((), ())),
                                preferred_element_type=F32)
        for q in range(UP_CHUNK // LANES):
            got = jnp.take_along_axis(dense[:, q * LANES:(q + 1) * LANES], col, axis=1)
            acc = jnp.where(row == (j * te + c * UP_CHUNK) // LANES + q, got, acc)
    act_ref[...] = acc


def _peer_up(h2, u, e, tb=1024, te=1024):
    return pl.pallas_call(
        functools.partial(_peer_up_kernel, te=te),
        out_shape=jax.ShapeDtypeStruct((N_TOK, PEER_PICKS), F32),
        grid=(N_TOK // tb, PEER_EXPERTS // te),
        in_specs=[pl.BlockSpec((tb, D_MODEL), lambda i, j: (i, 0)),
                  pl.BlockSpec((te, D_MODEL), lambda i, j: (j, 0)),
                  pl.BlockSpec((tb, PEER_PICKS), lambda i, j: (i, 0))],
        out_specs=pl.BlockSpec((tb, PEER_PICKS), lambda i, j: (i, 0)),
        compiler_params=_params(("parallel", "arbitrary"), 44),
        name="peer_up",
    )(h2, u, e)


def _peer_coef_kernel(e_ref, g_ref, act_ref, p_ref, row_scr, col_scr, w_scr, c_scr, *, tp):
    e = e_ref[...]
    row_scr[...] = e >> 7
    col_scr[...] = e & (PEER_NKEYS - 1)
    act = act_ref[...]
    w_scr[...] = g_ref[...] * (0.5 * act * (1.0 + lax.erf(act * math.sqrt(0.5))))
    iota = lax.broadcasted_iota(I32, (PEER_NKEYS, PEER_PICKS), 0)

    def body(t, carry):
        rows = row_scr[pl.ds(t, 1), :]
        cols = col_scr[pl.ds(t, 1), :]
        w = w_scr[pl.ds(t, 1), :]
        left = jnp.where(iota == rows, 1.0, 0.0).astype(BF16)
        right = jnp.where(iota == cols, w, 0.0).astype(BF16)
        c_scr[pl.ds(pl.multiple_of(t * PEER_NKEYS, PEER_NKEYS), PEER_NKEYS), :] = lax.dot_general(
            left, right, (((1,), (1,)), ((), ())), preferred_element_type=F32)
        return carry

    lax.fori_loop(0, tp, body, 0, unroll=32)
    for k1 in range(PEER_NKEYS):
        p_ref[:, k1 * PEER_NKEYS:(k1 + 1) * PEER_NKEYS] = (
            c_scr[pl.ds(k1, tp, stride=PEER_NKEYS), :].astype(p_ref.dtype))


def _peer_coef(e, g, act, tp=64):
    tok = pl.BlockSpec((tp, PEER_PICKS), lambda i: (i, 0))
    return pl.pallas_call(
        functools.partial(_peer_coef_kernel, tp=tp),
        out_shape=jax.ShapeDtypeStruct((N_TOK, PEER_EXPERTS), BF16),
        grid=(N_TOK // tp,),
        in_specs=[tok, tok, tok],
        out_specs=pl.BlockSpec((tp, PEER_EXPERTS), lambda i: (i, 0)),
        scratch_shapes=[pltpu.VMEM((tp, PEER_PICKS), I32),
                        pltpu.VMEM((tp, PEER_PICKS), I32),
                        pltpu.VMEM((tp, PEER_PICKS), F32),
                        pltpu.VMEM((tp * PEER_NKEYS, PEER_NKEYS), F32)],
        compiler_params=_params(("parallel",), 32),
        name="peer_coef",
    )(e, g, act)


def _peer_down_kernel(p_ref, v_ref, x_ref, g_ref, o_ref):
    j = pl.program_id(1)

    @pl.when(j == 0)
    def _():
        o_ref[...] = x_ref[...]

    o_ref[...] += jnp.dot(p_ref[...], v_ref[...], preferred_element_type=F32)

    @pl.when(j == pl.num_programs(1) - 1)
    def _():
        o_ref[...] = _rms(o_ref[...], g_ref[...])


def _peer_down(coef, v, x1, g_final, tm=1024, tk=1024):
    return pl.pallas_call(
        _peer_down_kernel,
        out_shape=jax.ShapeDtypeStruct((N_TOK, D_MODEL), F32),
        grid=(N_TOK // tm, PEER_EXPERTS // tk),
        in_specs=[pl.BlockSpec((tm, tk), lambda i, j: (i, j)),
                  pl.BlockSpec((tk, D_MODEL), lambda i, j: (j, 0)),
                  pl.BlockSpec((tm, D_MODEL), lambda i, j: (i, 0), pipeline_mode=pl.Buffered(1)),
                  pl.BlockSpec((1, D_MODEL), lambda i, j: (0, 0))],
        out_specs=pl.BlockSpec((tm, D_MODEL), lambda i, j: (i, 0)),
        compiler_params=_params(("parallel", "arbitrary"), 48),
        name="peer_down",
    )(coef, v, x1, g_final)


def kernel(x, rel_bias, norm_mix_g, w_in, sink_a, w_oa, w_ob, w_out, norm_ffn_g,
           peer_wq, peer_keys, peer_u, peer_v, norm_final_g):
    assert x.shape == (BATCH, SEQ, D_MODEL) and w_in.shape[0] == 1
    x2 = x.reshape(N_TOK, D_MODEL)
    proj = _inproj(x2, norm_mix_g, w_in[0].astype(BF16))

    (oa,) = _attention(proj, rel_bias, sink_a[0], dil=1, hw=A_HALF_WINDOW,
                       q_cb=QA_CB, k_cb=KA_CB, v_cb=VA_CB, n_heads=A_Q_HEADS, kv_group=A_GROUP,
                       head_base=0, has_sink=True, with_lse=False)
    obs, lses = [], []
    for gi, dil in enumerate(B_DILATIONS):
        off = gi * B_HEADS_PER_GROUP
        o, lse = _attention(proj, rel_bias, sink_a[0], dil=dil, hw=B_HALF_SPAN,
                            q_cb=QB_CB + off, k_cb=KB_CB + off, v_cb=VB_CB + off,
                            n_heads=B_HEADS_PER_GROUP, kv_group=1,
                            head_base=A_Q_HEADS + off, has_sink=False, with_lse=True)
        obs.append(o)
        lses.append(lse)

    x1, h2 = _outproj(oa, obs, lses, proj, x2, w_oa[0].astype(BF16), w_ob[0].astype(BF16),
                      w_out[0].astype(BF16), norm_ffn_g)

    keys = peer_keys[0].reshape(PEER_HEADS * 2, PEER_NKEYS, PEER_QDIM // 2).astype(BF16)
    e, gate = _route(h2, peer_wq[0].astype(BF16), keys)
    act = _peer_up(h2, peer_u[0].astype(BF16), e)
    coef = _peer_coef(e, gate, act)
    out = _peer_down(coef, peer_v[0].astype(BF16), x1, norm_final_g.reshape(1, D_MODEL))
    return out.reshape(BATCH, SEQ, D_MODEL)
```

```python
import functools
import math

import numpy as np
import jax
import jax.numpy as jnp
from jax import lax
from jax.experimental import pallas as pl
from jax.experimental.pallas import tpu as pltpu

F32 = jnp.float32
BF16 = jnp.bfloat16
I32 = jnp.int32

D_MODEL = 2048
BATCH = 2
SEQ = 4096
N_TOK = BATCH * SEQ
HEAD_DIM = 128
LANES = 128

A_Q_HEADS = 8
A_KV_HEADS = 2
A_GROUP = A_Q_HEADS // A_KV_HEADS
A_HALF_WINDOW = 128
B_DILATIONS = (1, 4, 16)
B_GROUPS = 3
B_HEADS_PER_GROUP = 4
B_HALF_SPAN = 64
N_BUCKETS = 32
MAX_DISTANCE = 1024
N_ATTN_HEADS = A_Q_HEADS + B_GROUPS * B_HEADS_PER_GROUP

PEER_HEADS = 8
PEER_NKEYS = 128
PEER_EXPERTS = PEER_NKEYS * PEER_NKEYS
PEER_QDIM = 256
PEER_TOPK = 16
PEER_PICKS = PEER_HEADS * PEER_TOPK
EPS = 1e-6

A_Q_W = A_Q_HEADS * HEAD_DIM
A_KV_W = A_KV_HEADS * HEAD_DIM
B_W = B_GROUPS * B_HEADS_PER_GROUP * HEAD_DIM
B_OUT_W = B_HEADS_PER_GROUP * HEAD_DIM
IN_WIDTH = A_Q_W + 2 * A_KV_W + 3 * B_W + 2 * D_MODEL
QA_CB = 0
KA_CB = A_Q_W // LANES
VA_CB = (A_Q_W + A_KV_W) // LANES
QB_CB = (A_Q_W + 2 * A_KV_W) // LANES
KB_CB = QB_CB + B_W // LANES
VB_CB = KB_CB + B_W // LANES
GA_OFF = A_Q_W + 2 * A_KV_W + 3 * B_W
GB_OFF = GA_OFF + D_MODEL
IN_CB = IN_WIDTH // LANES

NEG = -1e30
ATT_SCALE = HEAD_DIM ** -0.5
ATT_QT = 128

MIB = 1024 * 1024


def _params(sem, vmem_mib):
    return pltpu.CompilerParams(dimension_semantics=sem, vmem_limit_bytes=vmem_mib * MIB)


def _rms(x, g):
    return x * lax.rsqrt(jnp.mean(x * x, axis=-1, keepdims=True) + EPS) * g


def _inproj_kernel(x_ref, g_ref, w_ref, o_ref, h_scr):
    @pl.when(pl.program_id(1) == 0)
    def _():
        h_scr[...] = _rms(x_ref[...], g_ref[...]).astype(BF16)

    o_ref[...] = jnp.dot(h_scr[...], w_ref[...], preferred_element_type=F32).astype(o_ref.dtype)


def _inproj(x2, g, w, tm=512, tn=1024):
    n = w.shape[1]
    return pl.pallas_call(
        _inproj_kernel,
        out_shape=jax.ShapeDtypeStruct((N_TOK, n), BF16),
        grid=(N_TOK // tm, n // tn),
        in_specs=[
            pl.BlockSpec((tm, D_MODEL), lambda i, j: (i, 0)),
            pl.BlockSpec((1, D_MODEL), lambda i, j: (0, 0)),
            pl.BlockSpec((D_MODEL, tn), lambda i, j: (0, j)),
        ],
        out_specs=pl.BlockSpec((tm, tn), lambda i, j: (i, j)),
        scratch_shapes=[pltpu.VMEM((tm, D_MODEL), BF16)],
        compiler_params=_params(("parallel", "arbitrary"), 40),
        name="inproj",
    )(x2, g, w)


def _t5_bucket_np(rel):
    half = N_BUCKETS // 2
    max_exact = half // 2
    ret = np.where(rel > 0, half, 0)
    n = np.abs(rel)
    nf = np.maximum(n, 1).astype(np.float64)
    large = max_exact + (np.log(nf / max_exact) / math.log(MAX_DISTANCE / max_exact) * (half - max_exact)).astype(np.int64)
    large = np.minimum(large, half - 1)
    return (ret + np.where(n < max_exact, n, large)).astype(np.int32)


def _bucket_matrix(hw, dist_scale):
    w = ATT_QT + 2 * hw
    delta = (np.arange(w)[None, :] - hw) - np.arange(ATT_QT)[:, None]
    bkt = _t5_bucket_np(delta * dist_scale)
    return np.where(np.abs(delta) <= hw, bkt, -1).astype(np.int32)


def _attn_kernel(tab_ref, sink_ref, q_ref, k_ref, v_ref, bkt_ref, *rest,
                 dil, hw, head_base, has_sink, with_lse):
    n_out = 2 if with_lse else 1
    outs, scr = rest[:n_out], rest[n_out:]
    o_ref = outs[0]
    kpad, vpad, bias_scr = scr[:3]
    if dil > 1:
        q32, k32, v32, o32, qres = scr[3:8]
        if with_lse:
            l32 = scr[8]
    seq = SEQ // dil
    qt = ATT_QT
    win = qt + 2 * hw
    head = head_base + pl.program_id(1)

    bkt = bkt_ref[...]
    bias = jnp.full((qt, win), NEG, F32)
    for b in range(N_BUCKETS):
        bias = jnp.where(bkt == b, tab_ref[b, head], bias)
    bias_scr[...] = bias

    zeros = jnp.zeros((hw, HEAD_DIM), BF16)
    for pad_ref in (kpad, vpad):
        pad_ref[0:hw, :] = zeros
        pad_ref[hw + seq:hw + seq + hw, :] = zeros
    if dil == 1:
        kpad[hw:hw + seq, :] = k_ref[0]
        vpad[hw:hw + seq, :] = v_ref[0]
    else:
        q32[...] = q_ref[0].astype(F32)
        k32[...] = k_ref[0].astype(F32)
        v32[...] = v_ref[0].astype(F32)

    def residue(r, carry):
        if dil > 1:
            kpad[hw:hw + seq, :] = k32[pl.ds(r, seq, stride=dil), :].astype(BF16)
            vpad[hw:hw + seq, :] = v32[pl.ds(r, seq, stride=dil), :].astype(BF16)
            qres[...] = q32[pl.ds(r, seq, stride=dil), :].astype(BF16)

        def body(t, c):
            q0 = pl.multiple_of(t * qt, qt)
            if dil > 1:
                q = qres[pl.ds(q0, qt), :]
            else:
                q = q_ref[0, pl.ds(q0, qt), :]
            kw = kpad[pl.ds(q0, win), :]
            vw = vpad[pl.ds(q0, win), :]
            s = lax.dot_general(q, kw, (((1,), (1,)), ((), ())), preferred_element_type=F32)
            s = s * ATT_SCALE + bias_scr[...]
            kpos = q0 - hw + lax.broadcasted_iota(I32, (qt, win), 1)
            s = jnp.where((kpos >= 0) & (kpos < seq), s, NEG)
            m = jnp.max(s, axis=-1, keepdims=True)
            if has_sink:
                sk = sink_ref[head]
                m = jnp.maximum(m, sk)
            p = jnp.exp(s - m)
            den = jnp.sum(p, axis=-1, keepdims=True)
            if has_sink:
                den = den + jnp.exp(sk - m)
            o = jnp.dot(p.astype(BF16), vw, preferred_element_type=F32) / den
            if with_lse:
                lse = jnp.broadcast_to(m + jnp.log(den), (qt, HEAD_DIM))
            if dil > 1:
                o32[pl.ds(r + q0 * dil, qt, stride=dil), :] = o
                if with_lse:
                    l32[pl.ds(r + q0 * dil, qt, stride=dil), :] = lse
            else:
                o_ref[0, pl.ds(q0, qt), :] = o.astype(o_ref.dtype)
                if with_lse:
                    outs[1][0, pl.ds(q0, qt), :] = lse
            return c

        lax.fori_loop(0, seq // qt, body, 0)
        return carry

    if dil > 1:
        lax.fori_loop(0, dil, residue, 0)
        o_ref[0] = o32[...].astype(o_ref.dtype)
        if with_lse:
            outs[1][0] = l32[...]
    else:
        residue(0, 0)


def _attention(proj, rel_bias, sink, *, dil, hw, q_cb, k_cb, v_cb, n_heads, kv_group,
               head_base, has_sink, with_lse):
    seq = SEQ // dil
    pv = proj.reshape(BATCH, SEQ, IN_WIDTH)
    bkt = jnp.asarray(_bucket_matrix(hw, dil))
    win = ATT_QT + 2 * hw
    out_w = n_heads * HEAD_DIM
    blk = (1, SEQ, HEAD_DIM)
    o_spec = pl.BlockSpec(blk, lambda b, h: (b, 0, h))
    out_shape = [jax.ShapeDtypeStruct((BATCH, SEQ, out_w), BF16)]
    out_specs = [o_spec]
    if with_lse:
        out_shape.append(jax.ShapeDtypeStruct((BATCH, SEQ, out_w), F32))
        out_specs.append(o_spec)
    scratch = [pltpu.VMEM((seq + 2 * hw, HEAD_DIM), BF16),
               pltpu.VMEM((seq + 2 * hw, HEAD_DIM), BF16),
               pltpu.VMEM((ATT_QT, win), F32)]
    if dil > 1:
        scratch += [pltpu.VMEM((SEQ, HEAD_DIM), F32)] * 4 + [pltpu.VMEM((seq, HEAD_DIM), BF16)]
        if with_lse:
            scratch.append(pltpu.VMEM((SEQ, HEAD_DIM), F32))
    kern = functools.partial(_attn_kernel, dil=dil, hw=hw, head_base=head_base,
                             has_sink=has_sink, with_lse=with_lse)
    outs = pl.pallas_call(
        kern,
        out_shape=out_shape,
        grid=(BATCH, n_heads),
        in_specs=[
            pl.BlockSpec(memory_space=pltpu.SMEM),
            pl.BlockSpec(memory_space=pltpu.SMEM),
            pl.BlockSpec(blk, lambda b, h: (b, 0, q_cb + h)),
            pl.BlockSpec(blk, lambda b, h: (b, 0, k_cb + h // kv_group)),
            pl.BlockSpec(blk, lambda b, h: (b, 0, v_cb + h // kv_group)),
            pl.BlockSpec((ATT_QT, win), lambda b, h: (0, 0)),
        ],
        out_specs=out_specs,
        scratch_shapes=scratch,
        compiler_params=_params(("parallel", "arbitrary"), 40),
        name=f"attn_d{dil}_h{head_base}",
    )(rel_bias, sink, pv, pv, pv, bkt)
    return [o.reshape(N_TOK, out_w) for o in outs]


def _outproj_kernel(oa_ref, o0_ref, o1_ref, o2_ref, l0_ref, l1_ref, l2_ref, ga_ref, gb_ref,
                    x_ref, woa_ref, wob_ref, wout_ref, gn_ref, x1_ref, h2_ref):
    l0, l1, l2 = l0_ref[...], l1_ref[...], l2_ref[...]
    mx = jnp.maximum(jnp.maximum(l0, l1), l2)
    e0, e1, e2 = jnp.exp(l0 - mx), jnp.exp(l1 - mx), jnp.exp(l2 - mx)
    den = e0 + e1 + e2
    ob = ((e0 / den) * o0_ref[...].astype(F32) + (e1 / den) * o1_ref[...].astype(F32)
          + (e2 / den) * o2_ref[...].astype(F32))
    ya = jnp.dot(oa_ref[...], woa_ref[...], preferred_element_type=F32)
    yb = jnp.dot(ob.astype(BF16), wob_ref[...], preferred_element_type=F32)
    merged = (jax.nn.sigmoid(ga_ref[...].astype(F32)) * ya
              + jax.nn.sigmoid(gb_ref[...].astype(F32)) * yb)
    x1 = x_ref[...] + jnp.dot(merged.astype(BF16), wout_ref[...], preferred_element_type=F32)
    x1_ref[...] = x1
    h2_ref[...] = _rms(x1, gn_ref[...]).astype(BF16)


def _outproj(oa, obs, lses, proj, x2, w_oa, w_ob, w_out, g_ffn, tm=256):
    row = lambda w: pl.BlockSpec((tm, w), lambda i: (i, 0))
    const = lambda shape: pl.BlockSpec(shape, lambda i: (0, 0), pipeline_mode=pl.Buffered(1))
    return pl.pallas_call(
        _outproj_kernel,
        out_shape=[jax.ShapeDtypeStruct((N_TOK, D_MODEL), F32),
                   jax.ShapeDtypeStruct((N_TOK, D_MODEL), BF16)],
        grid=(N_TOK // tm,),
        in_specs=[row(A_Q_W), row(B_OUT_W), row(B_OUT_W), row(B_OUT_W),
                  row(B_OUT_W), row(B_OUT_W), row(B_OUT_W),
                  pl.BlockSpec((tm, D_MODEL), lambda i: (i, GA_OFF // D_MODEL)),
                  pl.BlockSpec((tm, D_MODEL), lambda i: (i, GB_OFF // D_MODEL)),
                  row(D_MODEL),
                  const((A_Q_W, D_MODEL)), const((B_OUT_W, D_MODEL)), const((D_MODEL, D_MODEL)),
                  const((1, D_MODEL))],
        out_specs=[row(D_MODEL), row(D_MODEL)],
        compiler_params=_params(("parallel",), 48),
        name="outproj",
    )(oa, *obs, *lses, proj, proj, x2, w_oa, w_ob, w_out, g_ffn)


SUBLANES = 8
_FAR = 1024.0


def _tree(op, xs):
    xs = list(xs)
    while len(xs) > 1:
        nxt = [op(xs[i], xs[i + 1]) for i in range(0, len(xs) - 1, 2)]
        if len(xs) % 2:
            nxt.append(xs[-1])
        xs = nxt
    return xs[0]


def _all_sublanes(op, x):
    for shift in (4, 2, 1):
        x = op(x, pltpu.roll(x, shift, axis=0))
    return x


def _pop_max(tiles, sub_iota, payload=None):
    m = _all_sublanes(jnp.maximum, _tree(jnp.maximum, tiles))
    first = _tree(jnp.minimum, [jnp.where(t == m, float(SUBLANES * v), _FAR) for v, t in enumerate(tiles)])
    row = _all_sublanes(jnp.minimum, first + sub_iota)
    base = row - sub_iota
    hits = [base == float(SUBLANES * v) for v in range(len(tiles))]
    if payload is None:
        val = row
    else:
        val = _all_sublanes(jnp.maximum, _tree(jnp.maximum, [jnp.where(h, p, -1.0)
                                                            for h, p in zip(hits, payload)]))
    return m, val, [jnp.where(h, -jnp.inf, t) for h, t in zip(hits, tiles)]


def _route_kernel(h_ref, wq_ref, keys_ref, e_ref, g_ref,
                  q_scr, s_scr, ts_scr, ti_scr, bs_scr, et_scr, gt_scr, *, tm):
    k = PEER_TOPK
    nk = float(PEER_NKEYS)
    q_scr[...] = jnp.dot(h_ref[...], wq_ref[...], preferred_element_type=F32).astype(BF16)
    sub_iota = lax.broadcasted_iota(I32, (SUBLANES, LANES), 0).astype(F32)
    halves = [slice(i * LANES, (i + 1) * LANES) for i in range(tm // LANES)]

    def level1(hc, carry):
        col = pl.multiple_of(hc * LANES, LANES)
        s_scr[...] = lax.dot_general(keys_ref[hc], q_scr[:, pl.ds(col, LANES)],
                                     (((1,), (1,)), ((), ())), preferred_element_type=F32)
        tiles = [[s_scr[SUBLANES * v:SUBLANES * (v + 1), lanes] for v in range(PEER_NKEYS // SUBLANES)]
                 for lanes in halves]
        for kk in range(k):
            for i, lanes in enumerate(halves):
                m, row, tiles[i] = _pop_max(tiles[i], sub_iota)
                ts_scr[hc, kk:kk + 1, lanes] = m[0:1]
                ti_scr[hc, kk:kk + 1, lanes] = row[0:1]
        return carry

    lax.fori_loop(0, 2 * PEER_HEADS, level1, 0)

    def level2(h, carry):
        a, b = 2 * h, 2 * h + 1

        def candidates(lanes):
            s2 = [ts_scr[b, 0:8, lanes], ts_scr[b, 8:16, lanes]]
            i2 = [ti_scr[b, 0:8, lanes], ti_scr[b, 8:16, lanes]]
            s1 = lambda k1: ts_scr[a, k1:k1 + 1, lanes]
            i1 = lambda k1: ti_scr[a, k1:k1 + 1, lanes] * nk
            tiles = [s1(0) + s2[0], s1(0) + s2[1]]
            pay = [i1(0) + i2[0], i1(0) + i2[1]]
            for k1 in range(1, 8):
                allowed = k // (k1 + 1)
                t = s1(k1) + s2[0]
                tiles.append(t if allowed >= SUBLANES else jnp.where(sub_iota < float(allowed), t, -jnp.inf))
                pay.append(i1(k1) + i2[0])
            tiles.append(ts_scr[a, 8:16, lanes] + ts_scr[b, 0:1, lanes])
            pay.append(ti_scr[a, 8:16, lanes] * nk + ti_scr[b, 0:1, lanes])
            return tiles, pay

        cands = [candidates(lanes) for lanes in halves]
        tiles = [c[0] for c in cands]
        for kk in range(k):
            for i, lanes in enumerate(halves):
                m, expert, tiles[i] = _pop_max(tiles[i], sub_iota, payload=cands[i][1])
                bs_scr[kk:kk + 1, lanes] = m[0:1]
                et_scr[h, kk:kk + 1, lanes] = expert[0:1]
        for lanes in halves:
            bs = bs_scr[:, lanes]
            ex = jnp.exp(bs - jnp.max(bs, axis=0, keepdims=True))
            gt_scr[h, :, lanes] = ex / jnp.sum(ex, axis=0, keepdims=True)
        return carry

    lax.fori_loop(0, PEER_HEADS, level2, 0)

    for lanes in halves:
        et = jnp.concatenate([et_scr[h, :, lanes] for h in range(PEER_HEADS)], axis=0)
        gt = jnp.concatenate([gt_scr[h, :, lanes] for h in range(PEER_HEADS)], axis=0)
        e_ref[lanes, :] = et.T.astype(I32)
        g_ref[lanes, :] = gt.T


def _route(h2, wq, keys, tm=256):
    k = PEER_TOPK
    return pl.pallas_call(
        functools.partial(_route_kernel, tm=tm),
        out_shape=[jax.ShapeDtypeStruct((N_TOK, PEER_PICKS), I32),
                   jax.ShapeDtypeStruct((N_TOK, PEER_PICKS), F32)],
        grid=(N_TOK // tm,),
        in_specs=[pl.BlockSpec((tm, D_MODEL), lambda i: (i, 0)),
                  pl.BlockSpec((D_MODEL, PEER_HEADS * PEER_QDIM), lambda i: (0, 0),
                               pipeline_mode=pl.Buffered(1)),
                  pl.BlockSpec((PEER_HEADS * 2, PEER_NKEYS, PEER_QDIM // 2), lambda i: (0, 0, 0))],
        out_specs=[pl.BlockSpec((tm, PEER_PICKS), lambda i: (i, 0)),
                   pl.BlockSpec((tm, PEER_PICKS), lambda i: (i, 0))],
        scratch_shapes=[
            pltpu.VMEM((tm, PEER_HEADS * PEER_QDIM), BF16),
            pltpu.VMEM((PEER_NKEYS, tm), F32),
            pltpu.VMEM((2 * PEER_HEADS, k, tm), F32),
            pltpu.VMEM((2 * PEER_HEADS, k, tm), F32),
            pltpu.VMEM((k, tm), F32),
            pltpu.VMEM((PEER_HEADS, k, tm), F32),
            pltpu.VMEM((PEER_HEADS, k, tm), F32),
        ],
        compiler_params=_params(("parallel",), 32),
        name="peer_route",
    )(h2, wq, keys)


UP_CHUNK = 256


def _peer_up_kernel(h_ref, u_ref, e_ref, act_ref, *, te):
    j = pl.program_id(1)

    @pl.when(j == 0)
    def _():
        act_ref[...] = jnp.zeros_like(act_ref)

    e = e_ref[...]
    row = e >> 7
    col = e & (PEER_NKEYS - 1)
    acc = act_ref[...]
    h = h_ref[...]
    for c in range(te // UP_CHUNK):
        dense = lax.dot_general(h, u_ref[c * UP_CHUNK:(c + 1) * UP_CHUNK, :], (((1,), (1,)), ((), ())),
                                preferred_element_type=F32)
        for q in range(UP_CHUNK // LANES):
            got = jnp.take_along_axis(dense[:, q * LANES:(q + 1) * LANES], col, axis=1)
            acc = jnp.where(row == (j * te + c * UP_CHUNK) // LANES + q, got, acc)
    act_ref[...] = acc


def _peer_up(h2, u, e, tb=1024, te=1024):
    return pl.pallas_call(
        functools.partial(_peer_up_kernel, te=te),
        out_shape=jax.ShapeDtypeStruct((N_TOK, PEER_PICKS), F32),
        grid=(N_TOK // tb, PEER_EXPERTS // te),
        in_specs=[pl.BlockSpec((tb, D_MODEL), lambda i, j: (i, 0)),
                  pl.BlockSpec((te, D_MODEL), lambda i, j: (j, 0)),
                  pl.BlockSpec((tb, PEER_PICKS), lambda i, j: (i, 0))],
        out_specs=pl.BlockSpec((tb, PEER_PICKS), lambda i, j: (i, 0)),
        compiler_params=_params(("parallel", "arbitrary"), 44),
        name="peer_up",
    )(h2, u, e)


def _peer_coef_kernel(e_ref, g_ref, act_ref, p_ref, row_scr, col_scr, w_scr, c_scr, *, tp):
    e = e_ref[...]
    row_scr[...] = (e >> 7).astype(F32)
    col_scr[...] = (e & (PEER_NKEYS - 1)).astype(F32)
    act = act_ref[...]
    w_scr[...] = g_ref[...] * (0.5 * act * (1.0 + lax.erf(act * math.sqrt(0.5))))
    sub = 16
    iota = lax.broadcasted_iota(I32, (PEER_NKEYS // sub, sub, PEER_PICKS), 0) * sub \
        + lax.broadcasted_iota(I32, (PEER_NKEYS // sub, sub, PEER_PICKS), 1)
    iota = iota.astype(F32).astype(BF16)
    one = jnp.ones((), BF16)
    zero = jnp.zeros((), BF16)

    def bcast(ref, t):
        return jnp.broadcast_to(ref[t:t + 1, :], (sub, PEER_PICKS)).astype(BF16)[None]

    def products(t0):
        for t in range(t0, t0 + sub):
            rows, cols, w = bcast(row_scr, t), bcast(col_scr, t), bcast(w_scr, t)
            left = jnp.where(iota == rows, one, zero).reshape(PEER_NKEYS, PEER_PICKS)
            right = jnp.where(iota == cols, w, zero).reshape(PEER_NKEYS, PEER_PICKS)
            c_scr[t] = lax.dot_general(left, right, (((1,), (1,)), ((), ())),
                                       preferred_element_type=F32)

    def regroup(t0):
        sw = jnp.swapaxes(c_scr[t0:t0 + sub], 0, 1)
        for k1 in range(PEER_NKEYS):
            p_ref[t0:t0 + sub, k1 * PEER_NKEYS:(k1 + 1) * PEER_NKEYS] = sw[k1].astype(p_ref.dtype)

    for t0 in range(0, tp, sub):
        products(t0)
        if t0:
            regroup(t0 - sub)
    regroup(tp - sub)


def _peer_coef(e, g, act, tp=64):
    tok = pl.BlockSpec((tp, PEER_PICKS), lambda i: (i, 0))
    return pl.pallas_call(
        functools.partial(_peer_coef_kernel, tp=tp),
        out_shape=jax.ShapeDtypeStruct((N_TOK, PEER_EXPERTS), BF16),
        grid=(N_TOK // tp,),
        in_specs=[tok, tok, tok],
        out_specs=pl.BlockSpec((tp, PEER_EXPERTS), lambda i: (i, 0)),
        scratch_shapes=[pltpu.VMEM((tp, PEER_PICKS), F32),
                        pltpu.VMEM((tp, PEER_PICKS), F32),
                        pltpu.VMEM((tp, PEER_PICKS), F32),
                        pltpu.VMEM((tp, PEER_NKEYS, PEER_NKEYS), F32)],
        compiler_params=_params(("parallel",), 32),
        name="peer_coef",
    )(e, g, act)


def _peer_down_kernel(p_ref, v_ref, x_ref, g_ref, o_ref):
    j = pl.program_id(1)

    @pl.when(j == 0)
    def _():
        o_ref[...] = x_ref[...]

    o_ref[...] += jnp.dot(p_ref[...], v_ref[...], preferred_element_type=F32)

    @pl.when(j == pl.num_programs(1) - 1)
    def _():
        o_ref[...] = _rms(o_ref[...], g_ref[...])


def _peer_down(coef, v, x1, g_final, tm=1024, tk=1024):
    return pl.pallas_call(
        _peer_down_kernel,
        out_shape=jax.ShapeDtypeStruct((N_TOK, D_MODEL), F32),
        grid=(N_TOK // tm, PEER_EXPERTS // tk),
        in_specs=[pl.BlockSpec((tm, tk), lambda i, j: (i, j)),
                  pl.BlockSpec((tk, D_MODEL), lambda i, j: (j, 0)),
                  pl.BlockSpec((tm, D_MODEL), lambda i, j: (i, 0), pipeline_mode=pl.Buffered(1)),
                  pl.BlockSpec((1, D_MODEL), lambda i, j: (0, 0))],
        out_specs=pl.BlockSpec((tm, D_MODEL), lambda i, j: (i, 0)),
        compiler_params=_params(("parallel", "arbitrary"), 48),
        name="peer_down",
    )(coef, v, x1, g_final)


def kernel(x, rel_bias, norm_mix_g, w_in, sink_a, w_oa, w_ob, w_out, norm_ffn_g,
           peer_wq, peer_keys, peer_u, peer_v, norm_final_g):
    assert x.shape == (BATCH, SEQ, D_MODEL) and w_in.shape[0] == 1
    x2 = x.reshape(N_TOK, D_MODEL)
    proj = _inproj(x2, norm_mix_g, w_in[0].astype(BF16))

    (oa,) = _attention(proj, rel_bias, sink_a[0], dil=1, hw=A_HALF_WINDOW,
                       q_cb=QA_CB, k_cb=KA_CB, v_cb=VA_CB, n_heads=A_Q_HEADS, kv_group=A_GROUP,
                       head_base=0, has_sink=True, with_lse=False)
    obs, lses = [], []
    for gi, dil in enumerate(B_DILATIONS):
        off = gi * B_HEADS_PER_GROUP
        o, lse = _attention(proj, rel_bias, sink_a[0], dil=dil, hw=B_HALF_SPAN,
                            q_cb=QB_CB + off, k_cb=KB_CB + off, v_cb=VB_CB + off,
                            n_heads=B_HEADS_PER_GROUP, kv_group=1,
                            head_base=A_Q_HEADS + off, has_sink=False, with_lse=True)
        obs.append(o)
        lses.append(lse)

    x1, h2 = _outproj(oa, obs, lses, proj, x2, w_oa[0].astype(BF16), w_ob[0].astype(BF16),
                      w_out[0].astype(BF16), norm_ffn_g)

    keys = peer_keys[0].reshape(PEER_HEADS * 2, PEER_NKEYS, PEER_QDIM // 2).astype(BF16)
    e, gate = _route(h2, peer_wq[0].astype(BF16), keys)
    act = _peer_up(h2, peer_u[0].astype(BF16), e)
    coef = _peer_coef(e, gate, act)
    out = _peer_down(coef, peer_v[0].astype(BF16), x1, norm_final_g.reshape(1, D_MODEL))
    return out.reshape(BATCH, SEQ, D_MODEL)
```

```python
import functools
import math

import numpy as np
import jax
import jax.numpy as jnp
from jax import lax
from jax.experimental import pallas as pl
from jax.experimental.pallas import tpu as pltpu

F32 = jnp.float32
BF16 = jnp.bfloat16
I32 = jnp.int32

D_MODEL = 2048
BATCH = 2
SEQ = 4096
N_TOK = BATCH * SEQ
HEAD_DIM = 128
LANES = 128

A_Q_HEADS = 8
A_KV_HEADS = 2
A_GROUP = A_Q_HEADS // A_KV_HEADS
A_HALF_WINDOW = 128
B_DILATIONS = (1, 4, 16)
B_GROUPS = 3
B_HEADS_PER_GROUP = 4
B_HALF_SPAN = 64
N_BUCKETS = 32
MAX_DISTANCE = 1024
N_ATTN_HEADS = A_Q_HEADS + B_GROUPS * B_HEADS_PER_GROUP

PEER_HEADS = 8
PEER_NKEYS = 128
PEER_EXPERTS = PEER_NKEYS * PEER_NKEYS
PEER_QDIM = 256
PEER_TOPK = 16
PEER_PICKS = PEER_HEADS * PEER_TOPK
EPS = 1e-6

A_Q_W = A_Q_HEADS * HEAD_DIM
A_KV_W = A_KV_HEADS * HEAD_DIM
B_W = B_GROUPS * B_HEADS_PER_GROUP * HEAD_DIM
B_OUT_W = B_HEADS_PER_GROUP * HEAD_DIM
IN_WIDTH = A_Q_W + 2 * A_KV_W + 3 * B_W + 2 * D_MODEL
QA_CB = 0
KA_CB = A_Q_W // LANES
VA_CB = (A_Q_W + A_KV_W) // LANES
QB_CB = (A_Q_W + 2 * A_KV_W) // LANES
KB_CB = QB_CB + B_W // LANES
VB_CB = KB_CB + B_W // LANES
GA_OFF = A_Q_W + 2 * A_KV_W + 3 * B_W
GB_OFF = GA_OFF + D_MODEL
IN_CB = IN_WIDTH // LANES

NEG = -1e30
ATT_SCALE = HEAD_DIM ** -0.5
ATT_QT = 128
ATT_GROUP = 8

MIB = 1024 * 1024


def _params(sem, vmem_mib):
    return pltpu.CompilerParams(dimension_semantics=sem, vmem_limit_bytes=vmem_mib * MIB)


def _rms(x, g):
    return x * lax.rsqrt(jnp.mean(x * x, axis=-1, keepdims=True) + EPS) * g


def _inproj_kernel(x_ref, g_ref, w_ref, o_ref, h_scr):
    @pl.when(pl.program_id(1) == 0)
    def _():
        h_scr[...] = _rms(x_ref[...], g_ref[...]).astype(BF16)

    o_ref[...] = jnp.dot(h_scr[...], w_ref[...], preferred_element_type=F32).astype(o_ref.dtype)


def _inproj(x2, g, w, tm=1024, tn=1024):
    n = w.shape[1]
    return pl.pallas_call(
        _inproj_kernel,
        out_shape=jax.ShapeDtypeStruct((N_TOK, n), BF16),
        grid=(N_TOK // tm, n // tn),
        in_specs=[
            pl.BlockSpec((tm, D_MODEL), lambda i, j: (i, 0)),
            pl.BlockSpec((1, D_MODEL), lambda i, j: (0, 0)),
            pl.BlockSpec((D_MODEL, tn), lambda i, j: (0, j)),
        ],
        out_specs=pl.BlockSpec((tm, tn), lambda i, j: (i, j)),
        scratch_shapes=[pltpu.VMEM((tm, D_MODEL), BF16)],
        compiler_params=_params(("parallel", "arbitrary"), 40),
        name="inproj",
    )(x2, g, w)


def _t5_bucket_np(rel):
    half = N_BUCKETS // 2
    max_exact = half // 2
    ret = np.where(rel > 0, half, 0)
    n = np.abs(rel)
    nf = np.maximum(n, 1).astype(np.float64)
    large = max_exact + (np.log(nf / max_exact) / math.log(MAX_DISTANCE / max_exact) * (half - max_exact)).astype(np.int64)
    large = np.minimum(large, half - 1)
    return (ret + np.where(n < max_exact, n, large)).astype(np.int32)


def _bucket_matrix(hw, dist_scale):
    w = ATT_QT + 2 * hw
    delta = (np.arange(w)[None, :] - hw) - np.arange(ATT_QT)[:, None]
    bkt = _t5_bucket_np(delta * dist_scale)
    return np.where(np.abs(delta) <= hw, bkt, -1).astype(np.int32)


def _attn_kernel(tab_ref, sink_ref, q_ref, k_ref, v_ref, bkt_ref, *rest,
                 dil, hw, head_base, has_sink, with_lse):
    n_out = 2 if with_lse else 1
    outs, scr = rest[:n_out], rest[n_out:]
    o_ref = outs[0]
    kpad, vpad, bias_scr = scr[:3]
    if dil > 1:
        q32, k32, v32, o32, qres = scr[3:8]
        if with_lse:
            l32 = scr[8]
    seq = SEQ // dil
    qt = ATT_QT
    win = qt + 2 * hw
    head = head_base + pl.program_id(1)

    bkt = bkt_ref[...]
    bias = jnp.full((qt, win), NEG, F32)
    for b in range(N_BUCKETS):
        bias = jnp.where(bkt == b, tab_ref[b, head], bias)
    bias_scr[...] = bias

    zeros = jnp.zeros((hw, HEAD_DIM), BF16)
    for pad_ref in (kpad, vpad):
        pad_ref[0:hw, :] = zeros
        pad_ref[hw + seq:hw + seq + hw, :] = zeros
    if dil == 1:
        kpad[hw:hw + seq, :] = k_ref[0]
        vpad[hw:hw + seq, :] = v_ref[0]
    else:
        q32[...] = q_ref[0].astype(F32)
        k32[...] = k_ref[0].astype(F32)
        v32[...] = v_ref[0].astype(F32)

    def residue(r, carry):
        if dil > 1:
            kpad[hw:hw + seq, :] = k32[pl.ds(r, seq, stride=dil), :].astype(BF16)
            vpad[hw:hw + seq, :] = v32[pl.ds(r, seq, stride=dil), :].astype(BF16)
            qres[...] = q32[pl.ds(r, seq, stride=dil), :].astype(BF16)

        def scores(q0, q, kw):
            s = lax.dot_general(q, kw, (((1,), (1,)), ((), ())), preferred_element_type=F32)
            s = s * ATT_SCALE + bias_scr[...]
            kpos = q0 - hw + lax.broadcasted_iota(I32, (qt, win), 1)
            return jnp.where((kpos >= 0) & (kpos < seq), s, NEG)

        def softmax(s):
            m = jnp.max(s, axis=-1, keepdims=True)
            if has_sink:
                sk = sink_ref[head]
                m = jnp.maximum(m, sk)
            p = jnp.exp(s - m)
            den = jnp.sum(p, axis=-1, keepdims=True)
            if has_sink:
                den = den + jnp.exp(sk - m)
            return p.astype(BF16), m, den

        def values(p, m, den, vw):
            o = jnp.dot(p, vw, preferred_element_type=F32) / den
            lse = jnp.broadcast_to(m + jnp.log(den), (qt, HEAD_DIM)) if with_lse else None
            return o, lse

        group = min(ATT_GROUP, seq // qt)

        def body(t, c):
            starts = [pl.multiple_of((t * group + u) * qt, qt) for u in range(group)]
            q_src = qres if dil > 1 else q_ref.at[0]
            s_all = [scores(q0, q_src[pl.ds(q0, qt), :], kpad[pl.ds(q0, win), :]) for q0 in starts]
            p_all = [softmax(s) for s in s_all]
            done = [values(*pmd, vpad[pl.ds(q0, win), :]) for q0, pmd in zip(starts, p_all)]
            for q0, (o, lse) in zip(starts, done):
                if dil > 1:
                    o32[pl.ds(r + q0 * dil, qt, stride=dil), :] = o
                    if with_lse:
                        l32[pl.ds(r + q0 * dil, qt, stride=dil), :] = lse
                else:
                    o_ref[0, pl.ds(q0, qt), :] = o.astype(o_ref.dtype)
                    if with_lse:
                        outs[1][0, pl.ds(q0, qt), :] = lse
            return c

        lax.fori_loop(0, seq // (qt * group), body, 0)
        return carry

    if dil > 1:
        lax.fori_loop(0, dil, residue, 0)
        o_ref[0] = o32[...].astype(o_ref.dtype)
        if with_lse:
            outs[1][0] = l32[...]
    else:
        residue(0, 0)


def _attention(proj, rel_bias, sink, *, dil, hw, q_cb, k_cb, v_cb, n_heads, kv_group,
               head_base, has_sink, with_lse):
    seq = SEQ // dil
    pv = proj.reshape(BATCH, SEQ, IN_WIDTH)
    bkt = jnp.asarray(_bucket_matrix(hw, dil))
    win = ATT_QT + 2 * hw
    out_w = n_heads * HEAD_DIM
    blk = (1, SEQ, HEAD_DIM)
    o_spec = pl.BlockSpec(blk, lambda b, h: (b, 0, h))
    out_shape = [jax.ShapeDtypeStruct((BATCH, SEQ, out_w), BF16)]
    out_specs = [o_spec]
    if with_lse:
        out_shape.append(jax.ShapeDtypeStruct((BATCH, SEQ, out_w), F32))
        out_specs.append(o_spec)
    scratch = [pltpu.VMEM((seq + 2 * hw, HEAD_DIM), BF16),
               pltpu.VMEM((seq + 2 * hw, HEAD_DIM), BF16),
               pltpu.VMEM((ATT_QT, win), F32)]
    if dil > 1:
        scratch += [pltpu.VMEM((SEQ, HEAD_DIM), F32)] * 4 + [pltpu.VMEM((seq, HEAD_DIM), BF16)]
        if with_lse:
            scratch.append(pltpu.VMEM((SEQ, HEAD_DIM), F32))
    kern = functools.partial(_attn_kernel, dil=dil, hw=hw, head_base=head_base,
                             has_sink=has_sink, with_lse=with_lse)
    outs = pl.pallas_call(
        kern,
        out_shape=out_shape,
        grid=(BATCH, n_heads),
        in_specs=[
            pl.BlockSpec(memory_space=pltpu.SMEM),
            pl.BlockSpec(memory_space=pltpu.SMEM),
            pl.BlockSpec(blk, lambda b, h: (b, 0, q_cb + h)),
            pl.BlockSpec(blk, lambda b, h: (b, 0, k_cb + h // kv_group)),
            pl.BlockSpec(blk, lambda b, h: (b, 0, v_cb + h // kv_group)),
            pl.BlockSpec((ATT_QT, win), lambda b, h: (0, 0)),
        ],
        out_specs=out_specs,
        scratch_shapes=scratch,
        compiler_params=_params(("parallel", "arbitrary"), 40),
        name=f"attn_d{dil}_h{head_base}",
    )(rel_bias, sink, pv, pv, pv, bkt)
    return [o.reshape(N_TOK, out_w) for o in outs]


def _outproj_kernel(oa_ref, o0_ref, o1_ref, o2_ref, l0_ref, l1_ref, l2_ref, ga_ref, gb_ref,
                    x_ref, woa_ref, wob_ref, wout_ref, gn_ref, x1_ref, h2_ref):
    l0, l1, l2 = l0_ref[...], l1_ref[...], l2_ref[...]
    mx = jnp.maximum(jnp.maximum(l0, l1), l2)
    e0, e1, e2 = jnp.exp(l0 - mx), jnp.exp(l1 - mx), jnp.exp(l2 - mx)
    den = e0 + e1 + e2
    ob = ((e0 / den) * o0_ref[...].astype(F32) + (e1 / den) * o1_ref[...].astype(F32)
          + (e2 / den) * o2_ref[...].astype(F32))
    ya = jnp.dot(oa_ref[...], woa_ref[...], preferred_element_type=F32)
    yb = jnp.dot(ob.astype(BF16), wob_ref[...], preferred_element_type=F32)
    merged = (jax.nn.sigmoid(ga_ref[...].astype(F32)) * ya
              + jax.nn.sigmoid(gb_ref[...].astype(F32)) * yb)
    x1 = x_ref[...] + jnp.dot(merged.astype(BF16), wout_ref[...], preferred_element_type=F32)
    x1_ref[...] = x1
    h2_ref[...] = _rms(x1, gn_ref[...]).astype(BF16)


def _outproj(oa, obs, lses, proj, x2, w_oa, w_ob, w_out, g_ffn, tm=256):
    row = lambda w: pl.BlockSpec((tm, w), lambda i: (i, 0))
    const = lambda shape: pl.BlockSpec(shape, lambda i: (0, 0), pipeline_mode=pl.Buffered(1))
    return pl.pallas_call(
        _outproj_kernel,
        out_shape=[jax.ShapeDtypeStruct((N_TOK, D_MODEL), F32),
                   jax.ShapeDtypeStruct((N_TOK, D_MODEL), BF16)],
        grid=(N_TOK // tm,),
        in_specs=[row(A_Q_W), row(B_OUT_W), row(B_OUT_W), row(B_OUT_W),
                  row(B_OUT_W), row(B_OUT_W), row(B_OUT_W),
                  pl.BlockSpec((tm, D_MODEL), lambda i: (i, GA_OFF // D_MODEL)),
                  pl.BlockSpec((tm, D_MODEL), lambda i: (i, GB_OFF // D_MODEL)),
                  row(D_MODEL),
                  const((A_Q_W, D_MODEL)), const((B_OUT_W, D_MODEL)), const((D_MODEL, D_MODEL)),
                  const((1, D_MODEL))],
        out_specs=[row(D_MODEL), row(D_MODEL)],
        compiler_params=_params(("parallel",), 48),
        name="outproj",
    )(oa, *obs, *lses, proj, proj, x2, w_oa, w_ob, w_out, g_ffn)


SUBLANES = 8
_FAR = 1024.0


def _tree(op, xs):
    xs = list(xs)
    while len(xs) > 1:
        nxt = [op(xs[i], xs[i + 1]) for i in range(0, len(xs) - 1, 2)]
        if len(xs) % 2:
            nxt.append(xs[-1])
        xs = nxt
    return xs[0]


def _all_sublanes(op, x):
    for shift in (4, 2, 1):
        x = op(x, pltpu.roll(x, shift, axis=0))
    return x


def _pop_max(tiles, sub_iota, payload=None):
    m = _all_sublanes(jnp.maximum, _tree(jnp.maximum, tiles))
    first = _tree(jnp.minimum, [jnp.where(t == m, float(SUBLANES * v), _FAR) for v, t in enumerate(tiles)])
    row = _all_sublanes(jnp.minimum, first + sub_iota)
    base = row - sub_iota
    hits = [base == float(SUBLANES * v) for v in range(len(tiles))]
    if payload is None:
        val = row
    else:
        val = _all_sublanes(jnp.maximum, _tree(jnp.maximum, [jnp.where(h, p, -1.0)
                                                            for h, p in zip(hits, payload)]))
    return m, val, [jnp.where(h, -jnp.inf, t) for h, t in zip(hits, tiles)]


def _route_kernel(h_ref, wq_ref, keys_ref, e_ref, g_ref,
                  q_scr, s_scr, ts_scr, ti_scr, bs_scr, et_scr, gt_scr, *, tm):
    k = PEER_TOPK
    nk = float(PEER_NKEYS)
    q_scr[...] = jnp.dot(h_ref[...], wq_ref[...], preferred_element_type=F32).astype(BF16)
    sub_iota = lax.broadcasted_iota(I32, (SUBLANES, LANES), 0).astype(F32)
    halves = [slice(i * LANES, (i + 1) * LANES) for i in range(tm // LANES)]

    def level1(hc, carry):
        col = pl.multiple_of(hc * LANES, LANES)
        s_scr[...] = lax.dot_general(keys_ref[hc], q_scr[:, pl.ds(col, LANES)],
                                     (((1,), (1,)), ((), ())), preferred_element_type=F32)
        tiles = [[s_scr[SUBLANES * v:SUBLANES * (v + 1), lanes] for v in range(PEER_NKEYS // SUBLANES)]
                 for lanes in halves]
        for kk in range(k):
            for i, lanes in enumerate(halves):
                m, row, tiles[i] = _pop_max(tiles[i], sub_iota)
                ts_scr[hc, kk:kk + 1, lanes] = m[0:1]
                ti_scr[hc, kk:kk + 1, lanes] = row[0:1]
        return carry

    lax.fori_loop(0, 2 * PEER_HEADS, level1, 0)

    def level2(h, carry):
        a, b = 2 * h, 2 * h + 1

        def candidates(lanes):
            s2 = [ts_scr[b, 0:8, lanes], ts_scr[b, 8:16, lanes]]
            i2 = [ti_scr[b, 0:8, lanes], ti_scr[b, 8:16, lanes]]
            s1 = lambda k1: ts_scr[a, k1:k1 + 1, lanes]
            i1 = lambda k1: ti_scr[a, k1:k1 + 1, lanes] * nk
            tiles = [s1(0) + s2[0], s1(0) + s2[1]]
            pay = [i1(0) + i2[0], i1(0) + i2[1]]
            for k1 in range(1, 8):
                allowed = k // (k1 + 1)
                t = s1(k1) + s2[0]
                tiles.append(t if allowed >= SUBLANES else jnp.where(sub_iota < float(allowed), t, -jnp.inf))
                pay.append(i1(k1) + i2[0])
            tiles.append(ts_scr[a, 8:16, lanes] + ts_scr[b, 0:1, lanes])
            pay.append(ti_scr[a, 8:16, lanes] * nk + ti_scr[b, 0:1, lanes])
            return tiles, pay

        cands = [candidates(lanes) for lanes in halves]
        tiles = [c[0] for c in cands]
        for kk in range(k):
            for i, lanes in enumerate(halves):
                m, expert, tiles[i] = _pop_max(tiles[i], sub_iota, payload=cands[i][1])
                bs_scr[kk:kk + 1, lanes] = m[0:1]
                et_scr[h, kk:kk + 1, lanes] = expert[0:1]
        for lanes in halves:
            bs = bs_scr[:, lanes]
            ex = jnp.exp(bs - jnp.max(bs, axis=0, keepdims=True))
            gt_scr[h, :, lanes] = ex / jnp.sum(ex, axis=0, keepdims=True)
        return carry

    lax.fori_loop(0, PEER_HEADS, level2, 0)

    for lanes in halves:
        et = jnp.concatenate([et_scr[h, :, lanes] for h in range(PEER_HEADS)], axis=0)
        gt = jnp.concatenate([gt_scr[h, :, lanes] for h in range(PEER_HEADS)], axis=0)
        e_ref[lanes, :] = et.T.astype(I32)
        g_ref[lanes, :] = gt.T


def _route(h2, wq, keys, tm=256):
    k = PEER_TOPK
    return pl.pallas_call(
        functools.partial(_route_kernel, tm=tm),
        out_shape=[jax.ShapeDtypeStruct((N_TOK, PEER_PICKS), I32),
                   jax.ShapeDtypeStruct((N_TOK, PEER_PICKS), F32)],
        grid=(N_TOK // tm,),
        in_specs=[pl.BlockSpec((tm, D_MODEL), lambda i: (i, 0)),
                  pl.BlockSpec((D_MODEL, PEER_HEADS * PEER_QDIM), lambda i: (0, 0),
                               pipeline_mode=pl.Buffered(1)),
                  pl.BlockSpec((PEER_HEADS * 2, PEER_NKEYS, PEER_QDIM // 2), lambda i: (0, 0, 0))],
        out_specs=[pl.BlockSpec((tm, PEER_PICKS), lambda i: (i, 0)),
                   pl.BlockSpec((tm, PEER_PICKS), lambda i: (i, 0))],
        scratch_shapes=[
            pltpu.VMEM((tm, PEER_HEADS * PEER_QDIM), BF16),
            pltpu.VMEM((PEER_NKEYS, tm), F32),
            pltpu.VMEM((2 * PEER_HEADS, k, tm), F32),
            pltpu.VMEM((2 * PEER_HEADS, k, tm), F32),
            pltpu.VMEM((k, tm), F32),
            pltpu.VMEM((PEER_HEADS, k, tm), F32),
            pltpu.VMEM((PEER_HEADS, k, tm), F32),
        ],
        compiler_params=_params(("parallel",), 32),
        name="peer_route",
    )(h2, wq, keys)


UP_CHUNK = 256
UP_ROWS = 256


def _peer_up_kernel(h_ref, u_ref, e_ref, act_ref, *, te):
    j = pl.program_id(1)

    @pl.when(j == 0)
    def _():
        act_ref[...] = jnp.zeros_like(act_ref)

    tb = h_ref.shape[0]
    u = [u_ref[c * UP_CHUNK:(c + 1) * UP_CHUNK, :].astype(BF16) for c in range(te // UP_CHUNK)]
    for m in range(tb // UP_ROWS):
        rows = slice(m * UP_ROWS, (m + 1) * UP_ROWS)
        e = e_ref[rows, :]
        row = e >> 7
        col = (e & (PEER_NKEYS - 1))[..., None]
        acc = act_ref[rows, :]
        h = h_ref[rows, :]
        for c in range(te // UP_CHUNK):
            dense = lax.dot_general(h, u[c], (((1,), (1,)), ((), ())), preferred_element_type=F32)
            for q in range(UP_CHUNK // LANES):
                got = lax.gather(dense[:, q * LANES:(q + 1) * LANES], col, _LANE_GATHER, (1, 1),
                                 mode=lax.GatherScatterMode.PROMISE_IN_BOUNDS)
                acc = jnp.where(row == (j * te + c * UP_CHUNK) // LANES + q, got, acc)
        act_ref[rows, :] = acc


_LANE_GATHER = lax.GatherDimensionNumbers(
    offset_dims=(), collapsed_slice_dims=(1,), start_index_map=(1,),
    operand_batching_dims=(0,), start_indices_batching_dims=(0,))


def _peer_up(h2, u, e, tb=2048, te=512):
    return pl.pallas_call(
        functools.partial(_peer_up_kernel, te=te),
        out_shape=jax.ShapeDtypeStruct((N_TOK, PEER_PICKS), F32),
        grid=(N_TOK // tb, PEER_EXPERTS // te),
        in_specs=[pl.BlockSpec((tb, D_MODEL), lambda i, j: (i, 0)),
                  pl.BlockSpec((te, D_MODEL), lambda i, j: (j, 0)),
                  pl.BlockSpec((tb, PEER_PICKS), lambda i, j: (i, 0))],
        out_specs=pl.BlockSpec((tb, PEER_PICKS), lambda i, j: (i, 0)),
        compiler_params=_params(("parallel", "arbitrary"), 48),
        name="peer_up",
    )(h2, u, e)


def _peer_coef_kernel(e_ref, g_ref, act_ref, p_ref, row_scr, col_scr, w_scr, c_scr, *, tp):
    e = e_ref[...]
    row_scr[...] = (e >> 7).astype(F32)
    col_scr[...] = (e & (PEER_NKEYS - 1)).astype(F32)
    act = act_ref[...]
    w_scr[...] = g_ref[...] * (0.5 * act * (1.0 + lax.erf(act * math.sqrt(0.5))))
    sub = 16
    iota = lax.broadcasted_iota(I32, (PEER_NKEYS // sub, sub, PEER_PICKS), 0) * sub \
        + lax.broadcasted_iota(I32, (PEER_NKEYS // sub, sub, PEER_PICKS), 1)
    iota = iota.astype(F32).astype(BF16)
    one = jnp.ones((), BF16)
    zero = jnp.zeros((), BF16)

    def bcast(ref, t):
        return jnp.broadcast_to(ref[t:t + 1, :], (sub, PEER_PICKS)).astype(BF16)[None]

    def products(t0):
        for t in range(t0, t0 + sub):
            rows, cols, w = bcast(row_scr, t), bcast(col_scr, t), bcast(w_scr, t)
            left = jnp.where(iota == rows, one, zero).reshape(PEER_NKEYS, PEER_PICKS)
            right = jnp.where(iota == cols, w, zero).reshape(PEER_NKEYS, PEER_PICKS)
            c_scr[t] = lax.dot_general(left, right, (((1,), (1,)), ((), ())),
                                       preferred_element_type=F32)

    def regroup(t0):
        sw = jnp.swapaxes(c_scr[t0:t0 + sub], 0, 1)
        for k1 in range(PEER_NKEYS):
            p_ref[t0:t0 + sub, k1 * PEER_NKEYS:(k1 + 1) * PEER_NKEYS] = sw[k1].astype(p_ref.dtype)

    for t0 in range(0, tp, sub):
        products(t0)
        if t0:
            regroup(t0 - sub)
    regroup(tp - sub)


def _peer_coef(e, g, act, tp=64):
    tok = pl.BlockSpec((tp, PEER_PICKS), lambda i: (i, 0))
    return pl.pallas_call(
        functools.partial(_peer_coef_kernel, tp=tp),
        out_shape=jax.ShapeDtypeStruct((N_TOK, PEER_EXPERTS), BF16),
        grid=(N_TOK // tp,),
        in_specs=[tok, tok, tok],
        out_specs=pl.BlockSpec((tp, PEER_EXPERTS), lambda i: (i, 0)),
        scratch_shapes=[pltpu.VMEM((tp, PEER_PICKS), F32),
                        pltpu.VMEM((tp, PEER_PICKS), F32),
                        pltpu.VMEM((tp, PEER_PICKS), F32),
                        pltpu.VMEM((tp, PEER_NKEYS, PEER_NKEYS), F32)],
        compiler_params=_params(("parallel",), 32),
        name="peer_coef",
    )(e, g, act)


def _peer_down_kernel(p_ref, v_ref, x_ref, g_ref, o_ref):
    j = pl.program_id(1)

    @pl.when(j == 0)
    def _():
        o_ref[...] = x_ref[...]

    o_ref[...] += jnp.dot(p_ref[...], v_ref[...], preferred_element_type=F32)

    @pl.when(j == pl.num_programs(1) - 1)
    def _():
        o_ref[...] = _rms(o_ref[...], g_ref[...])


def _peer_down(coef, v, x1, g_final, tm=1024, tk=1024):
    return pl.pallas_call(
        _peer_down_kernel,
        out_shape=jax.ShapeDtypeStruct((N_TOK, D_MODEL), F32),
        grid=(N_TOK // tm, PEER_EXPERTS // tk),
        in_specs=[pl.BlockSpec((tm, tk), lambda i, j: (i, j)),
                  pl.BlockSpec((tk, D_MODEL), lambda i, j: (j, 0)),
                  pl.BlockSpec((tm, D_MODEL), lambda i, j: (i, 0), pipeline_mode=pl.Buffered(1)),
                  pl.BlockSpec((1, D_MODEL), lambda i, j: (0, 0))],
        out_specs=pl.BlockSpec((tm, D_MODEL), lambda i, j: (i, 0)),
        compiler_params=_params(("parallel", "arbitrary"), 48),
        name="peer_down",
    )(coef, v, x1, g_final)


def kernel(x, rel_bias, norm_mix_g, w_in, sink_a, w_oa, w_ob, w_out, norm_ffn_g,
           peer_wq, peer_keys, peer_u, peer_v, norm_final_g):
    assert x.shape == (BATCH, SEQ, D_MODEL) and w_in.shape[0] == 1
    x2 = x.reshape(N_TOK, D_MODEL)
    proj = _inproj(x2, norm_mix_g, w_in[0].astype(BF16))

    (oa,) = _attention(proj, rel_bias, sink_a[0], dil=1, hw=A_HALF_WINDOW,
                       q_cb=QA_CB, k_cb=KA_CB, v_cb=VA_CB, n_heads=A_Q_HEADS, kv_group=A_GROUP,
                       head_base=0, has_sink=True, with_lse=False)
    obs, lses = [], []
    for gi, dil in enumerate(B_DILATIONS):
        off = gi * B_HEADS_PER_GROUP
        o, lse = _attention(proj, rel_bias, sink_a[0], dil=dil, hw=B_HALF_SPAN,
                            q_cb=QB_CB + off, k_cb=KB_CB + off, v_cb=VB_CB + off,
                            n_heads=B_HEADS_PER_GROUP, kv_group=1,
                            head_base=A_Q_HEADS + off, has_sink=False, with_lse=True)
        obs.append(o)
        lses.append(lse)

    x1, h2 = _outproj(oa, obs, lses, proj, x2, w_oa[0].astype(BF16), w_ob[0].astype(BF16),
                      w_out[0].astype(BF16), norm_ffn_g)

    keys = peer_keys[0].reshape(PEER_HEADS * 2, PEER_NKEYS, PEER_QDIM // 2).astype(BF16)
    e, gate = _route(h2, peer_wq[0].astype(BF16), keys)
    act = _peer_up(h2, peer_u[0], e)
    coef = _peer_coef(e, gate, act)
    out = _peer_down(coef, peer_v[0].astype(BF16), x1, norm_final_g.reshape(1, D_MODEL))
    return out.reshape(BATCH, SEQ, D_MODEL)
```

```python
import functools
import math

import numpy as np
import jax
import jax.numpy as jnp
from jax import lax
from jax.experimental import pallas as pl
from jax.experimental.pallas import tpu as pltpu

F32 = jnp.float32
BF16 = jnp.bfloat16
I32 = jnp.int32

D_MODEL = 2048
BATCH = 2
SEQ = 4096
N_TOK = BATCH * SEQ
HEAD_DIM = 128
LANES = 128
SUBLANES = 8

A_Q_HEADS = 8
A_KV_HEADS = 2
A_GROUP = A_Q_HEADS // A_KV_HEADS
A_HALF_WINDOW = 128
B_DILATIONS = (1, 4, 16)
B_GROUPS = 3
B_HEADS_PER_GROUP = 4
B_HALF_SPAN = 64
N_BUCKETS = 32
MAX_DISTANCE = 1024
N_ATTN_HEADS = A_Q_HEADS + B_GROUPS * B_HEADS_PER_GROUP

PEER_HEADS = 8
PEER_NKEYS = 128
PEER_EXPERTS = PEER_NKEYS * PEER_NKEYS
PEER_QDIM = 256
PEER_TOPK = 16
PEER_PICKS = PEER_HEADS * PEER_TOPK
EPS = 1e-6

A_Q_W = A_Q_HEADS * HEAD_DIM
A_KV_W = A_KV_HEADS * HEAD_DIM
B_W = B_GROUPS * B_HEADS_PER_GROUP * HEAD_DIM
B_OUT_W = B_HEADS_PER_GROUP * HEAD_DIM
IN_WIDTH = A_Q_W + 2 * A_KV_W + 3 * B_W + 2 * D_MODEL
QA_CB = 0
KA_CB = A_Q_W // LANES
VA_CB = (A_Q_W + A_KV_W) // LANES
QB_CB = (A_Q_W + 2 * A_KV_W) // LANES
KB_CB = QB_CB + B_W // LANES
VB_CB = KB_CB + B_W // LANES
GA_OFF = A_Q_W + 2 * A_KV_W + 3 * B_W
GB_OFF = GA_OFF + D_MODEL
IN_CB = IN_WIDTH // LANES

NEG = -1e30
ATT_SCALE = HEAD_DIM ** -0.5
ATT_QT = 128
ATT_GROUP = 8

MIB = 1024 * 1024


def _params(sem, vmem_mib):
    return pltpu.CompilerParams(dimension_semantics=sem, vmem_limit_bytes=vmem_mib * MIB)


def _rms(x, g):
    return x * lax.rsqrt(jnp.mean(x * x, axis=-1, keepdims=True) + EPS) * g


def _inproj_kernel(x_ref, g_ref, w_ref, o_ref, h_scr):
    @pl.when(pl.program_id(1) == 0)
    def _():
        h_scr[...] = _rms(x_ref[...], g_ref[...]).astype(BF16)

    o_ref[...] = jnp.dot(h_scr[...], w_ref[...], preferred_element_type=F32).astype(o_ref.dtype)


def _inproj(x2, g, w, tm=1024, tn=1024):
    n = w.shape[1]
    return pl.pallas_call(
        _inproj_kernel,
        out_shape=jax.ShapeDtypeStruct((N_TOK, n), BF16),
        grid=(N_TOK // tm, n // tn),
        in_specs=[
            pl.BlockSpec((tm, D_MODEL), lambda i, j: (i, 0)),
            pl.BlockSpec((1, D_MODEL), lambda i, j: (0, 0)),
            pl.BlockSpec((D_MODEL, tn), lambda i, j: (0, j)),
        ],
        out_specs=pl.BlockSpec((tm, tn), lambda i, j: (i, j)),
        scratch_shapes=[pltpu.VMEM((tm, D_MODEL), BF16)],
        compiler_params=_params(("parallel", "arbitrary"), 40),
        name="inproj",
    )(x2, g, w)


def _t5_bucket_np(rel):
    half = N_BUCKETS // 2
    max_exact = half // 2
    ret = np.where(rel > 0, half, 0)
    n = np.abs(rel)
    nf = np.maximum(n, 1).astype(np.float64)
    large = max_exact + (np.log(nf / max_exact) / math.log(MAX_DISTANCE / max_exact) * (half - max_exact)).astype(np.int64)
    large = np.minimum(large, half - 1)
    return (ret + np.where(n < max_exact, n, large)).astype(np.int32)


def _bucket_matrix(hw, dist_scale):
    w = ATT_QT + 2 * hw
    delta = (np.arange(w)[None, :] - hw) - np.arange(ATT_QT)[:, None]
    bkt = _t5_bucket_np(delta * dist_scale)
    return np.where(np.abs(delta) <= hw, bkt, -1).astype(np.int32)


def _attn_kernel(tab_ref, sink_ref, q_ref, k_ref, v_ref, bkt_ref, *rest,
                 dil, hw, head_base, has_sink, with_lse):
    n_out = 2 if with_lse else 1
    outs, scr = rest[:n_out], rest[n_out:]
    o_ref = outs[0]
    kpad, vpad, bias_scr = scr[:3]
    if dil > 1:
        q32, k32, v32, o32, qres = scr[3:8]
        if with_lse:
            l32 = scr[8]
    seq = SEQ // dil
    qt = ATT_QT
    win = qt + 2 * hw
    head = head_base + pl.program_id(1)

    bkt = bkt_ref[...]
    bias = jnp.full((qt, win), NEG, F32)
    for b in range(N_BUCKETS):
        bias = jnp.where(bkt == b, tab_ref[b, head], bias)
    bias_scr[...] = bias

    zeros = jnp.zeros((hw, HEAD_DIM), BF16)
    for pad_ref in (kpad, vpad):
        pad_ref[0:hw, :] = zeros
        pad_ref[hw + seq:hw + seq + hw, :] = zeros
    if dil == 1:
        kpad[hw:hw + seq, :] = k_ref[0]
        vpad[hw:hw + seq, :] = v_ref[0]
    else:
        q32[...] = q_ref[0].astype(F32)
        k32[...] = k_ref[0].astype(F32)
        v32[...] = v_ref[0].astype(F32)

    def residue(r, carry):
        if dil > 1:
            kpad[hw:hw + seq, :] = k32[pl.ds(r, seq, stride=dil), :].astype(BF16)
            vpad[hw:hw + seq, :] = v32[pl.ds(r, seq, stride=dil), :].astype(BF16)
            qres[...] = q32[pl.ds(r, seq, stride=dil), :].astype(BF16)

        def scores(q0, q, kw):
            s = lax.dot_general(q, kw, (((1,), (1,)), ((), ())), preferred_element_type=F32)
            s = s * ATT_SCALE + bias_scr[...]
            kpos = q0 - hw + lax.broadcasted_iota(I32, (qt, win), 1)
            return jnp.where((kpos >= 0) & (kpos < seq), s, NEG)

        def softmax(s):
            m = jnp.max(s, axis=-1, keepdims=True)
            if has_sink:
                sk = sink_ref[head]
                m = jnp.maximum(m, sk)
            p = jnp.exp(s - m)
            den = jnp.sum(p, axis=-1, keepdims=True)
            if has_sink:
                den = den + jnp.exp(sk - m)
            return p.astype(BF16), m, den

        def values(p, m, den, vw):
            o = jnp.dot(p, vw, preferred_element_type=F32) / den
            lse = jnp.broadcast_to(m + jnp.log(den), (qt, HEAD_DIM)) if with_lse else None
            return o, lse

        group = min(ATT_GROUP, seq // qt)

        def body(t, c):
            starts = [pl.multiple_of((t * group + u) * qt, qt) for u in range(group)]
            q_src = qres if dil > 1 else q_ref.at[0]
            s_all = [scores(q0, q_src[pl.ds(q0, qt), :], kpad[pl.ds(q0, win), :]) for q0 in starts]
            p_all = [softmax(s) for s in s_all]
            done = [values(*pmd, vpad[pl.ds(q0, win), :]) for q0, pmd in zip(starts, p_all)]
            for q0, (o, lse) in zip(starts, done):
                if dil > 1:
                    o32[pl.ds(r + q0 * dil, qt, stride=dil), :] = o
                    if with_lse:
                        l32[pl.ds(r + q0 * dil, qt, stride=dil), :] = lse
                else:
                    o_ref[0, pl.ds(q0, qt), :] = o.astype(o_ref.dtype)
                    if with_lse:
                        outs[1][0, pl.ds(q0, qt), :] = lse
            return c

        lax.fori_loop(0, seq // (qt * group), body, 0)
        return carry

    if dil > 1:
        lax.fori_loop(0, dil, residue, 0)
        o_ref[0] = o32[...].astype(o_ref.dtype)
        if with_lse:
            outs[1][0] = l32[...]
    else:
        residue(0, 0)


def _attention(proj, rel_bias, sink, *, dil, hw, q_cb, k_cb, v_cb, n_heads, kv_group,
               head_base, has_sink, with_lse):
    seq = SEQ // dil
    pv = proj.reshape(BATCH, SEQ, IN_WIDTH)
    bkt = jnp.asarray(_bucket_matrix(hw, dil))
    win = ATT_QT + 2 * hw
    out_w = n_heads * HEAD_DIM
    blk = (1, SEQ, HEAD_DIM)
    o_spec = pl.BlockSpec(blk, lambda b, h: (b, 0, h))
    out_shape = [jax.ShapeDtypeStruct((BATCH, SEQ, out_w), BF16)]
    out_specs = [o_spec]
    if with_lse:
        out_shape.append(jax.ShapeDtypeStruct((BATCH, SEQ, out_w), F32))
        out_specs.append(o_spec)
    scratch = [pltpu.VMEM((seq + 2 * hw, HEAD_DIM), BF16),
               pltpu.VMEM((seq + 2 * hw, HEAD_DIM), BF16),
               pltpu.VMEM((ATT_QT, win), F32)]
    if dil > 1:
        scratch += [pltpu.VMEM((SEQ, HEAD_DIM), F32)] * 4 + [pltpu.VMEM((seq, HEAD_DIM), BF16)]
        if with_lse:
            scratch.append(pltpu.VMEM((SEQ, HEAD_DIM), F32))
    kern = functools.partial(_attn_kernel, dil=dil, hw=hw, head_base=head_base,
                             has_sink=has_sink, with_lse=with_lse)
    outs = pl.pallas_call(
        kern,
        out_shape=out_shape,
        grid=(BATCH, n_heads),
        in_specs=[
            pl.BlockSpec(memory_space=pltpu.SMEM),
            pl.BlockSpec(memory_space=pltpu.SMEM),
            pl.BlockSpec(blk, lambda b, h: (b, 0, q_cb + h)),
            pl.BlockSpec(blk, lambda b, h: (b, 0, k_cb + h // kv_group)),
            pl.BlockSpec(blk, lambda b, h: (b, 0, v_cb + h // kv_group)),
            pl.BlockSpec((ATT_QT, win), lambda b, h: (0, 0)),
        ],
        out_specs=out_specs,
        scratch_shapes=scratch,
        compiler_params=_params(("parallel", "arbitrary"), 40),
        name=f"attn_d{dil}_h{head_base}",
    )(rel_bias, sink, pv, pv, pv, bkt)
    return [o.reshape(N_TOK, out_w) for o in outs]


def _outproj_kernel(oa_ref, o0_ref, o1_ref, o2_ref, l0_ref, l1_ref, l2_ref, ga_ref, gb_ref,
                    x_ref, woa_ref, wob_ref, wout_ref, gn_ref, x1_ref, h2_ref):
    l0, l1, l2 = l0_ref[...], l1_ref[...], l2_ref[...]
    mx = jnp.maximum(jnp.maximum(l0, l1), l2)
    e0, e1, e2 = jnp.exp(l0 - mx), jnp.exp(l1 - mx), jnp.exp(l2 - mx)
    den = e0 + e1 + e2
    ob = ((e0 / den) * o0_ref[...].astype(F32) + (e1 / den) * o1_ref[...].astype(F32)
          + (e2 / den) * o2_ref[...].astype(F32))
    ya = jnp.dot(oa_ref[...], woa_ref[...], preferred_element_type=F32)
    yb = jnp.dot(ob.astype(BF16), wob_ref[...], preferred_element_type=F32)
    merged = (jax.nn.sigmoid(ga_ref[...].astype(F32)) * ya
              + jax.nn.sigmoid(gb_ref[...].astype(F32)) * yb)
    x1 = x_ref[...] + jnp.dot(merged.astype(BF16), wout_ref[...], preferred_element_type=F32)
    x1_ref[...] = x1
    h2_ref[...] = _rms(x1, gn_ref[...]).astype(BF16)


def _outproj(oa, obs, lses, proj, x2, w_oa, w_ob, w_out, g_ffn, tm=256):
    row = lambda w: pl.BlockSpec((tm, w), lambda i: (i, 0))
    const = lambda shape: pl.BlockSpec(shape, lambda i: (0, 0), pipeline_mode=pl.Buffered(1))
    return pl.pallas_call(
        _outproj_kernel,
        out_shape=[jax.ShapeDtypeStruct((N_TOK, D_MODEL), F32),
                   jax.ShapeDtypeStruct((N_TOK, D_MODEL), BF16)],
        grid=(N_TOK // tm,),
        in_specs=[row(A_Q_W), row(B_OUT_W), row(B_OUT_W), row(B_OUT_W),
                  row(B_OUT_W), row(B_OUT_W), row(B_OUT_W),
                  pl.BlockSpec((tm, D_MODEL), lambda i: (i, GA_OFF // D_MODEL)),
                  pl.BlockSpec((tm, D_MODEL), lambda i: (i, GB_OFF // D_MODEL)),
                  row(D_MODEL),
                  const((A_Q_W, D_MODEL)), const((B_OUT_W, D_MODEL)), const((D_MODEL, D_MODEL)),
                  const((1, D_MODEL))],
        out_specs=[row(D_MODEL), row(D_MODEL)],
        compiler_params=_params(("parallel",), 48),
        name="outproj",
    )(oa, *obs, *lses, proj, proj, x2, w_oa, w_ob, w_out, g_ffn)


_FAR = 1024.0
RT_SLICE = 256
UP_CHUNK = 256
UP_ROWS = 256

_LANE_GATHER = lax.GatherDimensionNumbers(
    offset_dims=(), collapsed_slice_dims=(1,), start_index_map=(1,),
    operand_batching_dims=(0,), start_indices_batching_dims=(0,))


def _tree(op, xs):
    xs = list(xs)
    while len(xs) > 1:
        nxt = [op(xs[i], xs[i + 1]) for i in range(0, len(xs) - 1, 2)]
        if len(xs) % 2:
            nxt.append(xs[-1])
        xs = nxt
    return xs[0]


def _all_sublanes(op, x):
    for shift in (4, 2, 1):
        x = op(x, pltpu.roll(x, shift, axis=0))
    return x


def _pop_max(tiles, sub_iota, payload=None):
    m = _all_sublanes(jnp.maximum, _tree(jnp.maximum, tiles))
    first = _tree(jnp.minimum, [jnp.where(t == m, float(SUBLANES * v), _FAR) for v, t in enumerate(tiles)])
    row = _all_sublanes(jnp.minimum, first + sub_iota)
    base = row - sub_iota
    hits = [base == float(SUBLANES * v) for v in range(len(tiles))]
    if payload is None:
        val = row
    else:
        val = _all_sublanes(jnp.maximum, _tree(jnp.maximum, [jnp.where(h, p, -1.0)
                                                            for h, p in zip(hits, payload)]))
    return m, val, [jnp.where(h, -jnp.inf, t) for h, t in zip(hits, tiles)]


def _route_scores(h, q_ref, keys_ref, s_scr):
    for c in range(2):
        col = pl.multiple_of((2 * h + c) * LANES, LANES)
        s_scr[c] = lax.dot_general(keys_ref[2 * h + c], q_ref[:, pl.ds(col, LANES)],
                                   (((1,), (1,)), ((), ())), preferred_element_type=F32)


def _route_topk(s_scr, ts_scr, ti_scr, bs_scr, et_ref, gt_ref):
    k = PEER_TOPK
    nk = float(PEER_NKEYS)
    sub_iota = lax.broadcasted_iota(I32, (SUBLANES, LANES), 0).astype(F32)
    halves = [slice(i * LANES, (i + 1) * LANES) for i in range(RT_SLICE // LANES)]

    chains = []
    for c in range(2):
        for lanes in halves:
            tiles = [s_scr[c, SUBLANES * v:SUBLANES * (v + 1), lanes] for v in range(PEER_NKEYS // SUBLANES)]
            chains.append([c, lanes, tiles])
    for kk in range(k):
        for chain in chains:
            c, lanes, tiles = chain
            m, row, chain[2] = _pop_max(tiles, sub_iota)
            ts_scr[c, kk:kk + 1, lanes] = m[0:1]
            ti_scr[c, kk:kk + 1, lanes] = row[0:1]

    def candidates(lanes):
        s2 = [ts_scr[1, 0:8, lanes], ts_scr[1, 8:16, lanes]]
        i2 = [ti_scr[1, 0:8, lanes], ti_scr[1, 8:16, lanes]]
        s1 = lambda k1: ts_scr[0, k1:k1 + 1, lanes]
        i1 = lambda k1: ti_scr[0, k1:k1 + 1, lanes] * nk
        tiles = [s1(0) + s2[0], s1(0) + s2[1]]
        pay = [i1(0) + i2[0], i1(0) + i2[1]]
        for k1 in range(1, 8):
            allowed = k // (k1 + 1)
            t = s1(k1) + s2[0]
            tiles.append(t if allowed >= SUBLANES else jnp.where(sub_iota < float(allowed), t, -jnp.inf))
            pay.append(i1(k1) + i2[0])
        tiles.append(ts_scr[0, 8:16, lanes] + ts_scr[1, 0:1, lanes])
        pay.append(ti_scr[0, 8:16, lanes] * nk + ti_scr[1, 0:1, lanes])
        return tiles, pay

    cands = [candidates(lanes) for lanes in halves]
    tiles = [c[0] for c in cands]
    for kk in range(k):
        for i, lanes in enumerate(halves):
            m, expert, tiles[i] = _pop_max(tiles[i], sub_iota, payload=cands[i][1])
            bs_scr[kk:kk + 1, lanes] = m[0:1]
            et_ref[kk:kk + 1, lanes] = expert[0:1]
    for lanes in halves:
        bs = bs_scr[:, lanes]
        ex = jnp.exp(bs - jnp.max(bs, axis=0, keepdims=True))
        gt_ref[:, lanes] = ex / jnp.sum(ex, axis=0, keepdims=True)


def _expert_up(j, h_ref, u_ref, e_ref, act_ref, *, te):
    tb = h_ref.shape[0]
    u = [u_ref[c * UP_CHUNK:(c + 1) * UP_CHUNK, :].astype(BF16) for c in range(te // UP_CHUNK)]
    for m in range(tb // UP_ROWS):
        rows = slice(m * UP_ROWS, (m + 1) * UP_ROWS)
        e = e_ref[rows, :]
        row = e >> 7
        col = (e & (PEER_NKEYS - 1))[..., None]
        acc = act_ref[rows, :]
        h = h_ref[rows, :]
        for c in range(te // UP_CHUNK):
            dense = lax.dot_general(h, u[c], (((1,), (1,)), ((), ())), preferred_element_type=F32)
            for q in range(UP_CHUNK // LANES):
                got = lax.gather(dense[:, q * LANES:(q + 1) * LANES], col, _LANE_GATHER, (1, 1),
                                 mode=lax.GatherScatterMode.PROMISE_IN_BOUNDS)
                acc = jnp.where(row == (j * te + c * UP_CHUNK) // LANES + q, got, acc)
        act_ref[rows, :] = acc


def _route_kernel(h_ref, wq_ref, keys_ref, e_ref, g_ref,
                  q_scr, s_scr, ts_scr, ti_scr, bs_scr, et_scr, gt_scr):
    q_scr[...] = jnp.dot(h_ref[...], wq_ref[...], preferred_element_type=F32).astype(BF16)

    def head(h, carry):
        _route_scores(h, q_scr, keys_ref, s_scr)
        _route_topk(s_scr, ts_scr, ti_scr, bs_scr, et_scr.at[h], gt_scr.at[h])
        return carry

    lax.fori_loop(0, PEER_HEADS, head, 0)

    for half in range(RT_SLICE // LANES):
        lanes = slice(half * LANES, (half + 1) * LANES)
        et = jnp.concatenate([et_scr[h, :, lanes] for h in range(PEER_HEADS)], axis=0)
        gt = jnp.concatenate([gt_scr[h, :, lanes] for h in range(PEER_HEADS)], axis=0)
        e_ref[lanes, :] = et.T.astype(I32)
        g_ref[lanes, :] = gt.T


def _route(h2, wq, keys):
    k = PEER_TOPK
    tok = pl.BlockSpec((RT_SLICE, PEER_PICKS), lambda i: (i, 0))
    return pl.pallas_call(
        _route_kernel,
        out_shape=[jax.ShapeDtypeStruct((N_TOK, PEER_PICKS), I32),
                   jax.ShapeDtypeStruct((N_TOK, PEER_PICKS), F32)],
        grid=(N_TOK // RT_SLICE,),
        in_specs=[pl.BlockSpec((RT_SLICE, D_MODEL), lambda i: (i, 0)),
                  pl.BlockSpec((D_MODEL, PEER_HEADS * PEER_QDIM), lambda i: (0, 0),
                               pipeline_mode=pl.Buffered(1)),
                  pl.BlockSpec((PEER_HEADS * 2, PEER_NKEYS, PEER_QDIM // 2), lambda i: (0, 0, 0))],
        out_specs=[tok, tok],
        scratch_shapes=[
            pltpu.VMEM((RT_SLICE, PEER_HEADS * PEER_QDIM), BF16),
            pltpu.VMEM((2, PEER_NKEYS, RT_SLICE), F32),
            pltpu.VMEM((2, k, RT_SLICE), F32),
            pltpu.VMEM((2, k, RT_SLICE), F32),
            pltpu.VMEM((k, RT_SLICE), F32),
            pltpu.VMEM((PEER_HEADS, k, RT_SLICE), F32),
            pltpu.VMEM((PEER_HEADS, k, RT_SLICE), F32),
        ],
        compiler_params=_params(("parallel",), 32),
        name="peer_route",
    )(h2, wq, keys)


def _peer_up_kernel(h_ref, u_ref, e_ref, v_ref, act_ref, vb_ref, *, te):
    i, j = pl.program_id(0), pl.program_id(1)

    @pl.when(j == 0)
    def _():
        act_ref[...] = jnp.zeros_like(act_ref)

    @pl.when(i == 0)
    def _():
        vb_ref[...] = v_ref[...].astype(BF16)

    _expert_up(j, h_ref, u_ref, e_ref, act_ref, te=te)


def _peer_up(h2, u, e, v, tb=2048, te=512):
    nj = PEER_EXPERTS // te
    v_tile = pl.BlockSpec((te, D_MODEL), lambda i, j: (jnp.where(i == 0, j, nj - 1), 0))
    return pl.pallas_call(
        functools.partial(_peer_up_kernel, te=te),
        out_shape=[jax.ShapeDtypeStruct((N_TOK, PEER_PICKS), F32),
                   jax.ShapeDtypeStruct((PEER_EXPERTS, D_MODEL), BF16)],
        grid=(N_TOK // tb, nj),
        in_specs=[pl.BlockSpec((tb, D_MODEL), lambda i, j: (i, 0)),
                  pl.BlockSpec((te, D_MODEL), lambda i, j: (j, 0)),
                  pl.BlockSpec((tb, PEER_PICKS), lambda i, j: (i, 0)),
                  v_tile],
        out_specs=[pl.BlockSpec((tb, PEER_PICKS), lambda i, j: (i, 0)), v_tile],
        compiler_params=_params(("arbitrary", "arbitrary"), 56),
        name="peer_up",
    )(h2, u, e, v)


def _peer_coef_kernel(e_ref, g_ref, act_ref, p_ref, row_scr, col_scr, w_scr, c_scr, *, tp):
    e = e_ref[...]
    row_scr[...] = (e >> 7).astype(F32)
    col_scr[...] = (e & (PEER_NKEYS - 1)).astype(F32)
    act = act_ref[...]
    w_scr[...] = g_ref[...] * (0.5 * act * (1.0 + lax.erf(act * math.sqrt(0.5))))
    sub = 16
    iota = lax.broadcasted_iota(I32, (PEER_NKEYS // sub, sub, PEER_PICKS), 0) * sub \
        + lax.broadcasted_iota(I32, (PEER_NKEYS // sub, sub, PEER_PICKS), 1)
    iota = iota.astype(F32).astype(BF16)
    one = jnp.ones((), BF16)
    zero = jnp.zeros((), BF16)

    def bcast(ref, t):
        return jnp.broadcast_to(ref[t:t + 1, :], (sub, PEER_PICKS)).astype(BF16)[None]

    def products(t0):
        for t in range(t0, t0 + sub):
            rows, cols, w = bcast(row_scr, t), bcast(col_scr, t), bcast(w_scr, t)
            left = jnp.where(iota == rows, one, zero).reshape(PEER_NKEYS, PEER_PICKS)
            right = jnp.where(iota == cols, w, zero).reshape(PEER_NKEYS, PEER_PICKS)
            c_scr[t] = lax.dot_general(left, right, (((1,), (1,)), ((), ())),
                                       preferred_element_type=F32)

    def regroup(t0):
        sw = jnp.swapaxes(c_scr[t0:t0 + sub], 0, 1)
        for k1 in range(PEER_NKEYS):
            p_ref[t0:t0 + sub, k1 * PEER_NKEYS:(k1 + 1) * PEER_NKEYS] = sw[k1].astype(p_ref.dtype)

    for t0 in range(0, tp, sub):
        products(t0)
        if t0:
            regroup(t0 - sub)
    regroup(tp - sub)


def _peer_coef(e, g, act, tp=64):
    tok = pl.BlockSpec((tp, PEER_PICKS), lambda i: (i, 0))
    return pl.pallas_call(
        functools.partial(_peer_coef_kernel, tp=tp),
        out_shape=jax.ShapeDtypeStruct((N_TOK, PEER_EXPERTS), BF16),
        grid=(N_TOK // tp,),
        in_specs=[tok, tok, tok],
        out_specs=pl.BlockSpec((tp, PEER_EXPERTS), lambda i: (i, 0)),
        scratch_shapes=[pltpu.VMEM((tp, PEER_PICKS), F32),
                        pltpu.VMEM((tp, PEER_PICKS), F32),
                        pltpu.VMEM((tp, PEER_PICKS), F32),
                        pltpu.VMEM((tp, PEER_NKEYS, PEER_NKEYS), F32)],
        compiler_params=_params(("parallel",), 32),
        name="peer_coef",
    )(e, g, act)


def _peer_down_kernel(p_ref, v_ref, x_ref, g_ref, o_ref):
    j = pl.program_id(1)

    @pl.when(j == 0)
    def _():
        o_ref[...] = x_ref[...]

    o_ref[...] += jnp.dot(p_ref[...], v_ref[...], preferred_element_type=F32)

    @pl.when(j == pl.num_programs(1) - 1)
    def _():
        o_ref[...] = _rms(o_ref[...], g_ref[...])


def _peer_down(coef, v, x1, g_final, tm=1024, tk=1024):
    return pl.pallas_call(
        _peer_down_kernel,
        out_shape=jax.ShapeDtypeStruct((N_TOK, D_MODEL), F32),
        grid=(N_TOK // tm, PEER_EXPERTS // tk),
        in_specs=[pl.BlockSpec((tm, tk), lambda i, j: (i, j)),
                  pl.BlockSpec((tk, D_MODEL), lambda i, j: (j, 0)),
                  pl.BlockSpec((tm, D_MODEL), lambda i, j: (i, 0), pipeline_mode=pl.Buffered(1)),
                  pl.BlockSpec((1, D_MODEL), lambda i, j: (0, 0))],
        out_specs=pl.BlockSpec((tm, D_MODEL), lambda i, j: (i, 0)),
        compiler_params=_params(("parallel", "arbitrary"), 48),
        name="peer_down",
    )(coef, v, x1, g_final)


def kernel(x, rel_bias, norm_mix_g, w_in, sink_a, w_oa, w_ob, w_out, norm_ffn_g,
           peer_wq, peer_keys, peer_u, peer_v, norm_final_g):
    assert x.shape == (BATCH, SEQ, D_MODEL) and w_in.shape[0] == 1
    x2 = x.reshape(N_TOK, D_MODEL)
    proj = _inproj(x2, norm_mix_g, w_in[0].astype(BF16))

    (oa,) = _attention(proj, rel_bias, sink_a[0], dil=1, hw=A_HALF_WINDOW,
                       q_cb=QA_CB, k_cb=KA_CB, v_cb=VA_CB, n_heads=A_Q_HEADS, kv_group=A_GROUP,
                       head_base=0, has_sink=True, with_lse=False)
    obs, lses = [], []
    for gi, dil in enumerate(B_DILATIONS):
        off = gi * B_HEADS_PER_GROUP
        o, lse = _attention(proj, rel_bias, sink_a[0], dil=dil, hw=B_HALF_SPAN,
                            q_cb=QB_CB + off, k_cb=KB_CB + off, v_cb=VB_CB + off,
                            n_heads=B_HEADS_PER_GROUP, kv_group=1,
                            head_base=A_Q_HEADS + off, has_sink=False, with_lse=True)
        obs.append(o)
        lses.append(lse)

    x1, h2 = _outproj(oa, obs, lses, proj, x2, w_oa[0].astype(BF16), w_ob[0].astype(BF16),
                      w_out[0].astype(BF16), norm_ffn_g)

    keys = peer_keys[0].reshape(PEER_HEADS * 2, PEER_NKEYS, PEER_QDIM // 2).astype(BF16)
    e, gate = _route(h2, peer_wq[0].astype(BF16), keys)
    act, v_bf16 = _peer_up(h2, peer_u[0], e, peer_v[0])
    coef = _peer_coef(e, gate, act)
    out = _peer_down(coef, v_bf16, x1, norm_final_g.reshape(1, D_MODEL))
    return out.reshape(BATCH, SEQ, D_MODEL)
```

```python
import functools
import math

import numpy as np
import jax
import jax.numpy as jnp
from jax import lax
from jax.experimental import pallas as pl
from jax.experimental.pallas import tpu as pltpu

F32 = jnp.float32
BF16 = jnp.bfloat16
I32 = jnp.int32

D_MODEL = 2048
BATCH = 2
SEQ = 4096
N_TOK = BATCH * SEQ
HEAD_DIM = 128
LANES = 128
SUBLANES = 8

A_Q_HEADS = 8
A_KV_HEADS = 2
A_GROUP = A_Q_HEADS // A_KV_HEADS
A_HALF_WINDOW = 128
B_DILATIONS = (1, 4, 16)
B_GROUPS = 3
B_HEADS_PER_GROUP = 4
B_HALF_SPAN = 64
N_BUCKETS = 32
MAX_DISTANCE = 1024
N_ATTN_HEADS = A_Q_HEADS + B_GROUPS * B_HEADS_PER_GROUP

PEER_HEADS = 8
PEER_NKEYS = 128
PEER_EXPERTS = PEER_NKEYS * PEER_NKEYS
PEER_QDIM = 256
PEER_TOPK = 16
PEER_PICKS = PEER_HEADS * PEER_TOPK
EPS = 1e-6

A_Q_W = A_Q_HEADS * HEAD_DIM
A_KV_W = A_KV_HEADS * HEAD_DIM
B_W = B_GROUPS * B_HEADS_PER_GROUP * HEAD_DIM
B_OUT_W = B_HEADS_PER_GROUP * HEAD_DIM
IN_WIDTH = A_Q_W + 2 * A_KV_W + 3 * B_W + 2 * D_MODEL
QA_CB = 0
KA_CB = A_Q_W // LANES
VA_CB = (A_Q_W + A_KV_W) // LANES
QB_CB = (A_Q_W + 2 * A_KV_W) // LANES
KB_CB = QB_CB + B_W // LANES
VB_CB = KB_CB + B_W // LANES
GA_OFF = A_Q_W + 2 * A_KV_W + 3 * B_W
GB_OFF = GA_OFF + D_MODEL
IN_CB = IN_WIDTH // LANES

NEG = -1e30
ATT_SCALE = HEAD_DIM ** -0.5
ATT_QT = 128
ATT_GROUP = 8

MIB = 1024 * 1024


def _params(sem, vmem_mib):
    return pltpu.CompilerParams(dimension_semantics=sem, vmem_limit_bytes=vmem_mib * MIB)


def _rms(x, g):
    return x * lax.rsqrt(jnp.mean(x * x, axis=-1, keepdims=True) + EPS) * g


def _inproj_kernel(x_ref, g_ref, w_ref, o_ref, h_scr):
    @pl.when(pl.program_id(1) == 0)
    def _():
        h_scr[...] = _rms(x_ref[...], g_ref[...]).astype(BF16)

    o_ref[...] = jnp.dot(h_scr[...], w_ref[...], preferred_element_type=F32).astype(o_ref.dtype)


def _inproj(x2, g, w, tm=1024, tn=1024):
    n = w.shape[1]
    return pl.pallas_call(
        _inproj_kernel,
        out_shape=jax.ShapeDtypeStruct((N_TOK, n), BF16),
        grid=(N_TOK // tm, n // tn),
        in_specs=[
            pl.BlockSpec((tm, D_MODEL), lambda i, j: (i, 0)),
            pl.BlockSpec((1, D_MODEL), lambda i, j: (0, 0)),
            pl.BlockSpec((D_MODEL, tn), lambda i, j: (0, j)),
        ],
        out_specs=pl.BlockSpec((tm, tn), lambda i, j: (i, j)),
        scratch_shapes=[pltpu.VMEM((tm, D_MODEL), BF16)],
        compiler_params=_params(("parallel", "arbitrary"), 40),
        name="inproj",
    )(x2, g, w)


def _t5_bucket_np(rel):
    half = N_BUCKETS // 2
    max_exact = half // 2
    ret = np.where(rel > 0, half, 0)
    n = np.abs(rel)
    nf = np.maximum(n, 1).astype(np.float64)
    large = max_exact + (np.log(nf / max_exact) / math.log(MAX_DISTANCE / max_exact) * (half - max_exact)).astype(np.int64)
    large = np.minimum(large, half - 1)
    return (ret + np.where(n < max_exact, n, large)).astype(np.int32)


def _bucket_matrix(hw, dist_scale):
    w = ATT_QT + 2 * hw
    delta = (np.arange(w)[None, :] - hw) - np.arange(ATT_QT)[:, None]
    bkt = _t5_bucket_np(delta * dist_scale)
    return np.where(np.abs(delta) <= hw, bkt, -1).astype(np.int32)


def _residues_per_trip(dil):
    tiles = SEQ // dil // ATT_QT
    return max(1, min(dil, ATT_GROUP // tiles))


def _attn_kernel(tab_ref, sink_ref, q_ref, k_ref, v_ref, bkt_ref, *rest,
                 dil, hw, head_base, has_sink, with_lse):
    n_out = 2 if with_lse else 1
    outs, scr = rest[:n_out], rest[n_out:]
    o_ref = outs[0]
    kpad, vpad, bias_scr = scr[:3]
    if dil > 1:
        q32, k32, v32, o32, qres = scr[3:8]
        if with_lse:
            l32 = scr[8]
    seq = SEQ // dil
    qt = ATT_QT
    win = qt + 2 * hw
    n_res = _residues_per_trip(dil)
    head = head_base + pl.program_id(1)

    bkt = bkt_ref[...]
    bias = jnp.full((qt, win), NEG, F32)
    for b in range(N_BUCKETS):
        bias = jnp.where(bkt == b, tab_ref[b, head], bias)
    bias_scr[...] = bias

    zeros = jnp.zeros((hw, HEAD_DIM), BF16)
    for pad_ref in (kpad, vpad):
        for u in range(n_res):
            pad_ref[u, 0:hw, :] = zeros
            pad_ref[u, hw + seq:hw + seq + hw, :] = zeros
    if dil == 1:
        kpad[0, hw:hw + seq, :] = k_ref[0]
        vpad[0, hw:hw + seq, :] = v_ref[0]
    else:
        q32[...] = q_ref[0].astype(F32)
        k32[...] = k_ref[0].astype(F32)
        v32[...] = v_ref[0].astype(F32)

    def scores(q0, q, kw):
        s = lax.dot_general(q, kw, (((1,), (1,)), ((), ())), preferred_element_type=F32)
        s = s * ATT_SCALE + bias_scr[...]
        kpos = q0 - hw + lax.broadcasted_iota(I32, (qt, win), 1)
        return jnp.where((kpos >= 0) & (kpos < seq), s, NEG)

    def softmax(s):
        m = jnp.max(s, axis=-1, keepdims=True)
        if has_sink:
            sk = sink_ref[head]
            m = jnp.maximum(m, sk)
        p = jnp.exp(s - m)
        den = jnp.sum(p, axis=-1, keepdims=True)
        if has_sink:
            den = den + jnp.exp(sk - m)
        return p.astype(BF16), m, den

    def values(p, m, den, vw):
        o = jnp.dot(p, vw, preferred_element_type=F32) / den
        lse = jnp.broadcast_to(m + jnp.log(den), (qt, HEAD_DIM)) if with_lse else None
        return o, lse

    group = min(ATT_GROUP, seq // qt)

    def residues(g, carry):
        r0 = g * n_res
        if dil > 1:
            for u in range(n_res):
                kpad[u, hw:hw + seq, :] = k32[pl.ds(r0 + u, seq, stride=dil), :].astype(BF16)
                vpad[u, hw:hw + seq, :] = v32[pl.ds(r0 + u, seq, stride=dil), :].astype(BF16)
                qres[u] = q32[pl.ds(r0 + u, seq, stride=dil), :].astype(BF16)

        def body(t, c):
            work = [(u, pl.multiple_of((t * group + i) * qt, qt)) for u in range(n_res) for i in range(group)]
            q_src = (lambda u: qres.at[u]) if dil > 1 else (lambda u: q_ref.at[0])
            s_all = [scores(q0, q_src(u)[pl.ds(q0, qt), :], kpad[u, pl.ds(q0, win), :]) for u, q0 in work]
            p_all = [softmax(s) for s in s_all]
            done = [values(*pmd, vpad[u, pl.ds(q0, win), :]) for (u, q0), pmd in zip(work, p_all)]
            for (u, q0), (o, lse) in zip(work, done):
                if dil > 1:
                    o32[pl.ds(r0 + u + q0 * dil, qt, stride=dil), :] = o
                    if with_lse:
                        l32[pl.ds(r0 + u + q0 * dil, qt, stride=dil), :] = lse
                else:
                    o_ref[0, pl.ds(q0, qt), :] = o.astype(o_ref.dtype)
                    if with_lse:
                        outs[1][0, pl.ds(q0, qt), :] = lse
            return c

        lax.fori_loop(0, seq // (qt * group), body, 0)
        return carry

    if dil > 1:
        lax.fori_loop(0, dil // n_res, residues, 0)
        o_ref[0] = o32[...].astype(o_ref.dtype)
        if with_lse:
            outs[1][0] = l32[...]
    else:
        residues(0, 0)


def _attention(proj, rel_bias, sink, *, dil, hw, q_cb, k_cb, v_cb, n_heads, kv_group,
               head_base, has_sink, with_lse):
    seq = SEQ // dil
    pv = proj.reshape(BATCH, SEQ, IN_WIDTH)
    bkt = jnp.asarray(_bucket_matrix(hw, dil))
    win = ATT_QT + 2 * hw
    out_w = n_heads * HEAD_DIM
    blk = (1, SEQ, HEAD_DIM)
    o_spec = pl.BlockSpec(blk, lambda b, h: (b, 0, h))
    out_shape = [jax.ShapeDtypeStruct((BATCH, SEQ, out_w), BF16)]
    out_specs = [o_spec]
    if with_lse:
        out_shape.append(jax.ShapeDtypeStruct((BATCH, SEQ, out_w), F32))
        out_specs.append(o_spec)
    n_res = _residues_per_trip(dil)
    scratch = [pltpu.VMEM((n_res, seq + 2 * hw, HEAD_DIM), BF16),
               pltpu.VMEM((n_res, seq + 2 * hw, HEAD_DIM), BF16),
               pltpu.VMEM((ATT_QT, win), F32)]
    if dil > 1:
        scratch += [pltpu.VMEM((SEQ, HEAD_DIM), F32)] * 4 + [pltpu.VMEM((n_res, seq, HEAD_DIM), BF16)]
        if with_lse:
            scratch.append(pltpu.VMEM((SEQ, HEAD_DIM), F32))
    kern = functools.partial(_attn_kernel, dil=dil, hw=hw, head_base=head_base,
                             has_sink=has_sink, with_lse=with_lse)
    outs = pl.pallas_call(
        kern,
        out_shape=out_shape,
        grid=(BATCH, n_heads),
        in_specs=[
            pl.BlockSpec(memory_space=pltpu.SMEM),
            pl.BlockSpec(memory_space=pltpu.SMEM),
            pl.BlockSpec(blk, lambda b, h: (b, 0, q_cb + h)),
            pl.BlockSpec(blk, lambda b, h: (b, 0, k_cb + h // kv_group)),
            pl.BlockSpec(blk, lambda b, h: (b, 0, v_cb + h // kv_group)),
            pl.BlockSpec((ATT_QT, win), lambda b, h: (0, 0)),
        ],
        out_specs=out_specs,
        scratch_shapes=scratch,
        compiler_params=_params(("parallel", "arbitrary"), 40),
        name=f"attn_d{dil}_h{head_base}",
    )(rel_bias, sink, pv, pv, pv, bkt)
    return [o.reshape(N_TOK, out_w) for o in outs]


def _outproj_kernel(oa_ref, o0_ref, o1_ref, o2_ref, l0_ref, l1_ref, l2_ref, ga_ref, gb_ref,
                    x_ref, woa_ref, wob_ref, wout_ref, gn_ref, x1_ref, h2_ref):
    l0, l1, l2 = l0_ref[...], l1_ref[...], l2_ref[...]
    mx = jnp.maximum(jnp.maximum(l0, l1), l2)
    e0, e1, e2 = jnp.exp(l0 - mx), jnp.exp(l1 - mx), jnp.exp(l2 - mx)
    den = e0 + e1 + e2
    ob = ((e0 / den) * o0_ref[...].astype(F32) + (e1 / den) * o1_ref[...].astype(F32)
          + (e2 / den) * o2_ref[...].astype(F32))
    ya = jnp.dot(oa_ref[...], woa_ref[...], preferred_element_type=F32)
    yb = jnp.dot(ob.astype(BF16), wob_ref[...], preferred_element_type=F32)
    merged = (jax.nn.sigmoid(ga_ref[...].astype(F32)) * ya
              + jax.nn.sigmoid(gb_ref[...].astype(F32)) * yb)
    x1 = x_ref[...] + jnp.dot(merged.astype(BF16), wout_ref[...], preferred_element_type=F32)
    x1_ref[...] = x1
    h2_ref[...] = _rms(x1, gn_ref[...]).astype(BF16)


def _outproj(oa, obs, lses, proj, x2, w_oa, w_ob, w_out, g_ffn, tm=256):
    row = lambda w: pl.BlockSpec((tm, w), lambda i: (i, 0))
    const = lambda shape: pl.BlockSpec(shape, lambda i: (0, 0), pipeline_mode=pl.Buffered(1))
    return pl.pallas_call(
        _outproj_kernel,
        out_shape=[jax.ShapeDtypeStruct((N_TOK, D_MODEL), F32),
                   jax.ShapeDtypeStruct((N_TOK, D_MODEL), BF16)],
        grid=(N_TOK // tm,),
        in_specs=[row(A_Q_W), row(B_OUT_W), row(B_OUT_W), row(B_OUT_W),
                  row(B_OUT_W), row(B_OUT_W), row(B_OUT_W),
                  pl.BlockSpec((tm, D_MODEL), lambda i: (i, GA_OFF // D_MODEL)),
                  pl.BlockSpec((tm, D_MODEL), lambda i: (i, GB_OFF // D_MODEL)),
                  row(D_MODEL),
                  const((A_Q_W, D_MODEL)), const((B_OUT_W, D_MODEL)), const((D_MODEL, D_MODEL)),
                  const((1, D_MODEL))],
        out_specs=[row(D_MODEL), row(D_MODEL)],
        compiler_params=_params(("parallel",), 48),
        name="outproj",
    )(oa, *obs, *lses, proj, proj, x2, w_oa, w_ob, w_out, g_ffn)


_FAR = 1024.0
RT_SLICE = 256
UP_CHUNK = 256
UP_ROWS = 256

_LANE_GATHER = lax.GatherDimensionNumbers(
    offset_dims=(), collapsed_slice_dims=(1,), start_index_map=(1,),
    operand_batching_dims=(0,), start_indices_batching_dims=(0,))


def _tree(op, xs):
    xs = list(xs)
    while len(xs) > 1:
        nxt = [op(xs[i], xs[i + 1]) for i in range(0, len(xs) - 1, 2)]
        if len(xs) % 2:
            nxt.append(xs[-1])
        xs = nxt
    return xs[0]


def _all_sublanes(op, x):
    for shift in (4, 2, 1):
        x = op(x, pltpu.roll(x, shift, axis=0))
    return x


def _pop_max(tiles, sub_iota, payload=None):
    m = _all_sublanes(jnp.maximum, _tree(jnp.maximum, tiles))
    first = _tree(jnp.minimum, [jnp.where(t == m, float(SUBLANES * v), _FAR) for v, t in enumerate(tiles)])
    row = _all_sublanes(jnp.minimum, first + sub_iota)
    base = row - sub_iota
    hits = [base == float(SUBLANES * v) for v in range(len(tiles))]
    if payload is None:
        val = row
    else:
        val = _all_sublanes(jnp.maximum, _tree(jnp.maximum, [jnp.where(h, p, -1.0)
                                                            for h, p in zip(hits, payload)]))
    return m, val, [jnp.where(h, -jnp.inf, t) for h, t in zip(hits, tiles)]


def _route_scores(h, q_ref, keys_ref, s_scr):
    for c in range(2):
        col = pl.multiple_of((2 * h + c) * LANES, LANES)
        s_scr[c] = lax.dot_general(keys_ref[2 * h + c], q_ref[:, pl.ds(col, LANES)],
                                   (((1,), (1,)), ((), ())), preferred_element_type=F32)


def _route_topk(s_scr, ts_scr, ti_scr, bs_scr, et_ref, gt_ref):
    k = PEER_TOPK
    nk = float(PEER_NKEYS)
    sub_iota = lax.broadcasted_iota(I32, (SUBLANES, LANES), 0).astype(F32)
    halves = [slice(i * LANES, (i + 1) * LANES) for i in range(RT_SLICE // LANES)]

    chains = []
    for c in range(2):
        for lanes in halves:
            tiles = [s_scr[c, SUBLANES * v:SUBLANES * (v + 1), lanes] for v in range(PEER_NKEYS // SUBLANES)]
            chains.append([c, lanes, tiles])
    for kk in range(k):
        for chain in chains:
            c, lanes, tiles = chain
            m, row, chain[2] = _pop_max(tiles, sub_iota)
            ts_scr[c, kk:kk + 1, lanes] = m[0:1]
            ti_scr[c, kk:kk + 1, lanes] = row[0:1]

    def candidates(lanes):
        s2 = [ts_scr[1, 0:8, lanes], ts_scr[1, 8:16, lanes]]
        i2 = [ti_scr[1, 0:8, lanes], ti_scr[1, 8:16, lanes]]
        s1 = lambda k1: ts_scr[0, k1:k1 + 1, lanes]
        i1 = lambda k1: ti_scr[0, k1:k1 + 1, lanes] * nk
        tiles = [s1(0) + s2[0], s1(0) + s2[1]]
        pay = [i1(0) + i2[0], i1(0) + i2[1]]
        for k1 in range(1, 8):
            allowed = k // (k1 + 1)
            t = s1(k1) + s2[0]
            tiles.append(t if allowed >= SUBLANES else jnp.where(sub_iota < float(allowed), t, -jnp.inf))
            pay.append(i1(k1) + i2[0])
        tiles.append(ts_scr[0, 8:16, lanes] + ts_scr[1, 0:1, lanes])
        pay.append(ti_scr[0, 8:16, lanes] * nk + ti_scr[1, 0:1, lanes])
        return tiles, pay

    cands = [candidates(lanes) for lanes in halves]
    tiles = [c[0] for c in cands]
    for kk in range(k):
        for i, lanes in enumerate(halves):
            m, expert, tiles[i] = _pop_max(tiles[i], sub_iota, payload=cands[i][1])
            bs_scr[kk:kk + 1, lanes] = m[0:1]
            et_ref[kk:kk + 1, lanes] = expert[0:1]
    for lanes in halves:
        bs = bs_scr[:, lanes]
        ex = jnp.exp(bs - jnp.max(bs, axis=0, keepdims=True))
        gt_ref[:, lanes] = ex / jnp.sum(ex, axis=0, keepdims=True)


def _expert_up(j, h_ref, u_ref, e_ref, act_ref, *, te):
    tb = h_ref.shape[0]
    u = [u_ref[c * UP_CHUNK:(c + 1) * UP_CHUNK, :].astype(BF16) for c in range(te // UP_CHUNK)]
    for m in range(tb // UP_ROWS):
        rows = slice(m * UP_ROWS, (m + 1) * UP_ROWS)
        e = e_ref[rows, :]
        row = e >> 7
        col = (e & (PEER_NKEYS - 1))[..., None]
        acc = act_ref[rows, :]
        h = h_ref[rows, :]
        for c in range(te // UP_CHUNK):
            dense = lax.dot_general(h, u[c], (((1,), (1,)), ((), ())), preferred_element_type=F32)
            for q in range(UP_CHUNK // LANES):
                got = lax.gather(dense[:, q * LANES:(q + 1) * LANES], col, _LANE_GATHER, (1, 1),
                                 mode=lax.GatherScatterMode.PROMISE_IN_BOUNDS)
                acc = jnp.where(row == (j * te + c * UP_CHUNK) // LANES + q, got, acc)
        act_ref[rows, :] = acc


def _route_kernel(h_ref, wq_ref, keys_ref, e_ref, g_ref,
                  q_scr, s_scr, ts_scr, ti_scr, bs_scr, et_scr, gt_scr):
    q_scr[...] = jnp.dot(h_ref[...], wq_ref[...], preferred_element_type=F32).astype(BF16)

    def head(h, carry):
        _route_scores(h, q_scr, keys_ref, s_scr)
        _route_topk(s_scr, ts_scr, ti_scr, bs_scr, et_scr.at[h], gt_scr.at[h])
        return carry

    lax.fori_loop(0, PEER_HEADS, head, 0)

    for half in range(RT_SLICE // LANES):
        lanes = slice(half * LANES, (half + 1) * LANES)
        et = jnp.concatenate([et_scr[h, :, lanes] for h in range(PEER_HEADS)], axis=0)
        gt = jnp.concatenate([gt_scr[h, :, lanes] for h in range(PEER_HEADS)], axis=0)
        e_ref[lanes, :] = et.T.astype(I32)
        g_ref[lanes, :] = gt.T


def _route(h2, wq, keys):
    k = PEER_TOPK
    tok = pl.BlockSpec((RT_SLICE, PEER_PICKS), lambda i: (i, 0))
    return pl.pallas_call(
        _route_kernel,
        out_shape=[jax.ShapeDtypeStruct((N_TOK, PEER_PICKS), I32),
                   jax.ShapeDtypeStruct((N_TOK, PEER_PICKS), F32)],
        grid=(N_TOK // RT_SLICE,),
        in_specs=[pl.BlockSpec((RT_SLICE, D_MODEL), lambda i: (i, 0)),
                  pl.BlockSpec((D_MODEL, PEER_HEADS * PEER_QDIM), lambda i: (0, 0),
                               pipeline_mode=pl.Buffered(1)),
                  pl.BlockSpec((PEER_HEADS * 2, PEER_NKEYS, PEER_QDIM // 2), lambda i: (0, 0, 0))],
        out_specs=[tok, tok],
        scratch_shapes=[
            pltpu.VMEM((RT_SLICE, PEER_HEADS * PEER_QDIM), BF16),
            pltpu.VMEM((2, PEER_NKEYS, RT_SLICE), F32),
            pltpu.VMEM((2, k, RT_SLICE), F32),
            pltpu.VMEM((2, k, RT_SLICE), F32),
            pltpu.VMEM((k, RT_SLICE), F32),
            pltpu.VMEM((PEER_HEADS, k, RT_SLICE), F32),
            pltpu.VMEM((PEER_HEADS, k, RT_SLICE), F32),
        ],
        compiler_params=_params(("parallel",), 32),
        name="peer_route",
    )(h2, wq, keys)


def _peer_up_kernel(h_ref, u_ref, e_ref, v_ref, act_ref, vb_ref, *, te):
    i, j = pl.program_id(0), pl.program_id(1)

    @pl.when(j == 0)
    def _():
        act_ref[...] = jnp.zeros_like(act_ref)

    @pl.when(i == 0)
    def _():
        vb_ref[...] = v_ref[...].astype(BF16)

    _expert_up(j, h_ref, u_ref, e_ref, act_ref, te=te)


def _peer_up(h2, u, e, v, tb=2048, te=512):
    nj = PEER_EXPERTS // te
    v_tile = pl.BlockSpec((te, D_MODEL), lambda i, j: (jnp.where(i == 0, j, nj - 1), 0))
    return pl.pallas_call(
        functools.partial(_peer_up_kernel, te=te),
        out_shape=[jax.ShapeDtypeStruct((N_TOK, PEER_PICKS), F32),
                   jax.ShapeDtypeStruct((PEER_EXPERTS, D_MODEL), BF16)],
        grid=(N_TOK // tb, nj),
        in_specs=[pl.BlockSpec((tb, D_MODEL), lambda i, j: (i, 0)),
                  pl.BlockSpec((te, D_MODEL), lambda i, j: (j, 0)),
                  pl.BlockSpec((tb, PEER_PICKS), lambda i, j: (i, 0)),
                  v_tile],
        out_specs=[pl.BlockSpec((tb, PEER_PICKS), lambda i, j: (i, 0)), v_tile],
        compiler_params=_params(("arbitrary", "arbitrary"), 56),
        name="peer_up",
    )(h2, u, e, v)


def _peer_coef_kernel(e_ref, g_ref, act_ref, p_ref, row_scr, col_scr, w_scr, c_scr, *, tp):
    e = e_ref[...]
    row_scr[...] = (e >> 7).astype(F32)
    col_scr[...] = (e & (PEER_NKEYS - 1)).astype(F32)
    act = act_ref[...]
    w_scr[...] = g_ref[...] * (0.5 * act * (1.0 + lax.erf(act * math.sqrt(0.5))))
    sub = 16
    iota = lax.broadcasted_iota(I32, (PEER_NKEYS // sub, sub, PEER_PICKS), 0) * sub \
        + lax.broadcasted_iota(I32, (PEER_NKEYS // sub, sub, PEER_PICKS), 1)
    iota = iota.astype(F32).astype(BF16)
    one = jnp.ones((), BF16)
    zero = jnp.zeros((), BF16)

    def bcast(ref, t):
        return jnp.broadcast_to(ref[pl.ds(t, 1), :], (sub, PEER_PICKS)).astype(BF16)[None]

    def products(t0):
        for t in range(sub):
            rows, cols, w = bcast(row_scr, t0 + t), bcast(col_scr, t0 + t), bcast(w_scr, t0 + t)
            left = jnp.where(iota == rows, one, zero).reshape(PEER_NKEYS, PEER_PICKS)
            right = jnp.where(iota == cols, w, zero).reshape(PEER_NKEYS, PEER_PICKS)
            c_scr[t0 + t] = lax.dot_general(left, right, (((1,), (1,)), ((), ())),
                                            preferred_element_type=F32)

    def regroup(t0):
        sw = jnp.swapaxes(c_scr[pl.ds(t0, sub)], 0, 1)
        for k1 in range(PEER_NKEYS):
            p_ref[pl.ds(t0, sub), k1 * PEER_NKEYS:(k1 + 1) * PEER_NKEYS] = sw[k1].astype(p_ref.dtype)

    def block(blk, carry):
        base = pl.multiple_of(blk * COEF_UNROLL, COEF_UNROLL)
        for t0 in range(0, COEF_UNROLL, sub):
            products(base + t0)
            if t0:
                regroup(base + t0 - sub)
        regroup(base + COEF_UNROLL - sub)
        return carry

    lax.fori_loop(0, tp // COEF_UNROLL, block, 0)


COEF_UNROLL = 64


def _peer_coef(e, g, act, tp=128):
    tok = pl.BlockSpec((tp, PEER_PICKS), lambda i: (i, 0))
    return pl.pallas_call(
        functools.partial(_peer_coef_kernel, tp=tp),
        out_shape=jax.ShapeDtypeStruct((N_TOK, PEER_EXPERTS), BF16),
        grid=(N_TOK // tp,),
        in_specs=[tok, tok, tok],
        out_specs=pl.BlockSpec((tp, PEER_EXPERTS), lambda i: (i, 0)),
        scratch_shapes=[pltpu.VMEM((tp, PEER_PICKS), F32),
                        pltpu.VMEM((tp, PEER_PICKS), F32),
                        pltpu.VMEM((tp, PEER_PICKS), F32),
                        pltpu.VMEM((tp, PEER_NKEYS, PEER_NKEYS), F32)],
        compiler_params=_params(("parallel",), 32),
        name="peer_coef",
    )(e, g, act)


def _peer_down_kernel(p_ref, v_ref, x_ref, g_ref, o_ref):
    j = pl.program_id(1)

    @pl.when(j == 0)
    def _():
        o_ref[...] = x_ref[...]

    o_ref[...] += jnp.dot(p_ref[...], v_ref[...], preferred_element_type=F32)

    @pl.when(j == pl.num_programs(1) - 1)
    def _():
        o_ref[...] = _rms(o_ref[...], g_ref[...])


def _peer_down(coef, v, x1, g_final, tm=1024, tk=2048):
    return pl.pallas_call(
        _peer_down_kernel,
        out_shape=jax.ShapeDtypeStruct((N_TOK, D_MODEL), F32),
        grid=(N_TOK // tm, PEER_EXPERTS // tk),
        in_specs=[pl.BlockSpec((tm, tk), lambda i, j: (i, j)),
                  pl.BlockSpec((tk, D_MODEL), lambda i, j: (j, 0)),
                  pl.BlockSpec((tm, D_MODEL), lambda i, j: (i, 0), pipeline_mode=pl.Buffered(1)),
                  pl.BlockSpec((1, D_MODEL), lambda i, j: (0, 0))],
        out_specs=pl.BlockSpec((tm, D_MODEL), lambda i, j: (i, 0)),
        compiler_params=_params(("parallel", "arbitrary"), 60),
        name="peer_down",
    )(coef, v, x1, g_final)


def kernel(x, rel_bias, norm_mix_g, w_in, sink_a, w_oa, w_ob, w_out, norm_ffn_g,
           peer_wq, peer_keys, peer_u, peer_v, norm_final_g):
    assert x.shape == (BATCH, SEQ, D_MODEL) and w_in.shape[0] == 1
    x2 = x.reshape(N_TOK, D_MODEL)
    proj = _inproj(x2, norm_mix_g, w_in[0].astype(BF16))

    (oa,) = _attention(proj, rel_bias, sink_a[0], dil=1, hw=A_HALF_WINDOW,
                       q_cb=QA_CB, k_cb=KA_CB, v_cb=VA_CB, n_heads=A_Q_HEADS, kv_group=A_GROUP,
                       head_base=0, has_sink=True, with_lse=False)
    obs, lses = [], []
    for gi, dil in enumerate(B_DILATIONS):
        off = gi * B_HEADS_PER_GROUP
        o, lse = _attention(proj, rel_bias, sink_a[0], dil=dil, hw=B_HALF_SPAN,
                            q_cb=QB_CB + off, k_cb=KB_CB + off, v_cb=VB_CB + off,
                            n_heads=B_HEADS_PER_GROUP, kv_group=1,
                            head_base=A_Q_HEADS + off, has_sink=False, with_lse=True)
        obs.append(o)
        lses.append(lse)

    x1, h2 = _outproj(oa, obs, lses, proj, x2, w_oa[0].astype(BF16), w_ob[0].astype(BF16),
                      w_out[0].astype(BF16), norm_ffn_g)

    keys = peer_keys[0].reshape(PEER_HEADS * 2, PEER_NKEYS, PEER_QDIM // 2).astype(BF16)
    e, gate = _route(h2, peer_wq[0].astype(BF16), keys)
    act, v_bf16 = _peer_up(h2, peer_u[0], e, peer_v[0])
    coef = _peer_coef(e, gate, act)
    out = _peer_down(coef, v_bf16, x1, norm_final_g.reshape(1, D_MODEL))
    return out.reshape(BATCH, SEQ, D_MODEL)
```

```python
import functools
import math

import numpy as np
import jax
import jax.numpy as jnp
from jax import lax
from jax.experimental import pallas as pl
from jax.experimental.pallas import tpu as pltpu

F32 = jnp.float32
BF16 = jnp.bfloat16
I32 = jnp.int32

D_MODEL = 2048
BATCH = 2
SEQ = 4096
N_TOK = BATCH * SEQ
HEAD_DIM = 128
LANES = 128
SUBLANES = 8

A_Q_HEADS = 8
A_KV_HEADS = 2
A_GROUP = A_Q_HEADS // A_KV_HEADS
A_HALF_WINDOW = 128
B_DILATIONS = (1, 4, 16)
B_GROUPS = 3
B_HEADS_PER_GROUP = 4
B_HALF_SPAN = 64
N_BUCKETS = 32
MAX_DISTANCE = 1024
N_ATTN_HEADS = A_Q_HEADS + B_GROUPS * B_HEADS_PER_GROUP

PEER_HEADS = 8
PEER_NKEYS = 128
PEER_EXPERTS = PEER_NKEYS * PEER_NKEYS
PEER_QDIM = 256
PEER_TOPK = 16
PEER_PICKS = PEER_HEADS * PEER_TOPK
EPS = 1e-6

A_Q_W = A_Q_HEADS * HEAD_DIM
A_KV_W = A_KV_HEADS * HEAD_DIM
B_W = B_GROUPS * B_HEADS_PER_GROUP * HEAD_DIM
B_OUT_W = B_HEADS_PER_GROUP * HEAD_DIM
IN_WIDTH = A_Q_W + 2 * A_KV_W + 3 * B_W + 2 * D_MODEL
QA_CB = 0
KA_CB = A_Q_W // LANES
VA_CB = (A_Q_W + A_KV_W) // LANES
QB_CB = (A_Q_W + 2 * A_KV_W) // LANES
KB_CB = QB_CB + B_W // LANES
VB_CB = KB_CB + B_W // LANES
GA_OFF = A_Q_W + 2 * A_KV_W + 3 * B_W
GB_OFF = GA_OFF + D_MODEL
IN_CB = IN_WIDTH // LANES

NEG = -1e30
ATT_SCALE = HEAD_DIM ** -0.5
ATT_QT = 128
ATT_GROUP = 8

MIB = 1024 * 1024


def _params(sem, vmem_mib):
    return pltpu.CompilerParams(dimension_semantics=sem, vmem_limit_bytes=vmem_mib * MIB)


def _rms(x, g):
    return x * lax.rsqrt(jnp.mean(x * x, axis=-1, keepdims=True) + EPS) * g


def _inproj_kernel(x_ref, g_ref, w_ref, o_ref, h_scr):
    @pl.when(pl.program_id(1) == 0)
    def _():
        h_scr[...] = _rms(x_ref[...], g_ref[...]).astype(BF16)

    o_ref[...] = jnp.dot(h_scr[...], w_ref[...], preferred_element_type=F32).astype(o_ref.dtype)


def _inproj(x2, g, w, tm=1024, tn=1024):
    n = w.shape[1]
    return pl.pallas_call(
        _inproj_kernel,
        out_shape=jax.ShapeDtypeStruct((N_TOK, n), BF16),
        grid=(N_TOK // tm, n // tn),
        in_specs=[
            pl.BlockSpec((tm, D_MODEL), lambda i, j: (i, 0)),
            pl.BlockSpec((1, D_MODEL), lambda i, j: (0, 0)),
            pl.BlockSpec((D_MODEL, tn), lambda i, j: (0, j)),
        ],
        out_specs=pl.BlockSpec((tm, tn), lambda i, j: (i, j)),
        scratch_shapes=[pltpu.VMEM((tm, D_MODEL), BF16)],
        compiler_params=_params(("parallel", "arbitrary"), 40),
        name="inproj",
    )(x2, g, w)


def _t5_bucket_np(rel):
    half = N_BUCKETS // 2
    max_exact = half // 2
    ret = np.where(rel > 0, half, 0)
    n = np.abs(rel)
    nf = np.maximum(n, 1).astype(np.float64)
    large = max_exact + (np.log(nf / max_exact) / math.log(MAX_DISTANCE / max_exact) * (half - max_exact)).astype(np.int64)
    large = np.minimum(large, half - 1)
    return (ret + np.where(n < max_exact, n, large)).astype(np.int32)


def _bucket_matrix(hw, dist_scale):
    w = ATT_QT + 2 * hw
    delta = (np.arange(w)[None, :] - hw) - np.arange(ATT_QT)[:, None]
    bkt = _t5_bucket_np(delta * dist_scale)
    return np.where(np.abs(delta) <= hw, bkt, -1).astype(np.int32)


def _residues_per_trip(dil):
    tiles = SEQ // dil // ATT_QT
    return max(1, min(dil, ATT_GROUP // tiles))


def _attn_kernel(tab_ref, sink_ref, q_ref, k_ref, v_ref, bkt_ref, *rest,
                 dil, hw, head_base, has_sink, with_lse):
    n_out = 2 if with_lse else 1
    outs, scr = rest[:n_out], rest[n_out:]
    o_ref = outs[0]
    kpad, vpad, bias_scr = scr[:3]
    if dil > 1:
        q32, k32, v32, o32, qres = scr[3:8]
        if with_lse:
            l32 = scr[8]
    seq = SEQ // dil
    qt = ATT_QT
    win = qt + 2 * hw
    n_res = _residues_per_trip(dil)
    head = head_base + pl.program_id(1)

    bkt = bkt_ref[...]
    bias = jnp.full((qt, win), NEG, F32)
    for b in range(N_BUCKETS):
        bias = jnp.where(bkt == b, tab_ref[b, head], bias)
    bias_scr[...] = bias

    zeros = jnp.zeros((hw, HEAD_DIM), BF16)
    for pad_ref in (kpad, vpad):
        for u in range(n_res):
            pad_ref[u, 0:hw, :] = zeros
            pad_ref[u, hw + seq:hw + seq + hw, :] = zeros
    if dil == 1:
        kpad[0, hw:hw + seq, :] = k_ref[0]
        vpad[0, hw:hw + seq, :] = v_ref[0]
    else:
        q32[...] = q_ref[0].astype(F32)
        k32[...] = k_ref[0].astype(F32)
        v32[...] = v_ref[0].astype(F32)

    def scores(q0, q, kw):
        s = lax.dot_general(q, kw, (((1,), (1,)), ((), ())), preferred_element_type=F32)
        s = s * ATT_SCALE + bias_scr[...]
        kpos = q0 - hw + lax.broadcasted_iota(I32, (qt, win), 1)
        return jnp.where((kpos >= 0) & (kpos < seq), s, NEG)

    def softmax(s):
        m = jnp.max(s, axis=-1, keepdims=True)
        if has_sink:
            sk = sink_ref[head]
            m = jnp.maximum(m, sk)
        p = jnp.exp(s - m)
        den = jnp.sum(p, axis=-1, keepdims=True)
        if has_sink:
            den = den + jnp.exp(sk - m)
        return p.astype(BF16), m, den

    def values(p, m, den, vw):
        o = jnp.dot(p, vw, preferred_element_type=F32) / den
        lse = jnp.broadcast_to(m + jnp.log(den), (qt, HEAD_DIM)) if with_lse else None
        return o, lse

    group = min(ATT_GROUP, seq // qt)

    def residues(g, carry):
        r0 = g * n_res
        if dil > 1:
            for u in range(n_res):
                kpad[u, hw:hw + seq, :] = k32[pl.ds(r0 + u, seq, stride=dil), :].astype(BF16)
                vpad[u, hw:hw + seq, :] = v32[pl.ds(r0 + u, seq, stride=dil), :].astype(BF16)
                qres[u] = q32[pl.ds(r0 + u, seq, stride=dil), :].astype(BF16)

        def body(t, c):
            work = [(u, pl.multiple_of((t * group + i) * qt, qt)) for u in range(n_res) for i in range(group)]
            q_src = (lambda u: qres.at[u]) if dil > 1 else (lambda u: q_ref.at[0])
            s_all = [scores(q0, q_src(u)[pl.ds(q0, qt), :], kpad[u, pl.ds(q0, win), :]) for u, q0 in work]
            p_all = [softmax(s) for s in s_all]
            done = [values(*pmd, vpad[u, pl.ds(q0, win), :]) for (u, q0), pmd in zip(work, p_all)]
            for (u, q0), (o, lse) in zip(work, done):
                if dil > 1:
                    o32[pl.ds(r0 + u + q0 * dil, qt, stride=dil), :] = o
                    if with_lse:
                        l32[pl.ds(r0 + u + q0 * dil, qt, stride=dil), :] = lse
                else:
                    o_ref[0, pl.ds(q0, qt), :] = o.astype(o_ref.dtype)
                    if with_lse:
                        outs[1][0, pl.ds(q0, qt), :] = lse
            return c

        lax.fori_loop(0, seq // (qt * group), body, 0)
        return carry

    if dil > 1:
        lax.fori_loop(0, dil // n_res, residues, 0)
        o_ref[0] = o32[...].astype(o_ref.dtype)
        if with_lse:
            outs[1][0] = l32[...]
    else:
        residues(0, 0)


def _attention(proj, rel_bias, sink, *, dil, hw, q_cb, k_cb, v_cb, n_heads, kv_group,
               head_base, has_sink, with_lse):
    seq = SEQ // dil
    pv = proj.reshape(BATCH, SEQ, IN_WIDTH)
    bkt = jnp.asarray(_bucket_matrix(hw, dil))
    win = ATT_QT + 2 * hw
    out_w = n_heads * HEAD_DIM
    blk = (1, SEQ, HEAD_DIM)
    o_spec = pl.BlockSpec(blk, lambda b, h: (b, 0, h))
    out_shape = [jax.ShapeDtypeStruct((BATCH, SEQ, out_w), BF16)]
    out_specs = [o_spec]
    if with_lse:
        out_shape.append(jax.ShapeDtypeStruct((BATCH, SEQ, out_w), F32))
        out_specs.append(o_spec)
    n_res = _residues_per_trip(dil)
    scratch = [pltpu.VMEM((n_res, seq + 2 * hw, HEAD_DIM), BF16),
               pltpu.VMEM((n_res, seq + 2 * hw, HEAD_DIM), BF16),
               pltpu.VMEM((ATT_QT, win), F32)]
    if dil > 1:
        scratch += [pltpu.VMEM((SEQ, HEAD_DIM), F32)] * 4 + [pltpu.VMEM((n_res, seq, HEAD_DIM), BF16)]
        if with_lse:
            scratch.append(pltpu.VMEM((SEQ, HEAD_DIM), F32))
    kern = functools.partial(_attn_kernel, dil=dil, hw=hw, head_base=head_base,
                             has_sink=has_sink, with_lse=with_lse)
    outs = pl.pallas_call(
        kern,
        out_shape=out_shape,
        grid=(BATCH, n_heads),
        in_specs=[
            pl.BlockSpec(memory_space=pltpu.SMEM),
            pl.BlockSpec(memory_space=pltpu.SMEM),
            pl.BlockSpec(blk, lambda b, h: (b, 0, q_cb + h)),
            pl.BlockSpec(blk, lambda b, h: (b, 0, k_cb + h // kv_group)),
            pl.BlockSpec(blk, lambda b, h: (b, 0, v_cb + h // kv_group)),
            pl.BlockSpec((ATT_QT, win), lambda b, h: (0, 0)),
        ],
        out_specs=out_specs,
        scratch_shapes=scratch,
        compiler_params=_params(("parallel", "arbitrary"), 40),
        name=f"attn_d{dil}_h{head_base}",
    )(rel_bias, sink, pv, pv, pv, bkt)
    return [o.reshape(N_TOK, out_w) for o in outs]


def _outproj_kernel(oa_ref, o0_ref, o1_ref, o2_ref, l0_ref, l1_ref, l2_ref, ga_ref, gb_ref,
                    x_ref, woa_ref, wob_ref, wout_ref, gn_ref, x1_ref, h2_ref):
    l0, l1, l2 = l0_ref[...], l1_ref[...], l2_ref[...]
    mx = jnp.maximum(jnp.maximum(l0, l1), l2)
    e0, e1, e2 = jnp.exp(l0 - mx), jnp.exp(l1 - mx), jnp.exp(l2 - mx)
    den = e0 + e1 + e2
    ob = ((e0 / den) * o0_ref[...].astype(F32) + (e1 / den) * o1_ref[...].astype(F32)
          + (e2 / den) * o2_ref[...].astype(F32))
    ya = jnp.dot(oa_ref[...], woa_ref[...], preferred_element_type=F32)
    yb = jnp.dot(ob.astype(BF16), wob_ref[...], preferred_element_type=F32)
    merged = (jax.nn.sigmoid(ga_ref[...].astype(F32)) * ya
              + jax.nn.sigmoid(gb_ref[...].astype(F32)) * yb)
    x1 = x_ref[...] + jnp.dot(merged.astype(BF16), wout_ref[...], preferred_element_type=F32)
    x1_ref[...] = x1
    h2_ref[...] = _rms(x1, gn_ref[...]).astype(BF16)


def _outproj(oa, obs, lses, proj, x2, w_oa, w_ob, w_out, g_ffn, tm=256):
    row = lambda w: pl.BlockSpec((tm, w), lambda i: (i, 0))
    const = lambda shape: pl.BlockSpec(shape, lambda i: (0, 0), pipeline_mode=pl.Buffered(1))
    return pl.pallas_call(
        _outproj_kernel,
        out_shape=[jax.ShapeDtypeStruct((N_TOK, D_MODEL), F32),
                   jax.ShapeDtypeStruct((N_TOK, D_MODEL), BF16)],
        grid=(N_TOK // tm,),
        in_specs=[row(A_Q_W), row(B_OUT_W), row(B_OUT_W), row(B_OUT_W),
                  row(B_OUT_W), row(B_OUT_W), row(B_OUT_W),
                  pl.BlockSpec((tm, D_MODEL), lambda i: (i, GA_OFF // D_MODEL)),
                  pl.BlockSpec((tm, D_MODEL), lambda i: (i, GB_OFF // D_MODEL)),
                  row(D_MODEL),
                  const((A_Q_W, D_MODEL)), const((B_OUT_W, D_MODEL)), const((D_MODEL, D_MODEL)),
                  const((1, D_MODEL))],
        out_specs=[row(D_MODEL), row(D_MODEL)],
        compiler_params=_params(("parallel",), 48),
        name="outproj",
    )(oa, *obs, *lses, proj, proj, x2, w_oa, w_ob, w_out, g_ffn)


_FAR = 1024.0
RT_SLICE = 256
UP_CHUNK = 256
UP_ROWS = 256

_LANE_GATHER = lax.GatherDimensionNumbers(
    offset_dims=(), collapsed_slice_dims=(1,), start_index_map=(1,),
    operand_batching_dims=(0,), start_indices_batching_dims=(0,))


def _tree(op, xs):
    xs = list(xs)
    while len(xs) > 1:
        nxt = [op(xs[i], xs[i + 1]) for i in range(0, len(xs) - 1, 2)]
        if len(xs) % 2:
            nxt.append(xs[-1])
        xs = nxt
    return xs[0]


def _all_sublanes(op, x):
    for shift in (4, 2, 1):
        x = op(x, pltpu.roll(x, shift, axis=0))
    return x


def _pop_max(tiles, sub_iota, payload=None):
    m = _all_sublanes(jnp.maximum, _tree(jnp.maximum, tiles))
    first = _tree(jnp.minimum, [jnp.where(t == m, float(SUBLANES * v), _FAR) for v, t in enumerate(tiles)])
    row = _all_sublanes(jnp.minimum, first + sub_iota)
    base = row - sub_iota
    hits = [base == float(SUBLANES * v) for v in range(len(tiles))]
    if payload is None:
        val = row
    else:
        val = _all_sublanes(jnp.maximum, _tree(jnp.maximum, [jnp.where(h, p, -1.0)
                                                            for h, p in zip(hits, payload)]))
    return m, val, [jnp.where(h, -jnp.inf, t) for h, t in zip(hits, tiles)]


def _route_scores(h, q_ref, keys_ref, s_scr):
    for c in range(2):
        col = pl.multiple_of((2 * h + c) * LANES, LANES)
        s_scr[c] = lax.dot_general(keys_ref[2 * h + c], q_ref[:, pl.ds(col, LANES)],
                                   (((1,), (1,)), ((), ())), preferred_element_type=F32)


def _batcher_pairs(n):
    size = 16
    pairs = []
    p = 1
    while p < size:
        k = p
        while k >= 1:
            for j in range(k % p, size - k, 2 * k):
                for i in range(min(k, size - j - k)):
                    if (i + j) // (2 * p) == (i + j + k) // (2 * p) and i + j + k < n:
                        pairs.append((i + j, i + j + k))
            k //= 2
        p *= 2
    return pairs


_FAR_ID = 1e9


def _top_sorted(vals, ids, k):
    vals, ids = list(vals), list(ids)
    n = len(vals)
    for i, j in _batcher_pairs(n):
        up = vals[j] > vals[i]
        vals[i], vals[j] = jnp.where(up, vals[j], vals[i]), jnp.where(up, vals[i], vals[j])
        ids[i], ids[j] = jnp.where(up, ids[j], ids[i]), jnp.where(up, ids[i], ids[j])
    top_v, top_i = [], []
    for it in range(k):
        m = _all_sublanes(jnp.maximum, vals[0])
        first = _all_sublanes(jnp.minimum, jnp.where(vals[0] == m, ids[0], _FAR_ID))
        win = ids[0] == first
        top_v.append(m)
        top_i.append(first)
        for r in range(min(k - it, n)):
            if r + 1 < n:
                vals[r] = jnp.where(win, vals[r + 1], vals[r])
                ids[r] = jnp.where(win, ids[r + 1], ids[r])
            else:
                vals[r] = jnp.where(win, -jnp.inf, vals[r])
    runner_up = _all_sublanes(jnp.maximum, vals[0])
    tie = jnp.zeros_like(runner_up)
    for a, b in zip(top_v, top_v[1:] + [runner_up]):
        tie = jnp.where(a == b, 1.0, tie)
    return top_v, top_i, tie


def _route_topk(s_scr, ts_scr, ti_scr, bs_scr, et_ref, gt_ref, *, exact):
    k = PEER_TOPK
    nk = float(PEER_NKEYS)
    sub_iota = lax.broadcasted_iota(I32, (SUBLANES, LANES), 0).astype(F32)
    halves = [slice(i * LANES, (i + 1) * LANES) for i in range(RT_SLICE // LANES)]
    tie = None if exact else jnp.zeros((SUBLANES, LANES), F32)

    chains = []
    for c in range(2):
        for lanes in halves:
            tiles = [s_scr[c, SUBLANES * v:SUBLANES * (v + 1), lanes] for v in range(PEER_NKEYS // SUBLANES)]
            chains.append([c, lanes, tiles])
    if exact:
        for kk in range(k):
            for chain in chains:
                c, lanes, tiles = chain
                m, row, chain[2] = _pop_max(tiles, sub_iota)
                ts_scr[c, kk:kk + 1, lanes] = m[0:1]
                ti_scr[c, kk:kk + 1, lanes] = row[0:1]
    else:
        row_ids = [float(SUBLANES * v) + sub_iota for v in range(PEER_NKEYS // SUBLANES)]
        for c, lanes, tiles in chains:
            top_v, top_i, t = _top_sorted(tiles, row_ids, k)
            tie = jnp.maximum(tie, t)
            for kk in range(k):
                ts_scr[c, kk:kk + 1, lanes] = top_v[kk][0:1]
                ti_scr[c, kk:kk + 1, lanes] = top_i[kk][0:1]

    def candidates(lanes):
        s2 = [ts_scr[1, 0:8, lanes], ts_scr[1, 8:16, lanes]]
        i2 = [ti_scr[1, 0:8, lanes], ti_scr[1, 8:16, lanes]]
        s1 = lambda k1: ts_scr[0, k1:k1 + 1, lanes]
        i1 = lambda k1: ti_scr[0, k1:k1 + 1, lanes] * nk
        tiles = [s1(0) + s2[0], s1(0) + s2[1]]
        pay = [i1(0) + i2[0], i1(0) + i2[1]]
        for k1 in range(1, 8):
            allowed = k // (k1 + 1)
            t = s1(k1) + s2[0]
            tiles.append(t if allowed >= SUBLANES else jnp.where(sub_iota < float(allowed), t, -jnp.inf))
            pay.append(i1(k1) + i2[0])
        tiles.append(ts_scr[0, 8:16, lanes] + ts_scr[1, 0:1, lanes])
        pay.append(ti_scr[0, 8:16, lanes] * nk + ti_scr[1, 0:1, lanes])
        return tiles, pay

    cands = [candidates(lanes) for lanes in halves]
    if exact:
        tiles = [c[0] for c in cands]
        for kk in range(k):
            for i, lanes in enumerate(halves):
                m, expert, tiles[i] = _pop_max(tiles[i], sub_iota, payload=cands[i][1])
                bs_scr[kk:kk + 1, lanes] = m[0:1]
                et_ref[kk:kk + 1, lanes] = expert[0:1]
    else:
        for (tiles, pay), lanes in zip(cands, halves):
            top_v, top_i, t = _top_sorted(tiles, pay, k)
            tie = jnp.maximum(tie, t)
            for kk in range(k):
                bs_scr[kk:kk + 1, lanes] = top_v[kk][0:1]
                et_ref[kk:kk + 1, lanes] = top_i[kk][0:1]
    for lanes in halves:
        bs = bs_scr[:, lanes]
        ex = jnp.exp(bs - jnp.max(bs, axis=0, keepdims=True))
        gt_ref[:, lanes] = ex / jnp.sum(ex, axis=0, keepdims=True)
    return tie


def _expert_up(j, h_ref, u_ref, e_ref, act_ref, *, te):
    tb = h_ref.shape[0]
    u = [u_ref[c * UP_CHUNK:(c + 1) * UP_CHUNK, :].astype(BF16) for c in range(te // UP_CHUNK)]
    for m in range(tb // UP_ROWS):
        rows = slice(m * UP_ROWS, (m + 1) * UP_ROWS)
        e = e_ref[rows, :]
        row = e >> 7
        col = (e & (PEER_NKEYS - 1))[..., None]
        acc = act_ref[rows, :]
        h = h_ref[rows, :]
        for c in range(te // UP_CHUNK):
            dense = lax.dot_general(h, u[c], (((1,), (1,)), ((), ())), preferred_element_type=F32)
            for q in range(UP_CHUNK // LANES):
                got = lax.gather(dense[:, q * LANES:(q + 1) * LANES], col, _LANE_GATHER, (1, 1),
                                 mode=lax.GatherScatterMode.PROMISE_IN_BOUNDS)
                acc = jnp.where(row == (j * te + c * UP_CHUNK) // LANES + q, got, acc)
        act_ref[rows, :] = acc


def _route_kernel(h_ref, wq_ref, keys_ref, e_ref, g_ref,
                  q_scr, s_scr, ts_scr, ti_scr, bs_scr, et_scr, gt_scr):
    q_scr[...] = jnp.dot(h_ref[...], wq_ref[...], preferred_element_type=F32).astype(BF16)

    def head(h, carry):
        _route_scores(h, q_scr, keys_ref, s_scr)
        args = (s_scr, ts_scr, ti_scr, bs_scr, et_scr.at[h], gt_scr.at[h])
        tie = _route_topk(*args, exact=False)

        @pl.when(jnp.max(tie) > 0.0)
        def _():
            _route_topk(*args, exact=True)

        return carry

    lax.fori_loop(0, PEER_HEADS, head, 0)

    for half in range(RT_SLICE // LANES):
        lanes = slice(half * LANES, (half + 1) * LANES)
        et = jnp.concatenate([et_scr[h, :, lanes] for h in range(PEER_HEADS)], axis=0)
        gt = jnp.concatenate([gt_scr[h, :, lanes] for h in range(PEER_HEADS)], axis=0)
        e_ref[lanes, :] = et.T.astype(I32)
        g_ref[lanes, :] = gt.T


def _route(h2, wq, keys):
    k = PEER_TOPK
    tok = pl.BlockSpec((RT_SLICE, PEER_PICKS), lambda i: (i, 0))
    return pl.pallas_call(
        _route_kernel,
        out_shape=[jax.ShapeDtypeStruct((N_TOK, PEER_PICKS), I32),
                   jax.ShapeDtypeStruct((N_TOK, PEER_PICKS), F32)],
        grid=(N_TOK // RT_SLICE,),
        in_specs=[pl.BlockSpec((RT_SLICE, D_MODEL), lambda i: (i, 0)),
                  pl.BlockSpec((D_MODEL, PEER_HEADS * PEER_QDIM), lambda i: (0, 0),
                               pipeline_mode=pl.Buffered(1)),
                  pl.BlockSpec((PEER_HEADS * 2, PEER_NKEYS, PEER_QDIM // 2), lambda i: (0, 0, 0))],
        out_specs=[tok, tok],
        scratch_shapes=[
            pltpu.VMEM((RT_SLICE, PEER_HEADS * PEER_QDIM), BF16),
            pltpu.VMEM((2, PEER_NKEYS, RT_SLICE), F32),
            pltpu.VMEM((2, k, RT_SLICE), F32),
            pltpu.VMEM((2, k, RT_SLICE), F32),
            pltpu.VMEM((k, RT_SLICE), F32),
            pltpu.VMEM((PEER_HEADS, k, RT_SLICE), F32),
            pltpu.VMEM((PEER_HEADS, k, RT_SLICE), F32),
        ],
        compiler_params=_params(("parallel",), 32),
        name="peer_route",
    )(h2, wq, keys)


def _peer_up_kernel(h_ref, u_ref, e_ref, v_ref, act_ref, vb_ref, *, te):
    i, j = pl.program_id(0), pl.program_id(1)

    @pl.when(j == 0)
    def _():
        act_ref[...] = jnp.zeros_like(act_ref)

    @pl.when(i == 0)
    def _():
        vb_ref[...] = v_ref[...].astype(BF16)

    _expert_up(j, h_ref, u_ref, e_ref, act_ref, te=te)


def _peer_up(h2, u, e, v, tb=2048, te=512):
    nj = PEER_EXPERTS // te
    v_tile = pl.BlockSpec((te, D_MODEL), lambda i, j: (jnp.where(i == 0, j, nj - 1), 0))
    return pl.pallas_call(
        functools.partial(_peer_up_kernel, te=te),
        out_shape=[jax.ShapeDtypeStruct((N_TOK, PEER_PICKS), F32),
                   jax.ShapeDtypeStruct((PEER_EXPERTS, D_MODEL), BF16)],
        grid=(N_TOK // tb, nj),
        in_specs=[pl.BlockSpec((tb, D_MODEL), lambda i, j: (i, 0)),
                  pl.BlockSpec((te, D_MODEL), lambda i, j: (j, 0)),
                  pl.BlockSpec((tb, PEER_PICKS), lambda i, j: (i, 0)),
                  v_tile],
        out_specs=[pl.BlockSpec((tb, PEER_PICKS), lambda i, j: (i, 0)), v_tile],
        compiler_params=_params(("arbitrary", "arbitrary"), 56),
        name="peer_up",
    )(h2, u, e, v)


def _peer_coef_kernel(e_ref, g_ref, act_ref, p_ref, row_scr, col_scr, w_scr, c_scr, *, tp):
    e = e_ref[...]
    row_scr[...] = (e >> 7).astype(F32)
    col_scr[...] = (e & (PEER_NKEYS - 1)).astype(F32)
    act = act_ref[...]
    w_scr[...] = g_ref[...] * (0.5 * act * (1.0 + lax.erf(act * math.sqrt(0.5))))
    sub = 16
    iota = lax.broadcasted_iota(I32, (PEER_NKEYS // sub, sub, PEER_PICKS), 0) * sub \
        + lax.broadcasted_iota(I32, (PEER_NKEYS // sub, sub, PEER_PICKS), 1)
    iota = iota.astype(F32).astype(BF16)
    one = jnp.ones((), BF16)
    zero = jnp.zeros((), BF16)

    def bcast(ref, t):
        return jnp.broadcast_to(ref[pl.ds(t, 1), :], (sub, PEER_PICKS)).astype(BF16)[None]

    def products(t0):
        for t in range(sub):
            rows, cols, w = bcast(row_scr, t0 + t), bcast(col_scr, t0 + t), bcast(w_scr, t0 + t)
            left = jnp.where(iota == rows, one, zero).reshape(PEER_NKEYS, PEER_PICKS)
            right = jnp.where(iota == cols, w, zero).reshape(PEER_NKEYS, PEER_PICKS)
            c_scr[t0 + t] = lax.dot_general(left, right, (((1,), (1,)), ((), ())),
                                            preferred_element_type=F32)

    def regroup(t0):
        sw = jnp.swapaxes(c_scr[pl.ds(t0, sub)], 0, 1)
        for k1 in range(PEER_NKEYS):
            p_ref[pl.ds(t0, sub), k1 * PEER_NKEYS:(k1 + 1) * PEER_NKEYS] = sw[k1].astype(p_ref.dtype)

    def block(blk, carry):
        base = pl.multiple_of(blk * COEF_UNROLL, COEF_UNROLL)
        for t0 in range(0, COEF_UNROLL, sub):
            products(base + t0)
            if t0:
                regroup(base + t0 - sub)
        regroup(base + COEF_UNROLL - sub)
        return carry

    lax.fori_loop(0, tp // COEF_UNROLL, block, 0)


COEF_UNROLL = 64


def _peer_coef(e, g, act, tp=128):
    tok = pl.BlockSpec((tp, PEER_PICKS), lambda i: (i, 0))
    return pl.pallas_call(
        functools.partial(_peer_coef_kernel, tp=tp),
        out_shape=jax.ShapeDtypeStruct((N_TOK, PEER_EXPERTS), BF16),
        grid=(N_TOK // tp,),
        in_specs=[tok, tok, tok],
        out_specs=pl.BlockSpec((tp, PEER_EXPERTS), lambda i: (i, 0)),
        scratch_shapes=[pltpu.VMEM((tp, PEER_PICKS), F32),
                        pltpu.VMEM((tp, PEER_PICKS), F32),
                        pltpu.VMEM((tp, PEER_PICKS), F32),
                        pltpu.VMEM((tp, PEER_NKEYS, PEER_NKEYS), F32)],
        compiler_params=_params(("parallel",), 32),
        name="peer_coef",
    )(e, g, act)


def _peer_down_kernel(p_ref, v_ref, x_ref, g_ref, o_ref):
    j = pl.program_id(1)

    @pl.when(j == 0)
    def _():
        o_ref[...] = x_ref[...]

    o_ref[...] += jnp.dot(p_ref[...], v_ref[...], preferred_element_type=F32)

    @pl.when(j == pl.num_programs(1) - 1)
    def _():
        o_ref[...] = _rms(o_ref[...], g_ref[...])


def _peer_down(coef, v, x1, g_final, tm=1024, tk=2048):
    return pl.pallas_call(
        _peer_down_kernel,
        out_shape=jax.ShapeDtypeStruct((N_TOK, D_MODEL), F32),
        grid=(N_TOK // tm, PEER_EXPERTS // tk),
        in_specs=[pl.BlockSpec((tm, tk), lambda i, j: (i, j)),
                  pl.BlockSpec((tk, D_MODEL), lambda i, j: (j, 0)),
                  pl.BlockSpec((tm, D_MODEL), lambda i, j: (i, 0), pipeline_mode=pl.Buffered(1)),
                  pl.BlockSpec((1, D_MODEL), lambda i, j: (0, 0))],
        out_specs=pl.BlockSpec((tm, D_MODEL), lambda i, j: (i, 0)),
        compiler_params=_params(("parallel", "arbitrary"), 60),
        name="peer_down",
    )(coef, v, x1, g_final)


def kernel(x, rel_bias, norm_mix_g, w_in, sink_a, w_oa, w_ob, w_out, norm_ffn_g,
           peer_wq, peer_keys, peer_u, peer_v, norm_final_g):
    assert x.shape == (BATCH, SEQ, D_MODEL) and w_in.shape[0] == 1
    x2 = x.reshape(N_TOK, D_MODEL)
    proj = _inproj(x2, norm_mix_g, w_in[0].astype(BF16))

    (oa,) = _attention(proj, rel_bias, sink_a[0], dil=1, hw=A_HALF_WINDOW,
                       q_cb=QA_CB, k_cb=KA_CB, v_cb=VA_CB, n_heads=A_Q_HEADS, kv_group=A_GROUP,
                       head_base=0, has_sink=True, with_lse=False)
    obs, lses = [], []
    for gi, dil in enumerate(B_DILATIONS):
        off = gi * B_HEADS_PER_GROUP
        o, lse = _attention(proj, rel_bias, sink_a[0], dil=dil, hw=B_HALF_SPAN,
                            q_cb=QB_CB + off, k_cb=KB_CB + off, v_cb=VB_CB + off,
                            n_heads=B_HEADS_PER_GROUP, kv_group=1,
                            head_base=A_Q_HEADS + off, has_sink=False, with_lse=True)
        obs.append(o)
        lses.append(lse)

    x1, h2 = _outproj(oa, obs, lses, proj, x2, w_oa[0].astype(BF16), w_ob[0].astype(BF16),
                      w_out[0].astype(BF16), norm_ffn_g)

    keys = peer_keys[0].reshape(PEER_HEADS * 2, PEER_NKEYS, PEER_QDIM // 2).astype(BF16)
    e, gate = _route(h2, peer_wq[0].astype(BF16), keys)
    act, v_bf16 = _peer_up(h2, peer_u[0], e, peer_v[0])
    coef = _peer_coef(e, gate, act)
    out = _peer_down(coef, v_bf16, x1, norm_final_g.reshape(1, D_MODEL))
    return out.reshape(BATCH, SEQ, D_MODEL)
```

```python
import functools
import math

import numpy as np
import jax
import jax.numpy as jnp
from jax import lax
from jax.experimental import pallas as pl
from jax.experimental.pallas import tpu as pltpu

F32 = jnp.float32
BF16 = jnp.bfloat16
I32 = jnp.int32

D_MODEL = 2048
BATCH = 2
SEQ = 4096
N_TOK = BATCH * SEQ
HEAD_DIM = 128
LANES = 128
SUBLANES = 8

A_Q_HEADS = 8
A_KV_HEADS = 2
A_GROUP = A_Q_HEADS // A_KV_HEADS
A_HALF_WINDOW = 128
B_DILATIONS = (1, 4, 16)
B_GROUPS = 3
B_HEADS_PER_GROUP = 4
B_HALF_SPAN = 64
N_BUCKETS = 32
MAX_DISTANCE = 1024
N_ATTN_HEADS = A_Q_HEADS + B_GROUPS * B_HEADS_PER_GROUP

PEER_HEADS = 8
PEER_NKEYS = 128
PEER_EXPERTS = PEER_NKEYS * PEER_NKEYS
PEER_QDIM = 256
PEER_TOPK = 16
PEER_PICKS = PEER_HEADS * PEER_TOPK
EPS = 1e-6

A_Q_W = A_Q_HEADS * HEAD_DIM
A_KV_W = A_KV_HEADS * HEAD_DIM
B_W = B_GROUPS * B_HEADS_PER_GROUP * HEAD_DIM
B_OUT_W = B_HEADS_PER_GROUP * HEAD_DIM
IN_WIDTH = A_Q_W + 2 * A_KV_W + 3 * B_W + 2 * D_MODEL
QA_CB = 0
KA_CB = A_Q_W // LANES
VA_CB = (A_Q_W + A_KV_W) // LANES
QB_CB = (A_Q_W + 2 * A_KV_W) // LANES
KB_CB = QB_CB + B_W // LANES
VB_CB = KB_CB + B_W // LANES
GA_OFF = A_Q_W + 2 * A_KV_W + 3 * B_W
GB_OFF = GA_OFF + D_MODEL
IN_CB = IN_WIDTH // LANES

NEG = -1e30
ATT_SCALE = HEAD_DIM ** -0.5
ATT_QT = 128
ATT_GROUP = 8

MIB = 1024 * 1024


def _params(sem, vmem_mib):
    return pltpu.CompilerParams(dimension_semantics=sem, vmem_limit_bytes=vmem_mib * MIB)


def _rms(x, g):
    return x * lax.rsqrt(jnp.mean(x * x, axis=-1, keepdims=True) + EPS) * g


def _inproj_kernel(x_ref, g_ref, w_ref, o_ref, h_scr):
    @pl.when(pl.program_id(1) == 0)
    def _():
        h_scr[...] = _rms(x_ref[...], g_ref[...]).astype(BF16)

    o_ref[...] = jnp.dot(h_scr[...], w_ref[...], preferred_element_type=F32).astype(o_ref.dtype)


def _inproj(x2, g, w, tm=1024, tn=1024):
    n = w.shape[1]
    return pl.pallas_call(
        _inproj_kernel,
        out_shape=jax.ShapeDtypeStruct((N_TOK, n), BF16),
        grid=(N_TOK // tm, n // tn),
        in_specs=[
            pl.BlockSpec((tm, D_MODEL), lambda i, j: (i, 0)),
            pl.BlockSpec((1, D_MODEL), lambda i, j: (0, 0)),
            pl.BlockSpec((D_MODEL, tn), lambda i, j: (0, j)),
        ],
        out_specs=pl.BlockSpec((tm, tn), lambda i, j: (i, j)),
        scratch_shapes=[pltpu.VMEM((tm, D_MODEL), BF16)],
        compiler_params=_params(("parallel", "arbitrary"), 40),
        name="inproj",
    )(x2, g, w)


def _t5_bucket_np(rel):
    half = N_BUCKETS // 2
    max_exact = half // 2
    ret = np.where(rel > 0, half, 0)
    n = np.abs(rel)
    nf = np.maximum(n, 1).astype(np.float64)
    large = max_exact + (np.log(nf / max_exact) / math.log(MAX_DISTANCE / max_exact) * (half - max_exact)).astype(np.int64)
    large = np.minimum(large, half - 1)
    return (ret + np.where(n < max_exact, n, large)).astype(np.int32)


def _bucket_matrix(hw, dist_scale):
    w = ATT_QT + 2 * hw
    delta = (np.arange(w)[None, :] - hw) - np.arange(ATT_QT)[:, None]
    bkt = _t5_bucket_np(delta * dist_scale)
    return np.where(np.abs(delta) <= hw, bkt, -1).astype(np.int32)


def _residues_per_trip(dil):
    tiles = SEQ // dil // ATT_QT
    return max(1, min(dil, ATT_GROUP // tiles))


def _attn_kernel(tab_ref, sink_ref, q_ref, k_ref, v_ref, bkt_ref, *rest,
                 dil, hw, head_base, has_sink, with_lse):
    n_out = 2 if with_lse else 1
    outs, scr = rest[:n_out], rest[n_out:]
    o_ref = outs[0]
    kpad, vpad, bias_scr = scr[:3]
    if dil > 1:
        q32, k32, v32, o32, qres = scr[3:8]
        if with_lse:
            l32 = scr[8]
    seq = SEQ // dil
    qt = ATT_QT
    win = qt + 2 * hw
    n_res = _residues_per_trip(dil)
    head = head_base + pl.program_id(1)

    bkt = bkt_ref[...]
    bias = jnp.full((qt, win), NEG, F32)
    for b in range(N_BUCKETS):
        bias = jnp.where(bkt == b, tab_ref[b, head], bias)
    bias_scr[...] = bias

    zeros = jnp.zeros((hw, HEAD_DIM), BF16)
    for pad_ref in (kpad, vpad):
        for u in range(n_res):
            pad_ref[u, 0:hw, :] = zeros
            pad_ref[u, hw + seq:hw + seq + hw, :] = zeros
    if dil == 1:
        kpad[0, hw:hw + seq, :] = k_ref[0]
        vpad[0, hw:hw + seq, :] = v_ref[0]
    else:
        q32[...] = q_ref[0].astype(F32)
        k32[...] = k_ref[0].astype(F32)
        v32[...] = v_ref[0].astype(F32)

    def scores(q0, q, kw):
        s = lax.dot_general(q, kw, (((1,), (1,)), ((), ())), preferred_element_type=F32)
        s = s * ATT_SCALE + bias_scr[...]
        kpos = q0 - hw + lax.broadcasted_iota(I32, (qt, win), 1)
        return jnp.where((kpos >= 0) & (kpos < seq), s, NEG)

    def softmax(s):
        m = jnp.max(s, axis=-1, keepdims=True)
        if has_sink:
            sk = sink_ref[head]
            m = jnp.maximum(m, sk)
        p = jnp.exp(s - m)
        den = jnp.sum(p, axis=-1, keepdims=True)
        if has_sink:
            den = den + jnp.exp(sk - m)
        return p.astype(BF16), m, den

    def values(p, m, den, vw):
        o = jnp.dot(p, vw, preferred_element_type=F32) / den
        lse = jnp.broadcast_to(m + jnp.log(den), (qt, HEAD_DIM)) if with_lse else None
        return o, lse

    group = min(ATT_GROUP, seq // qt)

    def residues(g, carry):
        r0 = g * n_res
        if dil > 1:
            for u in range(n_res):
                kpad[u, hw:hw + seq, :] = k32[pl.ds(r0 + u, seq, stride=dil), :].astype(BF16)
                vpad[u, hw:hw + seq, :] = v32[pl.ds(r0 + u, seq, stride=dil), :].astype(BF16)
                qres[u] = q32[pl.ds(r0 + u, seq, stride=dil), :].astype(BF16)

        def body(t, c):
            work = [(u, pl.multiple_of((t * group + i) * qt, qt)) for u in range(n_res) for i in range(group)]
            q_src = (lambda u: qres.at[u]) if dil > 1 else (lambda u: q_ref.at[0])
            s_all = [scores(q0, q_src(u)[pl.ds(q0, qt), :], kpad[u, pl.ds(q0, win), :]) for u, q0 in work]
            p_all = [softmax(s) for s in s_all]
            done = [values(*pmd, vpad[u, pl.ds(q0, win), :]) for (u, q0), pmd in zip(work, p_all)]
            for (u, q0), (o, lse) in zip(work, done):
                if dil > 1:
                    o32[pl.ds(r0 + u + q0 * dil, qt, stride=dil), :] = o
                    if with_lse:
                        l32[pl.ds(r0 + u + q0 * dil, qt, stride=dil), :] = lse
                else:
                    o_ref[0, pl.ds(q0, qt), :] = o.astype(o_ref.dtype)
                    if with_lse:
                        outs[1][0, pl.ds(q0, qt), :] = lse
            return c

        lax.fori_loop(0, seq // (qt * group), body, 0)
        return carry

    if dil > 1:
        lax.fori_loop(0, dil // n_res, residues, 0)
        o_ref[0] = o32[...].astype(o_ref.dtype)
        if with_lse:
            outs[1][0] = l32[...]
    else:
        residues(0, 0)


def _attention(proj, rel_bias, sink, *, dil, hw, q_cb, k_cb, v_cb, n_heads, kv_group,
               head_base, has_sink, with_lse):
    seq = SEQ // dil
    pv = proj.reshape(BATCH, SEQ, IN_WIDTH)
    bkt = jnp.asarray(_bucket_matrix(hw, dil))
    win = ATT_QT + 2 * hw
    out_w = n_heads * HEAD_DIM
    blk = (1, SEQ, HEAD_DIM)
    o_spec = pl.BlockSpec(blk, lambda b, h: (b, 0, h))
    out_shape = [jax.ShapeDtypeStruct((BATCH, SEQ, out_w), BF16)]
    out_specs = [o_spec]
    if with_lse:
        out_shape.append(jax.ShapeDtypeStruct((BATCH, SEQ, out_w), F32))
        out_specs.append(o_spec)
    n_res = _residues_per_trip(dil)
    scratch = [pltpu.VMEM((n_res, seq + 2 * hw, HEAD_DIM), BF16),
               pltpu.VMEM((n_res, seq + 2 * hw, HEAD_DIM), BF16),
               pltpu.VMEM((ATT_QT, win), F32)]
    if dil > 1:
        scratch += [pltpu.VMEM((SEQ, HEAD_DIM), F32)] * 4 + [pltpu.VMEM((n_res, seq, HEAD_DIM), BF16)]
        if with_lse:
            scratch.append(pltpu.VMEM((SEQ, HEAD_DIM), F32))
    kern = functools.partial(_attn_kernel, dil=dil, hw=hw, head_base=head_base,
                             has_sink=has_sink, with_lse=with_lse)
    outs = pl.pallas_call(
        kern,
        out_shape=out_shape,
        grid=(BATCH, n_heads),
        in_specs=[
            pl.BlockSpec(memory_space=pltpu.SMEM),
            pl.BlockSpec(memory_space=pltpu.SMEM),
            pl.BlockSpec(blk, lambda b, h: (b, 0, q_cb + h)),
            pl.BlockSpec(blk, lambda b, h: (b, 0, k_cb + h // kv_group)),
            pl.BlockSpec(blk, lambda b, h: (b, 0, v_cb + h // kv_group)),
            pl.BlockSpec((ATT_QT, win), lambda b, h: (0, 0)),
        ],
        out_specs=out_specs,
        scratch_shapes=scratch,
        compiler_params=_params(("parallel", "arbitrary"), 40),
        name=f"attn_d{dil}_h{head_base}",
    )(rel_bias, sink, pv, pv, pv, bkt)
    return [o.reshape(N_TOK, out_w) for o in outs]


def _outproj_kernel(oa_ref, o0_ref, o1_ref, o2_ref, l0_ref, l1_ref, l2_ref, ga_ref, gb_ref,
                    x_ref, woa_ref, wob_ref, wout_ref, gn_ref, x1_ref, h2_ref):
    l0, l1, l2 = l0_ref[...], l1_ref[...], l2_ref[...]
    mx = jnp.maximum(jnp.maximum(l0, l1), l2)
    e0, e1, e2 = jnp.exp(l0 - mx), jnp.exp(l1 - mx), jnp.exp(l2 - mx)
    den = e0 + e1 + e2
    ob = ((e0 / den) * o0_ref[...].astype(F32) + (e1 / den) * o1_ref[...].astype(F32)
          + (e2 / den) * o2_ref[...].astype(F32))
    ya = jnp.dot(oa_ref[...], woa_ref[...], preferred_element_type=F32)
    yb = jnp.dot(ob.astype(BF16), wob_ref[...], preferred_element_type=F32)
    merged = (jax.nn.sigmoid(ga_ref[...].astype(F32)) * ya
              + jax.nn.sigmoid(gb_ref[...].astype(F32)) * yb)
    x1 = x_ref[...] + jnp.dot(merged.astype(BF16), wout_ref[...], preferred_element_type=F32)
    x1_ref[...] = x1
    h2_ref[...] = _rms(x1, gn_ref[...]).astype(BF16)


def _outproj(oa, obs, lses, proj, x2, w_oa, w_ob, w_out, g_ffn, tm=256):
    row = lambda w: pl.BlockSpec((tm, w), lambda i: (i, 0))
    const = lambda shape: pl.BlockSpec(shape, lambda i: (0, 0), pipeline_mode=pl.Buffered(1))
    return pl.pallas_call(
        _outproj_kernel,
        out_shape=[jax.ShapeDtypeStruct((N_TOK, D_MODEL), F32),
                   jax.ShapeDtypeStruct((N_TOK, D_MODEL), BF16)],
        grid=(N_TOK // tm,),
        in_specs=[row(A_Q_W), row(B_OUT_W), row(B_OUT_W), row(B_OUT_W),
                  row(B_OUT_W), row(B_OUT_W), row(B_OUT_W),
                  pl.BlockSpec((tm, D_MODEL), lambda i: (i, GA_OFF // D_MODEL)),
                  pl.BlockSpec((tm, D_MODEL), lambda i: (i, GB_OFF // D_MODEL)),
                  row(D_MODEL),
                  const((A_Q_W, D_MODEL)), const((B_OUT_W, D_MODEL)), const((D_MODEL, D_MODEL)),
                  const((1, D_MODEL))],
        out_specs=[row(D_MODEL), row(D_MODEL)],
        compiler_params=_params(("parallel",), 48),
        name="outproj",
    )(oa, *obs, *lses, proj, proj, x2, w_oa, w_ob, w_out, g_ffn)


_FAR = 1024.0
RT_SLICE = 256
RT_HEADS = 4
UP_CHUNK = 256
UP_ROWS = 256

_LANE_GATHER = lax.GatherDimensionNumbers(
    offset_dims=(), collapsed_slice_dims=(1,), start_index_map=(1,),
    operand_batching_dims=(0,), start_indices_batching_dims=(0,))


def _tree(op, xs):
    xs = list(xs)
    while len(xs) > 1:
        nxt = [op(xs[i], xs[i + 1]) for i in range(0, len(xs) - 1, 2)]
        if len(xs) % 2:
            nxt.append(xs[-1])
        xs = nxt
    return xs[0]


def _all_sublanes(op, x):
    for shift in (4, 2, 1):
        x = op(x, pltpu.roll(x, shift, axis=0))
    return x


def _pop_max(tiles, sub_iota, payload=None):
    m = _all_sublanes(jnp.maximum, _tree(jnp.maximum, tiles))
    first = _tree(jnp.minimum, [jnp.where(t == m, float(SUBLANES * v), _FAR) for v, t in enumerate(tiles)])
    row = _all_sublanes(jnp.minimum, first + sub_iota)
    base = row - sub_iota
    hits = [base == float(SUBLANES * v) for v in range(len(tiles))]
    if payload is None:
        val = row
    else:
        val = _all_sublanes(jnp.maximum, _tree(jnp.maximum, [jnp.where(h, p, -1.0)
                                                            for h, p in zip(hits, payload)]))
    return m, val, [jnp.where(h, -jnp.inf, t) for h, t in zip(hits, tiles)]


def _route_scores(h, q_ref, keys_ref, s_scr):
    for c in range(2):
        col = pl.multiple_of((2 * h + c) * LANES, LANES)
        s_scr[c] = lax.dot_general(keys_ref[2 * h + c], q_ref[:, pl.ds(col, LANES)],
                                   (((1,), (1,)), ((), ())), preferred_element_type=F32)


def _batcher_pairs(n):
    size = 16
    pairs = []
    p = 1
    while p < size:
        k = p
        while k >= 1:
            for j in range(k % p, size - k, 2 * k):
                for i in range(min(k, size - j - k)):
                    if (i + j) // (2 * p) == (i + j + k) // (2 * p) and i + j + k < n:
                        pairs.append((i + j, i + j + k))
            k //= 2
        p *= 2
    return pairs


_FAR_ID = 1e9


def _top_sorted(vals, ids, k):
    vals, ids = list(vals), list(ids)
    n = len(vals)
    for i, j in _batcher_pairs(n):
        up = vals[j] > vals[i]
        vals[i], vals[j] = jnp.where(up, vals[j], vals[i]), jnp.where(up, vals[i], vals[j])
        ids[i], ids[j] = jnp.where(up, ids[j], ids[i]), jnp.where(up, ids[i], ids[j])
    top_v, top_i = [], []
    for it in range(k):
        m = _all_sublanes(jnp.maximum, vals[0])
        first = _all_sublanes(jnp.minimum, jnp.where(vals[0] == m, ids[0], _FAR_ID))
        win = ids[0] == first
        top_v.append(m)
        top_i.append(first)
        for r in range(min(k - it, n)):
            if r + 1 < n:
                vals[r] = jnp.where(win, vals[r + 1], vals[r])
                ids[r] = jnp.where(win, ids[r + 1], ids[r])
            else:
                vals[r] = jnp.where(win, -jnp.inf, vals[r])
    runner_up = _all_sublanes(jnp.maximum, vals[0])
    tie = jnp.zeros_like(runner_up)
    for a, b in zip(top_v, top_v[1:] + [runner_up]):
        tie = jnp.where(a == b, 1.0, tie)
    return top_v, top_i, tie


def _route_topk(s_scr, ts_scr, ti_scr, bs_scr, et_ref, gt_ref, *, exact):
    k = PEER_TOPK
    nk = float(PEER_NKEYS)
    sub_iota = lax.broadcasted_iota(I32, (SUBLANES, LANES), 0).astype(F32)
    halves = [slice(i * LANES, (i + 1) * LANES) for i in range(RT_SLICE // LANES)]
    tie = None if exact else jnp.zeros((SUBLANES, LANES), F32)

    chains = []
    for c in range(2):
        for lanes in halves:
            tiles = [s_scr[c, SUBLANES * v:SUBLANES * (v + 1), lanes] for v in range(PEER_NKEYS // SUBLANES)]
            chains.append([c, lanes, tiles])
    if exact:
        for kk in range(k):
            for chain in chains:
                c, lanes, tiles = chain
                m, row, chain[2] = _pop_max(tiles, sub_iota)
                ts_scr[c, kk:kk + 1, lanes] = m[0:1]
                ti_scr[c, kk:kk + 1, lanes] = row[0:1]
    else:
        row_ids = [float(SUBLANES * v) + sub_iota for v in range(PEER_NKEYS // SUBLANES)]
        for c, lanes, tiles in chains:
            top_v, top_i, t = _top_sorted(tiles, row_ids, k)
            tie = jnp.maximum(tie, t)
            for kk in range(k):
                ts_scr[c, kk:kk + 1, lanes] = top_v[kk][0:1]
                ti_scr[c, kk:kk + 1, lanes] = top_i[kk][0:1]

    def candidates(lanes):
        s2 = [ts_scr[1, 0:8, lanes], ts_scr[1, 8:16, lanes]]
        i2 = [ti_scr[1, 0:8, lanes], ti_scr[1, 8:16, lanes]]
        s1 = lambda k1: ts_scr[0, k1:k1 + 1, lanes]
        i1 = lambda k1: ti_scr[0, k1:k1 + 1, lanes] * nk
        tiles = [s1(0) + s2[0], s1(0) + s2[1]]
        pay = [i1(0) + i2[0], i1(0) + i2[1]]
        for k1 in range(1, 8):
            allowed = k // (k1 + 1)
            t = s1(k1) + s2[0]
            tiles.append(t if allowed >= SUBLANES else jnp.where(sub_iota < float(allowed), t, -jnp.inf))
            pay.append(i1(k1) + i2[0])
        tiles.append(ts_scr[0, 8:16, lanes] + ts_scr[1, 0:1, lanes])
        pay.append(ti_scr[0, 8:16, lanes] * nk + ti_scr[1, 0:1, lanes])
        return tiles, pay

    cands = [candidates(lanes) for lanes in halves]
    if exact:
        tiles = [c[0] for c in cands]
        for kk in range(k):
            for i, lanes in enumerate(halves):
                m, expert, tiles[i] = _pop_max(tiles[i], sub_iota, payload=cands[i][1])
                bs_scr[kk:kk + 1, lanes] = m[0:1]
                et_ref[kk:kk + 1, lanes] = expert[0:1]
    else:
        for (tiles, pay), lanes in zip(cands, halves):
            top_v, top_i, t = _top_sorted(tiles, pay, k)
            tie = jnp.maximum(tie, t)
            for kk in range(k):
                bs_scr[kk:kk + 1, lanes] = top_v[kk][0:1]
                et_ref[kk:kk + 1, lanes] = top_i[kk][0:1]
    for lanes in halves:
        bs = bs_scr[:, lanes]
        ex = jnp.exp(bs - jnp.max(bs, axis=0, keepdims=True))
        gt_ref[:, lanes] = ex / jnp.sum(ex, axis=0, keepdims=True)
    return tie


def _expert_up(j, h_ref, u_ref, e_ref, act_ref, *, te):
    tb = h_ref.shape[0]
    u = [u_ref[c * UP_CHUNK:(c + 1) * UP_CHUNK, :].astype(BF16) for c in range(te // UP_CHUNK)]
    for m in range(tb // UP_ROWS):
        rows = slice(m * UP_ROWS, (m + 1) * UP_ROWS)
        e = e_ref[rows, :]
        row = e >> 7
        col = (e & (PEER_NKEYS - 1))[..., None]
        acc = act_ref[rows, :]
        h = h_ref[rows, :]
        for c in range(te // UP_CHUNK):
            dense = lax.dot_general(h, u[c], (((1,), (1,)), ((), ())), preferred_element_type=F32)
            for q in range(UP_CHUNK // LANES):
                got = lax.gather(dense[:, q * LANES:(q + 1) * LANES], col, _LANE_GATHER, (1, 1),
                                 mode=lax.GatherScatterMode.PROMISE_IN_BOUNDS)
                acc = jnp.where(row == (j * te + c * UP_CHUNK) // LANES + q, got, acc)
        act_ref[rows, :] = acc


def _route_kernel(h_ref, wq_ref, keys_ref, e_ref, g_ref,
                  q_scr, s_scr, ts_scr, ti_scr, bs_scr, et_scr, gt_scr):
    q_scr[...] = jnp.dot(h_ref[...], wq_ref[...], preferred_element_type=F32).astype(BF16)

    def heads(g, carry):
        hs = [g * RT_HEADS + u for u in range(RT_HEADS)]
        for u, h in enumerate(hs):
            _route_scores(h, q_scr, keys_ref, s_scr.at[u])
        args = [(s_scr.at[u], ts_scr.at[u], ti_scr.at[u], bs_scr.at[u], et_scr.at[h], gt_scr.at[h])
                for u, h in enumerate(hs)]
        ties = [_route_topk(*a, exact=False) for a in args]
        for a, tie in zip(args, ties):
            @pl.when(jnp.max(tie) > 0.0)
            def _():
                _route_topk(*a, exact=True)

        return carry

    lax.fori_loop(0, PEER_HEADS // RT_HEADS, heads, 0)

    for half in range(RT_SLICE // LANES):
        lanes = slice(half * LANES, (half + 1) * LANES)
        et = jnp.concatenate([et_scr[h, :, lanes] for h in range(PEER_HEADS)], axis=0)
        gt = jnp.concatenate([gt_scr[h, :, lanes] for h in range(PEER_HEADS)], axis=0)
        e_ref[lanes, :] = et.T.astype(I32)
        g_ref[lanes, :] = gt.T


def _route(h2, wq, keys):
    k = PEER_TOPK
    tok = pl.BlockSpec((RT_SLICE, PEER_PICKS), lambda i: (i, 0))
    return pl.pallas_call(
        _route_kernel,
        out_shape=[jax.ShapeDtypeStruct((N_TOK, PEER_PICKS), I32),
                   jax.ShapeDtypeStruct((N_TOK, PEER_PICKS), F32)],
        grid=(N_TOK // RT_SLICE,),
        in_specs=[pl.BlockSpec((RT_SLICE, D_MODEL), lambda i: (i, 0)),
                  pl.BlockSpec((D_MODEL, PEER_HEADS * PEER_QDIM), lambda i: (0, 0),
                               pipeline_mode=pl.Buffered(1)),
                  pl.BlockSpec((PEER_HEADS * 2, PEER_NKEYS, PEER_QDIM // 2), lambda i: (0, 0, 0))],
        out_specs=[tok, tok],
        scratch_shapes=[
            pltpu.VMEM((RT_SLICE, PEER_HEADS * PEER_QDIM), BF16),
            pltpu.VMEM((RT_HEADS, 2, PEER_NKEYS, RT_SLICE), F32),
            pltpu.VMEM((RT_HEADS, 2, k, RT_SLICE), F32),
            pltpu.VMEM((RT_HEADS, 2, k, RT_SLICE), F32),
            pltpu.VMEM((RT_HEADS, k, RT_SLICE), F32),
            pltpu.VMEM((PEER_HEADS, k, RT_SLICE), F32),
            pltpu.VMEM((PEER_HEADS, k, RT_SLICE), F32),
        ],
        compiler_params=_params(("parallel",), 32),
        name="peer_route",
    )(h2, wq, keys)


def _peer_up_kernel(h_ref, u_ref, e_ref, v_ref, act_ref, vb_ref, *, te):
    i, j = pl.program_id(0), pl.program_id(1)

    @pl.when(j == 0)
    def _():
        act_ref[...] = jnp.zeros_like(act_ref)

    @pl.when(i == 0)
    def _():
        vb_ref[...] = v_ref[...].astype(BF16)

    _expert_up(j, h_ref, u_ref, e_ref, act_ref, te=te)


def _peer_up(h2, u, e, v, tb=2048, te=512):
    nj = PEER_EXPERTS // te
    v_tile = pl.BlockSpec((te, D_MODEL), lambda i, j: (jnp.where(i == 0, j, nj - 1), 0))
    return pl.pallas_call(
        functools.partial(_peer_up_kernel, te=te),
        out_shape=[jax.ShapeDtypeStruct((N_TOK, PEER_PICKS), F32),
                   jax.ShapeDtypeStruct((PEER_EXPERTS, D_MODEL), BF16)],
        grid=(N_TOK // tb, nj),
        in_specs=[pl.BlockSpec((tb, D_MODEL), lambda i, j: (i, 0)),
                  pl.BlockSpec((te, D_MODEL), lambda i, j: (j, 0)),
                  pl.BlockSpec((tb, PEER_PICKS), lambda i, j: (i, 0)),
                  v_tile],
        out_specs=[pl.BlockSpec((tb, PEER_PICKS), lambda i, j: (i, 0)), v_tile],
        compiler_params=_params(("arbitrary", "arbitrary"), 56),
        name="peer_up",
    )(h2, u, e, v)


def _peer_coef_kernel(e_ref, g_ref, act_ref, p_ref, row_scr, col_scr, w_scr, c_scr, *, tp):
    e = e_ref[...]
    row_scr[...] = (e >> 7).astype(F32)
    col_scr[...] = (e & (PEER_NKEYS - 1)).astype(F32)
    act = act_ref[...]
    w_scr[...] = g_ref[...] * (0.5 * act * (1.0 + lax.erf(act * math.sqrt(0.5))))
    sub = 16
    iota = lax.broadcasted_iota(I32, (PEER_NKEYS // sub, sub, PEER_PICKS), 0) * sub \
        + lax.broadcasted_iota(I32, (PEER_NKEYS // sub, sub, PEER_PICKS), 1)
    iota = iota.astype(F32).astype(BF16)
    one = jnp.ones((), BF16)
    zero = jnp.zeros((), BF16)

    def bcast(ref, t):
        return jnp.broadcast_to(ref[pl.ds(t, 1), :], (sub, PEER_PICKS)).astype(BF16)[None]

    def products(t0):
        for t in range(sub):
            rows, cols, w = bcast(row_scr, t0 + t), bcast(col_scr, t0 + t), bcast(w_scr, t0 + t)
            left = jnp.where(iota == rows, one, zero).reshape(PEER_NKEYS, PEER_PICKS)
            right = jnp.where(iota == cols, w, zero).reshape(PEER_NKEYS, PEER_PICKS)
            c_scr[t0 + t] = lax.dot_general(left, right, (((1,), (1,)), ((), ())),
                                            preferred_element_type=F32)

    def regroup(t0):
        sw = jnp.swapaxes(c_scr[pl.ds(t0, sub)], 0, 1)
        for k1 in range(PEER_NKEYS):
            p_ref[pl.ds(t0, sub), k1 * PEER_NKEYS:(k1 + 1) * PEER_NKEYS] = sw[k1].astype(p_ref.dtype)

    def block(blk, carry):
        base = pl.multiple_of(blk * COEF_UNROLL, COEF_UNROLL)
        for t0 in range(0, COEF_UNROLL, sub):
            products(base + t0)
            if t0:
                regroup(base + t0 - sub)
        regroup(base + COEF_UNROLL - sub)
        return carry

    lax.fori_loop(0, tp // COEF_UNROLL, block, 0)


COEF_UNROLL = 64


def _peer_coef(e, g, act, tp=128):
    tok = pl.BlockSpec((tp, PEER_PICKS), lambda i: (i, 0))
    return pl.pallas_call(
        functools.partial(_peer_coef_kernel, tp=tp),
        out_shape=jax.ShapeDtypeStruct((N_TOK, PEER_EXPERTS), BF16),
        grid=(N_TOK // tp,),
        in_specs=[tok, tok, tok],
        out_specs=pl.BlockSpec((tp, PEER_EXPERTS), lambda i: (i, 0)),
        scratch_shapes=[pltpu.VMEM((tp, PEER_PICKS), F32),
                        pltpu.VMEM((tp, PEER_PICKS), F32),
                        pltpu.VMEM((tp, PEER_PICKS), F32),
                        pltpu.VMEM((tp, PEER_NKEYS, PEER_NKEYS), F32)],
        compiler_params=_params(("parallel",), 32),
        name="peer_coef",
    )(e, g, act)


def _peer_down_kernel(p_ref, v_ref, x_ref, g_ref, o_ref):
    j = pl.program_id(1)

    @pl.when(j == 0)
    def _():
        o_ref[...] = x_ref[...]

    o_ref[...] += jnp.dot(p_ref[...], v_ref[...], preferred_element_type=F32)

    @pl.when(j == pl.num_programs(1) - 1)
    def _():
        o_ref[...] = _rms(o_ref[...], g_ref[...])


def _peer_down(coef, v, x1, g_final, tm=1024, tk=2048):
    return pl.pallas_call(
        _peer_down_kernel,
        out_shape=jax.ShapeDtypeStruct((N_TOK, D_MODEL), F32),
        grid=(N_TOK // tm, PEER_EXPERTS // tk),
        in_specs=[pl.BlockSpec((tm, tk), lambda i, j: (i, j)),
                  pl.BlockSpec((tk, D_MODEL), lambda i, j: (j, 0)),
                  pl.BlockSpec((tm, D_MODEL), lambda i, j: (i, 0), pipeline_mode=pl.Buffered(1)),
                  pl.BlockSpec((1, D_MODEL), lambda i, j: (0, 0))],
        out_specs=pl.BlockSpec((tm, D_MODEL), lambda i, j: (i, 0)),
        compiler_params=_params(("parallel", "arbitrary"), 60),
        name="peer_down",
    )(coef, v, x1, g_final)


def kernel(x, rel_bias, norm_mix_g, w_in, sink_a, w_oa, w_ob, w_out, norm_ffn_g,
           peer_wq, peer_keys, peer_u, peer_v, norm_final_g):
    assert x.shape == (BATCH, SEQ, D_MODEL) and w_in.shape[0] == 1
    x2 = x.reshape(N_TOK, D_MODEL)
    proj = _inproj(x2, norm_mix_g, w_in[0].astype(BF16))

    (oa,) = _attention(proj, rel_bias, sink_a[0], dil=1, hw=A_HALF_WINDOW,
                       q_cb=QA_CB, k_cb=KA_CB, v_cb=VA_CB, n_heads=A_Q_HEADS, kv_group=A_GROUP,
                       head_base=0, has_sink=True, with_lse=False)
    obs, lses = [], []
    for gi, dil in enumerate(B_DILATIONS):
        off = gi * B_HEADS_PER_GROUP
        o, lse = _attention(proj, rel_bias, sink_a[0], dil=dil, hw=B_HALF_SPAN,
                            q_cb=QB_CB + off, k_cb=KB_CB + off, v_cb=VB_CB + off,
                            n_heads=B_HEADS_PER_GROUP, kv_group=1,
                            head_base=A_Q_HEADS + off, has_sink=False, with_lse=True)
        obs.append(o)
        lses.append(lse)

    x1, h2 = _outproj(oa, obs, lses, proj, x2, w_oa[0].astype(BF16), w_ob[0].astype(BF16),
                      w_out[0].astype(BF16), norm_ffn_g)

    keys = peer_keys[0].reshape(PEER_HEADS * 2, PEER_NKEYS, PEER_QDIM // 2).astype(BF16)
    e, gate = _route(h2, peer_wq[0].astype(BF16), keys)
    act, v_bf16 = _peer_up(h2, peer_u[0], e, peer_v[0])
    coef = _peer_coef(e, gate, act)
    out = _peer_down(coef, v_bf16, x1, norm_final_g.reshape(1, D_MODEL))
    return out.reshape(BATCH, SEQ, D_MODEL)
```

```python
import functools
import math

import numpy as np
import jax
import jax.numpy as jnp
from jax import lax
from jax.experimental import pallas as pl
from jax.experimental.pallas import tpu as pltpu

F32 = jnp.float32
BF16 = jnp.bfloat16
I32 = jnp.int32

D_MODEL = 2048
BATCH = 2
SEQ = 4096
N_TOK = BATCH * SEQ
HEAD_DIM = 128
LANES = 128
SUBLANES = 8

A_Q_HEADS = 8
A_KV_HEADS = 2
A_GROUP = A_Q_HEADS // A_KV_HEADS
A_HALF_WINDOW = 128
B_DILATIONS = (1, 4, 16)
B_GROUPS = 3
B_HEADS_PER_GROUP = 4
B_HALF_SPAN = 64
N_BUCKETS = 32
MAX_DISTANCE = 1024
N_ATTN_HEADS = A_Q_HEADS + B_GROUPS * B_HEADS_PER_GROUP

PEER_HEADS = 8
PEER_NKEYS = 128
PEER_EXPERTS = PEER_NKEYS * PEER_NKEYS
PEER_QDIM = 256
PEER_TOPK = 16
PEER_PICKS = PEER_HEADS * PEER_TOPK
EPS = 1e-6

A_Q_W = A_Q_HEADS * HEAD_DIM
A_KV_W = A_KV_HEADS * HEAD_DIM
B_W = B_GROUPS * B_HEADS_PER_GROUP * HEAD_DIM
B_OUT_W = B_HEADS_PER_GROUP * HEAD_DIM
IN_WIDTH = A_Q_W + 2 * A_KV_W + 3 * B_W + 2 * D_MODEL
QA_CB = 0
KA_CB = A_Q_W // LANES
VA_CB = (A_Q_W + A_KV_W) // LANES
QB_CB = (A_Q_W + 2 * A_KV_W) // LANES
KB_CB = QB_CB + B_W // LANES
VB_CB = KB_CB + B_W // LANES
GA_OFF = A_Q_W + 2 * A_KV_W + 3 * B_W
GB_OFF = GA_OFF + D_MODEL
IN_CB = IN_WIDTH // LANES

NEG = -1e30
ATT_SCALE = HEAD_DIM ** -0.5
ATT_QT = 128
ATT_GROUP = 8

MIB = 1024 * 1024


def _params(sem, vmem_mib):
    return pltpu.CompilerParams(dimension_semantics=sem, vmem_limit_bytes=vmem_mib * MIB)


def _rms(x, g):
    return x * lax.rsqrt(jnp.mean(x * x, axis=-1, keepdims=True) + EPS) * g


def _inproj_kernel(x_ref, g_ref, w_ref, o_ref, h_scr):
    @pl.when(pl.program_id(1) == 0)
    def _():
        h_scr[...] = _rms(x_ref[...], g_ref[...]).astype(BF16)

    o_ref[...] = jnp.dot(h_scr[...], w_ref[...], preferred_element_type=F32).astype(o_ref.dtype)


def _inproj(x2, g, w, tm=1024, tn=1024):
    n = w.shape[1]
    return pl.pallas_call(
        _inproj_kernel,
        out_shape=jax.ShapeDtypeStruct((N_TOK, n), BF16),
        grid=(N_TOK // tm, n // tn),
        in_specs=[
            pl.BlockSpec((tm, D_MODEL), lambda i, j: (i, 0)),
            pl.BlockSpec((1, D_MODEL), lambda i, j: (0, 0)),
            pl.BlockSpec((D_MODEL, tn), lambda i, j: (0, j)),
        ],
        out_specs=pl.BlockSpec((tm, tn), lambda i, j: (i, j)),
        scratch_shapes=[pltpu.VMEM((tm, D_MODEL), BF16)],
        compiler_params=_params(("parallel", "arbitrary"), 40),
        name="inproj",
    )(x2, g, w)


def _t5_bucket_np(rel):
    half = N_BUCKETS // 2
    max_exact = half // 2
    ret = np.where(rel > 0, half, 0)
    n = np.abs(rel)
    nf = np.maximum(n, 1).astype(np.float64)
    large = max_exact + (np.log(nf / max_exact) / math.log(MAX_DISTANCE / max_exact) * (half - max_exact)).astype(np.int64)
    large = np.minimum(large, half - 1)
    return (ret + np.where(n < max_exact, n, large)).astype(np.int32)


def _bucket_matrix(hw, dist_scale):
    w = ATT_QT + 2 * hw
    delta = (np.arange(w)[None, :] - hw) - np.arange(ATT_QT)[:, None]
    bkt = _t5_bucket_np(delta * dist_scale)
    return np.where(np.abs(delta) <= hw, bkt, -1).astype(np.int32)


def _residues_per_trip(dil):
    tiles = SEQ // dil // ATT_QT
    return max(1, min(dil, ATT_GROUP // tiles))


def _attn_kernel(tab_ref, sink_ref, q_ref, k_ref, v_ref, bkt_ref, *rest,
                 dil, hw, head_base, has_sink, with_lse):
    n_out = 2 if with_lse else 1
    outs, scr = rest[:n_out], rest[n_out:]
    o_ref = outs[0]
    kpad, vpad, bias_scr = scr[:3]
    if dil > 1:
        q32, k32, v32, o32, qres = scr[3:8]
        if with_lse:
            l32 = scr[8]
    seq = SEQ // dil
    qt = ATT_QT
    win = qt + 2 * hw
    n_res = _residues_per_trip(dil)
    head = head_base + pl.program_id(1)

    bkt = bkt_ref[...]
    bias = jnp.full((qt, win), NEG, F32)
    for b in range(N_BUCKETS):
        bias = jnp.where(bkt == b, tab_ref[b, head], bias)
    bias_scr[...] = bias

    zeros = jnp.zeros((hw, HEAD_DIM), BF16)
    for pad_ref in (kpad, vpad):
        for u in range(n_res):
            pad_ref[u, 0:hw, :] = zeros
            pad_ref[u, hw + seq:hw + seq + hw, :] = zeros
    if dil == 1:
        kpad[0, hw:hw + seq, :] = k_ref[0]
        vpad[0, hw:hw + seq, :] = v_ref[0]
    else:
        q32[...] = q_ref[0].astype(F32)
        k32[...] = k_ref[0].astype(F32)
        v32[...] = v_ref[0].astype(F32)

    def scores(q0, q, kw):
        s = lax.dot_general(q, kw, (((1,), (1,)), ((), ())), preferred_element_type=F32)
        s = s * ATT_SCALE + bias_scr[...]
        kpos = q0 - hw + lax.broadcasted_iota(I32, (qt, win), 1)
        return jnp.where((kpos >= 0) & (kpos < seq), s, NEG)

    def softmax(s):
        m = jnp.max(s, axis=-1, keepdims=True)
        if has_sink:
            sk = sink_ref[head]
            m = jnp.maximum(m, sk)
        p = jnp.exp(s - m)
        den = jnp.sum(p, axis=-1, keepdims=True)
        if has_sink:
            den = den + jnp.exp(sk - m)
        return p.astype(BF16), m, den

    def values(p, m, den, vw):
        o = jnp.dot(p, vw, preferred_element_type=F32) / den
        lse = jnp.broadcast_to(m + jnp.log(den), (qt, HEAD_DIM)) if with_lse else None
        return o, lse

    group = min(ATT_GROUP, seq // qt)

    def residues(g, carry):
        r0 = g * n_res
        if dil > 1:
            for u in range(n_res):
                kpad[u, hw:hw + seq, :] = k32[pl.ds(r0 + u, seq, stride=dil), :].astype(BF16)
                vpad[u, hw:hw + seq, :] = v32[pl.ds(r0 + u, seq, stride=dil), :].astype(BF16)
                qres[u] = q32[pl.ds(r0 + u, seq, stride=dil), :].astype(BF16)

        def body(t, c):
            work = [(u, pl.multiple_of((t * group + i) * qt, qt)) for u in range(n_res) for i in range(group)]
            q_src = (lambda u: qres.at[u]) if dil > 1 else (lambda u: q_ref.at[0])
            s_all = [scores(q0, q_src(u)[pl.ds(q0, qt), :], kpad[u, pl.ds(q0, win), :]) for u, q0 in work]
            p_all = [softmax(s) for s in s_all]
            done = [values(*pmd, vpad[u, pl.ds(q0, win), :]) for (u, q0), pmd in zip(work, p_all)]
            for (u, q0), (o, lse) in zip(work, done):
                if dil > 1:
                    o32[pl.ds(r0 + u + q0 * dil, qt, stride=dil), :] = o
                    if with_lse:
                        l32[pl.ds(r0 + u + q0 * dil, qt, stride=dil), :] = lse
                else:
                    o_ref[0, pl.ds(q0, qt), :] = o.astype(o_ref.dtype)
                    if with_lse:
                        outs[1][0, pl.ds(q0, qt), :] = lse
            return c

        lax.fori_loop(0, seq // (qt * group), body, 0)
        return carry

    if dil > 1:
        lax.fori_loop(0, dil // n_res, residues, 0)
        o_ref[0] = o32[...].astype(o_ref.dtype)
        if with_lse:
            outs[1][0] = l32[...]
    else:
        residues(0, 0)


def _attention(proj, rel_bias, sink, *, dil, hw, q_cb, k_cb, v_cb, n_heads, kv_group,
               head_base, has_sink, with_lse):
    seq = SEQ // dil
    pv = proj.reshape(BATCH, SEQ, IN_WIDTH)
    bkt = jnp.asarray(_bucket_matrix(hw, dil))
    win = ATT_QT + 2 * hw
    out_w = n_heads * HEAD_DIM
    blk = (1, SEQ, HEAD_DIM)
    o_spec = pl.BlockSpec(blk, lambda b, h: (b, 0, h))
    out_shape = [jax.ShapeDtypeStruct((BATCH, SEQ, out_w), BF16)]
    out_specs = [o_spec]
    if with_lse:
        out_shape.append(jax.ShapeDtypeStruct((BATCH, SEQ, out_w), F32))
        out_specs.append(o_spec)
    n_res = _residues_per_trip(dil)
    scratch = [pltpu.VMEM((n_res, seq + 2 * hw, HEAD_DIM), BF16),
               pltpu.VMEM((n_res, seq + 2 * hw, HEAD_DIM), BF16),
               pltpu.VMEM((ATT_QT, win), F32)]
    if dil > 1:
        scratch += [pltpu.VMEM((SEQ, HEAD_DIM), F32)] * 4 + [pltpu.VMEM((n_res, seq, HEAD_DIM), BF16)]
        if with_lse:
            scratch.append(pltpu.VMEM((SEQ, HEAD_DIM), F32))
    kern = functools.partial(_attn_kernel, dil=dil, hw=hw, head_base=head_base,
                             has_sink=has_sink, with_lse=with_lse)
    outs = pl.pallas_call(
        kern,
        out_shape=out_shape,
        grid=(BATCH, n_heads),
        in_specs=[
            pl.BlockSpec(memory_space=pltpu.SMEM),
            pl.BlockSpec(memory_space=pltpu.SMEM),
            pl.BlockSpec(blk, lambda b, h: (b, 0, q_cb + h)),
            pl.BlockSpec(blk, lambda b, h: (b, 0, k_cb + h // kv_group)),
            pl.BlockSpec(blk, lambda b, h: (b, 0, v_cb + h // kv_group)),
            pl.BlockSpec((ATT_QT, win), lambda b, h: (0, 0)),
        ],
        out_specs=out_specs,
        scratch_shapes=scratch,
        compiler_params=_params(("parallel", "arbitrary"), 40),
        name=f"attn_d{dil}_h{head_base}",
    )(rel_bias, sink, pv, pv, pv, bkt)
    return [o.reshape(N_TOK, out_w) for o in outs]


def _outproj_kernel(oa_ref, o0_ref, o1_ref, o2_ref, l0_ref, l1_ref, l2_ref, ga_ref, gb_ref,
                    x_ref, woa_ref, wob_ref, wout_ref, gn_ref, x1_ref, h2_ref):
    l0, l1, l2 = l0_ref[...], l1_ref[...], l2_ref[...]
    mx = jnp.maximum(jnp.maximum(l0, l1), l2)
    e0, e1, e2 = jnp.exp(l0 - mx), jnp.exp(l1 - mx), jnp.exp(l2 - mx)
    den = e0 + e1 + e2
    ob = ((e0 / den) * o0_ref[...].astype(F32) + (e1 / den) * o1_ref[...].astype(F32)
          + (e2 / den) * o2_ref[...].astype(F32))
    ya = jnp.dot(oa_ref[...], woa_ref[...], preferred_element_type=F32)
    yb = jnp.dot(ob.astype(BF16), wob_ref[...], preferred_element_type=F32)
    merged = (jax.nn.sigmoid(ga_ref[...].astype(F32)) * ya
              + jax.nn.sigmoid(gb_ref[...].astype(F32)) * yb)
    x1 = x_ref[...] + jnp.dot(merged.astype(BF16), wout_ref[...], preferred_element_type=F32)
    x1_ref[...] = x1
    h2_ref[...] = _rms(x1, gn_ref[...]).astype(BF16)


def _outproj(oa, obs, lses, proj, x2, w_oa, w_ob, w_out, g_ffn, tm=256):
    row = lambda w: pl.BlockSpec((tm, w), lambda i: (i, 0))
    const = lambda shape: pl.BlockSpec(shape, lambda i: (0, 0), pipeline_mode=pl.Buffered(1))
    return pl.pallas_call(
        _outproj_kernel,
        out_shape=[jax.ShapeDtypeStruct((N_TOK, D_MODEL), F32),
                   jax.ShapeDtypeStruct((N_TOK, D_MODEL), BF16)],
        grid=(N_TOK // tm,),
        in_specs=[row(A_Q_W), row(B_OUT_W), row(B_OUT_W), row(B_OUT_W),
                  row(B_OUT_W), row(B_OUT_W), row(B_OUT_W),
                  pl.BlockSpec((tm, D_MODEL), lambda i: (i, GA_OFF // D_MODEL)),
                  pl.BlockSpec((tm, D_MODEL), lambda i: (i, GB_OFF // D_MODEL)),
                  row(D_MODEL),
                  const((A_Q_W, D_MODEL)), const((B_OUT_W, D_MODEL)), const((D_MODEL, D_MODEL)),
                  const((1, D_MODEL))],
        out_specs=[row(D_MODEL), row(D_MODEL)],
        compiler_params=_params(("parallel",), 48),
        name="outproj",
    )(oa, *obs, *lses, proj, proj, x2, w_oa, w_ob, w_out, g_ffn)


_FAR = 1024.0
RT_SLICE = 256
RT_HEADS = 4
UP_CHUNK = 256
UP_ROWS = 256

_LANE_GATHER = lax.GatherDimensionNumbers(
    offset_dims=(), collapsed_slice_dims=(1,), start_index_map=(1,),
    operand_batching_dims=(0,), start_indices_batching_dims=(0,))


def _tree(op, xs):
    xs = list(xs)
    while len(xs) > 1:
        nxt = [op(xs[i], xs[i + 1]) for i in range(0, len(xs) - 1, 2)]
        if len(xs) % 2:
            nxt.append(xs[-1])
        xs = nxt
    return xs[0]


def _all_sublanes(op, x):
    for shift in (4, 2, 1):
        x = op(x, pltpu.roll(x, shift, axis=0))
    return x


def _pop_max(tiles, sub_iota, payload=None):
    m = _all_sublanes(jnp.maximum, _tree(jnp.maximum, tiles))
    first = _tree(jnp.minimum, [jnp.where(t == m, float(SUBLANES * v), _FAR) for v, t in enumerate(tiles)])
    row = _all_sublanes(jnp.minimum, first + sub_iota)
    base = row - sub_iota
    hits = [base == float(SUBLANES * v) for v in range(len(tiles))]
    if payload is None:
        val = row
    else:
        val = _all_sublanes(jnp.maximum, _tree(jnp.maximum, [jnp.where(h, p, -1.0)
                                                            for h, p in zip(hits, payload)]))
    return m, val, [jnp.where(h, -jnp.inf, t) for h, t in zip(hits, tiles)]


def _route_scores(h, q_ref, keys_ref, s_scr):
    for c in range(2):
        col = pl.multiple_of((2 * h + c) * LANES, LANES)
        s_scr[c] = lax.dot_general(keys_ref[2 * h + c], q_ref[:, pl.ds(col, LANES)],
                                   (((1,), (1,)), ((), ())), preferred_element_type=F32)


def _batcher_pairs(n):
    size = 16
    pairs = []
    p = 1
    while p < size:
        k = p
        while k >= 1:
            for j in range(k % p, size - k, 2 * k):
                for i in range(min(k, size - j - k)):
                    if (i + j) // (2 * p) == (i + j + k) // (2 * p) and i + j + k < n:
                        pairs.append((i + j, i + j + k))
            k //= 2
        p *= 2
    return pairs


_FAR_ID = 1e9


def _top_sorted(vals, ids, k):
    vals, ids = list(vals), list(ids)
    n = len(vals)
    for i, j in _batcher_pairs(n):
        up = vals[j] > vals[i]
        vals[i], vals[j] = jnp.where(up, vals[j], vals[i]), jnp.where(up, vals[i], vals[j])
        ids[i], ids[j] = jnp.where(up, ids[j], ids[i]), jnp.where(up, ids[i], ids[j])
    top_v, top_i = [], []
    for it in range(k):
        m = _all_sublanes(jnp.maximum, vals[0])
        first = _all_sublanes(jnp.minimum, jnp.where(vals[0] == m, ids[0], _FAR_ID))
        win = ids[0] == first
        top_v.append(m)
        top_i.append(first)
        for r in range(min(k - it, n)):
            if r + 1 < n:
                vals[r] = jnp.where(win, vals[r + 1], vals[r])
                ids[r] = jnp.where(win, ids[r + 1], ids[r])
            else:
                vals[r] = jnp.where(win, -jnp.inf, vals[r])
    runner_up = _all_sublanes(jnp.maximum, vals[0])
    tie = jnp.zeros_like(runner_up)
    for a, b in zip(top_v, top_v[1:] + [runner_up]):
        tie = jnp.where(a == b, 1.0, tie)
    return top_v, top_i, tie


def _route_topk(s_scr, ts_scr, ti_scr, bs_scr, et_ref, gt_ref, *, exact):
    k = PEER_TOPK
    nk = float(PEER_NKEYS)
    sub_iota = lax.broadcasted_iota(I32, (SUBLANES, LANES), 0).astype(F32)
    halves = [slice(i * LANES, (i + 1) * LANES) for i in range(RT_SLICE // LANES)]
    tie = None if exact else jnp.zeros((SUBLANES, LANES), F32)

    chains = []
    for c in range(2):
        for lanes in halves:
            tiles = [s_scr[c, SUBLANES * v:SUBLANES * (v + 1), lanes] for v in range(PEER_NKEYS // SUBLANES)]
            chains.append([c, lanes, tiles])
    if exact:
        for kk in range(k):
            for chain in chains:
                c, lanes, tiles = chain
                m, row, chain[2] = _pop_max(tiles, sub_iota)
                ts_scr[c, kk:kk + 1, lanes] = m[0:1]
                ti_scr[c, kk:kk + 1, lanes] = row[0:1]
    else:
        row_ids = [float(SUBLANES * v) + sub_iota for v in range(PEER_NKEYS // SUBLANES)]
        for c, lanes, tiles in chains:
            top_v, top_i, t = _top_sorted(tiles, row_ids, k)
            tie = jnp.maximum(tie, t)
            for kk in range(k):
                ts_scr[c, kk:kk + 1, lanes] = top_v[kk][0:1]
                ti_scr[c, kk:kk + 1, lanes] = top_i[kk][0:1]

    def candidates(lanes):
        s2 = [ts_scr[1, 0:8, lanes], ts_scr[1, 8:16, lanes]]
        i2 = [ti_scr[1, 0:8, lanes], ti_scr[1, 8:16, lanes]]
        s1 = lambda k1: ts_scr[0, k1:k1 + 1, lanes]
        i1 = lambda k1: ti_scr[0, k1:k1 + 1, lanes] * nk
        tiles = [s1(0) + s2[0], s1(0) + s2[1]]
        pay = [i1(0) + i2[0], i1(0) + i2[1]]
        for k1 in range(1, 8):
            allowed = k // (k1 + 1)
            t = s1(k1) + s2[0]
            tiles.append(t if allowed >= SUBLANES else jnp.where(sub_iota < float(allowed), t, -jnp.inf))
            pay.append(i1(k1) + i2[0])
        tiles.append(ts_scr[0, 8:16, lanes] + ts_scr[1, 0:1, lanes])
        pay.append(ti_scr[0, 8:16, lanes] * nk + ti_scr[1, 0:1, lanes])
        return tiles, pay

    cands = [candidates(lanes) for lanes in halves]
    if exact:
        tiles = [c[0] for c in cands]
        for kk in range(k):
            for i, lanes in enumerate(halves):
                m, expert, tiles[i] = _pop_max(tiles[i], sub_iota, payload=cands[i][1])
                bs_scr[kk:kk + 1, lanes] = m[0:1]
                et_ref[kk:kk + 1, lanes] = expert[0:1]
    else:
        for (tiles, pay), lanes in zip(cands, halves):
            top_v, top_i, t = _top_sorted(tiles, pay, k)
            tie = jnp.maximum(tie, t)
            for kk in range(k):
                bs_scr[kk:kk + 1, lanes] = top_v[kk][0:1]
                et_ref[kk:kk + 1, lanes] = top_i[kk][0:1]
    for lanes in halves:
        bs = bs_scr[:, lanes]
        ex = jnp.exp(bs - jnp.max(bs, axis=0, keepdims=True))
        gt_ref[:, lanes] = ex / jnp.sum(ex, axis=0, keepdims=True)
    return tie


def _expert_up(j, h_ref, u_ref, e_ref, act_ref, *, te):
    tb = h_ref.shape[0]
    u = [u_ref[c * UP_CHUNK:(c + 1) * UP_CHUNK, :].astype(BF16) for c in range(te // UP_CHUNK)]
    for m in range(tb // UP_ROWS):
        rows = slice(m * UP_ROWS, (m + 1) * UP_ROWS)
        e = e_ref[rows, :]
        row = e >> 7
        col = (e & (PEER_NKEYS - 1))[..., None]
        acc = act_ref[rows, :]
        h = h_ref[rows, :]
        for c in range(te // UP_CHUNK):
            dense = lax.dot_general(h, u[c], (((1,), (1,)), ((), ())), preferred_element_type=F32)
            for q in range(UP_CHUNK // LANES):
                got = lax.gather(dense[:, q * LANES:(q + 1) * LANES], col, _LANE_GATHER, (1, 1),
                                 mode=lax.GatherScatterMode.PROMISE_IN_BOUNDS)
                acc = jnp.where(row == (j * te + c * UP_CHUNK) // LANES + q, got, acc)
        act_ref[rows, :] = acc


def _route_kernel(h_ref, wq_ref, keys_ref, v_ref, e_ref, g_ref, vb_ref,
                  q_scr, s_scr, ts_scr, ti_scr, bs_scr, et_scr, gt_scr):
    vb_ref[...] = v_ref[...].astype(BF16)
    q_scr[...] = jnp.dot(h_ref[...], wq_ref[...], preferred_element_type=F32).astype(BF16)

    def heads(g, carry):
        hs = [g * RT_HEADS + u for u in range(RT_HEADS)]
        for u, h in enumerate(hs):
            _route_scores(h, q_scr, keys_ref, s_scr.at[u])
        args = [(s_scr.at[u], ts_scr.at[u], ti_scr.at[u], bs_scr.at[u], et_scr.at[h], gt_scr.at[h])
                for u, h in enumerate(hs)]
        ties = [_route_topk(*a, exact=False) for a in args]
        for a, tie in zip(args, ties):
            @pl.when(jnp.max(tie) > 0.0)
            def _():
                _route_topk(*a, exact=True)

        return carry

    lax.fori_loop(0, PEER_HEADS // RT_HEADS, heads, 0)

    for half in range(RT_SLICE // LANES):
        lanes = slice(half * LANES, (half + 1) * LANES)
        et = jnp.concatenate([et_scr[h, :, lanes] for h in range(PEER_HEADS)], axis=0)
        gt = jnp.concatenate([gt_scr[h, :, lanes] for h in range(PEER_HEADS)], axis=0)
        e_ref[lanes, :] = et.T.astype(I32)
        g_ref[lanes, :] = gt.T


def _route(h2, wq, keys, v):
    k = PEER_TOPK
    steps = N_TOK // RT_SLICE
    tok = pl.BlockSpec((RT_SLICE, PEER_PICKS), lambda i: (i, 0))
    v_slab = pl.BlockSpec((PEER_EXPERTS // steps, D_MODEL), lambda i: (i, 0))
    return pl.pallas_call(
        _route_kernel,
        out_shape=[jax.ShapeDtypeStruct((N_TOK, PEER_PICKS), I32),
                   jax.ShapeDtypeStruct((N_TOK, PEER_PICKS), F32),
                   jax.ShapeDtypeStruct((PEER_EXPERTS, D_MODEL), BF16)],
        grid=(steps,),
        in_specs=[pl.BlockSpec((RT_SLICE, D_MODEL), lambda i: (i, 0)),
                  pl.BlockSpec((D_MODEL, PEER_HEADS * PEER_QDIM), lambda i: (0, 0),
                               pipeline_mode=pl.Buffered(1)),
                  pl.BlockSpec((PEER_HEADS * 2, PEER_NKEYS, PEER_QDIM // 2), lambda i: (0, 0, 0)),
                  v_slab],
        out_specs=[tok, tok, v_slab],
        scratch_shapes=[
            pltpu.VMEM((RT_SLICE, PEER_HEADS * PEER_QDIM), BF16),
            pltpu.VMEM((RT_HEADS, 2, PEER_NKEYS, RT_SLICE), F32),
            pltpu.VMEM((RT_HEADS, 2, k, RT_SLICE), F32),
            pltpu.VMEM((RT_HEADS, 2, k, RT_SLICE), F32),
            pltpu.VMEM((RT_HEADS, k, RT_SLICE), F32),
            pltpu.VMEM((PEER_HEADS, k, RT_SLICE), F32),
            pltpu.VMEM((PEER_HEADS, k, RT_SLICE), F32),
        ],
        compiler_params=_params(("parallel",), 40),
        name="peer_route",
    )(h2, wq, keys, v)


def _peer_up_kernel(h_ref, u_ref, e_ref, act_ref, *, te):
    j = pl.program_id(1)

    @pl.when(j == 0)
    def _():
        act_ref[...] = jnp.zeros_like(act_ref)

    _expert_up(j, h_ref, u_ref, e_ref, act_ref, te=te)


def _peer_up(h2, u, e, tb=2048, te=1024):
    return pl.pallas_call(
        functools.partial(_peer_up_kernel, te=te),
        out_shape=jax.ShapeDtypeStruct((N_TOK, PEER_PICKS), F32),
        grid=(N_TOK // tb, PEER_EXPERTS // te),
        in_specs=[pl.BlockSpec((tb, D_MODEL), lambda i, j: (i, 0)),
                  pl.BlockSpec((te, D_MODEL), lambda i, j: (j, 0)),
                  pl.BlockSpec((tb, PEER_PICKS), lambda i, j: (i, 0))],
        out_specs=pl.BlockSpec((tb, PEER_PICKS), lambda i, j: (i, 0)),
        compiler_params=_params(("parallel", "arbitrary"), 56),
        name="peer_up",
    )(h2, u, e)


def _peer_coef_kernel(e_ref, g_ref, act_ref, p_ref, row_scr, col_scr, w_scr, c_scr, *, tp):
    e = e_ref[...]
    row_scr[...] = (e >> 7).astype(F32)
    col_scr[...] = (e & (PEER_NKEYS - 1)).astype(F32)
    act = act_ref[...]
    w_scr[...] = g_ref[...] * (0.5 * act * (1.0 + lax.erf(act * math.sqrt(0.5))))
    sub = 16
    iota = lax.broadcasted_iota(I32, (PEER_NKEYS // sub, sub, PEER_PICKS), 0) * sub \
        + lax.broadcasted_iota(I32, (PEER_NKEYS // sub, sub, PEER_PICKS), 1)
    iota = iota.astype(F32).astype(BF16)
    one = jnp.ones((), BF16)
    zero = jnp.zeros((), BF16)

    def bcast(ref, t):
        return jnp.broadcast_to(ref[pl.ds(t, 1), :], (sub, PEER_PICKS)).astype(BF16)[None]

    def products(t0):
        for t in range(sub):
            rows, cols, w = bcast(row_scr, t0 + t), bcast(col_scr, t0 + t), bcast(w_scr, t0 + t)
            left = jnp.where(iota == rows, one, zero).reshape(PEER_NKEYS, PEER_PICKS)
            right = jnp.where(iota == cols, w, zero).reshape(PEER_NKEYS, PEER_PICKS)
            c_scr[t0 + t] = lax.dot_general(left, right, (((1,), (1,)), ((), ())),
                                            preferred_element_type=F32)

    def regroup(t0):
        sw = jnp.swapaxes(c_scr[pl.ds(t0, sub)], 0, 1)
        for k1 in range(PEER_NKEYS):
            p_ref[pl.ds(t0, sub), k1 * PEER_NKEYS:(k1 + 1) * PEER_NKEYS] = sw[k1].astype(p_ref.dtype)

    def block(blk, carry):
        base = pl.multiple_of(blk * COEF_UNROLL, COEF_UNROLL)
        for t0 in range(0, COEF_UNROLL, sub):
            products(base + t0)
            if t0:
                regroup(base + t0 - sub)
        regroup(base + COEF_UNROLL - sub)
        return carry

    lax.fori_loop(0, tp // COEF_UNROLL, block, 0)


COEF_UNROLL = 64


def _peer_coef(e, g, act, tp=128):
    tok = pl.BlockSpec((tp, PEER_PICKS), lambda i: (i, 0))
    return pl.pallas_call(
        functools.partial(_peer_coef_kernel, tp=tp),
        out_shape=jax.ShapeDtypeStruct((N_TOK, PEER_EXPERTS), BF16),
        grid=(N_TOK // tp,),
        in_specs=[tok, tok, tok],
        out_specs=pl.BlockSpec((tp, PEER_EXPERTS), lambda i: (i, 0)),
        scratch_shapes=[pltpu.VMEM((tp, PEER_PICKS), F32),
                        pltpu.VMEM((tp, PEER_PICKS), F32),
                        pltpu.VMEM((tp, PEER_PICKS), F32),
                        pltpu.VMEM((tp, PEER_NKEYS, PEER_NKEYS), F32)],
        compiler_params=_params(("parallel",), 32),
        name="peer_coef",
    )(e, g, act)


def _peer_down_kernel(p_ref, v_ref, x_ref, g_ref, o_ref):
    j = pl.program_id(1)

    @pl.when(j == 0)
    def _():
        o_ref[...] = x_ref[...]

    o_ref[...] += jnp.dot(p_ref[...], v_ref[...], preferred_element_type=F32)

    @pl.when(j == pl.num_programs(1) - 1)
    def _():
        o_ref[...] = _rms(o_ref[...], g_ref[...])


def _peer_down(coef, v, x1, g_final, tm=1024, tk=2048):
    return pl.pallas_call(
        _peer_down_kernel,
        out_shape=jax.ShapeDtypeStruct((N_TOK, D_MODEL), F32),
        grid=(N_TOK // tm, PEER_EXPERTS // tk),
        in_specs=[pl.BlockSpec((tm, tk), lambda i, j: (i, j)),
                  pl.BlockSpec((tk, D_MODEL), lambda i, j: (j, 0)),
                  pl.BlockSpec((tm, D_MODEL), lambda i, j: (i, 0), pipeline_mode=pl.Buffered(1)),
                  pl.BlockSpec((1, D_MODEL), lambda i, j: (0, 0))],
        out_specs=pl.BlockSpec((tm, D_MODEL), lambda i, j: (i, 0)),
        compiler_params=_params(("parallel", "arbitrary"), 60),
        name="peer_down",
    )(coef, v, x1, g_final)


def kernel(x, rel_bias, norm_mix_g, w_in, sink_a, w_oa, w_ob, w_out, norm_ffn_g,
           peer_wq, peer_keys, peer_u, peer_v, norm_final_g):
    assert x.shape == (BATCH, SEQ, D_MODEL) and w_in.shape[0] == 1
    x2 = x.reshape(N_TOK, D_MODEL)
    proj = _inproj(x2, norm_mix_g, w_in[0].astype(BF16))

    (oa,) = _attention(proj, rel_bias, sink_a[0], dil=1, hw=A_HALF_WINDOW,
                       q_cb=QA_CB, k_cb=KA_CB, v_cb=VA_CB, n_heads=A_Q_HEADS, kv_group=A_GROUP,
                       head_base=0, has_sink=True, with_lse=False)
    obs, lses = [], []
    for gi, dil in enumerate(B_DILATIONS):
        off = gi * B_HEADS_PER_GROUP
        o, lse = _attention(proj, rel_bias, sink_a[0], dil=dil, hw=B_HALF_SPAN,
                            q_cb=QB_CB + off, k_cb=KB_CB + off, v_cb=VB_CB + off,
                            n_heads=B_HEADS_PER_GROUP, kv_group=1,
                            head_base=A_Q_HEADS + off, has_sink=False, with_lse=True)
        obs.append(o)
        lses.append(lse)

    x1, h2 = _outproj(oa, obs, lses, proj, x2, w_oa[0].astype(BF16), w_ob[0].astype(BF16),
                      w_out[0].astype(BF16), norm_ffn_g)

    keys = peer_keys[0].reshape(PEER_HEADS * 2, PEER_NKEYS, PEER_QDIM // 2).astype(BF16)
    e, gate, v_bf16 = _route(h2, peer_wq[0].astype(BF16), keys, peer_v[0])
    act = _peer_up(h2, peer_u[0], e)
    coef = _peer_coef(e, gate, act)
    out = _peer_down(coef, v_bf16, x1, norm_final_g.reshape(1, D_MODEL))
    return out.reshape(BATCH, SEQ, D_MODEL)
```

```python
import functools
import math

import numpy as np
import jax
import jax.numpy as jnp
from jax import lax
from jax.experimental import pallas as pl
from jax.experimental.pallas import tpu as pltpu

F32 = jnp.float32
BF16 = jnp.bfloat16
I32 = jnp.int32

D_MODEL = 2048
BATCH = 2
SEQ = 4096
N_TOK = BATCH * SEQ
HEAD_DIM = 128
LANES = 128
SUBLANES = 8

A_Q_HEADS = 8
A_KV_HEADS = 2
A_GROUP = A_Q_HEADS // A_KV_HEADS
A_HALF_WINDOW = 128
B_DILATIONS = (1, 4, 16)
B_GROUPS = 3
B_HEADS_PER_GROUP = 4
B_HALF_SPAN = 64
N_BUCKETS = 32
MAX_DISTANCE = 1024
N_ATTN_HEADS = A_Q_HEADS + B_GROUPS * B_HEADS_PER_GROUP

PEER_HEADS = 8
PEER_NKEYS = 128
PEER_KEY_BITS = PEER_NKEYS.bit_length() - 1
assert 1 << PEER_KEY_BITS == PEER_NKEYS
PEER_EXPERTS = PEER_NKEYS * PEER_NKEYS
PEER_QDIM = 256
PEER_TOPK = 16
PEER_PICKS = PEER_HEADS * PEER_TOPK
EPS = 1e-6

A_Q_W = A_Q_HEADS * HEAD_DIM
A_KV_W = A_KV_HEADS * HEAD_DIM
B_W = B_GROUPS * B_HEADS_PER_GROUP * HEAD_DIM
B_OUT_W = B_HEADS_PER_GROUP * HEAD_DIM
IN_WIDTH = A_Q_W + 2 * A_KV_W + 3 * B_W + 2 * D_MODEL
QA_CB = 0
KA_CB = A_Q_W // LANES
VA_CB = (A_Q_W + A_KV_W) // LANES
QB_CB = (A_Q_W + 2 * A_KV_W) // LANES
KB_CB = QB_CB + B_W // LANES
VB_CB = KB_CB + B_W // LANES
GA_OFF = A_Q_W + 2 * A_KV_W + 3 * B_W
GB_OFF = GA_OFF + D_MODEL
IN_CB = IN_WIDTH // LANES

NEG = -1e30
ATT_SCALE = HEAD_DIM ** -0.5
ATT_QT = 128
ATT_GROUP = 8

MIB = 1024 * 1024


def _params(sem, vmem_mib):
    return pltpu.CompilerParams(dimension_semantics=sem, vmem_limit_bytes=vmem_mib * MIB)


def _rms(x, g):
    return x * lax.rsqrt(jnp.mean(x * x, axis=-1, keepdims=True) + EPS) * g


def _inproj_kernel(x_ref, g_ref, w_ref, o_ref, h_scr):
    @pl.when(pl.program_id(1) == 0)
    def _():
        h_scr[...] = _rms(x_ref[...], g_ref[...]).astype(BF16)

    o_ref[...] = jnp.dot(h_scr[...], w_ref[...], preferred_element_type=F32).astype(o_ref.dtype)


def _inproj(x2, g, w, tm=1024, tn=1024):
    n = w.shape[1]
    return pl.pallas_call(
        _inproj_kernel,
        out_shape=jax.ShapeDtypeStruct((N_TOK, n), BF16),
        grid=(N_TOK // tm, n // tn),
        in_specs=[
            pl.BlockSpec((tm, D_MODEL), lambda i, j: (i, 0)),
            pl.BlockSpec((1, D_MODEL), lambda i, j: (0, 0)),
            pl.BlockSpec((D_MODEL, tn), lambda i, j: (0, j)),
        ],
        out_specs=pl.BlockSpec((tm, tn), lambda i, j: (i, j)),
        scratch_shapes=[pltpu.VMEM((tm, D_MODEL), BF16)],
        compiler_params=_params(("parallel", "arbitrary"), 40),
        name="inproj",
    )(x2, g, w)


def _t5_bucket_np(rel):
    half = N_BUCKETS // 2
    max_exact = half // 2
    ret = np.where(rel > 0, half, 0)
    n = np.abs(rel)
    nf = np.maximum(n, 1).astype(np.float64)
    large = max_exact + (np.log(nf / max_exact) / math.log(MAX_DISTANCE / max_exact) * (half - max_exact)).astype(np.int64)
    large = np.minimum(large, half - 1)
    return (ret + np.where(n < max_exact, n, large)).astype(np.int32)


def _bucket_matrix(hw, dist_scale):
    w = ATT_QT + 2 * hw
    delta = (np.arange(w)[None, :] - hw) - np.arange(ATT_QT)[:, None]
    bkt = _t5_bucket_np(delta * dist_scale)
    return np.where(np.abs(delta) <= hw, bkt, -1).astype(np.int32)


def _residues_per_trip(dil):
    tiles = SEQ // dil // ATT_QT
    return max(1, min(dil, ATT_GROUP // tiles))


def _bias_tile(bkt_ref, tab_ref, head, bias_scr):
    bkt = bkt_ref[...]
    bias = jnp.full(bkt.shape, NEG, F32)
    for b in range(N_BUCKETS):
        bias = jnp.where(bkt == b, tab_ref[b, head], bias)
    bias_scr[...] = bias


def _banded_attention(q_ref, k_ref, v_ref, bias_scr, kpad, vpad, stage, emit, *, dil, hw, sink):
    seq = SEQ // dil
    qt = ATT_QT
    win = qt + 2 * hw
    n_res = _residues_per_trip(dil)

    zeros = jnp.zeros((hw, HEAD_DIM), BF16)
    for pad_ref in (kpad, vpad):
        for u in range(n_res):
            pad_ref[u, 0:hw, :] = zeros
            pad_ref[u, hw + seq:hw + seq + hw, :] = zeros
    if dil == 1:
        kpad[0, hw:hw + seq, :] = k_ref[...]
        vpad[0, hw:hw + seq, :] = v_ref[...]
    else:
        q32, k32, v32, qres = stage
        q32[...] = q_ref[...].astype(F32)
        k32[...] = k_ref[...].astype(F32)
        v32[...] = v_ref[...].astype(F32)

    def scores(q0, q, kw):
        s = lax.dot_general(q, kw, (((1,), (1,)), ((), ())), preferred_element_type=F32)
        s = s * ATT_SCALE + bias_scr[...]
        kpos = q0 - hw + lax.broadcasted_iota(I32, (qt, win), 1)
        return jnp.where((kpos >= 0) & (kpos < seq), s, NEG)

    def softmax(s):
        m = jnp.max(s, axis=-1, keepdims=True)
        if sink is not None:
            m = jnp.maximum(m, sink)
        p = jnp.exp(s - m)
        den = jnp.sum(p, axis=-1, keepdims=True)
        if sink is not None:
            den = den + jnp.exp(sink - m)
        return p.astype(BF16), m, den

    group = min(ATT_GROUP, seq // qt)

    def residues(g, carry):
        r0 = g * n_res
        if dil > 1:
            for u in range(n_res):
                kpad[u, hw:hw + seq, :] = k32[pl.ds(r0 + u, seq, stride=dil), :].astype(BF16)
                vpad[u, hw:hw + seq, :] = v32[pl.ds(r0 + u, seq, stride=dil), :].astype(BF16)
                qres[u] = q32[pl.ds(r0 + u, seq, stride=dil), :].astype(BF16)

        def body(t, c):
            work = [(u, pl.multiple_of((t * group + i) * qt, qt)) for u in range(n_res) for i in range(group)]
            q_src = (lambda u: qres.at[u]) if dil > 1 else (lambda u: q_ref)
            s_all = [scores(q0, q_src(u)[pl.ds(q0, qt), :], kpad[u, pl.ds(q0, win), :]) for u, q0 in work]
            p_all = [softmax(s) for s in s_all]
            o_all = [jnp.dot(p, vpad[u, pl.ds(q0, win), :], preferred_element_type=F32) / den
                     for (u, q0), (p, m, den) in zip(work, p_all)]
            for (u, q0), (p, m, den), o in zip(work, p_all, o_all):
                emit(q0 if dil == 1 else r0 + u + q0 * dil, dil, o, m, den)
            return c

        lax.fori_loop(0, seq // (qt * group), body, 0)
        return carry

    if dil > 1:
        lax.fori_loop(0, dil // n_res, residues, 0)
    else:
        residues(0, 0)


def _attn_scratch(dil, hw):
    n_res = _residues_per_trip(dil)
    seq = SEQ // dil
    return [pltpu.VMEM((n_res, seq + 2 * hw, HEAD_DIM), BF16),
            pltpu.VMEM((n_res, seq + 2 * hw, HEAD_DIM), BF16),
            pltpu.VMEM((ATT_QT, ATT_QT + 2 * hw), F32)]


def _windowed_kernel(tab_ref, sink_ref, q_ref, k_ref, v_ref, bkt_ref, o_ref, kpad, vpad, bias_scr):
    head = pl.program_id(1)
    _bias_tile(bkt_ref, tab_ref, head, bias_scr)

    def emit(start, stride, o, m, den):
        o_ref[0, pl.ds(start, ATT_QT), :] = o.astype(o_ref.dtype)

    _banded_attention(q_ref.at[0], k_ref.at[0], v_ref.at[0], bias_scr, kpad, vpad, None, emit,
                      dil=1, hw=A_HALF_WINDOW, sink=sink_ref[head])


def _windowed_attention(proj, rel_bias, sink):
    pv = proj.reshape(BATCH, SEQ, IN_WIDTH)
    hw = A_HALF_WINDOW
    blk = (1, SEQ, HEAD_DIM)
    out = pl.pallas_call(
        _windowed_kernel,
        out_shape=jax.ShapeDtypeStruct((BATCH, SEQ, A_Q_W), BF16),
        grid=(BATCH, A_Q_HEADS),
        in_specs=[
            pl.BlockSpec(memory_space=pltpu.SMEM),
            pl.BlockSpec(memory_space=pltpu.SMEM),
            pl.BlockSpec(blk, lambda b, h: (b, 0, QA_CB + h)),
            pl.BlockSpec(blk, lambda b, h: (b, 0, KA_CB + h // A_GROUP)),
            pl.BlockSpec(blk, lambda b, h: (b, 0, VA_CB + h // A_GROUP)),
            pl.BlockSpec((ATT_QT, ATT_QT + 2 * hw), lambda b, h: (0, 0)),
        ],
        out_specs=pl.BlockSpec(blk, lambda b, h: (b, 0, h)),
        scratch_shapes=_attn_scratch(1, hw),
        compiler_params=_params(("parallel", "arbitrary"), 32),
        name="attn_windowed",
    )(rel_bias, sink, pv, pv, pv, jnp.asarray(_bucket_matrix(hw, 1)))
    return out.reshape(N_TOK, A_Q_W)


MERGE_ROWS = 512


def _dilated_kernel(tab_ref, *refs):
    n_g = B_GROUPS
    qkv = [refs[3 * g:3 * g + 3] for g in range(n_g)]
    bkts = refs[3 * n_g:4 * n_g]
    o_ref = refs[4 * n_g]
    scr = list(refs[4 * n_g + 1:])
    pads = [scr[3 * g:3 * g + 3] for g in range(n_g)]
    q32, k32, v32, o32, l32 = scr[3 * n_g:3 * n_g + 5]
    qres = scr[3 * n_g + 5:]
    slot = pl.program_id(1)

    for g, dil in enumerate(B_DILATIONS):
        kpad, vpad, bias_scr = pads[g]
        _bias_tile(bkts[g], tab_ref, A_Q_HEADS + g * B_HEADS_PER_GROUP + slot, bias_scr)

        def emit(start, stride, o, m, den, g=g):
            rows = pl.ds(start, ATT_QT) if stride == 1 else pl.ds(start, ATT_QT, stride=stride)
            o32[g, rows, :] = o
            l32[g, rows, :] = jnp.broadcast_to(m + jnp.log(den), (ATT_QT, HEAD_DIM))

        q_ref, k_ref, v_ref = qkv[g]
        stage = None if dil == 1 else (q32, k32, v32, qres[g])
        _banded_attention(q_ref.at[0], k_ref.at[0], v_ref.at[0], bias_scr, kpad, vpad, stage, emit,
                          dil=dil, hw=B_HALF_SPAN, sink=None)

    def merge(c, carry):
        rows = pl.ds(pl.multiple_of(c * MERGE_ROWS, MERGE_ROWS), MERGE_ROWS)
        lses = [l32[g, rows, :] for g in range(n_g)]
        mx = functools.reduce(jnp.maximum, lses)
        es = [jnp.exp(l - mx) for l in lses]
        den = functools.reduce(jnp.add, es)
        o_ref[0, rows, :] = functools.reduce(
            jnp.add, [(e / den) * o32[g, rows, :] for g, e in enumerate(es)]).astype(o_ref.dtype)
        return carry

    lax.fori_loop(0, SEQ // MERGE_ROWS, merge, 0)


def _dilated_attention(proj, rel_bias):
    pv = proj.reshape(BATCH, SEQ, IN_WIDTH)
    hw = B_HALF_SPAN
    blk = (1, SEQ, HEAD_DIM)
    in_specs = [pl.BlockSpec(memory_space=pltpu.SMEM)]
    args = [rel_bias]
    for g in range(B_GROUPS):
        for cb in (QB_CB, KB_CB, VB_CB):
            in_specs.append(pl.BlockSpec(blk, lambda b, h, c=cb + g * B_HEADS_PER_GROUP: (b, 0, c + h)))
            args.append(pv)
    for dil in B_DILATIONS:
        in_specs.append(pl.BlockSpec((ATT_QT, ATT_QT + 2 * hw), lambda b, h: (0, 0)))
        args.append(jnp.asarray(_bucket_matrix(hw, dil)))
    scratch = []
    for dil in B_DILATIONS:
        scratch += _attn_scratch(dil, hw)
    scratch += [pltpu.VMEM((SEQ, HEAD_DIM), F32)] * 3
    scratch += [pltpu.VMEM((B_GROUPS, SEQ, HEAD_DIM), F32)] * 2
    scratch += [pltpu.VMEM((_residues_per_trip(dil), SEQ // dil, HEAD_DIM), BF16) for dil in B_DILATIONS]
    out = pl.pallas_call(
        _dilated_kernel,
        out_shape=jax.ShapeDtypeStruct((BATCH, SEQ, B_OUT_W), BF16),
        grid=(BATCH, B_HEADS_PER_GROUP),
        in_specs=in_specs,
        out_specs=pl.BlockSpec(blk, lambda b, h: (b, 0, h)),
        scratch_shapes=scratch,
        compiler_params=_params(("parallel", "arbitrary"), 52),
        name="attn_dilated",
    )(*args)
    return out.reshape(N_TOK, B_OUT_W)


def _outproj_kernel(oa_ref, ob_ref, ga_ref, gb_ref, x_ref, woa_ref, wob_ref, wout_ref, gn_ref,
                    x1_ref, h2_ref):
    ya = jnp.dot(oa_ref[...], woa_ref[...], preferred_element_type=F32)
    yb = jnp.dot(ob_ref[...], wob_ref[...], preferred_element_type=F32)
    merged = (jax.nn.sigmoid(ga_ref[...].astype(F32)) * ya
              + jax.nn.sigmoid(gb_ref[...].astype(F32)) * yb)
    x1 = x_ref[...] + jnp.dot(merged.astype(BF16), wout_ref[...], preferred_element_type=F32)
    x1_ref[...] = x1
    h2_ref[...] = _rms(x1, gn_ref[...]).astype(BF16)


def _outproj(oa, ob, proj, x2, w_oa, w_ob, w_out, g_ffn, tm=256):
    row = lambda w: pl.BlockSpec((tm, w), lambda i: (i, 0))
    const = lambda shape: pl.BlockSpec(shape, lambda i: (0, 0), pipeline_mode=pl.Buffered(1))
    return pl.pallas_call(
        _outproj_kernel,
        out_shape=[jax.ShapeDtypeStruct((N_TOK, D_MODEL), F32),
                   jax.ShapeDtypeStruct((N_TOK, D_MODEL), BF16)],
        grid=(N_TOK // tm,),
        in_specs=[row(A_Q_W), row(B_OUT_W),
                  pl.BlockSpec((tm, D_MODEL), lambda i: (i, GA_OFF // D_MODEL)),
                  pl.BlockSpec((tm, D_MODEL), lambda i: (i, GB_OFF // D_MODEL)),
                  row(D_MODEL),
                  const((A_Q_W, D_MODEL)), const((B_OUT_W, D_MODEL)), const((D_MODEL, D_MODEL)),
                  const((1, D_MODEL))],
        out_specs=[row(D_MODEL), row(D_MODEL)],
        compiler_params=_params(("parallel",), 44),
        name="outproj",
    )(oa, ob, proj, proj, x2, w_oa, w_ob, w_out, g_ffn)


_FAR = 1024.0
RT_SLICE = 256
RT_HEADS = 4
UP_CHUNK = 256
UP_ROWS = 256

_LANE_GATHER = lax.GatherDimensionNumbers(
    offset_dims=(), collapsed_slice_dims=(1,), start_index_map=(1,),
    operand_batching_dims=(0,), start_indices_batching_dims=(0,))


def _tree(op, xs):
    xs = list(xs)
    while len(xs) > 1:
        nxt = [op(xs[i], xs[i + 1]) for i in range(0, len(xs) - 1, 2)]
        if len(xs) % 2:
            nxt.append(xs[-1])
        xs = nxt
    return xs[0]


def _all_sublanes(op, x):
    for shift in (4, 2, 1):
        x = op(x, pltpu.roll(x, shift, axis=0))
    return x


def _pop_max(tiles, sub_iota, payload=None):
    m = _all_sublanes(jnp.maximum, _tree(jnp.maximum, tiles))
    first = _tree(jnp.minimum, [jnp.where(t == m, float(SUBLANES * v), _FAR) for v, t in enumerate(tiles)])
    row = _all_sublanes(jnp.minimum, first + sub_iota)
    base = row - sub_iota
    hits = [base == float(SUBLANES * v) for v in range(len(tiles))]
    if payload is None:
        val = row
    else:
        val = _all_sublanes(jnp.maximum, _tree(jnp.maximum, [jnp.where(h, p, -1.0)
                                                            for h, p in zip(hits, payload)]))
    return m, val, [jnp.where(h, -jnp.inf, t) for h, t in zip(hits, tiles)]


def _route_scores(h, q_ref, keys_ref, s_scr):
    for c in range(2):
        col = pl.multiple_of((2 * h + c) * LANES, LANES)
        s_scr[c] = lax.dot_general(keys_ref[2 * h + c], q_ref[:, pl.ds(col, LANES)],
                                   (((1,), (1,)), ((), ())), preferred_element_type=F32)


def _batcher_pairs(n):
    size = 16
    pairs = []
    p = 1
    while p < size:
        k = p
        while k >= 1:
            for j in range(k % p, size - k, 2 * k):
                for i in range(min(k, size - j - k)):
                    if (i + j) // (2 * p) == (i + j + k) // (2 * p) and i + j + k < n:
                        pairs.append((i + j, i + j + k))
            k //= 2
        p *= 2
    return pairs


_FAR_ID = 1e9


def _top_sorted(vals, ids, k):
    vals, ids = list(vals), list(ids)
    n = len(vals)
    for i, j in _batcher_pairs(n):
        up = vals[j] > vals[i]
        vals[i], vals[j] = jnp.where(up, vals[j], vals[i]), jnp.where(up, vals[i], vals[j])
        ids[i], ids[j] = jnp.where(up, ids[j], ids[i]), jnp.where(up, ids[i], ids[j])
    top_v, top_i = [], []
    for it in range(k):
        m = _all_sublanes(jnp.maximum, vals[0])
        first = _all_sublanes(jnp.minimum, jnp.where(vals[0] == m, ids[0], _FAR_ID))
        win = ids[0] == first
        top_v.append(m)
        top_i.append(first)
        for r in range(min(k - it, n)):
            if r + 1 < n:
                vals[r] = jnp.where(win, vals[r + 1], vals[r])
                ids[r] = jnp.where(win, ids[r + 1], ids[r])
            else:
                vals[r] = jnp.where(win, -jnp.inf, vals[r])
    runner_up = _all_sublanes(jnp.maximum, vals[0])
    tie = jnp.zeros_like(runner_up)
    for a, b in zip(top_v, top_v[1:] + [runner_up]):
        tie = jnp.where(a == b, 1.0, tie)
    return top_v, top_i, tie


def _route_topk(s_scr, ts_scr, ti_scr, bs_scr, et_ref, gt_ref, *, exact):
    k = PEER_TOPK
    nk = float(PEER_NKEYS)
    sub_iota = lax.broadcasted_iota(I32, (SUBLANES, LANES), 0).astype(F32)
    halves = [slice(i * LANES, (i + 1) * LANES) for i in range(RT_SLICE // LANES)]
    tie = None if exact else jnp.zeros((SUBLANES, LANES), F32)

    chains = []
    for c in range(2):
        for lanes in halves:
            tiles = [s_scr[c, SUBLANES * v:SUBLANES * (v + 1), lanes] for v in range(PEER_NKEYS // SUBLANES)]
            chains.append([c, lanes, tiles])
    if exact:
        for kk in range(k):
            for chain in chains:
                c, lanes, tiles = chain
                m, row, chain[2] = _pop_max(tiles, sub_iota)
                ts_scr[c, kk:kk + 1, lanes] = m[0:1]
                ti_scr[c, kk:kk + 1, lanes] = row[0:1]
    else:
        row_ids = [float(SUBLANES * v) + sub_iota for v in range(PEER_NKEYS // SUBLANES)]
        for c, lanes, tiles in chains:
            top_v, top_i, t = _top_sorted(tiles, row_ids, k)
            tie = jnp.maximum(tie, t)
            for kk in range(k):
                ts_scr[c, kk:kk + 1, lanes] = top_v[kk][0:1]
                ti_scr[c, kk:kk + 1, lanes] = top_i[kk][0:1]

    def candidates(lanes):
        s2 = [ts_scr[1, 0:8, lanes], ts_scr[1, 8:16, lanes]]
        i2 = [ti_scr[1, 0:8, lanes], ti_scr[1, 8:16, lanes]]
        s1 = lambda k1: ts_scr[0, k1:k1 + 1, lanes]
        i1 = lambda k1: ti_scr[0, k1:k1 + 1, lanes] * nk
        tiles = [s1(0) + s2[0], s1(0) + s2[1]]
        pay = [i1(0) + i2[0], i1(0) + i2[1]]
        for k1 in range(1, 8):
            allowed = k // (k1 + 1)
            t = s1(k1) + s2[0]
            tiles.append(t if allowed >= SUBLANES else jnp.where(sub_iota < float(allowed), t, -jnp.inf))
            pay.append(i1(k1) + i2[0])
        tiles.append(ts_scr[0, 8:16, lanes] + ts_scr[1, 0:1, lanes])
        pay.append(ti_scr[0, 8:16, lanes] * nk + ti_scr[1, 0:1, lanes])
        return tiles, pay

    cands = [candidates(lanes) for lanes in halves]
    if exact:
        tiles = [c[0] for c in cands]
        for kk in range(k):
            for i, lanes in enumerate(halves):
                m, expert, tiles[i] = _pop_max(tiles[i], sub_iota, payload=cands[i][1])
                bs_scr[kk:kk + 1, lanes] = m[0:1]
                et_ref[kk:kk + 1, lanes] = expert[0:1]
    else:
        for (tiles, pay), lanes in zip(cands, halves):
            top_v, top_i, t = _top_sorted(tiles, pay, k)
            tie = jnp.maximum(tie, t)
            for kk in range(k):
                bs_scr[kk:kk + 1, lanes] = top_v[kk][0:1]
                et_ref[kk:kk + 1, lanes] = top_i[kk][0:1]
    for lanes in halves:
        bs = bs_scr[:, lanes]
        ex = jnp.exp(bs - jnp.max(bs, axis=0, keepdims=True))
        gt_ref[:, lanes] = ex / jnp.sum(ex, axis=0, keepdims=True)
    return tie


def _expert_up(j, h_ref, u_ref, e_ref, act_ref, *, te):
    tb = h_ref.shape[0]
    u = [u_ref[c * UP_CHUNK:(c + 1) * UP_CHUNK, :].astype(BF16) for c in range(te // UP_CHUNK)]
    for m in range(tb // UP_ROWS):
        rows = slice(m * UP_ROWS, (m + 1) * UP_ROWS)
        e = e_ref[rows, :]
        row = e >> PEER_KEY_BITS
        col = (e & (PEER_NKEYS - 1))[..., None]
        acc = act_ref[rows, :]
        h = h_ref[rows, :]
        for c in range(te // UP_CHUNK):
            dense = lax.dot_general(h, u[c], (((1,), (1,)), ((), ())), preferred_element_type=F32)
            for q in range(UP_CHUNK // LANES):
                got = lax.gather(dense[:, q * LANES:(q + 1) * LANES], col, _LANE_GATHER, (1, 1),
                                 mode=lax.GatherScatterMode.PROMISE_IN_BOUNDS)
                acc = jnp.where(row == (j * te + c * UP_CHUNK) // LANES + q, got, acc)
        act_ref[rows, :] = acc


def _route_kernel(h_ref, wq_ref, keys_ref, v_ref, e_ref, g_ref, vb_ref,
                  q_scr, s_scr, ts_scr, ti_scr, bs_scr, et_scr, gt_scr):
    vb_ref[...] = v_ref[...].astype(BF16)
    q_scr[...] = jnp.dot(h_ref[...], wq_ref[...], preferred_element_type=F32).astype(BF16)

    def heads(g, carry):
        hs = [g * RT_HEADS + u for u in range(RT_HEADS)]
        for u, h in enumerate(hs):
            _route_scores(h, q_scr, keys_ref, s_scr.at[u])
        args = [(s_scr.at[u], ts_scr.at[u], ti_scr.at[u], bs_scr.at[u], et_scr.at[h], gt_scr.at[h])
                for u, h in enumerate(hs)]
        ties = [_route_topk(*a, exact=False) for a in args]
        for a, tie in zip(args, ties):
            @pl.when(jnp.max(tie) > 0.0)
            def _():
                _route_topk(*a, exact=True)

        return carry

    lax.fori_loop(0, PEER_HEADS // RT_HEADS, heads, 0)

    for half in range(RT_SLICE // LANES):
        lanes = slice(half * LANES, (half + 1) * LANES)
        et = jnp.concatenate([et_scr[h, :, lanes] for h in range(PEER_HEADS)], axis=0)
        gt = jnp.concatenate([gt_scr[h, :, lanes] for h in range(PEER_HEADS)], axis=0)
        e_ref[lanes, :] = et.T.astype(I32)
        g_ref[lanes, :] = gt.T


def _route(h2, wq, keys, v):
    k = PEER_TOPK
    steps = N_TOK // RT_SLICE
    tok = pl.BlockSpec((RT_SLICE, PEER_PICKS), lambda i: (i, 0))
    v_slab = pl.BlockSpec((PEER_EXPERTS // steps, D_MODEL), lambda i: (i, 0))
    return pl.pallas_call(
        _route_kernel,
        out_shape=[jax.ShapeDtypeStruct((N_TOK, PEER_PICKS), I32),
                   jax.ShapeDtypeStruct((N_TOK, PEER_PICKS), F32),
                   jax.ShapeDtypeStruct((PEER_EXPERTS, D_MODEL), BF16)],
        grid=(steps,),
        in_specs=[pl.BlockSpec((RT_SLICE, D_MODEL), lambda i: (i, 0)),
                  pl.BlockSpec((D_MODEL, PEER_HEADS * PEER_QDIM), lambda i: (0, 0),
                               pipeline_mode=pl.Buffered(1)),
                  pl.BlockSpec((PEER_HEADS * 2, PEER_NKEYS, PEER_QDIM // 2), lambda i: (0, 0, 0)),
                  v_slab],
        out_specs=[tok, tok, v_slab],
        scratch_shapes=[
            pltpu.VMEM((RT_SLICE, PEER_HEADS * PEER_QDIM), BF16),
            pltpu.VMEM((RT_HEADS, 2, PEER_NKEYS, RT_SLICE), F32),
            pltpu.VMEM((RT_HEADS, 2, k, RT_SLICE), F32),
            pltpu.VMEM((RT_HEADS, 2, k, RT_SLICE), F32),
            pltpu.VMEM((RT_HEADS, k, RT_SLICE), F32),
            pltpu.VMEM((PEER_HEADS, k, RT_SLICE), F32),
            pltpu.VMEM((PEER_HEADS, k, RT_SLICE), F32),
        ],
        compiler_params=_params(("parallel",), 40),
        name="peer_route",
    )(h2, wq, keys, v)


def _peer_up_kernel(h_ref, u_ref, e_ref, act_ref, *, te):
    j = pl.program_id(1)

    @pl.when(j == 0)
    def _():
        act_ref[...] = jnp.zeros_like(act_ref)

    _expert_up(j, h_ref, u_ref, e_ref, act_ref, te=te)


def _peer_up(h2, u, e, tb=2048, te=1024):
    return pl.pallas_call(
        functools.partial(_peer_up_kernel, te=te),
        out_shape=jax.ShapeDtypeStruct((N_TOK, PEER_PICKS), F32),
        grid=(N_TOK // tb, PEER_EXPERTS // te),
        in_specs=[pl.BlockSpec((tb, D_MODEL), lambda i, j: (i, 0)),
                  pl.BlockSpec((te, D_MODEL), lambda i, j: (j, 0)),
                  pl.BlockSpec((tb, PEER_PICKS), lambda i, j: (i, 0))],
        out_specs=pl.BlockSpec((tb, PEER_PICKS), lambda i, j: (i, 0)),
        compiler_params=_params(("parallel", "arbitrary"), 56),
        name="peer_up",
    )(h2, u, e)


def _peer_coef_kernel(e_ref, g_ref, act_ref, p_ref, row_scr, col_scr, w_scr, c_scr, *, tp):
    e = e_ref[...]
    row_scr[...] = (e >> PEER_KEY_BITS).astype(F32)
    col_scr[...] = (e & (PEER_NKEYS - 1)).astype(F32)
    act = act_ref[...]
    w_scr[...] = g_ref[...] * (0.5 * act * (1.0 + lax.erf(act * math.sqrt(0.5))))
    sub = 16
    iota = lax.broadcasted_iota(I32, (PEER_NKEYS // sub, sub, PEER_PICKS), 0) * sub \
        + lax.broadcasted_iota(I32, (PEER_NKEYS // sub, sub, PEER_PICKS), 1)
    iota = iota.astype(F32).astype(BF16)
    one = jnp.ones((), BF16)
    zero = jnp.zeros((), BF16)

    def bcast(ref, t):
        return jnp.broadcast_to(ref[pl.ds(t, 1), :], (sub, PEER_PICKS)).astype(BF16)[None]

    def products(t0):
        for t in range(sub):
            rows, cols, w = bcast(row_scr, t0 + t), bcast(col_scr, t0 + t), bcast(w_scr, t0 + t)
            left = jnp.where(iota == rows, one, zero).reshape(PEER_NKEYS, PEER_PICKS)
            right = jnp.where(iota == cols, w, zero).reshape(PEER_NKEYS, PEER_PICKS)
            c_scr[t0 + t] = lax.dot_general(left, right, (((1,), (1,)), ((), ())),
                                            preferred_element_type=F32)

    def regroup(t0):
        sw = jnp.swapaxes(c_scr[pl.ds(t0, sub)], 0, 1)
        for k1 in range(PEER_NKEYS):
            p_ref[pl.ds(t0, sub), k1 * PEER_NKEYS:(k1 + 1) * PEER_NKEYS] = sw[k1].astype(p_ref.dtype)

    def block(blk, carry):
        base = pl.multiple_of(blk * COEF_UNROLL, COEF_UNROLL)
        for t0 in range(0, COEF_UNROLL, sub):
            products(base + t0)
            if t0:
                regroup(base + t0 - sub)
        regroup(base + COEF_UNROLL - sub)
        return carry

    lax.fori_loop(0, tp // COEF_UNROLL, block, 0)


COEF_UNROLL = 64


def _peer_coef(e, g, act, tp=128):
    tok = pl.BlockSpec((tp, PEER_PICKS), lambda i: (i, 0))
    return pl.pallas_call(
        functools.partial(_peer_coef_kernel, tp=tp),
        out_shape=jax.ShapeDtypeStruct((N_TOK, PEER_EXPERTS), BF16),
        grid=(N_TOK // tp,),
        in_specs=[tok, tok, tok],
        out_specs=pl.BlockSpec((tp, PEER_EXPERTS), lambda i: (i, 0)),
        scratch_shapes=[pltpu.VMEM((tp, PEER_PICKS), F32),
                        pltpu.VMEM((tp, PEER_PICKS), F32),
                        pltpu.VMEM((tp, PEER_PICKS), F32),
                        pltpu.VMEM((tp, PEER_NKEYS, PEER_NKEYS), F32)],
        compiler_params=_params(("parallel",), 32),
        name="peer_coef",
    )(e, g, act)


def _peer_down_kernel(p_ref, v_ref, x_ref, g_ref, o_ref):
    j = pl.program_id(1)

    @pl.when(j == 0)
    def _():
        o_ref[...] = x_ref[...]

    o_ref[...] += jnp.dot(p_ref[...], v_ref[...], preferred_element_type=F32)

    @pl.when(j == pl.num_programs(1) - 1)
    def _():
        o_ref[...] = _rms(o_ref[...], g_ref[...])


def _peer_down(coef, v, x1, g_final, tm=1024, tk=2048):
    return pl.pallas_call(
        _peer_down_kernel,
        out_shape=jax.ShapeDtypeStruct((N_TOK, D_MODEL), F32),
        grid=(N_TOK // tm, PEER_EXPERTS // tk),
        in_specs=[pl.BlockSpec((tm, tk), lambda i, j: (i, j)),
                  pl.BlockSpec((tk, D_MODEL), lambda i, j: (j, 0)),
                  pl.BlockSpec((tm, D_MODEL), lambda i, j: (i, 0), pipeline_mode=pl.Buffered(1)),
                  pl.BlockSpec((1, D_MODEL), lambda i, j: (0, 0))],
        out_specs=pl.BlockSpec((tm, D_MODEL), lambda i, j: (i, 0)),
        compiler_params=_params(("parallel", "arbitrary"), 60),
        name="peer_down",
    )(coef, v, x1, g_final)


def kernel(x, rel_bias, norm_mix_g, w_in, sink_a, w_oa, w_ob, w_out, norm_ffn_g,
           peer_wq, peer_keys, peer_u, peer_v, norm_final_g):
    assert x.shape == (BATCH, SEQ, D_MODEL) and w_in.shape[0] == 1
    x2 = x.reshape(N_TOK, D_MODEL)
    proj = _inproj(x2, norm_mix_g, w_in[0].astype(BF16))

    oa = _windowed_attention(proj, rel_bias, sink_a[0])
    ob = _dilated_attention(proj, rel_bias)
    x1, h2 = _outproj(oa, ob, proj, x2, w_oa[0].astype(BF16), w_ob[0].astype(BF16),
                      w_out[0].astype(BF16), norm_ffn_g)

    keys = peer_keys[0].reshape(PEER_HEADS * 2, PEER_NKEYS, PEER_QDIM // 2).astype(BF16)
    e, gate, v_bf16 = _route(h2, peer_wq[0].astype(BF16), keys, peer_v[0])
    act = _peer_up(h2, peer_u[0], e)
    coef = _peer_coef(e, gate, act)
    out = _peer_down(coef, v_bf16, x1, norm_final_g.reshape(1, D_MODEL))
    return out.reshape(BATCH, SEQ, D_MODEL)
```

```python
import functools
import math

import numpy as np
import jax
import jax.numpy as jnp
from jax import lax
from jax.experimental import pallas as pl
from jax.experimental.pallas import tpu as pltpu

F32 = jnp.float32
BF16 = jnp.bfloat16
I32 = jnp.int32

D_MODEL = 2048
BATCH = 2
SEQ = 4096
N_TOK = BATCH * SEQ
HEAD_DIM = 128
LANES = 128
SUBLANES = 8

A_Q_HEADS = 8
A_KV_HEADS = 2
A_GROUP = A_Q_HEADS // A_KV_HEADS
A_HALF_WINDOW = 128
B_DILATIONS = (1, 4, 16)
B_GROUPS = 3
B_HEADS_PER_GROUP = 4
B_HALF_SPAN = 64
N_BUCKETS = 32
MAX_DISTANCE = 1024
N_ATTN_HEADS = A_Q_HEADS + B_GROUPS * B_HEADS_PER_GROUP

PEER_HEADS = 8
PEER_NKEYS = 128
PEER_KEY_BITS = PEER_NKEYS.bit_length() - 1
assert 1 << PEER_KEY_BITS == PEER_NKEYS
PEER_EXPERTS = PEER_NKEYS * PEER_NKEYS
PEER_QDIM = 256
PEER_TOPK = 16
PEER_PICKS = PEER_HEADS * PEER_TOPK
EPS = 1e-6

A_Q_W = A_Q_HEADS * HEAD_DIM
A_KV_W = A_KV_HEADS * HEAD_DIM
B_W = B_GROUPS * B_HEADS_PER_GROUP * HEAD_DIM
B_OUT_W = B_HEADS_PER_GROUP * HEAD_DIM
IN_WIDTH = A_Q_W + 2 * A_KV_W + 3 * B_W + 2 * D_MODEL
QA_CB = 0
KA_CB = A_Q_W // LANES
VA_CB = (A_Q_W + A_KV_W) // LANES
QB_CB = (A_Q_W + 2 * A_KV_W) // LANES
KB_CB = QB_CB + B_W // LANES
VB_CB = KB_CB + B_W // LANES
GA_OFF = A_Q_W + 2 * A_KV_W + 3 * B_W
GB_OFF = GA_OFF + D_MODEL
IN_CB = IN_WIDTH // LANES

NEG = -1e30
ATT_SCALE = HEAD_DIM ** -0.5
ATT_QT = 128
ATT_GROUP = 8

MIB = 1024 * 1024


def _params(sem, vmem_mib):
    return pltpu.CompilerParams(dimension_semantics=sem, vmem_limit_bytes=vmem_mib * MIB)


def _rms(x, g):
    return x * lax.rsqrt(jnp.mean(x * x, axis=-1, keepdims=True) + EPS) * g


def _inproj_kernel(x_ref, g_ref, w_ref, o_ref, h_scr):
    @pl.when(pl.program_id(1) == 0)
    def _():
        h_scr[...] = _rms(x_ref[...], g_ref[...]).astype(BF16)

    res = jnp.dot(h_scr[...], w_ref[...], preferred_element_type=F32)
    for c in range(o_ref.shape[0]):
        o_ref[c] = res[:, c * LANES:(c + 1) * LANES].astype(o_ref.dtype)


def _inproj(x2, g, w, tm=1024, tn=1024):
    n = w.shape[1]
    return pl.pallas_call(
        _inproj_kernel,
        out_shape=jax.ShapeDtypeStruct((n // LANES, N_TOK, LANES), BF16),
        grid=(N_TOK // tm, n // tn),
        in_specs=[
            pl.BlockSpec((tm, D_MODEL), lambda i, j: (i, 0)),
            pl.BlockSpec((1, D_MODEL), lambda i, j: (0, 0)),
            pl.BlockSpec((D_MODEL, tn), lambda i, j: (0, j)),
        ],
        out_specs=pl.BlockSpec((tn // LANES, tm, LANES), lambda i, j: (j, i, 0)),
        scratch_shapes=[pltpu.VMEM((tm, D_MODEL), BF16)],
        compiler_params=_params(("parallel", "arbitrary"), 40),
        name="inproj",
    )(x2, g, w)


def _t5_bucket_np(rel):
    half = N_BUCKETS // 2
    max_exact = half // 2
    ret = np.where(rel > 0, half, 0)
    n = np.abs(rel)
    nf = np.maximum(n, 1).astype(np.float64)
    large = max_exact + (np.log(nf / max_exact) / math.log(MAX_DISTANCE / max_exact) * (half - max_exact)).astype(np.int64)
    large = np.minimum(large, half - 1)
    return (ret + np.where(n < max_exact, n, large)).astype(np.int32)


def _bucket_matrix(hw, dist_scale):
    w = ATT_QT + 2 * hw
    delta = (np.arange(w)[None, :] - hw) - np.arange(ATT_QT)[:, None]
    bkt = _t5_bucket_np(delta * dist_scale)
    return np.where(np.abs(delta) <= hw, bkt, -1).astype(np.int32)


def _residues_per_trip(dil):
    tiles = SEQ // dil // ATT_QT
    return max(1, min(dil, ATT_GROUP // tiles))


def _bias_tile(bkt_ref, tab_ref, head, bias_scr):
    bkt = bkt_ref[...]
    bias = jnp.full(bkt.shape, NEG, F32)
    for b in range(N_BUCKETS):
        bias = jnp.where(bkt == b, tab_ref[b, head], bias)
    bias_scr[...] = bias


def _banded_attention(q_ref, k_ref, v_ref, bias_scr, kpad, vpad, stage, emit, *, dil, hw, sink):
    seq = SEQ // dil
    qt = ATT_QT
    win = qt + 2 * hw
    n_res = _residues_per_trip(dil)

    zeros = jnp.zeros((hw, HEAD_DIM), BF16)
    for pad_ref in (kpad, vpad):
        for u in range(n_res):
            pad_ref[u, 0:hw, :] = zeros
            pad_ref[u, hw + seq:hw + seq + hw, :] = zeros
    if dil == 1:
        kpad[0, hw:hw + seq, :] = k_ref[...]
        vpad[0, hw:hw + seq, :] = v_ref[...]
    else:
        q32, k32, v32, qres = stage
        q32[...] = q_ref[...].astype(F32)
        k32[...] = k_ref[...].astype(F32)
        v32[...] = v_ref[...].astype(F32)

    def scores(q0, q, kw):
        s = lax.dot_general(q, kw, (((1,), (1,)), ((), ())), preferred_element_type=F32)
        s = s * ATT_SCALE + bias_scr[...]
        kpos = q0 - hw + lax.broadcasted_iota(I32, (qt, win), 1)
        return jnp.where((kpos >= 0) & (kpos < seq), s, NEG)

    def softmax(s):
        m = jnp.max(s, axis=-1, keepdims=True)
        if sink is not None:
            m = jnp.maximum(m, sink)
        p = jnp.exp(s - m)
        den = jnp.sum(p, axis=-1, keepdims=True)
        if sink is not None:
            den = den + jnp.exp(sink - m)
        return p.astype(BF16), m, den

    group = min(ATT_GROUP, seq // qt)

    def residues(g, carry):
        r0 = g * n_res
        if dil > 1:
            for u in range(n_res):
                kpad[u, hw:hw + seq, :] = k32[pl.ds(r0 + u, seq, stride=dil), :].astype(BF16)
                vpad[u, hw:hw + seq, :] = v32[pl.ds(r0 + u, seq, stride=dil), :].astype(BF16)
                qres[u] = q32[pl.ds(r0 + u, seq, stride=dil), :].astype(BF16)

        def body(t, c):
            work = [(u, pl.multiple_of((t * group + i) * qt, qt)) for u in range(n_res) for i in range(group)]
            q_src = (lambda u: qres.at[u]) if dil > 1 else (lambda u: q_ref)
            s_all = [scores(q0, q_src(u)[pl.ds(q0, qt), :], kpad[u, pl.ds(q0, win), :]) for u, q0 in work]
            p_all = [softmax(s) for s in s_all]
            o_all = [jnp.dot(p, vpad[u, pl.ds(q0, win), :], preferred_element_type=F32) / den
                     for (u, q0), (p, m, den) in zip(work, p_all)]
            for (u, q0), (p, m, den), o in zip(work, p_all, o_all):
                emit(q0 if dil == 1 else r0 + u + q0 * dil, dil, o, m, den)
            return c

        lax.fori_loop(0, seq // (qt * group), body, 0)
        return carry

    if dil > 1:
        lax.fori_loop(0, dil // n_res, residues, 0)
    else:
        residues(0, 0)


def _attn_scratch(dil, hw):
    n_res = _residues_per_trip(dil)
    seq = SEQ // dil
    return [pltpu.VMEM((n_res, seq + 2 * hw, HEAD_DIM), BF16),
            pltpu.VMEM((n_res, seq + 2 * hw, HEAD_DIM), BF16),
            pltpu.VMEM((ATT_QT, ATT_QT + 2 * hw), F32)]


def _windowed_kernel(tab_ref, sink_ref, q_ref, k_ref, v_ref, bkt_ref, o_ref, kpad, vpad, bias_scr):
    head = pl.program_id(1)
    _bias_tile(bkt_ref, tab_ref, head, bias_scr)

    def emit(start, stride, o, m, den):
        o_ref[0, 0, pl.ds(start, ATT_QT), :] = o.astype(o_ref.dtype)

    _banded_attention(q_ref.at[0, 0], k_ref.at[0, 0], v_ref.at[0, 0], bias_scr, kpad, vpad, None, emit,
                      dil=1, hw=A_HALF_WINDOW, sink=sink_ref[head])


def _windowed_attention(proj, rel_bias, sink):
    pv = proj.reshape(IN_CB, BATCH, SEQ, HEAD_DIM)
    hw = A_HALF_WINDOW
    blk = (1, 1, SEQ, HEAD_DIM)
    out = pl.pallas_call(
        _windowed_kernel,
        out_shape=jax.ShapeDtypeStruct((A_Q_HEADS, BATCH, SEQ, HEAD_DIM), BF16),
        grid=(BATCH, A_Q_HEADS),
        in_specs=[
            pl.BlockSpec(memory_space=pltpu.SMEM),
            pl.BlockSpec(memory_space=pltpu.SMEM),
            pl.BlockSpec(blk, lambda b, h: (QA_CB + h, b, 0, 0)),
            pl.BlockSpec(blk, lambda b, h: (KA_CB + h // A_GROUP, b, 0, 0)),
            pl.BlockSpec(blk, lambda b, h: (VA_CB + h // A_GROUP, b, 0, 0)),
            pl.BlockSpec((ATT_QT, ATT_QT + 2 * hw), lambda b, h: (0, 0)),
        ],
        out_specs=pl.BlockSpec(blk, lambda b, h: (h, b, 0, 0)),
        scratch_shapes=_attn_scratch(1, hw),
        compiler_params=_params(("parallel", "arbitrary"), 32),
        name="attn_windowed",
    )(rel_bias, sink, pv, pv, pv, jnp.asarray(_bucket_matrix(hw, 1)))
    return out.reshape(A_Q_HEADS, N_TOK, HEAD_DIM)


MERGE_ROWS = 512


def _dilated_kernel(tab_ref, *refs):
    n_g = B_GROUPS
    qkv = [refs[3 * g:3 * g + 3] for g in range(n_g)]
    bkts = refs[3 * n_g:4 * n_g]
    o_ref = refs[4 * n_g]
    scr = list(refs[4 * n_g + 1:])
    pads = [scr[3 * g:3 * g + 3] for g in range(n_g)]
    q32, k32, v32, o32, l32 = scr[3 * n_g:3 * n_g + 5]
    qres = scr[3 * n_g + 5:]
    slot = pl.program_id(1)

    for g, dil in enumerate(B_DILATIONS):
        kpad, vpad, bias_scr = pads[g]
        _bias_tile(bkts[g], tab_ref, A_Q_HEADS + g * B_HEADS_PER_GROUP + slot, bias_scr)

        def emit(start, stride, o, m, den, g=g):
            rows = pl.ds(start, ATT_QT) if stride == 1 else pl.ds(start, ATT_QT, stride=stride)
            o32[g, rows, :] = o
            l32[g, rows, :] = jnp.broadcast_to(m + jnp.log(den), (ATT_QT, HEAD_DIM))

        q_ref, k_ref, v_ref = qkv[g]
        stage = None if dil == 1 else (q32, k32, v32, qres[g])
        _banded_attention(q_ref.at[0, 0], k_ref.at[0, 0], v_ref.at[0, 0], bias_scr, kpad, vpad, stage, emit,
                          dil=dil, hw=B_HALF_SPAN, sink=None)

    def merge(c, carry):
        rows = pl.ds(pl.multiple_of(c * MERGE_ROWS, MERGE_ROWS), MERGE_ROWS)
        lses = [l32[g, rows, :] for g in range(n_g)]
        mx = functools.reduce(jnp.maximum, lses)
        es = [jnp.exp(l - mx) for l in lses]
        den = functools.reduce(jnp.add, es)
        o_ref[0, 0, rows, :] = functools.reduce(
            jnp.add, [(e / den) * o32[g, rows, :] for g, e in enumerate(es)]).astype(o_ref.dtype)
        return carry

    lax.fori_loop(0, SEQ // MERGE_ROWS, merge, 0)


def _dilated_attention(proj, rel_bias):
    pv = proj.reshape(IN_CB, BATCH, SEQ, HEAD_DIM)
    hw = B_HALF_SPAN
    blk = (1, 1, SEQ, HEAD_DIM)
    in_specs = [pl.BlockSpec(memory_space=pltpu.SMEM)]
    args = [rel_bias]
    for g in range(B_GROUPS):
        for cb in (QB_CB, KB_CB, VB_CB):
            in_specs.append(pl.BlockSpec(blk, lambda b, h, c=cb + g * B_HEADS_PER_GROUP: (c + h, b, 0, 0)))
            args.append(pv)
    for dil in B_DILATIONS:
        in_specs.append(pl.BlockSpec((ATT_QT, ATT_QT + 2 * hw), lambda b, h: (0, 0)))
        args.append(jnp.asarray(_bucket_matrix(hw, dil)))
    scratch = []
    for dil in B_DILATIONS:
        scratch += _attn_scratch(dil, hw)
    scratch += [pltpu.VMEM((SEQ, HEAD_DIM), F32)] * 3
    scratch += [pltpu.VMEM((B_GROUPS, SEQ, HEAD_DIM), F32)] * 2
    scratch += [pltpu.VMEM((_residues_per_trip(dil), SEQ // dil, HEAD_DIM), BF16) for dil in B_DILATIONS]
    out = pl.pallas_call(
        _dilated_kernel,
        out_shape=jax.ShapeDtypeStruct((B_HEADS_PER_GROUP, BATCH, SEQ, HEAD_DIM), BF16),
        grid=(BATCH, B_HEADS_PER_GROUP),
        in_specs=in_specs,
        out_specs=pl.BlockSpec(blk, lambda b, h: (h, b, 0, 0)),
        scratch_shapes=scratch,
        compiler_params=_params(("parallel", "arbitrary"), 52),
        name="attn_dilated",
    )(*args)
    return out.reshape(B_HEADS_PER_GROUP, N_TOK, HEAD_DIM)


def _outproj_kernel(oa_ref, ob_ref, ga_ref, gb_ref, x_ref, woa_ref, wob_ref, wout_ref, gn_ref,
                    x1_ref, h2_ref):
    wide = lambda ref: jnp.concatenate([ref[c] for c in range(ref.shape[0])], axis=-1)
    ya = jnp.dot(wide(oa_ref), woa_ref[...], preferred_element_type=F32)
    yb = jnp.dot(wide(ob_ref), wob_ref[...], preferred_element_type=F32)
    merged = (jax.nn.sigmoid(wide(ga_ref).astype(F32)) * ya
              + jax.nn.sigmoid(wide(gb_ref).astype(F32)) * yb)
    x1 = x_ref[...] + jnp.dot(merged.astype(BF16), wout_ref[...], preferred_element_type=F32)
    x1_ref[...] = x1
    h2_ref[...] = _rms(x1, gn_ref[...]).astype(BF16)


def _outproj(oa, ob, proj, x2, w_oa, w_ob, w_out, g_ffn, tm=256):
    row = lambda w: pl.BlockSpec((tm, w), lambda i: (i, 0))
    slabs = lambda n, first: pl.BlockSpec((n, tm, LANES), lambda i: (first // n, i, 0))
    gate_cb = D_MODEL // LANES
    const = lambda shape: pl.BlockSpec(shape, lambda i: (0, 0), pipeline_mode=pl.Buffered(1))
    return pl.pallas_call(
        _outproj_kernel,
        out_shape=[jax.ShapeDtypeStruct((N_TOK, D_MODEL), F32),
                   jax.ShapeDtypeStruct((N_TOK, D_MODEL), BF16)],
        grid=(N_TOK // tm,),
        in_specs=[slabs(A_Q_HEADS, 0), slabs(B_HEADS_PER_GROUP, 0),
                  slabs(gate_cb, GA_OFF // LANES), slabs(gate_cb, GB_OFF // LANES),
                  row(D_MODEL),
                  const((A_Q_W, D_MODEL)), const((B_OUT_W, D_MODEL)), const((D_MODEL, D_MODEL)),
                  const((1, D_MODEL))],
        out_specs=[row(D_MODEL), row(D_MODEL)],
        compiler_params=_params(("parallel",), 44),
        name="outproj",
    )(oa, ob, proj, proj, x2, w_oa, w_ob, w_out, g_ffn)


_FAR = 1024.0
RT_SLICE = 256
RT_HEADS = 4
UP_CHUNK = 256
UP_ROWS = 256

_LANE_GATHER = lax.GatherDimensionNumbers(
    offset_dims=(), collapsed_slice_dims=(1,), start_index_map=(1,),
    operand_batching_dims=(0,), start_indices_batching_dims=(0,))


def _tree(op, xs):
    xs = list(xs)
    while len(xs) > 1:
        nxt = [op(xs[i], xs[i + 1]) for i in range(0, len(xs) - 1, 2)]
        if len(xs) % 2:
            nxt.append(xs[-1])
        xs = nxt
    return xs[0]


def _all_sublanes(op, x):
    for shift in (4, 2, 1):
        x = op(x, pltpu.roll(x, shift, axis=0))
    return x


def _pop_max(tiles, sub_iota, payload=None):
    m = _all_sublanes(jnp.maximum, _tree(jnp.maximum, tiles))
    first = _tree(jnp.minimum, [jnp.where(t == m, float(SUBLANES * v), _FAR) for v, t in enumerate(tiles)])
    row = _all_sublanes(jnp.minimum, first + sub_iota)
    base = row - sub_iota
    hits = [base == float(SUBLANES * v) for v in range(len(tiles))]
    if payload is None:
        val = row
    else:
        val = _all_sublanes(jnp.maximum, _tree(jnp.maximum, [jnp.where(h, p, -1.0)
                                                            for h, p in zip(hits, payload)]))
    return m, val, [jnp.where(h, -jnp.inf, t) for h, t in zip(hits, tiles)]


def _route_scores(h, q_ref, keys_ref, s_scr):
    for c in range(2):
        col = pl.multiple_of((2 * h + c) * LANES, LANES)
        s_scr[c] = lax.dot_general(keys_ref[2 * h + c], q_ref[:, pl.ds(col, LANES)],
                                   (((1,), (1,)), ((), ())), preferred_element_type=F32)


def _batcher_pairs(n):
    size = 16
    pairs = []
    p = 1
    while p < size:
        k = p
        while k >= 1:
            for j in range(k % p, size - k, 2 * k):
                for i in range(min(k, size - j - k)):
                    if (i + j) // (2 * p) == (i + j + k) // (2 * p) and i + j + k < n:
                        pairs.append((i + j, i + j + k))
            k //= 2
        p *= 2
    return pairs


_FAR_ID = 1e9


def _top_sorted(vals, ids, k):
    vals, ids = list(vals), list(ids)
    n = len(vals)
    for i, j in _batcher_pairs(n):
        up = vals[j] > vals[i]
        vals[i], vals[j] = jnp.where(up, vals[j], vals[i]), jnp.where(up, vals[i], vals[j])
        ids[i], ids[j] = jnp.where(up, ids[j], ids[i]), jnp.where(up, ids[i], ids[j])
    top_v, top_i = [], []
    for it in range(k):
        m = _all_sublanes(jnp.maximum, vals[0])
        first = _all_sublanes(jnp.minimum, jnp.where(vals[0] == m, ids[0], _FAR_ID))
        win = ids[0] == first
        top_v.append(m)
        top_i.append(first)
        for r in range(min(k - it, n)):
            if r + 1 < n:
                vals[r] = jnp.where(win, vals[r + 1], vals[r])
                ids[r] = jnp.where(win, ids[r + 1], ids[r])
            else:
                vals[r] = jnp.where(win, -jnp.inf, vals[r])
    runner_up = _all_sublanes(jnp.maximum, vals[0])
    tie = jnp.zeros_like(runner_up)
    for a, b in zip(top_v, top_v[1:] + [runner_up]):
        tie = jnp.where(a == b, 1.0, tie)
    return top_v, top_i, tie


def _route_topk(s_scr, ts_scr, ti_scr, bs_scr, et_ref, gt_ref, *, exact):
    k = PEER_TOPK
    nk = float(PEER_NKEYS)
    sub_iota = lax.broadcasted_iota(I32, (SUBLANES, LANES), 0).astype(F32)
    halves = [slice(i * LANES, (i + 1) * LANES) for i in range(RT_SLICE // LANES)]
    tie = None if exact else jnp.zeros((SUBLANES, LANES), F32)

    chains = []
    for c in range(2):
        for lanes in halves:
            tiles = [s_scr[c, SUBLANES * v:SUBLANES * (v + 1), lanes] for v in range(PEER_NKEYS // SUBLANES)]
            chains.append([c, lanes, tiles])
    if exact:
        for kk in range(k):
            for chain in chains:
                c, lanes, tiles = chain
                m, row, chain[2] = _pop_max(tiles, sub_iota)
                ts_scr[c, kk:kk + 1, lanes] = m[0:1]
                ti_scr[c, kk:kk + 1, lanes] = row[0:1]
    else:
        row_ids = [float(SUBLANES * v) + sub_iota for v in range(PEER_NKEYS // SUBLANES)]
        for c, lanes, tiles in chains:
            top_v, top_i, t = _top_sorted(tiles, row_ids, k)
            tie = jnp.maximum(tie, t)
            for kk in range(k):
                ts_scr[c, kk:kk + 1, lanes] = top_v[kk][0:1]
                ti_scr[c, kk:kk + 1, lanes] = top_i[kk][0:1]

    def candidates(lanes):
        s2 = [ts_scr[1, 0:8, lanes], ts_scr[1, 8:16, lanes]]
        i2 = [ti_scr[1, 0:8, lanes], ti_scr[1, 8:16, lanes]]
        s1 = lambda k1: ts_scr[0, k1:k1 + 1, lanes]
        i1 = lambda k1: ti_scr[0, k1:k1 + 1, lanes] * nk
        tiles = [s1(0) + s2[0], s1(0) + s2[1]]
        pay = [i1(0) + i2[0], i1(0) + i2[1]]
        for k1 in range(1, 8):
            allowed = k // (k1 + 1)
            t = s1(k1) + s2[0]
            tiles.append(t if allowed >= SUBLANES else jnp.where(sub_iota < float(allowed), t, -jnp.inf))
            pay.append(i1(k1) + i2[0])
        tiles.append(ts_scr[0, 8:16, lanes] + ts_scr[1, 0:1, lanes])
        pay.append(ti_scr[0, 8:16, lanes] * nk + ti_scr[1, 0:1, lanes])
        return tiles, pay

    cands = [candidates(lanes) for lanes in halves]
    if exact:
        tiles = [c[0] for c in cands]
        for kk in range(k):
            for i, lanes in enumerate(halves):
                m, expert, tiles[i] = _pop_max(tiles[i], sub_iota, payload=cands[i][1])
                bs_scr[kk:kk + 1, lanes] = m[0:1]
                et_ref[kk:kk + 1, lanes] = expert[0:1]
    else:
        for (tiles, pay), lanes in zip(cands, halves):
            top_v, top_i, t = _top_sorted(tiles, pay, k)
            tie = jnp.maximum(tie, t)
            for kk in range(k):
                bs_scr[kk:kk + 1, lanes] = top_v[kk][0:1]
                et_ref[kk:kk + 1, lanes] = top_i[kk][0:1]
    for lanes in halves:
        bs = bs_scr[:, lanes]
        ex = jnp.exp(bs - jnp.max(bs, axis=0, keepdims=True))
        gt_ref[:, lanes] = ex / jnp.sum(ex, axis=0, keepdims=True)
    return tie


def _expert_up(j, h_ref, u_ref, e_ref, act_ref, *, te):
    tb = h_ref.shape[0]
    u = [u_ref[c * UP_CHUNK:(c + 1) * UP_CHUNK, :].astype(BF16) for c in range(te // UP_CHUNK)]
    for m in range(tb // UP_ROWS):
        rows = slice(m * UP_ROWS, (m + 1) * UP_ROWS)
        e = e_ref[rows, :]
        row = e >> PEER_KEY_BITS
        col = (e & (PEER_NKEYS - 1))[..., None]
        acc = act_ref[rows, :]
        h = h_ref[rows, :]
        for c in range(te // UP_CHUNK):
            dense = lax.dot_general(h, u[c], (((1,), (1,)), ((), ())), preferred_element_type=F32)
            for q in range(UP_CHUNK // LANES):
                got = lax.gather(dense[:, q * LANES:(q + 1) * LANES], col, _LANE_GATHER, (1, 1),
                                 mode=lax.GatherScatterMode.PROMISE_IN_BOUNDS)
                acc = jnp.where(row == (j * te + c * UP_CHUNK) // LANES + q, got, acc)
        act_ref[rows, :] = acc


def _route_kernel(h_ref, wq_ref, keys_ref, v_ref, e_ref, g_ref, vb_ref,
                  q_scr, s_scr, ts_scr, ti_scr, bs_scr, et_scr, gt_scr):
    vb_ref[...] = v_ref[...].astype(BF16)
    q_scr[...] = jnp.dot(h_ref[...], wq_ref[...], preferred_element_type=F32).astype(BF16)

    def heads(g, carry):
        hs = [g * RT_HEADS + u for u in range(RT_HEADS)]
        for u, h in enumerate(hs):
            _route_scores(h, q_scr, keys_ref, s_scr.at[u])
        args = [(s_scr.at[u], ts_scr.at[u], ti_scr.at[u], bs_scr.at[u], et_scr.at[h], gt_scr.at[h])
                for u, h in enumerate(hs)]
        ties = [_route_topk(*a, exact=False) for a in args]
        for a, tie in zip(args, ties):
            @pl.when(jnp.max(tie) > 0.0)
            def _():
                _route_topk(*a, exact=True)

        return carry

    lax.fori_loop(0, PEER_HEADS // RT_HEADS, heads, 0)

    for half in range(RT_SLICE // LANES):
        lanes = slice(half * LANES, (half + 1) * LANES)
        et = jnp.concatenate([et_scr[h, :, lanes] for h in range(PEER_HEADS)], axis=0)
        gt = jnp.concatenate([gt_scr[h, :, lanes] for h in range(PEER_HEADS)], axis=0)
        e_ref[lanes, :] = et.T.astype(I32)
        g_ref[lanes, :] = gt.T


def _route(h2, wq, keys, v):
    k = PEER_TOPK
    steps = N_TOK // RT_SLICE
    tok = pl.BlockSpec((RT_SLICE, PEER_PICKS), lambda i: (i, 0))
    v_slab = pl.BlockSpec((PEER_EXPERTS // steps, D_MODEL), lambda i: (i, 0))
    return pl.pallas_call(
        _route_kernel,
        out_shape=[jax.ShapeDtypeStruct((N_TOK, PEER_PICKS), I32),
                   jax.ShapeDtypeStruct((N_TOK, PEER_PICKS), F32),
                   jax.ShapeDtypeStruct((PEER_EXPERTS, D_MODEL), BF16)],
        grid=(steps,),
        in_specs=[pl.BlockSpec((RT_SLICE, D_MODEL), lambda i: (i, 0)),
                  pl.BlockSpec((D_MODEL, PEER_HEADS * PEER_QDIM), lambda i: (0, 0),
                               pipeline_mode=pl.Buffered(1)),
                  pl.BlockSpec((PEER_HEADS * 2, PEER_NKEYS, PEER_QDIM // 2), lambda i: (0, 0, 0)),
                  v_slab],
        out_specs=[tok, tok, v_slab],
        scratch_shapes=[
            pltpu.VMEM((RT_SLICE, PEER_HEADS * PEER_QDIM), BF16),
            pltpu.VMEM((RT_HEADS, 2, PEER_NKEYS, RT_SLICE), F32),
            pltpu.VMEM((RT_HEADS, 2, k, RT_SLICE), F32),
            pltpu.VMEM((RT_HEADS, 2, k, RT_SLICE), F32),
            pltpu.VMEM((RT_HEADS, k, RT_SLICE), F32),
            pltpu.VMEM((PEER_HEADS, k, RT_SLICE), F32),
            pltpu.VMEM((PEER_HEADS, k, RT_SLICE), F32),
        ],
        compiler_params=_params(("parallel",), 40),
        name="peer_route",
    )(h2, wq, keys, v)


def _peer_up_kernel(h_ref, u_ref, e_ref, act_ref, *, te):
    j = pl.program_id(1)

    @pl.when(j == 0)
    def _():
        act_ref[...] = jnp.zeros_like(act_ref)

    _expert_up(j, h_ref, u_ref, e_ref, act_ref, te=te)


def _peer_up(h2, u, e, tb=2048, te=1024):
    return pl.pallas_call(
        functools.partial(_peer_up_kernel, te=te),
        out_shape=jax.ShapeDtypeStruct((N_TOK, PEER_PICKS), F32),
        grid=(N_TOK // tb, PEER_EXPERTS // te),
        in_specs=[pl.BlockSpec((tb, D_MODEL), lambda i, j: (i, 0)),
                  pl.BlockSpec((te, D_MODEL), lambda i, j: (j, 0)),
                  pl.BlockSpec((tb, PEER_PICKS), lambda i, j: (i, 0))],
        out_specs=pl.BlockSpec((tb, PEER_PICKS), lambda i, j: (i, 0)),
        compiler_params=_params(("parallel", "arbitrary"), 56),
        name="peer_up",
    )(h2, u, e)


def _peer_coef_kernel(e_ref, g_ref, act_ref, p_ref, row_scr, col_scr, w_scr, c_scr, *, tp):
    e = e_ref[...]
    row_scr[...] = (e >> PEER_KEY_BITS).astype(F32)
    col_scr[...] = (e & (PEER_NKEYS - 1)).astype(F32)
    act = act_ref[...]
    w_scr[...] = g_ref[...] * (0.5 * act * (1.0 + lax.erf(act * math.sqrt(0.5))))
    sub = 16
    iota = lax.broadcasted_iota(I32, (PEER_NKEYS // sub, sub, PEER_PICKS), 0) * sub \
        + lax.broadcasted_iota(I32, (PEER_NKEYS // sub, sub, PEER_PICKS), 1)
    iota = iota.astype(F32).astype(BF16)
    one = jnp.ones((), BF16)
    zero = jnp.zeros((), BF16)

    def bcast(ref, t):
        return jnp.broadcast_to(ref[pl.ds(t, 1), :], (sub, PEER_PICKS)).astype(BF16)[None]

    def products(t0):
        for t in range(sub):
            rows, cols, w = bcast(row_scr, t0 + t), bcast(col_scr, t0 + t), bcast(w_scr, t0 + t)
            left = jnp.where(iota == rows, one, zero).reshape(PEER_NKEYS, PEER_PICKS)
            right = jnp.where(iota == cols, w, zero).reshape(PEER_NKEYS, PEER_PICKS)
            c_scr[t0 + t] = lax.dot_general(left, right, (((1,), (1,)), ((), ())),
                                            preferred_element_type=F32)

    def regroup(t0):
        sw = jnp.swapaxes(c_scr[pl.ds(t0, sub)], 0, 1)
        for k1 in range(PEER_NKEYS):
            p_ref[pl.ds(t0, sub), k1 * PEER_NKEYS:(k1 + 1) * PEER_NKEYS] = sw[k1].astype(p_ref.dtype)

    def block(blk, carry):
        base = pl.multiple_of(blk * COEF_UNROLL, COEF_UNROLL)
        for t0 in range(0, COEF_UNROLL, sub):
            products(base + t0)
            if t0:
                regroup(base + t0 - sub)
        regroup(base + COEF_UNROLL - sub)
        return carry

    lax.fori_loop(0, tp // COEF_UNROLL, block, 0)


COEF_UNROLL = 64


def _peer_coef(e, g, act, tp=128):
    tok = pl.BlockSpec((tp, PEER_PICKS), lambda i: (i, 0))
    return pl.pallas_call(
        functools.partial(_peer_coef_kernel, tp=tp),
        out_shape=jax.ShapeDtypeStruct((N_TOK, PEER_EXPERTS), BF16),
        grid=(N_TOK // tp,),
        in_specs=[tok, tok, tok],
        out_specs=pl.BlockSpec((tp, PEER_EXPERTS), lambda i: (i, 0)),
        scratch_shapes=[pltpu.VMEM((tp, PEER_PICKS), F32),
                        pltpu.VMEM((tp, PEER_PICKS), F32),
                        pltpu.VMEM((tp, PEER_PICKS), F32),
                        pltpu.VMEM((tp, PEER_NKEYS, PEER_NKEYS), F32)],
        compiler_params=_params(("parallel",), 32),
        name="peer_coef",
    )(e, g, act)


def _peer_down_kernel(p_ref, v_ref, x_ref, g_ref, o_ref):
    j = pl.program_id(1)

    @pl.when(j == 0)
    def _():
        o_ref[...] = x_ref[...]

    o_ref[...] += jnp.dot(p_ref[...], v_ref[...], preferred_element_type=F32)

    @pl.when(j == pl.num_programs(1) - 1)
    def _():
        o_ref[...] = _rms(o_ref[...], g_ref[...])


def _peer_down(coef, v, x1, g_final, tm=1024, tk=2048):
    return pl.pallas_call(
        _peer_down_kernel,
        out_shape=jax.ShapeDtypeStruct((N_TOK, D_MODEL), F32),
        grid=(N_TOK // tm, PEER_EXPERTS // tk),
        in_specs=[pl.BlockSpec((tm, tk), lambda i, j: (i, j)),
                  pl.BlockSpec((tk, D_MODEL), lambda i, j: (j, 0)),
                  pl.BlockSpec((tm, D_MODEL), lambda i, j: (i, 0), pipeline_mode=pl.Buffered(1)),
                  pl.BlockSpec((1, D_MODEL), lambda i, j: (0, 0))],
        out_specs=pl.BlockSpec((tm, D_MODEL), lambda i, j: (i, 0)),
        compiler_params=_params(("parallel", "arbitrary"), 60),
        name="peer_down",
    )(coef, v, x1, g_final)


def kernel(x, rel_bias, norm_mix_g, w_in, sink_a, w_oa, w_ob, w_out, norm_ffn_g,
           peer_wq, peer_keys, peer_u, peer_v, norm_final_g):
    assert x.shape == (BATCH, SEQ, D_MODEL) and w_in.shape[0] == 1
    x2 = x.reshape(N_TOK, D_MODEL)
    proj = _inproj(x2, norm_mix_g, w_in[0].astype(BF16))

    oa = _windowed_attention(proj, rel_bias, sink_a[0])
    ob = _dilated_attention(proj, rel_bias)
    x1, h2 = _outproj(oa, ob, proj, x2, w_oa[0].astype(BF16), w_ob[0].astype(BF16),
                      w_out[0].astype(BF16), norm_ffn_g)

    keys = peer_keys[0].reshape(PEER_HEADS * 2, PEER_NKEYS, PEER_QDIM // 2).astype(BF16)
    e, gate, v_bf16 = _route(h2, peer_wq[0].astype(BF16), keys, peer_v[0])
    act = _peer_up(h2, peer_u[0], e)
    coef = _peer_coef(e, gate, act)
    out = _peer_down(coef, v_bf16, x1, norm_final_g.reshape(1, D_MODEL))
    return out.reshape(BATCH, SEQ, D_MODEL)
```

```python
import functools
import math

import numpy as np
import jax
import jax.numpy as jnp
from jax import lax
from jax.experimental import pallas as pl
from jax.experimental.pallas import tpu as pltpu

F32 = jnp.float32
BF16 = jnp.bfloat16
I32 = jnp.int32

D_MODEL = 2048
BATCH = 2
SEQ = 4096
N_TOK = BATCH * SEQ
HEAD_DIM = 128
LANES = 128
SUBLANES = 8

A_Q_HEADS = 8
A_KV_HEADS = 2
A_GROUP = A_Q_HEADS // A_KV_HEADS
A_HALF_WINDOW = 128
B_DILATIONS = (1, 4, 16)
B_GROUPS = 3
B_HEADS_PER_GROUP = 4
B_HALF_SPAN = 64
N_BUCKETS = 32
MAX_DISTANCE = 1024
N_ATTN_HEADS = A_Q_HEADS + B_GROUPS * B_HEADS_PER_GROUP

PEER_HEADS = 8
PEER_NKEYS = 128
PEER_KEY_BITS = PEER_NKEYS.bit_length() - 1
assert 1 << PEER_KEY_BITS == PEER_NKEYS
PEER_EXPERTS = PEER_NKEYS * PEER_NKEYS
PEER_QDIM = 256
PEER_TOPK = 16
PEER_PICKS = PEER_HEADS * PEER_TOPK
EPS = 1e-6

A_Q_W = A_Q_HEADS * HEAD_DIM
A_KV_W = A_KV_HEADS * HEAD_DIM
B_W = B_GROUPS * B_HEADS_PER_GROUP * HEAD_DIM
B_OUT_W = B_HEADS_PER_GROUP * HEAD_DIM
IN_WIDTH = A_Q_W + 2 * A_KV_W + 3 * B_W + 2 * D_MODEL
QA_CB = 0
KA_CB = A_Q_W // LANES
VA_CB = (A_Q_W + A_KV_W) // LANES
QB_CB = (A_Q_W + 2 * A_KV_W) // LANES
KB_CB = QB_CB + B_W // LANES
VB_CB = KB_CB + B_W // LANES
GA_OFF = A_Q_W + 2 * A_KV_W + 3 * B_W
GB_OFF = GA_OFF + D_MODEL
IN_CB = IN_WIDTH // LANES

NEG = -1e30
ATT_SCALE = HEAD_DIM ** -0.5
ATT_QT = 128
ATT_GROUP = 8

MIB = 1024 * 1024


def _params(sem, vmem_mib):
    return pltpu.CompilerParams(dimension_semantics=sem, vmem_limit_bytes=vmem_mib * MIB)


def _rms(x, g):
    return x * lax.rsqrt(jnp.mean(x * x, axis=-1, keepdims=True) + EPS) * g


def _inproj_kernel(x_ref, g_ref, w_ref, o_ref, h_scr):
    @pl.when(pl.program_id(1) == 0)
    def _():
        h_scr[...] = _rms(x_ref[...], g_ref[...]).astype(BF16)

    res = jnp.dot(h_scr[...], w_ref[...], preferred_element_type=F32)
    for c in range(o_ref.shape[0]):
        o_ref[c] = res[:, c * LANES:(c + 1) * LANES].astype(o_ref.dtype)


def _inproj(x2, g, w, tm=1024, tn=1024):
    n = w.shape[1]
    return pl.pallas_call(
        _inproj_kernel,
        out_shape=jax.ShapeDtypeStruct((n // LANES, N_TOK, LANES), BF16),
        grid=(N_TOK // tm, n // tn),
        in_specs=[
            pl.BlockSpec((tm, D_MODEL), lambda i, j: (i, 0)),
            pl.BlockSpec((1, D_MODEL), lambda i, j: (0, 0)),
            pl.BlockSpec((D_MODEL, tn), lambda i, j: (0, j)),
        ],
        out_specs=pl.BlockSpec((tn // LANES, tm, LANES), lambda i, j: (j, i, 0)),
        scratch_shapes=[pltpu.VMEM((tm, D_MODEL), BF16)],
        compiler_params=_params(("parallel", "arbitrary"), 40),
        name="inproj",
    )(x2, g, w)


def _t5_bucket_np(rel):
    half = N_BUCKETS // 2
    max_exact = half // 2
    ret = np.where(rel > 0, half, 0)
    n = np.abs(rel)
    nf = np.maximum(n, 1).astype(np.float64)
    large = max_exact + (np.log(nf / max_exact) / math.log(MAX_DISTANCE / max_exact) * (half - max_exact)).astype(np.int64)
    large = np.minimum(large, half - 1)
    return (ret + np.where(n < max_exact, n, large)).astype(np.int32)


def _bucket_matrix(hw, dist_scale):
    w = ATT_QT + 2 * hw
    delta = (np.arange(w)[None, :] - hw) - np.arange(ATT_QT)[:, None]
    bkt = _t5_bucket_np(delta * dist_scale)
    return np.where(np.abs(delta) <= hw, bkt, -1).astype(np.int32)


def _residues_per_trip(dil):
    tiles = SEQ // dil // ATT_QT
    return max(1, min(dil, ATT_GROUP // tiles))


def _bias_tile(bkt_ref, tab_ref, head, bias_scr):
    bkt = bkt_ref[...]
    bias = jnp.full(bkt.shape, NEG, F32)
    for b in range(N_BUCKETS):
        bias = jnp.where(bkt == b, tab_ref[b, head], bias)
    bias_scr[...] = bias


def _banded_attention(q_ref, k_ref, v_ref, bias_scr, kpad, vpad, stage, emit, *, dil, hw, sink):
    seq = SEQ // dil
    qt = ATT_QT
    win = qt + 2 * hw
    n_res = _residues_per_trip(dil)

    zeros = jnp.zeros((hw, HEAD_DIM), BF16)
    for pad_ref in (kpad, vpad):
        for u in range(n_res):
            pad_ref[u, 0:hw, :] = zeros
            pad_ref[u, hw + seq:hw + seq + hw, :] = zeros
    if dil == 1:
        kpad[0, hw:hw + seq, :] = k_ref[...]
        vpad[0, hw:hw + seq, :] = v_ref[...]
    else:
        q32, k32, v32, qres = stage
        q32[...] = q_ref[...].astype(F32)
        k32[...] = k_ref[...].astype(F32)
        v32[...] = v_ref[...].astype(F32)

    def scores(q0, q, kw):
        s = lax.dot_general(q, kw, (((1,), (1,)), ((), ())), preferred_element_type=F32)
        s = s * ATT_SCALE + bias_scr[...]
        kpos = q0 - hw + lax.broadcasted_iota(I32, (qt, win), 1)
        return jnp.where((kpos >= 0) & (kpos < seq), s, NEG)

    def softmax(s):
        m = jnp.max(s, axis=-1, keepdims=True)
        if sink is not None:
            m = jnp.maximum(m, sink)
        p = jnp.exp(s - m)
        den = jnp.sum(p, axis=-1, keepdims=True)
        if sink is not None:
            den = den + jnp.exp(sink - m)
        return p.astype(BF16), m, den

    group = min(ATT_GROUP, seq // qt)

    def residues(g, carry):
        r0 = g * n_res
        if dil > 1:
            for u in range(n_res):
                kpad[u, hw:hw + seq, :] = k32[pl.ds(r0 + u, seq, stride=dil), :].astype(BF16)
                vpad[u, hw:hw + seq, :] = v32[pl.ds(r0 + u, seq, stride=dil), :].astype(BF16)
                qres[u] = q32[pl.ds(r0 + u, seq, stride=dil), :].astype(BF16)

        def body(t, c):
            work = [(u, pl.multiple_of((t * group + i) * qt, qt)) for u in range(n_res) for i in range(group)]
            q_src = (lambda u: qres.at[u]) if dil > 1 else (lambda u: q_ref)
            s_all = [scores(q0, q_src(u)[pl.ds(q0, qt), :], kpad[u, pl.ds(q0, win), :]) for u, q0 in work]
            p_all = [softmax(s) for s in s_all]
            o_all = [jnp.dot(p, vpad[u, pl.ds(q0, win), :], preferred_element_type=F32) / den
                     for (u, q0), (p, m, den) in zip(work, p_all)]
            for (u, q0), (p, m, den), o in zip(work, p_all, o_all):
                emit(q0 if dil == 1 else r0 + u + q0 * dil, dil, o, m, den)
            return c

        lax.fori_loop(0, seq // (qt * group), body, 0)
        return carry

    if dil > 1:
        lax.fori_loop(0, dil // n_res, residues, 0)
    else:
        residues(0, 0)


def _attn_scratch(dil, hw):
    n_res = _residues_per_trip(dil)
    seq = SEQ // dil
    return [pltpu.VMEM((n_res, seq + 2 * hw, HEAD_DIM), BF16),
            pltpu.VMEM((n_res, seq + 2 * hw, HEAD_DIM), BF16),
            pltpu.VMEM((ATT_QT, ATT_QT + 2 * hw), F32)]


def _windowed_kernel(tab_ref, sink_ref, q_ref, k_ref, v_ref, bkt_ref, o_ref, kpad, vpad, bias_scr):
    head = pl.program_id(1)
    _bias_tile(bkt_ref, tab_ref, head, bias_scr)

    def emit(start, stride, o, m, den):
        o_ref[0, 0, pl.ds(start, ATT_QT), :] = o.astype(o_ref.dtype)

    _banded_attention(q_ref.at[0, 0], k_ref.at[0, 0], v_ref.at[0, 0], bias_scr, kpad, vpad, None, emit,
                      dil=1, hw=A_HALF_WINDOW, sink=sink_ref[head])


def _windowed_attention(proj, rel_bias, sink):
    pv = proj.reshape(IN_CB, BATCH, SEQ, HEAD_DIM)
    hw = A_HALF_WINDOW
    blk = (1, 1, SEQ, HEAD_DIM)
    out = pl.pallas_call(
        _windowed_kernel,
        out_shape=jax.ShapeDtypeStruct((A_Q_HEADS, BATCH, SEQ, HEAD_DIM), BF16),
        grid=(BATCH, A_Q_HEADS),
        in_specs=[
            pl.BlockSpec(memory_space=pltpu.SMEM),
            pl.BlockSpec(memory_space=pltpu.SMEM),
            pl.BlockSpec(blk, lambda b, h: (QA_CB + h, b, 0, 0)),
            pl.BlockSpec(blk, lambda b, h: (KA_CB + h // A_GROUP, b, 0, 0)),
            pl.BlockSpec(blk, lambda b, h: (VA_CB + h // A_GROUP, b, 0, 0)),
            pl.BlockSpec((ATT_QT, ATT_QT + 2 * hw), lambda b, h: (0, 0)),
        ],
        out_specs=pl.BlockSpec(blk, lambda b, h: (h, b, 0, 0)),
        scratch_shapes=_attn_scratch(1, hw),
        compiler_params=_params(("parallel", "arbitrary"), 32),
        name="attn_windowed",
    )(rel_bias, sink, pv, pv, pv, jnp.asarray(_bucket_matrix(hw, 1)))
    return out.reshape(A_Q_HEADS, N_TOK, HEAD_DIM)


MERGE_ROWS = 512


def _dilated_kernel(tab_ref, *refs):
    n_g = B_GROUPS
    qkv = [refs[3 * g:3 * g + 3] for g in range(n_g)]
    bkts = refs[3 * n_g:4 * n_g]
    o_ref = refs[4 * n_g]
    scr = list(refs[4 * n_g + 1:])
    pads = [scr[3 * g:3 * g + 3] for g in range(n_g)]
    q32, k32, v32, o32, l32 = scr[3 * n_g:3 * n_g + 5]
    qres = scr[3 * n_g + 5:]
    slot = pl.program_id(1)

    for g, dil in enumerate(B_DILATIONS):
        kpad, vpad, bias_scr = pads[g]
        _bias_tile(bkts[g], tab_ref, A_Q_HEADS + g * B_HEADS_PER_GROUP + slot, bias_scr)

        def emit(start, stride, o, m, den, g=g):
            rows = pl.ds(start, ATT_QT) if stride == 1 else pl.ds(start, ATT_QT, stride=stride)
            o32[g, rows, :] = o
            l32[g, rows, :] = jnp.broadcast_to(m + jnp.log(den), (ATT_QT, HEAD_DIM))

        q_ref, k_ref, v_ref = qkv[g]
        stage = None if dil == 1 else (q32, k32, v32, qres[g])
        _banded_attention(q_ref.at[0, 0], k_ref.at[0, 0], v_ref.at[0, 0], bias_scr, kpad, vpad, stage, emit,
                          dil=dil, hw=B_HALF_SPAN, sink=None)

    def merge(c, carry):
        rows = pl.ds(pl.multiple_of(c * MERGE_ROWS, MERGE_ROWS), MERGE_ROWS)
        lses = [l32[g, rows, :] for g in range(n_g)]
        mx = functools.reduce(jnp.maximum, lses)
        es = [jnp.exp(l - mx) for l in lses]
        den = functools.reduce(jnp.add, es)
        o_ref[0, 0, rows, :] = functools.reduce(
            jnp.add, [(e / den) * o32[g, rows, :] for g, e in enumerate(es)]).astype(o_ref.dtype)
        return carry

    lax.fori_loop(0, SEQ // MERGE_ROWS, merge, 0)


def _dilated_attention(proj, rel_bias):
    pv = proj.reshape(IN_CB, BATCH, SEQ, HEAD_DIM)
    hw = B_HALF_SPAN
    blk = (1, 1, SEQ, HEAD_DIM)
    in_specs = [pl.BlockSpec(memory_space=pltpu.SMEM)]
    args = [rel_bias]
    for g in range(B_GROUPS):
        for cb in (QB_CB, KB_CB, VB_CB):
            in_specs.append(pl.BlockSpec(blk, lambda b, h, c=cb + g * B_HEADS_PER_GROUP: (c + h, b, 0, 0)))
            args.append(pv)
    for dil in B_DILATIONS:
        in_specs.append(pl.BlockSpec((ATT_QT, ATT_QT + 2 * hw), lambda b, h: (0, 0)))
        args.append(jnp.asarray(_bucket_matrix(hw, dil)))
    scratch = []
    for dil in B_DILATIONS:
        scratch += _attn_scratch(dil, hw)
    scratch += [pltpu.VMEM((SEQ, HEAD_DIM), F32)] * 3
    scratch += [pltpu.VMEM((B_GROUPS, SEQ, HEAD_DIM), F32)] * 2
    scratch += [pltpu.VMEM((_residues_per_trip(dil), SEQ // dil, HEAD_DIM), BF16) for dil in B_DILATIONS]
    out = pl.pallas_call(
        _dilated_kernel,
        out_shape=jax.ShapeDtypeStruct((B_HEADS_PER_GROUP, BATCH, SEQ, HEAD_DIM), BF16),
        grid=(BATCH, B_HEADS_PER_GROUP),
        in_specs=in_specs,
        out_specs=pl.BlockSpec(blk, lambda b, h: (h, b, 0, 0)),
        scratch_shapes=scratch,
        compiler_params=_params(("parallel", "arbitrary"), 52),
        name="attn_dilated",
    )(*args)
    return out.reshape(B_HEADS_PER_GROUP, N_TOK, HEAD_DIM)


def _outproj_kernel(oa_ref, ob_ref, ga_ref, gb_ref, x_ref, woa_ref, wob_ref, wout_ref, gn_ref,
                    x1_ref, h2_ref):
    wide = lambda ref: jnp.concatenate([ref[c] for c in range(ref.shape[0])], axis=-1)
    ya = jnp.dot(wide(oa_ref), woa_ref[...], preferred_element_type=F32)
    yb = jnp.dot(wide(ob_ref), wob_ref[...], preferred_element_type=F32)
    merged = (jax.nn.sigmoid(wide(ga_ref).astype(F32)) * ya
              + jax.nn.sigmoid(wide(gb_ref).astype(F32)) * yb)
    x1 = x_ref[...] + jnp.dot(merged.astype(BF16), wout_ref[...], preferred_element_type=F32)
    x1_ref[...] = x1
    h2_ref[...] = _rms(x1, gn_ref[...]).astype(BF16)


def _outproj(oa, ob, proj, x2, w_oa, w_ob, w_out, g_ffn, tm=256):
    row = lambda w: pl.BlockSpec((tm, w), lambda i: (i, 0))
    slabs = lambda n, first: pl.BlockSpec((n, tm, LANES), lambda i: (first // n, i, 0))
    gate_cb = D_MODEL // LANES
    const = lambda shape: pl.BlockSpec(shape, lambda i: (0, 0), pipeline_mode=pl.Buffered(1))
    return pl.pallas_call(
        _outproj_kernel,
        out_shape=[jax.ShapeDtypeStruct((N_TOK, D_MODEL), F32),
                   jax.ShapeDtypeStruct((N_TOK, D_MODEL), BF16)],
        grid=(N_TOK // tm,),
        in_specs=[slabs(A_Q_HEADS, 0), slabs(B_HEADS_PER_GROUP, 0),
                  slabs(gate_cb, GA_OFF // LANES), slabs(gate_cb, GB_OFF // LANES),
                  row(D_MODEL),
                  const((A_Q_W, D_MODEL)), const((B_OUT_W, D_MODEL)), const((D_MODEL, D_MODEL)),
                  const((1, D_MODEL))],
        out_specs=[row(D_MODEL), row(D_MODEL)],
        compiler_params=_params(("parallel",), 44),
        name="outproj",
    )(oa, ob, proj, proj, x2, w_oa, w_ob, w_out, g_ffn)


_FAR = 1024.0
RT_SLICE = 256
RT_HEADS = 4
UP_CHUNK = 256
UP_ROWS = 256

_LANE_GATHER = lax.GatherDimensionNumbers(
    offset_dims=(), collapsed_slice_dims=(1,), start_index_map=(1,),
    operand_batching_dims=(0,), start_indices_batching_dims=(0,))


def _tree(op, xs):
    xs = list(xs)
    while len(xs) > 1:
        nxt = [op(xs[i], xs[i + 1]) for i in range(0, len(xs) - 1, 2)]
        if len(xs) % 2:
            nxt.append(xs[-1])
        xs = nxt
    return xs[0]


def _all_sublanes(op, x):
    for shift in (4, 2, 1):
        x = op(x, pltpu.roll(x, shift, axis=0))
    return x


def _pop_max(tiles, sub_iota, payload=None):
    m = _all_sublanes(jnp.maximum, _tree(jnp.maximum, tiles))
    first = _tree(jnp.minimum, [jnp.where(t == m, float(SUBLANES * v), _FAR) for v, t in enumerate(tiles)])
    row = _all_sublanes(jnp.minimum, first + sub_iota)
    base = row - sub_iota
    hits = [base == float(SUBLANES * v) for v in range(len(tiles))]
    if payload is None:
        val = row
    else:
        val = _all_sublanes(jnp.maximum, _tree(jnp.maximum, [jnp.where(h, p, -1.0)
                                                            for h, p in zip(hits, payload)]))
    return m, val, [jnp.where(h, -jnp.inf, t) for h, t in zip(hits, tiles)]


def _route_scores(h, q_ref, keys_ref, s_scr):
    for c in range(2):
        col = pl.multiple_of((2 * h + c) * LANES, LANES)
        s_scr[c] = lax.dot_general(keys_ref[2 * h + c], q_ref[:, pl.ds(col, LANES)],
                                   (((1,), (1,)), ((), ())), preferred_element_type=F32)


def _batcher_pairs(n):
    size = 16
    pairs = []
    p = 1
    while p < size:
        k = p
        while k >= 1:
            for j in range(k % p, size - k, 2 * k):
                for i in range(min(k, size - j - k)):
                    if (i + j) // (2 * p) == (i + j + k) // (2 * p) and i + j + k < n:
                        pairs.append((i + j, i + j + k))
            k //= 2
        p *= 2
    return pairs


_FAR_ID = 1e9


def _top_sorted(vals, ids, k):
    vals, ids = list(vals), list(ids)
    n = len(vals)
    for i, j in _batcher_pairs(n):
        up = vals[j] > vals[i]
        vals[i], vals[j] = jnp.where(up, vals[j], vals[i]), jnp.where(up, vals[i], vals[j])
        ids[i], ids[j] = jnp.where(up, ids[j], ids[i]), jnp.where(up, ids[i], ids[j])
    top_v, top_i = [], []
    for it in range(k):
        m = _all_sublanes(jnp.maximum, vals[0])
        first = _all_sublanes(jnp.minimum, jnp.where(vals[0] == m, ids[0], _FAR_ID))
        win = ids[0] == first
        top_v.append(m)
        top_i.append(first)
        for r in range(min(k - it, n)):
            if r + 1 < n:
                vals[r] = jnp.where(win, vals[r + 1], vals[r])
                ids[r] = jnp.where(win, ids[r + 1], ids[r])
            else:
                vals[r] = jnp.where(win, -jnp.inf, vals[r])
    runner_up = _all_sublanes(jnp.maximum, vals[0])
    tie = jnp.zeros_like(runner_up)
    for a, b in zip(top_v, top_v[1:] + [runner_up]):
        tie = jnp.where(a == b, 1.0, tie)
    return top_v, top_i, tie


def _route_topk(s_scr, ts_scr, ti_scr, bs_scr, et_ref, gt_ref, *, exact):
    k = PEER_TOPK
    nk = float(PEER_NKEYS)
    sub_iota = lax.broadcasted_iota(I32, (SUBLANES, LANES), 0).astype(F32)
    halves = [slice(i * LANES, (i + 1) * LANES) for i in range(RT_SLICE // LANES)]
    tie = None if exact else jnp.zeros((SUBLANES, LANES), F32)

    chains = []
    for c in range(2):
        for lanes in halves:
            tiles = [s_scr[c, SUBLANES * v:SUBLANES * (v + 1), lanes] for v in range(PEER_NKEYS // SUBLANES)]
            chains.append([c, lanes, tiles])
    if exact:
        for kk in range(k):
            for chain in chains:
                c, lanes, tiles = chain
                m, row, chain[2] = _pop_max(tiles, sub_iota)
                ts_scr[c, kk:kk + 1, lanes] = m[0:1]
                ti_scr[c, kk:kk + 1, lanes] = row[0:1]
    else:
        row_ids = [float(SUBLANES * v) + sub_iota for v in range(PEER_NKEYS // SUBLANES)]
        for c, lanes, tiles in chains:
            top_v, top_i, t = _top_sorted(tiles, row_ids, k)
            tie = jnp.maximum(tie, t)
            for kk in range(k):
                ts_scr[c, kk:kk + 1, lanes] = top_v[kk][0:1]
                ti_scr[c, kk:kk + 1, lanes] = top_i[kk][0:1]

    def candidates(lanes):
        s2 = [ts_scr[1, 0:8, lanes], ts_scr[1, 8:16, lanes]]
        i2 = [ti_scr[1, 0:8, lanes], ti_scr[1, 8:16, lanes]]
        s1 = lambda k1: ts_scr[0, k1:k1 + 1, lanes]
        i1 = lambda k1: ti_scr[0, k1:k1 + 1, lanes] * nk
        tiles = [s1(0) + s2[0], s1(0) + s2[1]]
        pay = [i1(0) + i2[0], i1(0) + i2[1]]
        for k1 in range(1, 8):
            allowed = k // (k1 + 1)
            t = s1(k1) + s2[0]
            tiles.append(t if allowed >= SUBLANES else jnp.where(sub_iota < float(allowed), t, -jnp.inf))
            pay.append(i1(k1) + i2[0])
        tiles.append(ts_scr[0, 8:16, lanes] + ts_scr[1, 0:1, lanes])
        pay.append(ti_scr[0, 8:16, lanes] * nk + ti_scr[1, 0:1, lanes])
        return tiles, pay

    cands = [candidates(lanes) for lanes in halves]
    if exact:
        tiles = [c[0] for c in cands]
        for kk in range(k):
            for i, lanes in enumerate(halves):
                m, expert, tiles[i] = _pop_max(tiles[i], sub_iota, payload=cands[i][1])
                bs_scr[kk:kk + 1, lanes] = m[0:1]
                et_ref[kk:kk + 1, lanes] = expert[0:1]
    else:
        for (tiles, pay), lanes in zip(cands, halves):
            top_v, top_i, t = _top_sorted(tiles, pay, k)
            tie = jnp.maximum(tie, t)
            for kk in range(k):
                bs_scr[kk:kk + 1, lanes] = top_v[kk][0:1]
                et_ref[kk:kk + 1, lanes] = top_i[kk][0:1]
    for lanes in halves:
        bs = bs_scr[:, lanes]
        ex = jnp.exp(bs - jnp.max(bs, axis=0, keepdims=True))
        gt_ref[:, lanes] = ex / jnp.sum(ex, axis=0, keepdims=True)
    return tie


def _expert_up(j, h_ref, u_ref, e_ref, act_ref, *, te):
    tb = h_ref.shape[0]
    u = [u_ref[c * UP_CHUNK:(c + 1) * UP_CHUNK, :].astype(BF16) for c in range(te // UP_CHUNK)]
    for m in range(tb // UP_ROWS):
        rows = slice(m * UP_ROWS, (m + 1) * UP_ROWS)
        e = e_ref[rows, :]
        row = e >> PEER_KEY_BITS
        col = (e & (PEER_NKEYS - 1))[..., None]
        acc = act_ref[rows, :]
        h = h_ref[rows, :]
        for c in range(te // UP_CHUNK):
            dense = lax.dot_general(h, u[c], (((1,), (1,)), ((), ())), preferred_element_type=F32)
            for q in range(UP_CHUNK // LANES):
                got = lax.gather(dense[:, q * LANES:(q + 1) * LANES], col, _LANE_GATHER, (1, 1),
                                 mode=lax.GatherScatterMode.PROMISE_IN_BOUNDS)
                acc = jnp.where(row == (j * te + c * UP_CHUNK) // LANES + q, got, acc)
        act_ref[rows, :] = acc


def _route_kernel(h_ref, wq_ref, keys_ref, v_ref, e_ref, g_ref, vb_ref,
                  q_scr, s_scr, ts_scr, ti_scr, bs_scr, et_scr, gt_scr):
    vb_ref[...] = v_ref[...].astype(BF16)
    q_scr[...] = jnp.dot(h_ref[...], wq_ref[...], preferred_element_type=F32).astype(BF16)

    def heads(g, carry):
        hs = [g * RT_HEADS + u for u in range(RT_HEADS)]
        for u, h in enumerate(hs):
            _route_scores(h, q_scr, keys_ref, s_scr.at[u])
        args = [(s_scr.at[u], ts_scr.at[u], ti_scr.at[u], bs_scr.at[u], et_scr.at[h], gt_scr.at[h])
                for u, h in enumerate(hs)]
        ties = [_route_topk(*a, exact=False) for a in args]
        for a, tie in zip(args, ties):
            @pl.when(jnp.max(tie) > 0.0)
            def _():
                _route_topk(*a, exact=True)

        return carry

    lax.fori_loop(0, PEER_HEADS // RT_HEADS, heads, 0)

    for half in range(RT_SLICE // LANES):
        lanes = slice(half * LANES, (half + 1) * LANES)
        et = jnp.concatenate([et_scr[h, :, lanes] for h in range(PEER_HEADS)], axis=0)
        gt = jnp.concatenate([gt_scr[h, :, lanes] for h in range(PEER_HEADS)], axis=0)
        e_ref[lanes, :] = et.T.astype(I32)
        g_ref[lanes, :] = gt.T


def _route(h2, wq, keys, v):
    k = PEER_TOPK
    steps = N_TOK // RT_SLICE
    tok = pl.BlockSpec((RT_SLICE, PEER_PICKS), lambda i: (i, 0))
    v_slab = pl.BlockSpec((PEER_EXPERTS // steps, D_MODEL), lambda i: (i, 0))
    return pl.pallas_call(
        _route_kernel,
        out_shape=[jax.ShapeDtypeStruct((N_TOK, PEER_PICKS), I32),
                   jax.ShapeDtypeStruct((N_TOK, PEER_PICKS), F32),
                   jax.ShapeDtypeStruct((PEER_EXPERTS, D_MODEL), BF16)],
        grid=(steps,),
        in_specs=[pl.BlockSpec((RT_SLICE, D_MODEL), lambda i: (i, 0)),
                  pl.BlockSpec((D_MODEL, PEER_HEADS * PEER_QDIM), lambda i: (0, 0),
                               pipeline_mode=pl.Buffered(1)),
                  pl.BlockSpec((PEER_HEADS * 2, PEER_NKEYS, PEER_QDIM // 2), lambda i: (0, 0, 0)),
                  v_slab],
        out_specs=[tok, tok, v_slab],
        scratch_shapes=[
            pltpu.VMEM((RT_SLICE, PEER_HEADS * PEER_QDIM), BF16),
            pltpu.VMEM((RT_HEADS, 2, PEER_NKEYS, RT_SLICE), F32),
            pltpu.VMEM((RT_HEADS, 2, k, RT_SLICE), F32),
            pltpu.VMEM((RT_HEADS, 2, k, RT_SLICE), F32),
            pltpu.VMEM((RT_HEADS, k, RT_SLICE), F32),
            pltpu.VMEM((PEER_HEADS, k, RT_SLICE), F32),
            pltpu.VMEM((PEER_HEADS, k, RT_SLICE), F32),
        ],
        compiler_params=_params(("parallel",), 40),
        name="peer_route",
    )(h2, wq, keys, v)


def _peer_up_kernel(h_ref, u_ref, e_ref, act_ref, *, te):
    j = pl.program_id(1)

    @pl.when(j == 0)
    def _():
        act_ref[...] = jnp.zeros_like(act_ref)

    _expert_up(j, h_ref, u_ref, e_ref, act_ref, te=te)


def _peer_up(h2, u, e, tb=2048, te=1024):
    return pl.pallas_call(
        functools.partial(_peer_up_kernel, te=te),
        out_shape=jax.ShapeDtypeStruct((N_TOK, PEER_PICKS), F32),
        grid=(N_TOK // tb, PEER_EXPERTS // te),
        in_specs=[pl.BlockSpec((tb, D_MODEL), lambda i, j: (i, 0)),
                  pl.BlockSpec((te, D_MODEL), lambda i, j: (j, 0)),
                  pl.BlockSpec((tb, PEER_PICKS), lambda i, j: (i, 0))],
        out_specs=pl.BlockSpec((tb, PEER_PICKS), lambda i, j: (i, 0)),
        compiler_params=_params(("parallel", "arbitrary"), 56),
        name="peer_up",
    )(h2, u, e)


def _peer_coef_kernel(e_ref, g_ref, act_ref, p_ref, row_scr, col_scr, w_scr, c_scr, *, tp):
    e = e_ref[...]
    row_scr[...] = (e >> PEER_KEY_BITS).astype(F32)
    col_scr[...] = (e & (PEER_NKEYS - 1)).astype(F32)
    act = act_ref[...]
    w_scr[...] = g_ref[...] * (0.5 * act * (1.0 + lax.erf(act * math.sqrt(0.5))))
    sub = 16
    iota = lax.broadcasted_iota(I32, (PEER_NKEYS // sub, sub, PEER_PICKS), 0) * sub \
        + lax.broadcasted_iota(I32, (PEER_NKEYS // sub, sub, PEER_PICKS), 1)
    iota = iota.astype(F32).astype(BF16)
    one = jnp.ones((), BF16)
    zero = jnp.zeros((), BF16)

    def bcast(ref, t):
        return jnp.broadcast_to(ref[pl.ds(t, 1), :], (sub, PEER_PICKS)).astype(BF16)[None]

    def products(t0):
        for t in range(sub):
            rows, cols, w = bcast(row_scr, t0 + t), bcast(col_scr, t0 + t), bcast(w_scr, t0 + t)
            left = jnp.where(iota == rows, one, zero).reshape(PEER_NKEYS, PEER_PICKS)
            right = jnp.where(iota == cols, w, zero).reshape(PEER_NKEYS, PEER_PICKS)
            c_scr[t0 + t] = lax.dot_general(left, right, (((1,), (1,)), ((), ())),
                                            preferred_element_type=F32)

    def regroup(t0):
        sw = jnp.swapaxes(c_scr[pl.ds(t0, sub)], 0, 1)
        for k1 in range(PEER_NKEYS):
            p_ref[pl.ds(t0, sub), k1 * PEER_NKEYS:(k1 + 1) * PEER_NKEYS] = sw[k1].astype(p_ref.dtype)

    def block(blk, carry):
        base = pl.multiple_of(blk * COEF_UNROLL, COEF_UNROLL)
        for t0 in range(0, COEF_UNROLL, sub):
            products(base + t0)
            if t0:
                regroup(base + t0 - sub)
        regroup(base + COEF_UNROLL - sub)
        return carry

    lax.fori_loop(0, tp // COEF_UNROLL, block, 0)


COEF_UNROLL = 64


def _peer_coef(e, g, act, tp=128):
    tok = pl.BlockSpec((tp, PEER_PICKS), lambda i: (i, 0))
    return pl.pallas_call(
        functools.partial(_peer_coef_kernel, tp=tp),
        out_shape=jax.ShapeDtypeStruct((N_TOK, PEER_EXPERTS), BF16),
        grid=(N_TOK // tp,),
        in_specs=[tok, tok, tok],
        out_specs=pl.BlockSpec((tp, PEER_EXPERTS), lambda i: (i, 0)),
        scratch_shapes=[pltpu.VMEM((tp, PEER_PICKS), F32),
                        pltpu.VMEM((tp, PEER_PICKS), F32),
                        pltpu.VMEM((tp, PEER_PICKS), F32),
                        pltpu.VMEM((tp, PEER_NKEYS, PEER_NKEYS), F32)],
        compiler_params=_params(("parallel",), 32),
        name="peer_coef",
    )(e, g, act)


def _peer_down_kernel(p_ref, v_ref, x_ref, g_ref, o_ref):
    j = pl.program_id(1)

    @pl.when(j == 0)
    def _():
        o_ref[...] = jnp.zeros_like(o_ref)

    o_ref[...] += jnp.dot(p_ref[...], v_ref[...], preferred_element_type=F32)
    cols = pl.ds(pl.multiple_of(j * LANES, LANES), LANES)
    o_ref[:, cols] += x_ref[...]

    @pl.when(j == pl.num_programs(1) - 1)
    def _():
        o_ref[...] = _rms(o_ref[...], g_ref[...])


def _peer_down(coef, v, x1, g_final, tm=2048):
    tk = PEER_EXPERTS // (D_MODEL // LANES)
    return pl.pallas_call(
        _peer_down_kernel,
        out_shape=jax.ShapeDtypeStruct((N_TOK, D_MODEL), F32),
        grid=(N_TOK // tm, PEER_EXPERTS // tk),
        in_specs=[pl.BlockSpec((tm, tk), lambda i, j: (i, j)),
                  pl.BlockSpec((tk, D_MODEL), lambda i, j: (j, 0)),
                  pl.BlockSpec((tm, LANES), lambda i, j: (i, j)),
                  pl.BlockSpec((1, D_MODEL), lambda i, j: (0, 0))],
        out_specs=pl.BlockSpec((tm, D_MODEL), lambda i, j: (i, 0)),
        compiler_params=_params(("parallel", "arbitrary"), 60),
        name="peer_down",
    )(coef, v, x1, g_final)


def kernel(x, rel_bias, norm_mix_g, w_in, sink_a, w_oa, w_ob, w_out, norm_ffn_g,
           peer_wq, peer_keys, peer_u, peer_v, norm_final_g):
    assert x.shape == (BATCH, SEQ, D_MODEL) and w_in.shape[0] == 1
    x2 = x.reshape(N_TOK, D_MODEL)
    proj = _inproj(x2, norm_mix_g, w_in[0].astype(BF16))

    oa = _windowed_attention(proj, rel_bias, sink_a[0])
    ob = _dilated_attention(proj, rel_bias)
    x1, h2 = _outproj(oa, ob, proj, x2, w_oa[0].astype(BF16), w_ob[0].astype(BF16),
                      w_out[0].astype(BF16), norm_ffn_g)

    keys = peer_keys[0].reshape(PEER_HEADS * 2, PEER_NKEYS, PEER_QDIM // 2).astype(BF16)
    e, gate, v_bf16 = _route(h2, peer_wq[0].astype(BF16), keys, peer_v[0])
    act = _peer_up(h2, peer_u[0], e)
    coef = _peer_coef(e, gate, act)
    out = _peer_down(coef, v_bf16, x1, norm_final_g.reshape(1, D_MODEL))
    return out.reshape(BATCH, SEQ, D_MODEL)
```

```python
import functools
import math

import numpy as np
import jax
import jax.numpy as jnp
from jax import lax
from jax.experimental import pallas as pl
from jax.experimental.pallas import tpu as pltpu

F32 = jnp.float32
BF16 = jnp.bfloat16
I32 = jnp.int32

D_MODEL = 2048
BATCH = 2
SEQ = 4096
N_TOK = BATCH * SEQ
HEAD_DIM = 128
LANES = 128
SUBLANES = 8

A_Q_HEADS = 8
A_KV_HEADS = 2
A_GROUP = A_Q_HEADS // A_KV_HEADS
A_HALF_WINDOW = 128
B_DILATIONS = (1, 4, 16)
B_GROUPS = 3
B_HEADS_PER_GROUP = 4
B_HALF_SPAN = 64
N_BUCKETS = 32
MAX_DISTANCE = 1024
N_ATTN_HEADS = A_Q_HEADS + B_GROUPS * B_HEADS_PER_GROUP

PEER_HEADS = 8
PEER_NKEYS = 128
PEER_KEY_BITS = PEER_NKEYS.bit_length() - 1
assert 1 << PEER_KEY_BITS == PEER_NKEYS
PEER_EXPERTS = PEER_NKEYS * PEER_NKEYS
PEER_QDIM = 256
PEER_TOPK = 16
PEER_PICKS = PEER_HEADS * PEER_TOPK
EPS = 1e-6

A_Q_W = A_Q_HEADS * HEAD_DIM
A_KV_W = A_KV_HEADS * HEAD_DIM
B_W = B_GROUPS * B_HEADS_PER_GROUP * HEAD_DIM
B_OUT_W = B_HEADS_PER_GROUP * HEAD_DIM
IN_WIDTH = A_Q_W + 2 * A_KV_W + 3 * B_W + 2 * D_MODEL
QA_CB = 0
KA_CB = A_Q_W // LANES
VA_CB = (A_Q_W + A_KV_W) // LANES
QB_CB = (A_Q_W + 2 * A_KV_W) // LANES
KB_CB = QB_CB + B_W // LANES
VB_CB = KB_CB + B_W // LANES
GA_OFF = A_Q_W + 2 * A_KV_W + 3 * B_W
GB_OFF = GA_OFF + D_MODEL
IN_CB = IN_WIDTH // LANES

NEG = -1e30
ATT_SCALE = HEAD_DIM ** -0.5
ATT_QT = 128
ATT_GROUP = 8

MIB = 1024 * 1024


def _params(sem, vmem_mib):
    return pltpu.CompilerParams(dimension_semantics=sem, vmem_limit_bytes=vmem_mib * MIB)


def _rms(x, g):
    return x * lax.rsqrt(jnp.mean(x * x, axis=-1, keepdims=True) + EPS) * g


def _inproj_kernel(x_ref, g_ref, w_ref, o_ref, h_scr):
    @pl.when(pl.program_id(1) == 0)
    def _():
        h_scr[...] = _rms(x_ref[...], g_ref[...]).astype(BF16)

    res = jnp.dot(h_scr[...], w_ref[...], preferred_element_type=F32)
    for c in range(o_ref.shape[0]):
        o_ref[c] = res[:, c * LANES:(c + 1) * LANES].astype(o_ref.dtype)


def _inproj(x2, g, w, tm=1024, tn=2048):
    n = w.shape[1]
    return pl.pallas_call(
        _inproj_kernel,
        out_shape=jax.ShapeDtypeStruct((n // LANES, N_TOK, LANES), BF16),
        grid=(N_TOK // tm, n // tn),
        in_specs=[
            pl.BlockSpec((tm, D_MODEL), lambda i, j: (i, 0)),
            pl.BlockSpec((1, D_MODEL), lambda i, j: (0, 0)),
            pl.BlockSpec((D_MODEL, tn), lambda i, j: (0, j)),
        ],
        out_specs=pl.BlockSpec((tn // LANES, tm, LANES), lambda i, j: (j, i, 0)),
        scratch_shapes=[pltpu.VMEM((tm, D_MODEL), BF16)],
        compiler_params=_params(("parallel", "arbitrary"), 52),
        name="inproj",
    )(x2, g, w)


def _t5_bucket_np(rel):
    half = N_BUCKETS // 2
    max_exact = half // 2
    ret = np.where(rel > 0, half, 0)
    n = np.abs(rel)
    nf = np.maximum(n, 1).astype(np.float64)
    large = max_exact + (np.log(nf / max_exact) / math.log(MAX_DISTANCE / max_exact) * (half - max_exact)).astype(np.int64)
    large = np.minimum(large, half - 1)
    return (ret + np.where(n < max_exact, n, large)).astype(np.int32)


def _bucket_matrix(hw, dist_scale):
    w = ATT_QT + 2 * hw
    delta = (np.arange(w)[None, :] - hw) - np.arange(ATT_QT)[:, None]
    bkt = _t5_bucket_np(delta * dist_scale)
    return np.where(np.abs(delta) <= hw, bkt, -1).astype(np.int32)


def _residues_per_trip(dil):
    tiles = SEQ // dil // ATT_QT
    return max(1, min(dil, ATT_GROUP // tiles))


def _bias_tile(bkt_ref, tab_ref, head, bias_scr):
    bkt = bkt_ref[...]
    bias = jnp.full(bkt.shape, NEG, F32)
    for b in range(N_BUCKETS):
        bias = jnp.where(bkt == b, tab_ref[b, head], bias)
    bias_scr[...] = bias


def _banded_attention(q_ref, k_ref, v_ref, bias_scr, kpad, vpad, stage, emit, *, dil, hw, sink):
    seq = SEQ // dil
    qt = ATT_QT
    win = qt + 2 * hw
    n_res = _residues_per_trip(dil)

    zeros = jnp.zeros((hw, HEAD_DIM), BF16)
    for pad_ref in (kpad, vpad):
        for u in range(n_res):
            pad_ref[u, 0:hw, :] = zeros
            pad_ref[u, hw + seq:hw + seq + hw, :] = zeros
    if dil == 1:
        kpad[0, hw:hw + seq, :] = k_ref[...]
        vpad[0, hw:hw + seq, :] = v_ref[...]
    else:
        q32, k32, v32, qres = stage
        q32[...] = q_ref[...].astype(F32)
        k32[...] = k_ref[...].astype(F32)
        v32[...] = v_ref[...].astype(F32)

    def scores(q0, q, kw):
        s = lax.dot_general(q, kw, (((1,), (1,)), ((), ())), preferred_element_type=F32)
        s = s * ATT_SCALE + bias_scr[...]
        kpos = q0 - hw + lax.broadcasted_iota(I32, (qt, win), 1)
        return jnp.where((kpos >= 0) & (kpos < seq), s, NEG)

    def softmax(s):
        m = jnp.max(s, axis=-1, keepdims=True)
        if sink is not None:
            m = jnp.maximum(m, sink)
        p = jnp.exp(s - m)
        den = jnp.sum(p, axis=-1, keepdims=True)
        if sink is not None:
            den = den + jnp.exp(sink - m)
        return p.astype(BF16), m, den

    group = min(ATT_GROUP, seq // qt)

    def residues(g, carry):
        r0 = g * n_res
        if dil > 1:
            for u in range(n_res):
                kpad[u, hw:hw + seq, :] = k32[pl.ds(r0 + u, seq, stride=dil), :].astype(BF16)
                vpad[u, hw:hw + seq, :] = v32[pl.ds(r0 + u, seq, stride=dil), :].astype(BF16)
                qres[u] = q32[pl.ds(r0 + u, seq, stride=dil), :].astype(BF16)

        def body(t, c):
            work = [(u, pl.multiple_of((t * group + i) * qt, qt)) for u in range(n_res) for i in range(group)]
            q_src = (lambda u: qres.at[u]) if dil > 1 else (lambda u: q_ref)
            s_all = [scores(q0, q_src(u)[pl.ds(q0, qt), :], kpad[u, pl.ds(q0, win), :]) for u, q0 in work]
            p_all = [softmax(s) for s in s_all]
            o_all = [jnp.dot(p, vpad[u, pl.ds(q0, win), :], preferred_element_type=F32) / den
                     for (u, q0), (p, m, den) in zip(work, p_all)]
            for (u, q0), (p, m, den), o in zip(work, p_all, o_all):
                emit(q0 if dil == 1 else r0 + u + q0 * dil, dil, o, m, den)
            return c

        lax.fori_loop(0, seq // (qt * group), body, 0)
        return carry

    if dil > 1:
        lax.fori_loop(0, dil // n_res, residues, 0)
    else:
        residues(0, 0)


def _attn_scratch(dil, hw):
    n_res = _residues_per_trip(dil)
    seq = SEQ // dil
    return [pltpu.VMEM((n_res, seq + 2 * hw, HEAD_DIM), BF16),
            pltpu.VMEM((n_res, seq + 2 * hw, HEAD_DIM), BF16),
            pltpu.VMEM((ATT_QT, ATT_QT + 2 * hw), F32)]


def _windowed_kernel(tab_ref, sink_ref, q_ref, k_ref, v_ref, bkt_ref, o_ref, kpad, vpad, bias_scr):
    head = pl.program_id(1)
    _bias_tile(bkt_ref, tab_ref, head, bias_scr)

    def emit(start, stride, o, m, den):
        o_ref[0, 0, pl.ds(start, ATT_QT), :] = o.astype(o_ref.dtype)

    _banded_attention(q_ref.at[0, 0], k_ref.at[0, 0], v_ref.at[0, 0], bias_scr, kpad, vpad, None, emit,
                      dil=1, hw=A_HALF_WINDOW, sink=sink_ref[head])


def _windowed_attention(proj, rel_bias, sink):
    pv = proj.reshape(IN_CB, BATCH, SEQ, HEAD_DIM)
    hw = A_HALF_WINDOW
    blk = (1, 1, SEQ, HEAD_DIM)
    out = pl.pallas_call(
        _windowed_kernel,
        out_shape=jax.ShapeDtypeStruct((A_Q_HEADS, BATCH, SEQ, HEAD_DIM), BF16),
        grid=(BATCH, A_Q_HEADS),
        in_specs=[
            pl.BlockSpec(memory_space=pltpu.SMEM),
            pl.BlockSpec(memory_space=pltpu.SMEM),
            pl.BlockSpec(blk, lambda b, h: (QA_CB + h, b, 0, 0)),
            pl.BlockSpec(blk, lambda b, h: (KA_CB + h // A_GROUP, b, 0, 0)),
            pl.BlockSpec(blk, lambda b, h: (VA_CB + h // A_GROUP, b, 0, 0)),
            pl.BlockSpec((ATT_QT, ATT_QT + 2 * hw), lambda b, h: (0, 0)),
        ],
        out_specs=pl.BlockSpec(blk, lambda b, h: (h, b, 0, 0)),
        scratch_shapes=_attn_scratch(1, hw),
        compiler_params=_params(("parallel", "arbitrary"), 32),
        name="attn_windowed",
    )(rel_bias, sink, pv, pv, pv, jnp.asarray(_bucket_matrix(hw, 1)))
    return out.reshape(A_Q_HEADS, N_TOK, HEAD_DIM)


MERGE_ROWS = 512


def _dilated_kernel(tab_ref, *refs):
    n_g = B_GROUPS
    qkv = [refs[3 * g:3 * g + 3] for g in range(n_g)]
    bkts = refs[3 * n_g:4 * n_g]
    o_ref = refs[4 * n_g]
    scr = list(refs[4 * n_g + 1:])
    pads = [scr[3 * g:3 * g + 3] for g in range(n_g)]
    q32, k32, v32, o32, l32 = scr[3 * n_g:3 * n_g + 5]
    qres = scr[3 * n_g + 5:]
    slot = pl.program_id(1)

    for g, dil in enumerate(B_DILATIONS):
        kpad, vpad, bias_scr = pads[g]
        _bias_tile(bkts[g], tab_ref, A_Q_HEADS + g * B_HEADS_PER_GROUP + slot, bias_scr)

        def emit(start, stride, o, m, den, g=g):
            rows = pl.ds(start, ATT_QT) if stride == 1 else pl.ds(start, ATT_QT, stride=stride)
            o32[g, rows, :] = o
            l32[g, rows, :] = jnp.broadcast_to(m + jnp.log(den), (ATT_QT, HEAD_DIM))

        q_ref, k_ref, v_ref = qkv[g]
        stage = None if dil == 1 else (q32, k32, v32, qres[g])
        _banded_attention(q_ref.at[0, 0], k_ref.at[0, 0], v_ref.at[0, 0], bias_scr, kpad, vpad, stage, emit,
                          dil=dil, hw=B_HALF_SPAN, sink=None)

    def merge(c, carry):
        rows = pl.ds(pl.multiple_of(c * MERGE_ROWS, MERGE_ROWS), MERGE_ROWS)
        lses = [l32[g, rows, :] for g in range(n_g)]
        mx = functools.reduce(jnp.maximum, lses)
        es = [jnp.exp(l - mx) for l in lses]
        den = functools.reduce(jnp.add, es)
        o_ref[0, 0, rows, :] = functools.reduce(
            jnp.add, [(e / den) * o32[g, rows, :] for g, e in enumerate(es)]).astype(o_ref.dtype)
        return carry

    lax.fori_loop(0, SEQ // MERGE_ROWS, merge, 0)


def _dilated_attention(proj, rel_bias):
    pv = proj.reshape(IN_CB, BATCH, SEQ, HEAD_DIM)
    hw = B_HALF_SPAN
    blk = (1, 1, SEQ, HEAD_DIM)
    in_specs = [pl.BlockSpec(memory_space=pltpu.SMEM)]
    args = [rel_bias]
    for g in range(B_GROUPS):
        for cb in (QB_CB, KB_CB, VB_CB):
            in_specs.append(pl.BlockSpec(blk, lambda b, h, c=cb + g * B_HEADS_PER_GROUP: (c + h, b, 0, 0)))
            args.append(pv)
    for dil in B_DILATIONS:
        in_specs.append(pl.BlockSpec((ATT_QT, ATT_QT + 2 * hw), lambda b, h: (0, 0)))
        args.append(jnp.asarray(_bucket_matrix(hw, dil)))
    scratch = []
    for dil in B_DILATIONS:
        scratch += _attn_scratch(dil, hw)
    scratch += [pltpu.VMEM((SEQ, HEAD_DIM), F32)] * 3
    scratch += [pltpu.VMEM((B_GROUPS, SEQ, HEAD_DIM), F32)] * 2
    scratch += [pltpu.VMEM((_residues_per_trip(dil), SEQ // dil, HEAD_DIM), BF16) for dil in B_DILATIONS]
    out = pl.pallas_call(
        _dilated_kernel,
        out_shape=jax.ShapeDtypeStruct((B_HEADS_PER_GROUP, BATCH, SEQ, HEAD_DIM), BF16),
        grid=(BATCH, B_HEADS_PER_GROUP),
        in_specs=in_specs,
        out_specs=pl.BlockSpec(blk, lambda b, h: (h, b, 0, 0)),
        scratch_shapes=scratch,
        compiler_params=_params(("parallel", "arbitrary"), 52),
        name="attn_dilated",
    )(*args)
    return out.reshape(B_HEADS_PER_GROUP, N_TOK, HEAD_DIM)


def _outproj_kernel(oa_ref, ob_ref, ga_ref, gb_ref, x_ref, woa_ref, wob_ref, wout_ref, gn_ref,
                    x1_ref, h2_ref):
    wide = lambda ref: jnp.concatenate([ref[c] for c in range(ref.shape[0])], axis=-1)
    ya = jnp.dot(wide(oa_ref), woa_ref[...], preferred_element_type=F32)
    yb = jnp.dot(wide(ob_ref), wob_ref[...], preferred_element_type=F32)
    merged = (jax.nn.sigmoid(wide(ga_ref).astype(F32)) * ya
              + jax.nn.sigmoid(wide(gb_ref).astype(F32)) * yb)
    x1 = x_ref[...] + jnp.dot(merged.astype(BF16), wout_ref[...], preferred_element_type=F32)
    x1_ref[...] = x1
    h2_ref[...] = _rms(x1, gn_ref[...]).astype(BF16)


def _outproj(oa, ob, proj, x2, w_oa, w_ob, w_out, g_ffn, tm=256):
    row = lambda w: pl.BlockSpec((tm, w), lambda i: (i, 0))
    slabs = lambda n, first: pl.BlockSpec((n, tm, LANES), lambda i: (first // n, i, 0))
    gate_cb = D_MODEL // LANES
    const = lambda shape: pl.BlockSpec(shape, lambda i: (0, 0), pipeline_mode=pl.Buffered(1))
    return pl.pallas_call(
        _outproj_kernel,
        out_shape=[jax.ShapeDtypeStruct((N_TOK, D_MODEL), F32),
                   jax.ShapeDtypeStruct((N_TOK, D_MODEL), BF16)],
        grid=(N_TOK // tm,),
        in_specs=[slabs(A_Q_HEADS, 0), slabs(B_HEADS_PER_GROUP, 0),
                  slabs(gate_cb, GA_OFF // LANES), slabs(gate_cb, GB_OFF // LANES),
                  row(D_MODEL),
                  const((A_Q_W, D_MODEL)), const((B_OUT_W, D_MODEL)), const((D_MODEL, D_MODEL)),
                  const((1, D_MODEL))],
        out_specs=[row(D_MODEL), row(D_MODEL)],
        compiler_params=_params(("parallel",), 44),
        name="outproj",
    )(oa, ob, proj, proj, x2, w_oa, w_ob, w_out, g_ffn)


_FAR = 1024.0
RT_SLICE = 256
RT_HEADS = 4
UP_CHUNK = 256
UP_ROWS = 256

_LANE_GATHER = lax.GatherDimensionNumbers(
    offset_dims=(), collapsed_slice_dims=(1,), start_index_map=(1,),
    operand_batching_dims=(0,), start_indices_batching_dims=(0,))


def _tree(op, xs):
    xs = list(xs)
    while len(xs) > 1:
        nxt = [op(xs[i], xs[i + 1]) for i in range(0, len(xs) - 1, 2)]
        if len(xs) % 2:
            nxt.append(xs[-1])
        xs = nxt
    return xs[0]


def _all_sublanes(op, x):
    for shift in (4, 2, 1):
        x = op(x, pltpu.roll(x, shift, axis=0))
    return x


def _pop_max(tiles, sub_iota, payload=None):
    m = _all_sublanes(jnp.maximum, _tree(jnp.maximum, tiles))
    first = _tree(jnp.minimum, [jnp.where(t == m, float(SUBLANES * v), _FAR) for v, t in enumerate(tiles)])
    row = _all_sublanes(jnp.minimum, first + sub_iota)
    base = row - sub_iota
    hits = [base == float(SUBLANES * v) for v in range(len(tiles))]
    if payload is None:
        val = row
    else:
        val = _all_sublanes(jnp.maximum, _tree(jnp.maximum, [jnp.where(h, p, -1.0)
                                                            for h, p in zip(hits, payload)]))
    return m, val, [jnp.where(h, -jnp.inf, t) for h, t in zip(hits, tiles)]


def _route_scores(h, q_ref, keys_ref, s_scr):
    for c in range(2):
        col = pl.multiple_of((2 * h + c) * LANES, LANES)
        s_scr[c] = lax.dot_general(keys_ref[2 * h + c], q_ref[:, pl.ds(col, LANES)],
                                   (((1,), (1,)), ((), ())), preferred_element_type=F32)


def _batcher_pairs(n):
    size = 16
    pairs = []
    p = 1
    while p < size:
        k = p
        while k >= 1:
            for j in range(k % p, size - k, 2 * k):
                for i in range(min(k, size - j - k)):
                    if (i + j) // (2 * p) == (i + j + k) // (2 * p) and i + j + k < n:
                        pairs.append((i + j, i + j + k))
            k //= 2
        p *= 2
    return pairs


_FAR_ID = 1e9


def _top_sorted(vals, ids, k):
    vals, ids = list(vals), list(ids)
    n = len(vals)
    for i, j in _batcher_pairs(n):
        up = vals[j] > vals[i]
        vals[i], vals[j] = jnp.where(up, vals[j], vals[i]), jnp.where(up, vals[i], vals[j])
        ids[i], ids[j] = jnp.where(up, ids[j], ids[i]), jnp.where(up, ids[i], ids[j])
    top_v, top_i = [], []
    for it in range(k):
        m = _all_sublanes(jnp.maximum, vals[0])
        first = _all_sublanes(jnp.minimum, jnp.where(vals[0] == m, ids[0], _FAR_ID))
        win = ids[0] == first
        top_v.append(m)
        top_i.append(first)
        for r in range(min(k - it, n)):
            if r + 1 < n:
                vals[r] = jnp.where(win, vals[r + 1], vals[r])
                ids[r] = jnp.where(win, ids[r + 1], ids[r])
            else:
                vals[r] = jnp.where(win, -jnp.inf, vals[r])
    runner_up = _all_sublanes(jnp.maximum, vals[0])
    tie = jnp.zeros_like(runner_up)
    for a, b in zip(top_v, top_v[1:] + [runner_up]):
        tie = jnp.where(a == b, 1.0, tie)
    return top_v, top_i, tie


def _route_topk(s_scr, ts_scr, ti_scr, bs_scr, et_ref, gt_ref, *, exact):
    k = PEER_TOPK
    nk = float(PEER_NKEYS)
    sub_iota = lax.broadcasted_iota(I32, (SUBLANES, LANES), 0).astype(F32)
    halves = [slice(i * LANES, (i + 1) * LANES) for i in range(RT_SLICE // LANES)]
    tie = None if exact else jnp.zeros((SUBLANES, LANES), F32)

    chains = []
    for c in range(2):
        for lanes in halves:
            tiles = [s_scr[c, SUBLANES * v:SUBLANES * (v + 1), lanes] for v in range(PEER_NKEYS // SUBLANES)]
            chains.append([c, lanes, tiles])
    if exact:
        for kk in range(k):
            for chain in chains:
                c, lanes, tiles = chain
                m, row, chain[2] = _pop_max(tiles, sub_iota)
                ts_scr[c, kk:kk + 1, lanes] = m[0:1]
                ti_scr[c, kk:kk + 1, lanes] = row[0:1]
    else:
        row_ids = [float(SUBLANES * v) + sub_iota for v in range(PEER_NKEYS // SUBLANES)]
        for c, lanes, tiles in chains:
            top_v, top_i, t = _top_sorted(tiles, row_ids, k)
            tie = jnp.maximum(tie, t)
            for kk in range(k):
                ts_scr[c, kk:kk + 1, lanes] = top_v[kk][0:1]
                ti_scr[c, kk:kk + 1, lanes] = top_i[kk][0:1]

    def candidates(lanes):
        s2 = [ts_scr[1, 0:8, lanes], ts_scr[1, 8:16, lanes]]
        i2 = [ti_scr[1, 0:8, lanes], ti_scr[1, 8:16, lanes]]
        s1 = lambda k1: ts_scr[0, k1:k1 + 1, lanes]
        i1 = lambda k1: ti_scr[0, k1:k1 + 1, lanes] * nk
        tiles = [s1(0) + s2[0], s1(0) + s2[1]]
        pay = [i1(0) + i2[0], i1(0) + i2[1]]
        for k1 in range(1, 8):
            allowed = k // (k1 + 1)
            t = s1(k1) + s2[0]
            tiles.append(t if allowed >= SUBLANES else jnp.where(sub_iota < float(allowed), t, -jnp.inf))
            pay.append(i1(k1) + i2[0])
        tiles.append(ts_scr[0, 8:16, lanes] + ts_scr[1, 0:1, lanes])
        pay.append(ti_scr[0, 8:16, lanes] * nk + ti_scr[1, 0:1, lanes])
        return tiles, pay

    cands = [candidates(lanes) for lanes in halves]
    if exact:
        tiles = [c[0] for c in cands]
        for kk in range(k):
            for i, lanes in enumerate(halves):
                m, expert, tiles[i] = _pop_max(tiles[i], sub_iota, payload=cands[i][1])
                bs_scr[kk:kk + 1, lanes] = m[0:1]
                et_ref[kk:kk + 1, lanes] = expert[0:1]
    else:
        for (tiles, pay), lanes in zip(cands, halves):
            top_v, top_i, t = _top_sorted(tiles, pay, k)
            tie = jnp.maximum(tie, t)
            for kk in range(k):
                bs_scr[kk:kk + 1, lanes] = top_v[kk][0:1]
                et_ref[kk:kk + 1, lanes] = top_i[kk][0:1]
    for lanes in halves:
        bs = bs_scr[:, lanes]
        ex = jnp.exp(bs - jnp.max(bs, axis=0, keepdims=True))
        gt_ref[:, lanes] = ex / jnp.sum(ex, axis=0, keepdims=True)
    return tie


def _expert_up(j, h_ref, u_ref, e_ref, act_ref, *, te):
    tb = h_ref.shape[0]
    u = [u_ref[c * UP_CHUNK:(c + 1) * UP_CHUNK, :].astype(BF16) for c in range(te // UP_CHUNK)]
    for m in range(tb // UP_ROWS):
        rows = slice(m * UP_ROWS, (m + 1) * UP_ROWS)
        e = e_ref[rows, :]
        row = e >> PEER_KEY_BITS
        col = (e & (PEER_NKEYS - 1))[..., None]
        acc = act_ref[rows, :]
        h = h_ref[rows, :]
        for c in range(te // UP_CHUNK):
            dense = lax.dot_general(h, u[c], (((1,), (1,)), ((), ())), preferred_element_type=F32)
            for q in range(UP_CHUNK // LANES):
                got = lax.gather(dense[:, q * LANES:(q + 1) * LANES], col, _LANE_GATHER, (1, 1),
                                 mode=lax.GatherScatterMode.PROMISE_IN_BOUNDS)
                acc = jnp.where(row == (j * te + c * UP_CHUNK) // LANES + q, got, acc)
        act_ref[rows, :] = acc


def _route_kernel(h_ref, wq_ref, keys_ref, v_ref, e_ref, g_ref, vb_ref,
                  q_scr, s_scr, ts_scr, ti_scr, bs_scr, et_scr, gt_scr):
    vb_ref[...] = v_ref[...].astype(BF16)
    q_scr[...] = jnp.dot(h_ref[...], wq_ref[...], preferred_element_type=F32).astype(BF16)

    def heads(g, carry):
        hs = [g * RT_HEADS + u for u in range(RT_HEADS)]
        for u, h in enumerate(hs):
            _route_scores(h, q_scr, keys_ref, s_scr.at[u])
        args = [(s_scr.at[u], ts_scr.at[u], ti_scr.at[u], bs_scr.at[u], et_scr.at[h], gt_scr.at[h])
                for u, h in enumerate(hs)]
        ties = [_route_topk(*a, exact=False) for a in args]
        for a, tie in zip(args, ties):
            @pl.when(jnp.max(tie) > 0.0)
            def _():
                _route_topk(*a, exact=True)

        return carry

    lax.fori_loop(0, PEER_HEADS // RT_HEADS, heads, 0)

    for half in range(RT_SLICE // LANES):
        lanes = slice(half * LANES, (half + 1) * LANES)
        et = jnp.concatenate([et_scr[h, :, lanes] for h in range(PEER_HEADS)], axis=0)
        gt = jnp.concatenate([gt_scr[h, :, lanes] for h in range(PEER_HEADS)], axis=0)
        e_ref[lanes, :] = et.T.astype(I32)
        g_ref[lanes, :] = gt.T


def _route(h2, wq, keys, v):
    k = PEER_TOPK
    steps = N_TOK // RT_SLICE
    tok = pl.BlockSpec((RT_SLICE, PEER_PICKS), lambda i: (i, 0))
    v_slab = pl.BlockSpec((PEER_EXPERTS // steps, D_MODEL), lambda i: (i, 0))
    return pl.pallas_call(
        _route_kernel,
        out_shape=[jax.ShapeDtypeStruct((N_TOK, PEER_PICKS), I32),
                   jax.ShapeDtypeStruct((N_TOK, PEER_PICKS), F32),
                   jax.ShapeDtypeStruct((PEER_EXPERTS, D_MODEL), BF16)],
        grid=(steps,),
        in_specs=[pl.BlockSpec((RT_SLICE, D_MODEL), lambda i: (i, 0)),
                  pl.BlockSpec((D_MODEL, PEER_HEADS * PEER_QDIM), lambda i: (0, 0),
                               pipeline_mode=pl.Buffered(1)),
                  pl.BlockSpec((PEER_HEADS * 2, PEER_NKEYS, PEER_QDIM // 2), lambda i: (0, 0, 0)),
                  v_slab],
        out_specs=[tok, tok, v_slab],
        scratch_shapes=[
            pltpu.VMEM((RT_SLICE, PEER_HEADS * PEER_QDIM), BF16),
            pltpu.VMEM((RT_HEADS, 2, PEER_NKEYS, RT_SLICE), F32),
            pltpu.VMEM((RT_HEADS, 2, k, RT_SLICE), F32),
            pltpu.VMEM((RT_HEADS, 2, k, RT_SLICE), F32),
            pltpu.VMEM((RT_HEADS, k, RT_SLICE), F32),
            pltpu.VMEM((PEER_HEADS, k, RT_SLICE), F32),
            pltpu.VMEM((PEER_HEADS, k, RT_SLICE), F32),
        ],
        compiler_params=_params(("parallel",), 40),
        name="peer_route",
    )(h2, wq, keys, v)


def _peer_up_kernel(h_ref, u_ref, e_ref, act_ref, *, te):
    j = pl.program_id(1)

    @pl.when(j == 0)
    def _():
        act_ref[...] = jnp.zeros_like(act_ref)

    _expert_up(j, h_ref, u_ref, e_ref, act_ref, te=te)


def _peer_up(h2, u, e, tb=2048, te=1024):
    return pl.pallas_call(
        functools.partial(_peer_up_kernel, te=te),
        out_shape=jax.ShapeDtypeStruct((N_TOK, PEER_PICKS), F32),
        grid=(N_TOK // tb, PEER_EXPERTS // te),
        in_specs=[pl.BlockSpec((tb, D_MODEL), lambda i, j: (i, 0)),
                  pl.BlockSpec((te, D_MODEL), lambda i, j: (j, 0)),
                  pl.BlockSpec((tb, PEER_PICKS), lambda i, j: (i, 0))],
        out_specs=pl.BlockSpec((tb, PEER_PICKS), lambda i, j: (i, 0)),
        compiler_params=_params(("parallel", "arbitrary"), 56),
        name="peer_up",
    )(h2, u, e)


def _peer_coef_kernel(e_ref, g_ref, act_ref, p_ref, row_scr, col_scr, w_scr, c_scr, *, tp):
    e = e_ref[...]
    row_scr[...] = (e >> PEER_KEY_BITS).astype(F32)
    col_scr[...] = (e & (PEER_NKEYS - 1)).astype(F32)
    act = act_ref[...]
    w_scr[...] = g_ref[...] * (0.5 * act * (1.0 + lax.erf(act * math.sqrt(0.5))))
    sub = 16
    iota = lax.broadcasted_iota(I32, (PEER_NKEYS // sub, sub, PEER_PICKS), 0) * sub \
        + lax.broadcasted_iota(I32, (PEER_NKEYS // sub, sub, PEER_PICKS), 1)
    iota = iota.astype(F32).astype(BF16)
    one = jnp.ones((), BF16)
    zero = jnp.zeros((), BF16)

    def bcast(ref, t):
        return jnp.broadcast_to(ref[pl.ds(t, 1), :], (sub, PEER_PICKS)).astype(BF16)[None]

    def products(t0):
        for t in range(sub):
            rows, cols, w = bcast(row_scr, t0 + t), bcast(col_scr, t0 + t), bcast(w_scr, t0 + t)
            left = jnp.where(iota == rows, one, zero).reshape(PEER_NKEYS, PEER_PICKS)
            right = jnp.where(iota == cols, w, zero).reshape(PEER_NKEYS, PEER_PICKS)
            c_scr[t0 + t] = lax.dot_general(left, right, (((1,), (1,)), ((), ())),
                                            preferred_element_type=F32)

    def regroup(t0):
        sw = jnp.swapaxes(c_scr[pl.ds(t0, sub)], 0, 1)
        for k1 in range(PEER_NKEYS):
            p_ref[pl.ds(t0, sub), k1 * PEER_NKEYS:(k1 + 1) * PEER_NKEYS] = sw[k1].astype(p_ref.dtype)

    def block(blk, carry):
        base = pl.multiple_of(blk * COEF_UNROLL, COEF_UNROLL)
        for t0 in range(0, COEF_UNROLL, sub):
            products(base + t0)
            if t0:
                regroup(base + t0 - sub)
        regroup(base + COEF_UNROLL - sub)
        return carry

    lax.fori_loop(0, tp // COEF_UNROLL, block, 0)


COEF_UNROLL = 64


def _peer_coef(e, g, act, tp=128):
    tok = pl.BlockSpec((tp, PEER_PICKS), lambda i: (i, 0))
    return pl.pallas_call(
        functools.partial(_peer_coef_kernel, tp=tp),
        out_shape=jax.ShapeDtypeStruct((N_TOK, PEER_EXPERTS), BF16),
        grid=(N_TOK // tp,),
        in_specs=[tok, tok, tok],
        out_specs=pl.BlockSpec((tp, PEER_EXPERTS), lambda i: (i, 0)),
        scratch_shapes=[pltpu.VMEM((tp, PEER_PICKS), F32),
                        pltpu.VMEM((tp, PEER_PICKS), F32),
                        pltpu.VMEM((tp, PEER_PICKS), F32),
                        pltpu.VMEM((tp, PEER_NKEYS, PEER_NKEYS), F32)],
        compiler_params=_params(("parallel",), 32),
        name="peer_coef",
    )(e, g, act)


def _peer_down_kernel(p_ref, v_ref, x_ref, g_ref, o_ref):
    j = pl.program_id(1)

    @pl.when(j == 0)
    def _():
        o_ref[...] = jnp.zeros_like(o_ref)

    o_ref[...] += jnp.dot(p_ref[...], v_ref[...], preferred_element_type=F32)
    cols = pl.ds(pl.multiple_of(j * LANES, LANES), LANES)
    o_ref[:, cols] += x_ref[...]

    @pl.when(j == pl.num_programs(1) - 1)
    def _():
        o_ref[...] = _rms(o_ref[...], g_ref[...])


def _peer_down(coef, v, x1, g_final, tm=2048):
    tk = PEER_EXPERTS // (D_MODEL // LANES)
    return pl.pallas_call(
        _peer_down_kernel,
        out_shape=jax.ShapeDtypeStruct((N_TOK, D_MODEL), F32),
        grid=(N_TOK // tm, PEER_EXPERTS // tk),
        in_specs=[pl.BlockSpec((tm, tk), lambda i, j: (i, j)),
                  pl.BlockSpec((tk, D_MODEL), lambda i, j: (j, 0)),
                  pl.BlockSpec((tm, LANES), lambda i, j: (i, j)),
                  pl.BlockSpec((1, D_MODEL), lambda i, j: (0, 0))],
        out_specs=pl.BlockSpec((tm, D_MODEL), lambda i, j: (i, 0)),
        compiler_params=_params(("parallel", "arbitrary"), 60),
        name="peer_down",
    )(coef, v, x1, g_final)


def kernel(x, rel_bias, norm_mix_g, w_in, sink_a, w_oa, w_ob, w_out, norm_ffn_g,
           peer_wq, peer_keys, peer_u, peer_v, norm_final_g):
    assert x.shape == (BATCH, SEQ, D_MODEL) and w_in.shape[0] == 1
    x2 = x.reshape(N_TOK, D_MODEL)
    proj = _inproj(x2, norm_mix_g, w_in[0].astype(BF16))

    oa = _windowed_attention(proj, rel_bias, sink_a[0])
    ob = _dilated_attention(proj, rel_bias)
    x1, h2 = _outproj(oa, ob, proj, x2, w_oa[0].astype(BF16), w_ob[0].astype(BF16),
                      w_out[0].astype(BF16), norm_ffn_g)

    keys = peer_keys[0].reshape(PEER_HEADS * 2, PEER_NKEYS, PEER_QDIM // 2).astype(BF16)
    e, gate, v_bf16 = _route(h2, peer_wq[0].astype(BF16), keys, peer_v[0])
    act = _peer_up(h2, peer_u[0], e)
    coef = _peer_coef(e, gate, act)
    out = _peer_down(coef, v_bf16, x1, norm_final_g.reshape(1, D_MODEL))
    return out.reshape(BATCH, SEQ, D_MODEL)
```

```python
import functools
import math

import numpy as np
import jax
import jax.numpy as jnp
from jax import lax
from jax.experimental import pallas as pl
from jax.experimental.pallas import tpu as pltpu

F32 = jnp.float32
BF16 = jnp.bfloat16
I32 = jnp.int32

D_MODEL = 2048
BATCH = 2
SEQ = 4096
N_TOK = BATCH * SEQ
HEAD_DIM = 128
LANES = 128
SUBLANES = 8

A_Q_HEADS = 8
A_KV_HEADS = 2
A_GROUP = A_Q_HEADS // A_KV_HEADS
A_HALF_WINDOW = 128
B_DILATIONS = (1, 4, 16)
B_GROUPS = 3
B_HEADS_PER_GROUP = 4
B_HALF_SPAN = 64
N_BUCKETS = 32
MAX_DISTANCE = 1024
N_ATTN_HEADS = A_Q_HEADS + B_GROUPS * B_HEADS_PER_GROUP

PEER_HEADS = 8
PEER_NKEYS = 128
PEER_KEY_BITS = PEER_NKEYS.bit_length() - 1
assert 1 << PEER_KEY_BITS == PEER_NKEYS
PEER_EXPERTS = PEER_NKEYS * PEER_NKEYS
PEER_QDIM = 256
PEER_TOPK = 16
PEER_PICKS = PEER_HEADS * PEER_TOPK
EPS = 1e-6

A_Q_W = A_Q_HEADS * HEAD_DIM
A_KV_W = A_KV_HEADS * HEAD_DIM
B_W = B_GROUPS * B_HEADS_PER_GROUP * HEAD_DIM
B_OUT_W = B_HEADS_PER_GROUP * HEAD_DIM
IN_WIDTH = A_Q_W + 2 * A_KV_W + 3 * B_W + 2 * D_MODEL
QA_CB = 0
KA_CB = A_Q_W // LANES
VA_CB = (A_Q_W + A_KV_W) // LANES
QB_CB = (A_Q_W + 2 * A_KV_W) // LANES
KB_CB = QB_CB + B_W // LANES
VB_CB = KB_CB + B_W // LANES
GA_OFF = A_Q_W + 2 * A_KV_W + 3 * B_W
GB_OFF = GA_OFF + D_MODEL
IN_CB = IN_WIDTH // LANES

NEG = -1e30
ATT_SCALE = HEAD_DIM ** -0.5
ATT_QT = 128
ATT_GROUP = 8

MIB = 1024 * 1024


def _params(sem, vmem_mib):
    return pltpu.CompilerParams(dimension_semantics=sem, vmem_limit_bytes=vmem_mib * MIB)


def _rms(x, g):
    return x * lax.rsqrt(jnp.mean(x * x, axis=-1, keepdims=True) + EPS) * g


def _inproj_kernel(x_ref, g_ref, w_ref, o_ref, h_scr):
    @pl.when(pl.program_id(1) == 0)
    def _():
        h_scr[...] = _rms(x_ref[...], g_ref[...]).astype(BF16)

    res = jnp.dot(h_scr[...], w_ref[...], preferred_element_type=F32)
    for c in range(o_ref.shape[0]):
        o_ref[c] = res[:, c * LANES:(c + 1) * LANES].astype(o_ref.dtype)


def _inproj(x2, g, w, tm=1024, tn=2048):
    n = w.shape[1]
    return pl.pallas_call(
        _inproj_kernel,
        out_shape=jax.ShapeDtypeStruct((n // LANES, N_TOK, LANES), BF16),
        grid=(N_TOK // tm, n // tn),
        in_specs=[
            pl.BlockSpec((tm, D_MODEL), lambda i, j: (i, 0)),
            pl.BlockSpec((1, D_MODEL), lambda i, j: (0, 0)),
            pl.BlockSpec((D_MODEL, tn), lambda i, j: (0, j)),
        ],
        out_specs=pl.BlockSpec((tn // LANES, tm, LANES), lambda i, j: (j, i, 0)),
        scratch_shapes=[pltpu.VMEM((tm, D_MODEL), BF16)],
        compiler_params=_params(("parallel", "arbitrary"), 52),
        name="inproj",
    )(x2, g, w)


def _t5_bucket_np(rel):
    half = N_BUCKETS // 2
    max_exact = half // 2
    ret = np.where(rel > 0, half, 0)
    n = np.abs(rel)
    nf = np.maximum(n, 1).astype(np.float64)
    large = max_exact + (np.log(nf / max_exact) / math.log(MAX_DISTANCE / max_exact) * (half - max_exact)).astype(np.int64)
    large = np.minimum(large, half - 1)
    return (ret + np.where(n < max_exact, n, large)).astype(np.int32)


def _bucket_matrix(hw, dist_scale):
    w = ATT_QT + 2 * hw
    delta = (np.arange(w)[None, :] - hw) - np.arange(ATT_QT)[:, None]
    bkt = _t5_bucket_np(delta * dist_scale)
    return np.where(np.abs(delta) <= hw, bkt, -1).astype(np.int32)


def _residues_per_trip(dil):
    tiles = SEQ // dil // ATT_QT
    return max(1, min(dil, ATT_GROUP // tiles))


def _bias_tile(bkt_ref, tab_ref, head, bias_scr):
    bkt = bkt_ref[...]
    bias = jnp.full(bkt.shape, NEG, F32)
    for b in range(N_BUCKETS):
        bias = jnp.where(bkt == b, tab_ref[b, head], bias)
    bias_scr[...] = bias


def _banded_attention(q_ref, k_ref, v_ref, bias_scr, kpad, vpad, stage, emit, *, dil, hw, sink):
    seq = SEQ // dil
    qt = ATT_QT
    win = qt + 2 * hw
    n_res = _residues_per_trip(dil)

    zeros = jnp.zeros((hw, HEAD_DIM), BF16)
    for pad_ref in (kpad, vpad):
        for u in range(n_res):
            pad_ref[u, 0:hw, :] = zeros
            pad_ref[u, hw + seq:hw + seq + hw, :] = zeros
    if dil == 1:
        kpad[0, hw:hw + seq, :] = k_ref[...]
        vpad[0, hw:hw + seq, :] = v_ref[...]
    else:
        q32, k32, v32, qres = stage
        q32[...] = q_ref[...].astype(F32)
        k32[...] = k_ref[...].astype(F32)
        v32[...] = v_ref[...].astype(F32)

    def scores(q0, q, kw):
        s = lax.dot_general(q, kw, (((1,), (1,)), ((), ())), preferred_element_type=F32)
        s = s * ATT_SCALE + bias_scr[...]
        kpos = q0 - hw + lax.broadcasted_iota(I32, (qt, win), 1)
        return jnp.where((kpos >= 0) & (kpos < seq), s, NEG)

    def softmax(s):
        m = jnp.max(s, axis=-1, keepdims=True)
        if sink is not None:
            m = jnp.maximum(m, sink)
        p = jnp.exp(s - m)
        den = jnp.sum(p, axis=-1, keepdims=True)
        if sink is not None:
            den = den + jnp.exp(sink - m)
        return p.astype(BF16), m, den

    group = min(ATT_GROUP, seq // qt)

    def residues(g, carry):
        r0 = g * n_res
        if dil > 1:
            for u in range(n_res):
                kpad[u, hw:hw + seq, :] = k32[pl.ds(r0 + u, seq, stride=dil), :].astype(BF16)
                vpad[u, hw:hw + seq, :] = v32[pl.ds(r0 + u, seq, stride=dil), :].astype(BF16)
                qres[u] = q32[pl.ds(r0 + u, seq, stride=dil), :].astype(BF16)

        def body(t, c):
            work = [(u, pl.multiple_of((t * group + i) * qt, qt)) for u in range(n_res) for i in range(group)]
            q_src = (lambda u: qres.at[u]) if dil > 1 else (lambda u: q_ref)
            s_all = [scores(q0, q_src(u)[pl.ds(q0, qt), :], kpad[u, pl.ds(q0, win), :]) for u, q0 in work]
            p_all = [softmax(s) for s in s_all]
            o_all = [jnp.dot(p, vpad[u, pl.ds(q0, win), :], preferred_element_type=F32) / den
                     for (u, q0), (p, m, den) in zip(work, p_all)]
            for (u, q0), (p, m, den), o in zip(work, p_all, o_all):
                emit(q0 if dil == 1 else r0 + u + q0 * dil, dil, o, m, den)
            return c

        lax.fori_loop(0, seq // (qt * group), body, 0)
        return carry

    if dil > 1:
        lax.fori_loop(0, dil // n_res, residues, 0)
    else:
        residues(0, 0)


def _attn_scratch(dil, hw):
    n_res = _residues_per_trip(dil)
    seq = SEQ // dil
    return [pltpu.VMEM((n_res, seq + 2 * hw, HEAD_DIM), BF16),
            pltpu.VMEM((n_res, seq + 2 * hw, HEAD_DIM), BF16),
            pltpu.VMEM((ATT_QT, ATT_QT + 2 * hw), F32)]


def _windowed_kernel(tab_ref, sink_ref, q_ref, k_ref, v_ref, bkt_ref, o_ref, kpad, vpad, bias_scr):
    head = pl.program_id(1)
    _bias_tile(bkt_ref, tab_ref, head, bias_scr)

    def emit(start, stride, o, m, den):
        o_ref[0, 0, pl.ds(start, ATT_QT), :] = o.astype(o_ref.dtype)

    _banded_attention(q_ref.at[0, 0], k_ref.at[0, 0], v_ref.at[0, 0], bias_scr, kpad, vpad, None, emit,
                      dil=1, hw=A_HALF_WINDOW, sink=sink_ref[head])


def _windowed_attention(proj, rel_bias, sink):
    pv = proj.reshape(IN_CB, BATCH, SEQ, HEAD_DIM)
    hw = A_HALF_WINDOW
    blk = (1, 1, SEQ, HEAD_DIM)
    out = pl.pallas_call(
        _windowed_kernel,
        out_shape=jax.ShapeDtypeStruct((A_Q_HEADS, BATCH, SEQ, HEAD_DIM), BF16),
        grid=(BATCH, A_Q_HEADS),
        in_specs=[
            pl.BlockSpec(memory_space=pltpu.SMEM),
            pl.BlockSpec(memory_space=pltpu.SMEM),
            pl.BlockSpec(blk, lambda b, h: (QA_CB + h, b, 0, 0)),
            pl.BlockSpec(blk, lambda b, h: (KA_CB + h // A_GROUP, b, 0, 0)),
            pl.BlockSpec(blk, lambda b, h: (VA_CB + h // A_GROUP, b, 0, 0)),
            pl.BlockSpec((ATT_QT, ATT_QT + 2 * hw), lambda b, h: (0, 0)),
        ],
        out_specs=pl.BlockSpec(blk, lambda b, h: (h, b, 0, 0)),
        scratch_shapes=_attn_scratch(1, hw),
        compiler_params=_params(("parallel", "arbitrary"), 32),
        name="attn_windowed",
    )(rel_bias, sink, pv, pv, pv, jnp.asarray(_bucket_matrix(hw, 1)))
    return out.reshape(A_Q_HEADS, N_TOK, HEAD_DIM)


MERGE_ROWS = 512


def _dilated_kernel(tab_ref, *refs):
    n_g = B_GROUPS
    qkv = [refs[3 * g:3 * g + 3] for g in range(n_g)]
    bkts = refs[3 * n_g:4 * n_g]
    o_ref = refs[4 * n_g]
    scr = list(refs[4 * n_g + 1:])
    pads = [scr[3 * g:3 * g + 3] for g in range(n_g)]
    q32, k32, v32, o32, l32 = scr[3 * n_g:3 * n_g + 5]
    qres = scr[3 * n_g + 5:]
    slot = pl.program_id(1)

    for g, dil in enumerate(B_DILATIONS):
        kpad, vpad, bias_scr = pads[g]
        _bias_tile(bkts[g], tab_ref, A_Q_HEADS + g * B_HEADS_PER_GROUP + slot, bias_scr)

        def emit(start, stride, o, m, den, g=g):
            rows = pl.ds(start, ATT_QT) if stride == 1 else pl.ds(start, ATT_QT, stride=stride)
            o32[g, rows, :] = o
            l32[g, rows, :] = jnp.broadcast_to(m + jnp.log(den), (ATT_QT, HEAD_DIM))

        q_ref, k_ref, v_ref = qkv[g]
        stage = None if dil == 1 else (q32, k32, v32, qres[g])
        _banded_attention(q_ref.at[0, 0], k_ref.at[0, 0], v_ref.at[0, 0], bias_scr, kpad, vpad, stage, emit,
                          dil=dil, hw=B_HALF_SPAN, sink=None)

    def merge(c, carry):
        rows = pl.ds(pl.multiple_of(c * MERGE_ROWS, MERGE_ROWS), MERGE_ROWS)
        lses = [l32[g, rows, :] for g in range(n_g)]
        mx = functools.reduce(jnp.maximum, lses)
        es = [jnp.exp(l - mx) for l in lses]
        den = functools.reduce(jnp.add, es)
        o_ref[0, 0, rows, :] = functools.reduce(
            jnp.add, [(e / den) * o32[g, rows, :] for g, e in enumerate(es)]).astype(o_ref.dtype)
        return carry

    lax.fori_loop(0, SEQ // MERGE_ROWS, merge, 0)


def _dilated_attention(proj, rel_bias):
    pv = proj.reshape(IN_CB, BATCH, SEQ, HEAD_DIM)
    hw = B_HALF_SPAN
    blk = (1, 1, SEQ, HEAD_DIM)
    in_specs = [pl.BlockSpec(memory_space=pltpu.SMEM)]
    args = [rel_bias]
    for g in range(B_GROUPS):
        for cb in (QB_CB, KB_CB, VB_CB):
            in_specs.append(pl.BlockSpec(blk, lambda b, h, c=cb + g * B_HEADS_PER_GROUP: (c + h, b, 0, 0)))
            args.append(pv)
    for dil in B_DILATIONS:
        in_specs.append(pl.BlockSpec((ATT_QT, ATT_QT + 2 * hw), lambda b, h: (0, 0)))
        args.append(jnp.asarray(_bucket_matrix(hw, dil)))
    scratch = []
    for dil in B_DILATIONS:
        scratch += _attn_scratch(dil, hw)
    scratch += [pltpu.VMEM((SEQ, HEAD_DIM), F32)] * 3
    scratch += [pltpu.VMEM((B_GROUPS, SEQ, HEAD_DIM), F32)] * 2
    scratch += [pltpu.VMEM((_residues_per_trip(dil), SEQ // dil, HEAD_DIM), BF16) for dil in B_DILATIONS]
    out = pl.pallas_call(
        _dilated_kernel,
        out_shape=jax.ShapeDtypeStruct((B_HEADS_PER_GROUP, BATCH, SEQ, HEAD_DIM), BF16),
        grid=(BATCH, B_HEADS_PER_GROUP),
        in_specs=in_specs,
        out_specs=pl.BlockSpec(blk, lambda b, h: (h, b, 0, 0)),
        scratch_shapes=scratch,
        compiler_params=_params(("parallel", "arbitrary"), 52),
        name="attn_dilated",
    )(*args)
    return out.reshape(B_HEADS_PER_GROUP, N_TOK, HEAD_DIM)


def _outproj_kernel(oa_ref, ob_ref, ga_ref, gb_ref, x_ref, woa_ref, wob_ref, wout_ref, gn_ref,
                    x1_ref, h2_ref):
    wide = lambda ref: jnp.concatenate([ref[c] for c in range(ref.shape[0])], axis=-1)
    ya = jnp.dot(wide(oa_ref), woa_ref[...], preferred_element_type=F32)
    yb = jnp.dot(wide(ob_ref), wob_ref[...], preferred_element_type=F32)
    merged = (jax.nn.sigmoid(wide(ga_ref).astype(F32)) * ya
              + jax.nn.sigmoid(wide(gb_ref).astype(F32)) * yb)
    x1 = x_ref[...] + jnp.dot(merged.astype(BF16), wout_ref[...], preferred_element_type=F32)
    x1_ref[...] = x1
    h2_ref[...] = _rms(x1, gn_ref[...]).astype(BF16)


def _outproj(oa, ob, proj, x2, w_oa, w_ob, w_out, g_ffn, tm=512):
    row = lambda w: pl.BlockSpec((tm, w), lambda i: (i, 0))
    slabs = lambda n, first: pl.BlockSpec((n, tm, LANES), lambda i: (first // n, i, 0))
    gate_cb = D_MODEL // LANES
    const = lambda shape: pl.BlockSpec(shape, lambda i: (0, 0), pipeline_mode=pl.Buffered(1))
    return pl.pallas_call(
        _outproj_kernel,
        out_shape=[jax.ShapeDtypeStruct((N_TOK, D_MODEL), F32),
                   jax.ShapeDtypeStruct((N_TOK, D_MODEL), BF16)],
        grid=(N_TOK // tm,),
        in_specs=[slabs(A_Q_HEADS, 0), slabs(B_HEADS_PER_GROUP, 0),
                  slabs(gate_cb, GA_OFF // LANES), slabs(gate_cb, GB_OFF // LANES),
                  row(D_MODEL),
                  const((A_Q_W, D_MODEL)), const((B_OUT_W, D_MODEL)), const((D_MODEL, D_MODEL)),
                  const((1, D_MODEL))],
        out_specs=[row(D_MODEL), row(D_MODEL)],
        compiler_params=_params(("parallel",), 56),
        name="outproj",
    )(oa, ob, proj, proj, x2, w_oa, w_ob, w_out, g_ffn)


_FAR = 1024.0
RT_SLICE = 256
RT_HEADS = 4
UP_CHUNK = 256
UP_ROWS = 256

_LANE_GATHER = lax.GatherDimensionNumbers(
    offset_dims=(), collapsed_slice_dims=(1,), start_index_map=(1,),
    operand_batching_dims=(0,), start_indices_batching_dims=(0,))


def _tree(op, xs):
    xs = list(xs)
    while len(xs) > 1:
        nxt = [op(xs[i], xs[i + 1]) for i in range(0, len(xs) - 1, 2)]
        if len(xs) % 2:
            nxt.append(xs[-1])
        xs = nxt
    return xs[0]


def _all_sublanes(op, x):
    for shift in (4, 2, 1):
        x = op(x, pltpu.roll(x, shift, axis=0))
    return x


def _pop_max(tiles, sub_iota, payload=None):
    m = _all_sublanes(jnp.maximum, _tree(jnp.maximum, tiles))
    first = _tree(jnp.minimum, [jnp.where(t == m, float(SUBLANES * v), _FAR) for v, t in enumerate(tiles)])
    row = _all_sublanes(jnp.minimum, first + sub_iota)
    base = row - sub_iota
    hits = [base == float(SUBLANES * v) for v in range(len(tiles))]
    if payload is None:
        val = row
    else:
        val = _all_sublanes(jnp.maximum, _tree(jnp.maximum, [jnp.where(h, p, -1.0)
                                                            for h, p in zip(hits, payload)]))
    return m, val, [jnp.where(h, -jnp.inf, t) for h, t in zip(hits, tiles)]


def _route_scores(h, q_ref, keys_ref, s_scr):
    for c in range(2):
        col = pl.multiple_of((2 * h + c) * LANES, LANES)
        s_scr[c] = lax.dot_general(keys_ref[2 * h + c], q_ref[:, pl.ds(col, LANES)],
                                   (((1,), (1,)), ((), ())), preferred_element_type=F32)


def _batcher_pairs(n):
    size = 16
    pairs = []
    p = 1
    while p < size:
        k = p
        while k >= 1:
            for j in range(k % p, size - k, 2 * k):
                for i in range(min(k, size - j - k)):
                    if (i + j) // (2 * p) == (i + j + k) // (2 * p) and i + j + k < n:
                        pairs.append((i + j, i + j + k))
            k //= 2
        p *= 2
    return pairs


_FAR_ID = 1e9


def _top_sorted(vals, ids, k):
    vals, ids = list(vals), list(ids)
    n = len(vals)
    for i, j in _batcher_pairs(n):
        up = vals[j] > vals[i]
        vals[i], vals[j] = jnp.where(up, vals[j], vals[i]), jnp.where(up, vals[i], vals[j])
        ids[i], ids[j] = jnp.where(up, ids[j], ids[i]), jnp.where(up, ids[i], ids[j])
    top_v, top_i = [], []
    for it in range(k):
        m = _all_sublanes(jnp.maximum, vals[0])
        first = _all_sublanes(jnp.minimum, jnp.where(vals[0] == m, ids[0], _FAR_ID))
        win = ids[0] == first
        top_v.append(m)
        top_i.append(first)
        for r in range(min(k - it, n)):
            if r + 1 < n:
                vals[r] = jnp.where(win, vals[r + 1], vals[r])
                ids[r] = jnp.where(win, ids[r + 1], ids[r])
            else:
                vals[r] = jnp.where(win, -jnp.inf, vals[r])
    runner_up = _all_sublanes(jnp.maximum, vals[0])
    tie = jnp.zeros_like(runner_up)
    for a, b in zip(top_v, top_v[1:] + [runner_up]):
        tie = jnp.where(a == b, 1.0, tie)
    return top_v, top_i, tie


def _route_topk(s_scr, ts_scr, ti_scr, bs_scr, et_ref, gt_ref, *, exact):
    k = PEER_TOPK
    nk = float(PEER_NKEYS)
    sub_iota = lax.broadcasted_iota(I32, (SUBLANES, LANES), 0).astype(F32)
    halves = [slice(i * LANES, (i + 1) * LANES) for i in range(RT_SLICE // LANES)]
    tie = None if exact else jnp.zeros((SUBLANES, LANES), F32)

    chains = []
    for c in range(2):
        for lanes in halves:
            tiles = [s_scr[c, SUBLANES * v:SUBLANES * (v + 1), lanes] for v in range(PEER_NKEYS // SUBLANES)]
            chains.append([c, lanes, tiles])
    if exact:
        for kk in range(k):
            for chain in chains:
                c, lanes, tiles = chain
                m, row, chain[2] = _pop_max(tiles, sub_iota)
                ts_scr[c, kk:kk + 1, lanes] = m[0:1]
                ti_scr[c, kk:kk + 1, lanes] = row[0:1]
    else:
        row_ids = [float(SUBLANES * v) + sub_iota for v in range(PEER_NKEYS // SUBLANES)]
        for c, lanes, tiles in chains:
            top_v, top_i, t = _top_sorted(tiles, row_ids, k)
            tie = jnp.maximum(tie, t)
            for kk in range(k):
                ts_scr[c, kk:kk + 1, lanes] = top_v[kk][0:1]
                ti_scr[c, kk:kk + 1, lanes] = top_i[kk][0:1]

    def candidates(lanes):
        s2 = [ts_scr[1, 0:8, lanes], ts_scr[1, 8:16, lanes]]
        i2 = [ti_scr[1, 0:8, lanes], ti_scr[1, 8:16, lanes]]
        s1 = lambda k1: ts_scr[0, k1:k1 + 1, lanes]
        i1 = lambda k1: ti_scr[0, k1:k1 + 1, lanes] * nk
        tiles = [s1(0) + s2[0], s1(0) + s2[1]]
        pay = [i1(0) + i2[0], i1(0) + i2[1]]
        for k1 in range(1, 8):
            allowed = k // (k1 + 1)
            t = s1(k1) + s2[0]
            tiles.append(t if allowed >= SUBLANES else jnp.where(sub_iota < float(allowed), t, -jnp.inf))
            pay.append(i1(k1) + i2[0])
        tiles.append(ts_scr[0, 8:16, lanes] + ts_scr[1, 0:1, lanes])
        pay.append(ti_scr[0, 8:16, lanes] * nk + ti_scr[1, 0:1, lanes])
        return tiles, pay

    cands = [candidates(lanes) for lanes in halves]
    if exact:
        tiles = [c[0] for c in cands]
        for kk in range(k):
            for i, lanes in enumerate(halves):
                m, expert, tiles[i] = _pop_max(tiles[i], sub_iota, payload=cands[i][1])
                bs_scr[kk:kk + 1, lanes] = m[0:1]
                et_ref[kk:kk + 1, lanes] = expert[0:1]
    else:
        for (tiles, pay), lanes in zip(cands, halves):
            top_v, top_i, t = _top_sorted(tiles, pay, k)
            tie = jnp.maximum(tie, t)
            for kk in range(k):
                bs_scr[kk:kk + 1, lanes] = top_v[kk][0:1]
                et_ref[kk:kk + 1, lanes] = top_i[kk][0:1]
    for lanes in halves:
        bs = bs_scr[:, lanes]
        ex = jnp.exp(bs - jnp.max(bs, axis=0, keepdims=True))
        gt_ref[:, lanes] = ex / jnp.sum(ex, axis=0, keepdims=True)
    return tie


def _expert_up(j, h_ref, u_ref, e_ref, act_ref, *, te):
    tb = h_ref.shape[0]
    u = [u_ref[c * UP_CHUNK:(c + 1) * UP_CHUNK, :].astype(BF16) for c in range(te // UP_CHUNK)]
    for m in range(tb // UP_ROWS):
        rows = slice(m * UP_ROWS, (m + 1) * UP_ROWS)
        e = e_ref[rows, :]
        row = e >> PEER_KEY_BITS
        col = (e & (PEER_NKEYS - 1))[..., None]
        acc = act_ref[rows, :]
        h = h_ref[rows, :]
        for c in range(te // UP_CHUNK):
            dense = lax.dot_general(h, u[c], (((1,), (1,)), ((), ())), preferred_element_type=F32)
            for q in range(UP_CHUNK // LANES):
                got = lax.gather(dense[:, q * LANES:(q + 1) * LANES], col, _LANE_GATHER, (1, 1),
                                 mode=lax.GatherScatterMode.PROMISE_IN_BOUNDS)
                acc = jnp.where(row == (j * te + c * UP_CHUNK) // LANES + q, got, acc)
        act_ref[rows, :] = acc


def _route_kernel(h_ref, wq_ref, keys_ref, v_ref, e_ref, g_ref, vb_ref,
                  q_scr, s_scr, ts_scr, ti_scr, bs_scr, et_scr, gt_scr):
    vb_ref[...] = v_ref[...].astype(BF16)
    q_scr[...] = jnp.dot(h_ref[...], wq_ref[...], preferred_element_type=F32).astype(BF16)

    def heads(g, carry):
        hs = [g * RT_HEADS + u for u in range(RT_HEADS)]
        for u, h in enumerate(hs):
            _route_scores(h, q_scr, keys_ref, s_scr.at[u])
        args = [(s_scr.at[u], ts_scr.at[u], ti_scr.at[u], bs_scr.at[u], et_scr.at[h], gt_scr.at[h])
                for u, h in enumerate(hs)]
        ties = [_route_topk(*a, exact=False) for a in args]
        for a, tie in zip(args, ties):
            @pl.when(jnp.max(tie) > 0.0)
            def _():
                _route_topk(*a, exact=True)

        return carry

    lax.fori_loop(0, PEER_HEADS // RT_HEADS, heads, 0)

    for half in range(RT_SLICE // LANES):
        lanes = slice(half * LANES, (half + 1) * LANES)
        et = jnp.concatenate([et_scr[h, :, lanes] for h in range(PEER_HEADS)], axis=0)
        gt = jnp.concatenate([gt_scr[h, :, lanes] for h in range(PEER_HEADS)], axis=0)
        e_ref[lanes, :] = et.T.astype(I32)
        g_ref[lanes, :] = gt.T


def _route(h2, wq, keys, v):
    k = PEER_TOPK
    steps = N_TOK // RT_SLICE
    tok = pl.BlockSpec((RT_SLICE, PEER_PICKS), lambda i: (i, 0))
    v_slab = pl.BlockSpec((PEER_EXPERTS // steps, D_MODEL), lambda i: (i, 0))
    return pl.pallas_call(
        _route_kernel,
        out_shape=[jax.ShapeDtypeStruct((N_TOK, PEER_PICKS), I32),
                   jax.ShapeDtypeStruct((N_TOK, PEER_PICKS), F32),
                   jax.ShapeDtypeStruct((PEER_EXPERTS, D_MODEL), BF16)],
        grid=(steps,),
        in_specs=[pl.BlockSpec((RT_SLICE, D_MODEL), lambda i: (i, 0)),
                  pl.BlockSpec((D_MODEL, PEER_HEADS * PEER_QDIM), lambda i: (0, 0),
                               pipeline_mode=pl.Buffered(1)),
                  pl.BlockSpec((PEER_HEADS * 2, PEER_NKEYS, PEER_QDIM // 2), lambda i: (0, 0, 0)),
                  v_slab],
        out_specs=[tok, tok, v_slab],
        scratch_shapes=[
            pltpu.VMEM((RT_SLICE, PEER_HEADS * PEER_QDIM), BF16),
            pltpu.VMEM((RT_HEADS, 2, PEER_NKEYS, RT_SLICE), F32),
            pltpu.VMEM((RT_HEADS, 2, k, RT_SLICE), F32),
            pltpu.VMEM((RT_HEADS, 2, k, RT_SLICE), F32),
            pltpu.VMEM((RT_HEADS, k, RT_SLICE), F32),
            pltpu.VMEM((PEER_HEADS, k, RT_SLICE), F32),
            pltpu.VMEM((PEER_HEADS, k, RT_SLICE), F32),
        ],
        compiler_params=_params(("parallel",), 40),
        name="peer_route",
    )(h2, wq, keys, v)


def _peer_up_kernel(h_ref, u_ref, e_ref, act_ref, *, te):
    j = pl.program_id(1)

    @pl.when(j == 0)
    def _():
        act_ref[...] = jnp.zeros_like(act_ref)

    _expert_up(j, h_ref, u_ref, e_ref, act_ref, te=te)


def _peer_up(h2, u, e, tb=2048, te=1024):
    return pl.pallas_call(
        functools.partial(_peer_up_kernel, te=te),
        out_shape=jax.ShapeDtypeStruct((N_TOK, PEER_PICKS), F32),
        grid=(N_TOK // tb, PEER_EXPERTS // te),
        in_specs=[pl.BlockSpec((tb, D_MODEL), lambda i, j: (i, 0)),
                  pl.BlockSpec((te, D_MODEL), lambda i, j: (j, 0)),
                  pl.BlockSpec((tb, PEER_PICKS), lambda i, j: (i, 0))],
        out_specs=pl.BlockSpec((tb, PEER_PICKS), lambda i, j: (i, 0)),
        compiler_params=_params(("parallel", "arbitrary"), 56),
        name="peer_up",
    )(h2, u, e)


def _peer_coef_kernel(e_ref, g_ref, act_ref, p_ref, row_scr, col_scr, w_scr, c_scr, *, tp):
    e = e_ref[...]
    row_scr[...] = (e >> PEER_KEY_BITS).astype(F32)
    col_scr[...] = (e & (PEER_NKEYS - 1)).astype(F32)
    act = act_ref[...]
    w_scr[...] = g_ref[...] * (0.5 * act * (1.0 + lax.erf(act * math.sqrt(0.5))))
    sub = 16
    iota = lax.broadcasted_iota(I32, (PEER_NKEYS // sub, sub, PEER_PICKS), 0) * sub \
        + lax.broadcasted_iota(I32, (PEER_NKEYS // sub, sub, PEER_PICKS), 1)
    iota = iota.astype(F32).astype(BF16)
    one = jnp.ones((), BF16)
    zero = jnp.zeros((), BF16)

    def bcast(ref, t):
        return jnp.broadcast_to(ref[pl.ds(t, 1), :], (sub, PEER_PICKS)).astype(BF16)[None]

    def products(t0):
        for t in range(sub):
            rows, cols, w = bcast(row_scr, t0 + t), bcast(col_scr, t0 + t), bcast(w_scr, t0 + t)
            left = jnp.where(iota == rows, one, zero).reshape(PEER_NKEYS, PEER_PICKS)
            right = jnp.where(iota == cols, w, zero).reshape(PEER_NKEYS, PEER_PICKS)
            c_scr[t0 + t] = lax.dot_general(left, right, (((1,), (1,)), ((), ())),
                                            preferred_element_type=F32)

    def regroup(t0):
        sw = jnp.swapaxes(c_scr[pl.ds(t0, sub)], 0, 1)
        for k1 in range(PEER_NKEYS):
            p_ref[pl.ds(t0, sub), k1 * PEER_NKEYS:(k1 + 1) * PEER_NKEYS] = sw[k1].astype(p_ref.dtype)

    def block(blk, carry):
        base = pl.multiple_of(blk * COEF_UNROLL, COEF_UNROLL)
        for t0 in range(0, COEF_UNROLL, sub):
            products(base + t0)
            if t0:
                regroup(base + t0 - sub)
        regroup(base + COEF_UNROLL - sub)
        return carry

    lax.fori_loop(0, tp // COEF_UNROLL, block, 0)


COEF_UNROLL = 64


def _peer_coef(e, g, act, tp=128):
    tok = pl.BlockSpec((tp, PEER_PICKS), lambda i: (i, 0))
    return pl.pallas_call(
        functools.partial(_peer_coef_kernel, tp=tp),
        out_shape=jax.ShapeDtypeStruct((N_TOK, PEER_EXPERTS), BF16),
        grid=(N_TOK // tp,),
        in_specs=[tok, tok, tok],
        out_specs=pl.BlockSpec((tp, PEER_EXPERTS), lambda i: (i, 0)),
        scratch_shapes=[pltpu.VMEM((tp, PEER_PICKS), F32),
                        pltpu.VMEM((tp, PEER_PICKS), F32),
                        pltpu.VMEM((tp, PEER_PICKS), F32),
                        pltpu.VMEM((tp, PEER_NKEYS, PEER_NKEYS), F32)],
        compiler_params=_params(("parallel",), 32),
        name="peer_coef",
    )(e, g, act)


def _peer_down_kernel(p_ref, v_ref, x_ref, g_ref, o_ref):
    j = pl.program_id(1)

    @pl.when(j == 0)
    def _():
        o_ref[...] = jnp.zeros_like(o_ref)

    o_ref[...] += jnp.dot(p_ref[...], v_ref[...], preferred_element_type=F32)
    cols = pl.ds(pl.multiple_of(j * LANES, LANES), LANES)
    o_ref[:, cols] += x_ref[...]

    @pl.when(j == pl.num_programs(1) - 1)
    def _():
        o_ref[...] = _rms(o_ref[...], g_ref[...])


def _peer_down(coef, v, x1, g_final, tm=2048):
    tk = PEER_EXPERTS // (D_MODEL // LANES)
    return pl.pallas_call(
        _peer_down_kernel,
        out_shape=jax.ShapeDtypeStruct((N_TOK, D_MODEL), F32),
        grid=(N_TOK // tm, PEER_EXPERTS // tk),
        in_specs=[pl.BlockSpec((tm, tk), lambda i, j: (i, j)),
                  pl.BlockSpec((tk, D_MODEL), lambda i, j: (j, 0)),
                  pl.BlockSpec((tm, LANES), lambda i, j: (i, j)),
                  pl.BlockSpec((1, D_MODEL), lambda i, j: (0, 0))],
        out_specs=pl.BlockSpec((tm, D_MODEL), lambda i, j: (i, 0)),
        compiler_params=_params(("parallel", "arbitrary"), 60),
        name="peer_down",
    )(coef, v, x1, g_final)


def kernel(x, rel_bias, norm_mix_g, w_in, sink_a, w_oa, w_ob, w_out, norm_ffn_g,
           peer_wq, peer_keys, peer_u, peer_v, norm_final_g):
    assert x.shape == (BATCH, SEQ, D_MODEL) and w_in.shape[0] == 1
    x2 = x.reshape(N_TOK, D_MODEL)
    proj = _inproj(x2, norm_mix_g, w_in[0].astype(BF16))

    oa = _windowed_attention(proj, rel_bias, sink_a[0])
    ob = _dilated_attention(proj, rel_bias)
    x1, h2 = _outproj(oa, ob, proj, x2, w_oa[0].astype(BF16), w_ob[0].astype(BF16),
                      w_out[0].astype(BF16), norm_ffn_g)

    keys = peer_keys[0].reshape(PEER_HEADS * 2, PEER_NKEYS, PEER_QDIM // 2).astype(BF16)
    e, gate, v_bf16 = _route(h2, peer_wq[0].astype(BF16), keys, peer_v[0])
    act = _peer_up(h2, peer_u[0], e)
    coef = _peer_coef(e, gate, act)
    out = _peer_down(coef, v_bf16, x1, norm_final_g.reshape(1, D_MODEL))
    return out.reshape(BATCH, SEQ, D_MODEL)
```

```python
import functools
import math

import numpy as np
import jax
import jax.numpy as jnp
from jax import lax
from jax.experimental import pallas as pl
from jax.experimental.pallas import tpu as pltpu

F32 = jnp.float32
BF16 = jnp.bfloat16
I32 = jnp.int32

D_MODEL = 2048
BATCH = 2
SEQ = 4096
N_TOK = BATCH * SEQ
HEAD_DIM = 128
LANES = 128
SUBLANES = 8

A_Q_HEADS = 8
A_KV_HEADS = 2
A_GROUP = A_Q_HEADS // A_KV_HEADS
A_HALF_WINDOW = 128
B_DILATIONS = (1, 4, 16)
B_GROUPS = 3
B_HEADS_PER_GROUP = 4
B_HALF_SPAN = 64
N_BUCKETS = 32
MAX_DISTANCE = 1024
N_ATTN_HEADS = A_Q_HEADS + B_GROUPS * B_HEADS_PER_GROUP

PEER_HEADS = 8
PEER_NKEYS = 128
PEER_KEY_BITS = PEER_NKEYS.bit_length() - 1
assert 1 << PEER_KEY_BITS == PEER_NKEYS
PEER_EXPERTS = PEER_NKEYS * PEER_NKEYS
PEER_QDIM = 256
PEER_TOPK = 16
PEER_PICKS = PEER_HEADS * PEER_TOPK
EPS = 1e-6

A_Q_W = A_Q_HEADS * HEAD_DIM
A_KV_W = A_KV_HEADS * HEAD_DIM
B_W = B_GROUPS * B_HEADS_PER_GROUP * HEAD_DIM
B_OUT_W = B_HEADS_PER_GROUP * HEAD_DIM
IN_WIDTH = A_Q_W + 2 * A_KV_W + 3 * B_W + 2 * D_MODEL
QA_CB = 0
KA_CB = A_Q_W // LANES
VA_CB = (A_Q_W + A_KV_W) // LANES
QB_CB = (A_Q_W + 2 * A_KV_W) // LANES
KB_CB = QB_CB + B_W // LANES
VB_CB = KB_CB + B_W // LANES
GA_OFF = A_Q_W + 2 * A_KV_W + 3 * B_W
GB_OFF = GA_OFF + D_MODEL
IN_CB = IN_WIDTH // LANES

NEG = -1e30
ATT_SCALE = HEAD_DIM ** -0.5
ATT_QT = 128
ATT_GROUP = 8

MIB = 1024 * 1024


def _params(sem, vmem_mib):
    return pltpu.CompilerParams(dimension_semantics=sem, vmem_limit_bytes=vmem_mib * MIB)


def _rms(x, g):
    return x * lax.rsqrt(jnp.mean(x * x, axis=-1, keepdims=True) + EPS) * g


def _inproj_kernel(x_ref, g_ref, w_ref, o_ref, h_scr):
    @pl.when(pl.program_id(1) == 0)
    def _():
        h_scr[...] = _rms(x_ref[...], g_ref[...]).astype(BF16)

    res = jnp.dot(h_scr[...], w_ref[...], preferred_element_type=F32)
    for c in range(o_ref.shape[0]):
        o_ref[c] = res[:, c * LANES:(c + 1) * LANES].astype(o_ref.dtype)


def _inproj(x2, g, w, tm=1024, tn=2048):
    n = w.shape[1]
    return pl.pallas_call(
        _inproj_kernel,
        out_shape=jax.ShapeDtypeStruct((n // LANES, N_TOK, LANES), BF16),
        grid=(N_TOK // tm, n // tn),
        in_specs=[
            pl.BlockSpec((tm, D_MODEL), lambda i, j: (i, 0)),
            pl.BlockSpec((1, D_MODEL), lambda i, j: (0, 0)),
            pl.BlockSpec((D_MODEL, tn), lambda i, j: (0, j)),
        ],
        out_specs=pl.BlockSpec((tn // LANES, tm, LANES), lambda i, j: (j, i, 0)),
        scratch_shapes=[pltpu.VMEM((tm, D_MODEL), BF16)],
        compiler_params=_params(("parallel", "arbitrary"), 52),
        name="inproj",
    )(x2, g, w)


def _t5_bucket_np(rel):
    half = N_BUCKETS // 2
    max_exact = half // 2
    ret = np.where(rel > 0, half, 0)
    n = np.abs(rel)
    nf = np.maximum(n, 1).astype(np.float64)
    large = max_exact + (np.log(nf / max_exact) / math.log(MAX_DISTANCE / max_exact) * (half - max_exact)).astype(np.int64)
    large = np.minimum(large, half - 1)
    return (ret + np.where(n < max_exact, n, large)).astype(np.int32)


def _bucket_matrix(hw, dist_scale):
    w = ATT_QT + 2 * hw
    delta = (np.arange(w)[None, :] - hw) - np.arange(ATT_QT)[:, None]
    bkt = _t5_bucket_np(delta * dist_scale)
    return np.where(np.abs(delta) <= hw, bkt, -1).astype(np.int32)


def _residues_per_trip(dil):
    tiles = SEQ // dil // ATT_QT
    return max(1, min(dil, ATT_GROUP // tiles))


def _bias_tile(bkt_ref, tab_ref, head, bias_scr):
    bkt = bkt_ref[...]
    bias = jnp.full(bkt.shape, NEG, F32)
    for b in range(N_BUCKETS):
        bias = jnp.where(bkt == b, tab_ref[b, head], bias)
    bias_scr[...] = bias


def _banded_attention(q_ref, k_ref, v_ref, bias_scr, kpad, vpad, stage, emit, *, dil, hw, sink):
    seq = SEQ // dil
    qt = ATT_QT
    win = qt + 2 * hw
    n_res = _residues_per_trip(dil)

    zeros = jnp.zeros((hw, HEAD_DIM), BF16)
    for pad_ref in (kpad, vpad):
        for u in range(n_res):
            pad_ref[u, 0:hw, :] = zeros
            pad_ref[u, hw + seq:hw + seq + hw, :] = zeros
    if dil == 1:
        kpad[0, hw:hw + seq, :] = k_ref[...]
        vpad[0, hw:hw + seq, :] = v_ref[...]
    else:
        q32, k32, v32, qres = stage
        q32[...] = q_ref[...].astype(F32)
        k32[...] = k_ref[...].astype(F32)
        v32[...] = v_ref[...].astype(F32)

    def scores(q0, q, kw):
        s = lax.dot_general(q, kw, (((1,), (1,)), ((), ())), preferred_element_type=F32)
        s = s * ATT_SCALE + bias_scr[...]
        kpos = q0 - hw + lax.broadcasted_iota(I32, (qt, win), 1)
        return jnp.where((kpos >= 0) & (kpos < seq), s, NEG)

    def softmax(s):
        m = jnp.max(s, axis=-1, keepdims=True)
        if sink is not None:
            m = jnp.maximum(m, sink)
        p = jnp.exp(s - m)
        den = jnp.sum(p, axis=-1, keepdims=True)
        if sink is not None:
            den = den + jnp.exp(sink - m)
        return p.astype(BF16), m, den

    group = min(ATT_GROUP, seq // qt)

    def residues(g, carry):
        r0 = g * n_res
        if dil > 1:
            for u in range(n_res):
                kpad[u, hw:hw + seq, :] = k32[pl.ds(r0 + u, seq, stride=dil), :].astype(BF16)
                vpad[u, hw:hw + seq, :] = v32[pl.ds(r0 + u, seq, stride=dil), :].astype(BF16)
                qres[u] = q32[pl.ds(r0 + u, seq, stride=dil), :].astype(BF16)

        def body(t, c):
            work = [(u, pl.multiple_of((t * group + i) * qt, qt)) for u in range(n_res) for i in range(group)]
            q_src = (lambda u: qres.at[u]) if dil > 1 else (lambda u: q_ref)
            s_all = [scores(q0, q_src(u)[pl.ds(q0, qt), :], kpad[u, pl.ds(q0, win), :]) for u, q0 in work]
            p_all = [softmax(s) for s in s_all]
            o_all = [jnp.dot(p, vpad[u, pl.ds(q0, win), :], preferred_element_type=F32) / den
                     for (u, q0), (p, m, den) in zip(work, p_all)]
            for (u, q0), (p, m, den), o in zip(work, p_all, o_all):
                emit(q0 if dil == 1 else r0 + u + q0 * dil, dil, o, m, den)
            return c

        lax.fori_loop(0, seq // (qt * group), body, 0)
        return carry

    if dil > 1:
        lax.fori_loop(0, dil // n_res, residues, 0)
    else:
        residues(0, 0)


def _attn_scratch(dil, hw):
    n_res = _residues_per_trip(dil)
    seq = SEQ // dil
    return [pltpu.VMEM((n_res, seq + 2 * hw, HEAD_DIM), BF16),
            pltpu.VMEM((n_res, seq + 2 * hw, HEAD_DIM), BF16),
            pltpu.VMEM((ATT_QT, ATT_QT + 2 * hw), F32)]


def _windowed_kernel(tab_ref, sink_ref, q_ref, k_ref, v_ref, bkt_ref, o_ref, kpad, vpad, bias_scr):
    head = pl.program_id(1)
    _bias_tile(bkt_ref, tab_ref, head, bias_scr)

    def emit(start, stride, o, m, den):
        o_ref[0, 0, pl.ds(start, ATT_QT), :] = o.astype(o_ref.dtype)

    _banded_attention(q_ref.at[0, 0], k_ref.at[0, 0], v_ref.at[0, 0], bias_scr, kpad, vpad, None, emit,
                      dil=1, hw=A_HALF_WINDOW, sink=sink_ref[head])


def _windowed_attention(proj, rel_bias, sink):
    pv = proj.reshape(IN_CB, BATCH, SEQ, HEAD_DIM)
    hw = A_HALF_WINDOW
    blk = (1, 1, SEQ, HEAD_DIM)
    out = pl.pallas_call(
        _windowed_kernel,
        out_shape=jax.ShapeDtypeStruct((A_Q_HEADS, BATCH, SEQ, HEAD_DIM), BF16),
        grid=(BATCH, A_Q_HEADS),
        in_specs=[
            pl.BlockSpec(memory_space=pltpu.SMEM),
            pl.BlockSpec(memory_space=pltpu.SMEM),
            pl.BlockSpec(blk, lambda b, h: (QA_CB + h, b, 0, 0)),
            pl.BlockSpec(blk, lambda b, h: (KA_CB + h // A_GROUP, b, 0, 0)),
            pl.BlockSpec(blk, lambda b, h: (VA_CB + h // A_GROUP, b, 0, 0)),
            pl.BlockSpec((ATT_QT, ATT_QT + 2 * hw), lambda b, h: (0, 0)),
        ],
        out_specs=pl.BlockSpec(blk, lambda b, h: (h, b, 0, 0)),
        scratch_shapes=_attn_scratch(1, hw),
        compiler_params=_params(("parallel", "arbitrary"), 32),
        name="attn_windowed",
    )(rel_bias, sink, pv, pv, pv, jnp.asarray(_bucket_matrix(hw, 1)))
    return out.reshape(A_Q_HEADS, N_TOK, HEAD_DIM)


MERGE_ROWS = 512


def _dilated_kernel(tab_ref, *refs):
    n_g = B_GROUPS
    qkv = [refs[3 * g:3 * g + 3] for g in range(n_g)]
    bkts = refs[3 * n_g:4 * n_g]
    o_ref = refs[4 * n_g]
    scr = list(refs[4 * n_g + 1:])
    pads = [scr[3 * g:3 * g + 3] for g in range(n_g)]
    q32, k32, v32, o32, l32 = scr[3 * n_g:3 * n_g + 5]
    qres = scr[3 * n_g + 5:]
    slot = pl.program_id(1)

    for g, dil in enumerate(B_DILATIONS):
        kpad, vpad, bias_scr = pads[g]
        _bias_tile(bkts[g], tab_ref, A_Q_HEADS + g * B_HEADS_PER_GROUP + slot, bias_scr)

        def emit(start, stride, o, m, den, g=g):
            rows = pl.ds(start, ATT_QT) if stride == 1 else pl.ds(start, ATT_QT, stride=stride)
            o32[g, rows, :] = o
            l32[g, rows, :] = jnp.broadcast_to(m + jnp.log(den), (ATT_QT, HEAD_DIM))

        q_ref, k_ref, v_ref = qkv[g]
        stage = None if dil == 1 else (q32, k32, v32, qres[g])
        _banded_attention(q_ref.at[0, 0], k_ref.at[0, 0], v_ref.at[0, 0], bias_scr, kpad, vpad, stage, emit,
                          dil=dil, hw=B_HALF_SPAN, sink=None)

    def merge(c, carry):
        rows = pl.ds(pl.multiple_of(c * MERGE_ROWS, MERGE_ROWS), MERGE_ROWS)
        lses = [l32[g, rows, :] for g in range(n_g)]
        mx = functools.reduce(jnp.maximum, lses)
        es = [jnp.exp(l - mx) for l in lses]
        den = functools.reduce(jnp.add, es)
        o_ref[0, 0, rows, :] = functools.reduce(
            jnp.add, [(e / den) * o32[g, rows, :] for g, e in enumerate(es)]).astype(o_ref.dtype)
        return carry

    lax.fori_loop(0, SEQ // MERGE_ROWS, merge, 0)


def _dilated_attention(proj, rel_bias):
    pv = proj.reshape(IN_CB, BATCH, SEQ, HEAD_DIM)
    hw = B_HALF_SPAN
    blk = (1, 1, SEQ, HEAD_DIM)
    in_specs = [pl.BlockSpec(memory_space=pltpu.SMEM)]
    args = [rel_bias]
    for g in range(B_GROUPS):
        for cb in (QB_CB, KB_CB, VB_CB):
            in_specs.append(pl.BlockSpec(blk, lambda b, h, c=cb + g * B_HEADS_PER_GROUP: (c + h, b, 0, 0)))
            args.append(pv)
    for dil in B_DILATIONS:
        in_specs.append(pl.BlockSpec((ATT_QT, ATT_QT + 2 * hw), lambda b, h: (0, 0)))
        args.append(jnp.asarray(_bucket_matrix(hw, dil)))
    scratch = []
    for dil in B_DILATIONS:
        scratch += _attn_scratch(dil, hw)
    scratch += [pltpu.VMEM((SEQ, HEAD_DIM), F32)] * 3
    scratch += [pltpu.VMEM((B_GROUPS, SEQ, HEAD_DIM), F32)] * 2
    scratch += [pltpu.VMEM((_residues_per_trip(dil), SEQ // dil, HEAD_DIM), BF16) for dil in B_DILATIONS]
    out = pl.pallas_call(
        _dilated_kernel,
        out_shape=jax.ShapeDtypeStruct((B_HEADS_PER_GROUP, BATCH, SEQ, HEAD_DIM), BF16),
        grid=(BATCH, B_HEADS_PER_GROUP),
        in_specs=in_specs,
        out_specs=pl.BlockSpec(blk, lambda b, h: (h, b, 0, 0)),
        scratch_shapes=scratch,
        compiler_params=_params(("parallel", "arbitrary"), 52),
        name="attn_dilated",
    )(*args)
    return out.reshape(B_HEADS_PER_GROUP, N_TOK, HEAD_DIM)


def _outproj_kernel(oa_ref, ob_ref, ga_ref, gb_ref, x_ref, woa_ref, wob_ref, wout_ref, gn_ref,
                    x1_ref, h2_ref):
    wide = lambda ref: jnp.concatenate([ref[c] for c in range(ref.shape[0])], axis=-1)
    ya = jnp.dot(wide(oa_ref), woa_ref[...], preferred_element_type=F32)
    yb = jnp.dot(wide(ob_ref), wob_ref[...], preferred_element_type=F32)
    merged = (jax.nn.sigmoid(wide(ga_ref).astype(F32)) * ya
              + jax.nn.sigmoid(wide(gb_ref).astype(F32)) * yb)
    x1 = x_ref[...] + jnp.dot(merged.astype(BF16), wout_ref[...], preferred_element_type=F32)
    x1_ref[...] = x1
    h2_ref[...] = _rms(x1, gn_ref[...]).astype(BF16)


def _outproj(oa, ob, proj, x2, w_oa, w_ob, w_out, g_ffn, tm=512):
    row = lambda w: pl.BlockSpec((tm, w), lambda i: (i, 0))
    slabs = lambda n, first: pl.BlockSpec((n, tm, LANES), lambda i: (first // n, i, 0))
    gate_cb = D_MODEL // LANES
    const = lambda shape: pl.BlockSpec(shape, lambda i: (0, 0), pipeline_mode=pl.Buffered(1))
    return pl.pallas_call(
        _outproj_kernel,
        out_shape=[jax.ShapeDtypeStruct((N_TOK, D_MODEL), F32),
                   jax.ShapeDtypeStruct((N_TOK, D_MODEL), BF16)],
        grid=(N_TOK // tm,),
        in_specs=[slabs(A_Q_HEADS, 0), slabs(B_HEADS_PER_GROUP, 0),
                  slabs(gate_cb, GA_OFF // LANES), slabs(gate_cb, GB_OFF // LANES),
                  row(D_MODEL),
                  const((A_Q_W, D_MODEL)), const((B_OUT_W, D_MODEL)), const((D_MODEL, D_MODEL)),
                  const((1, D_MODEL))],
        out_specs=[row(D_MODEL), row(D_MODEL)],
        compiler_params=_params(("parallel",), 56),
        name="outproj",
    )(oa, ob, proj, proj, x2, w_oa, w_ob, w_out, g_ffn)


_FAR = 1024.0
RT_SLICE = 256
RT_HEADS = 4
UP_CHUNK = 256
UP_ROWS = 256

_LANE_GATHER = lax.GatherDimensionNumbers(
    offset_dims=(), collapsed_slice_dims=(1,), start_index_map=(1,),
    operand_batching_dims=(0,), start_indices_batching_dims=(0,))


def _tree(op, xs):
    xs = list(xs)
    while len(xs) > 1:
        nxt = [op(xs[i], xs[i + 1]) for i in range(0, len(xs) - 1, 2)]
        if len(xs) % 2:
            nxt.append(xs[-1])
        xs = nxt
    return xs[0]


def _all_sublanes(op, x):
    for shift in (4, 2, 1):
        x = op(x, pltpu.roll(x, shift, axis=0))
    return x


def _pop_max(tiles, sub_iota, payload=None):
    m = _all_sublanes(jnp.maximum, _tree(jnp.maximum, tiles))
    first = _tree(jnp.minimum, [jnp.where(t == m, float(SUBLANES * v), _FAR) for v, t in enumerate(tiles)])
    row = _all_sublanes(jnp.minimum, first + sub_iota)
    base = row - sub_iota
    hits = [base == float(SUBLANES * v) for v in range(len(tiles))]
    if payload is None:
        val = row
    else:
        val = _all_sublanes(jnp.maximum, _tree(jnp.maximum, [jnp.where(h, p, -1.0)
                                                            for h, p in zip(hits, payload)]))
    return m, val, [jnp.where(h, -jnp.inf, t) for h, t in zip(hits, tiles)]


def _route_scores(h, q_ref, keys_ref, s_scr):
    for c in range(2):
        col = pl.multiple_of((2 * h + c) * LANES, LANES)
        s_scr[c] = lax.dot_general(keys_ref[2 * h + c], q_ref[:, pl.ds(col, LANES)],
                                   (((1,), (1,)), ((), ())), preferred_element_type=F32)


def _batcher_pairs(n):
    size = 16
    pairs = []
    p = 1
    while p < size:
        k = p
        while k >= 1:
            for j in range(k % p, size - k, 2 * k):
                for i in range(min(k, size - j - k)):
                    if (i + j) // (2 * p) == (i + j + k) // (2 * p) and i + j + k < n:
                        pairs.append((i + j, i + j + k))
            k //= 2
        p *= 2
    return pairs


_FAR_ID = 1e9


def _top_sorted(vals, ids, k):
    vals, ids = list(vals), list(ids)
    n = len(vals)
    for i, j in _batcher_pairs(n):
        up = vals[j] > vals[i]
        vals[i], vals[j] = jnp.where(up, vals[j], vals[i]), jnp.where(up, vals[i], vals[j])
        ids[i], ids[j] = jnp.where(up, ids[j], ids[i]), jnp.where(up, ids[i], ids[j])
    top_v, top_i = [], []
    for it in range(k):
        m = _all_sublanes(jnp.maximum, vals[0])
        first = _all_sublanes(jnp.minimum, jnp.where(vals[0] == m, ids[0], _FAR_ID))
        win = ids[0] == first
        top_v.append(m)
        top_i.append(first)
        for r in range(min(k - it, n)):
            if r + 1 < n:
                vals[r] = jnp.where(win, vals[r + 1], vals[r])
                ids[r] = jnp.where(win, ids[r + 1], ids[r])
            else:
                vals[r] = jnp.where(win, -jnp.inf, vals[r])
    runner_up = _all_sublanes(jnp.maximum, vals[0])
    tie = jnp.zeros_like(runner_up)
    for a, b in zip(top_v, top_v[1:] + [runner_up]):
        tie = jnp.where(a == b, 1.0, tie)
    return top_v, top_i, tie


def _route_topk(s_scr, ts_scr, ti_scr, bs_scr, et_ref, gt_ref, *, exact):
    k = PEER_TOPK
    nk = float(PEER_NKEYS)
    sub_iota = lax.broadcasted_iota(I32, (SUBLANES, LANES), 0).astype(F32)
    halves = [slice(i * LANES, (i + 1) * LANES) for i in range(RT_SLICE // LANES)]
    tie = None if exact else jnp.zeros((SUBLANES, LANES), F32)

    chains = []
    for c in range(2):
        for lanes in halves:
            tiles = [s_scr[c, SUBLANES * v:SUBLANES * (v + 1), lanes] for v in range(PEER_NKEYS // SUBLANES)]
            chains.append([c, lanes, tiles])
    if exact:
        for kk in range(k):
            for chain in chains:
                c, lanes, tiles = chain
                m, row, chain[2] = _pop_max(tiles, sub_iota)
                ts_scr[c, kk:kk + 1, lanes] = m[0:1]
                ti_scr[c, kk:kk + 1, lanes] = row[0:1]
    else:
        row_ids = [float(SUBLANES * v) + sub_iota for v in range(PEER_NKEYS // SUBLANES)]
        for c, lanes, tiles in chains:
            top_v, top_i, t = _top_sorted(tiles, row_ids, k)
            tie = jnp.maximum(tie, t)
            for kk in range(k):
                ts_scr[c, kk:kk + 1, lanes] = top_v[kk][0:1]
                ti_scr[c, kk:kk + 1, lanes] = top_i[kk][0:1]

    def candidates(lanes):
        s2 = [ts_scr[1, 0:8, lanes], ts_scr[1, 8:16, lanes]]
        i2 = [ti_scr[1, 0:8, lanes], ti_scr[1, 8:16, lanes]]
        s1 = lambda k1: ts_scr[0, k1:k1 + 1, lanes]
        i1 = lambda k1: ti_scr[0, k1:k1 + 1, lanes] * nk
        tiles = [s1(0) + s2[0], s1(0) + s2[1]]
        pay = [i1(0) + i2[0], i1(0) + i2[1]]
        for k1 in range(1, 8):
            allowed = k // (k1 + 1)
            t = s1(k1) + s2[0]
            tiles.append(t if allowed >= SUBLANES else jnp.where(sub_iota < float(allowed), t, -jnp.inf))
            pay.append(i1(k1) + i2[0])
        tiles.append(ts_scr[0, 8:16, lanes] + ts_scr[1, 0:1, lanes])
        pay.append(ti_scr[0, 8:16, lanes] * nk + ti_scr[1, 0:1, lanes])
        return tiles, pay

    cands = [candidates(lanes) for lanes in halves]
    if exact:
        tiles = [c[0] for c in cands]
        for kk in range(k):
            for i, lanes in enumerate(halves):
                m, expert, tiles[i] = _pop_max(tiles[i], sub_iota, payload=cands[i][1])
                bs_scr[kk:kk + 1, lanes] = m[0:1]
                et_ref[kk:kk + 1, lanes] = expert[0:1]
    else:
        for (tiles, pay), lanes in zip(cands, halves):
            top_v, top_i, t = _top_sorted(tiles, pay, k)
            tie = jnp.maximum(tie, t)
            for kk in range(k):
                bs_scr[kk:kk + 1, lanes] = top_v[kk][0:1]
                et_ref[kk:kk + 1, lanes] = top_i[kk][0:1]
    for lanes in halves:
        bs = bs_scr[:, lanes]
        ex = jnp.exp(bs - jnp.max(bs, axis=0, keepdims=True))
        gt_ref[:, lanes] = ex / jnp.sum(ex, axis=0, keepdims=True)
    return tie


def _expert_up(j, h_ref, u_ref, e_ref, act_ref, *, te):
    tb = h_ref.shape[0]
    u = [u_ref[c * UP_CHUNK:(c + 1) * UP_CHUNK, :].astype(BF16) for c in range(te // UP_CHUNK)]
    for m in range(tb // UP_ROWS):
        rows = slice(m * UP_ROWS, (m + 1) * UP_ROWS)
        e = e_ref[rows, :]
        row = e >> PEER_KEY_BITS
        col = (e & (PEER_NKEYS - 1))[..., None]
        acc = act_ref[rows, :]
        h = h_ref[rows, :]
        for c in range(te // UP_CHUNK):
            dense = lax.dot_general(h, u[c], (((1,), (1,)), ((), ())), preferred_element_type=F32)
            for q in range(UP_CHUNK // LANES):
                got = lax.gather(dense[:, q * LANES:(q + 1) * LANES], col, _LANE_GATHER, (1, 1),
                                 mode=lax.GatherScatterMode.PROMISE_IN_BOUNDS)
                acc = jnp.where(row == (j * te + c * UP_CHUNK) // LANES + q, got, acc)
        act_ref[rows, :] = acc


def _route_kernel(h_ref, wq_ref, keys_ref, v_ref, e_ref, g_ref, vb_ref,
                  q_scr, s_scr, ts_scr, ti_scr, bs_scr, et_scr, gt_scr):
    vb_ref[...] = v_ref[...].astype(BF16)
    q_scr[...] = jnp.dot(h_ref[...], wq_ref[...], preferred_element_type=F32).astype(BF16)

    def heads(g, carry):
        hs = [g * RT_HEADS + u for u in range(RT_HEADS)]
        for u, h in enumerate(hs):
            _route_scores(h, q_scr, keys_ref, s_scr.at[u])
        args = [(s_scr.at[u], ts_scr.at[u], ti_scr.at[u], bs_scr.at[u], et_scr.at[h], gt_scr.at[h])
                for u, h in enumerate(hs)]
        ties = [_route_topk(*a, exact=False) for a in args]
        for a, tie in zip(args, ties):
            @pl.when(jnp.max(tie) > 0.0)
            def _():
                _route_topk(*a, exact=True)

        return carry

    lax.fori_loop(0, PEER_HEADS // RT_HEADS, heads, 0)

    for half in range(RT_SLICE // LANES):
        lanes = slice(half * LANES, (half + 1) * LANES)
        et = jnp.concatenate([et_scr[h, :, lanes] for h in range(PEER_HEADS)], axis=0)
        gt = jnp.concatenate([gt_scr[h, :, lanes] for h in range(PEER_HEADS)], axis=0)
        e_ref[lanes, :] = et.T.astype(I32)
        g_ref[lanes, :] = gt.T


def _route(h2, wq, keys, v):
    k = PEER_TOPK
    steps = N_TOK // RT_SLICE
    tok = pl.BlockSpec((RT_SLICE, PEER_PICKS), lambda i: (i, 0))
    v_slab = pl.BlockSpec((PEER_EXPERTS // steps, D_MODEL), lambda i: (i, 0))
    return pl.pallas_call(
        _route_kernel,
        out_shape=[jax.ShapeDtypeStruct((N_TOK, PEER_PICKS), I32),
                   jax.ShapeDtypeStruct((N_TOK, PEER_PICKS), F32),
                   jax.ShapeDtypeStruct((PEER_EXPERTS, D_MODEL), BF16)],
        grid=(steps,),
        in_specs=[pl.BlockSpec((RT_SLICE, D_MODEL), lambda i: (i, 0)),
                  pl.BlockSpec((D_MODEL, PEER_HEADS * PEER_QDIM), lambda i: (0, 0),
                               pipeline_mode=pl.Buffered(1)),
                  pl.BlockSpec((PEER_HEADS * 2, PEER_NKEYS, PEER_QDIM // 2), lambda i: (0, 0, 0)),
                  v_slab],
        out_specs=[tok, tok, v_slab],
        scratch_shapes=[
            pltpu.VMEM((RT_SLICE, PEER_HEADS * PEER_QDIM), BF16),
            pltpu.VMEM((RT_HEADS, 2, PEER_NKEYS, RT_SLICE), F32),
            pltpu.VMEM((RT_HEADS, 2, k, RT_SLICE), F32),
            pltpu.VMEM((RT_HEADS, 2, k, RT_SLICE), F32),
            pltpu.VMEM((RT_HEADS, k, RT_SLICE), F32),
            pltpu.VMEM((PEER_HEADS, k, RT_SLICE), F32),
            pltpu.VMEM((PEER_HEADS, k, RT_SLICE), F32),
        ],
        compiler_params=_params(("parallel",), 40),
        name="peer_route",
    )(h2, wq, keys, v)


def _peer_up_kernel(h_ref, u_ref, e_ref, act_ref, *, te):
    j = pl.program_id(1)

    @pl.when(j == 0)
    def _():
        act_ref[...] = jnp.zeros_like(act_ref)

    _expert_up(j, h_ref, u_ref, e_ref, act_ref, te=te)


def _peer_up(h2, u, e, tb=2048, te=1024):
    return pl.pallas_call(
        functools.partial(_peer_up_kernel, te=te),
        out_shape=jax.ShapeDtypeStruct((N_TOK, PEER_PICKS), F32),
        grid=(N_TOK // tb, PEER_EXPERTS // te),
        in_specs=[pl.BlockSpec((tb, D_MODEL), lambda i, j: (i, 0)),
                  pl.BlockSpec((te, D_MODEL), lambda i, j: (j, 0)),
                  pl.BlockSpec((tb, PEER_PICKS), lambda i, j: (i, 0))],
        out_specs=pl.BlockSpec((tb, PEER_PICKS), lambda i, j: (i, 0)),
        compiler_params=_params(("parallel", "arbitrary"), 56),
        name="peer_up",
    )(h2, u, e)


def _peer_coef_kernel(e_ref, g_ref, act_ref, p_ref, row_scr, col_scr, w_scr, c_scr, *, tp):
    e = e_ref[...]
    row_scr[...] = (e >> PEER_KEY_BITS).astype(F32)
    col_scr[...] = (e & (PEER_NKEYS - 1)).astype(F32)
    act = act_ref[...]
    w_scr[...] = g_ref[...] * (0.5 * act * (1.0 + lax.erf(act * math.sqrt(0.5))))
    sub = 16
    iota = lax.broadcasted_iota(I32, (PEER_NKEYS // sub, sub, PEER_PICKS), 0) * sub \
        + lax.broadcasted_iota(I32, (PEER_NKEYS // sub, sub, PEER_PICKS), 1)
    iota = iota.astype(F32).astype(BF16)
    one = jnp.ones((), BF16)
    zero = jnp.zeros((), BF16)

    def bcast(ref, t):
        return jnp.broadcast_to(ref[pl.ds(t, 1), :], (sub, PEER_PICKS)).astype(BF16)[None]

    def products(t0):
        for t in range(sub):
            rows, cols, w = bcast(row_scr, t0 + t), bcast(col_scr, t0 + t), bcast(w_scr, t0 + t)
            left = jnp.where(iota == rows, one, zero).reshape(PEER_NKEYS, PEER_PICKS)
            right = jnp.where(iota == cols, w, zero).reshape(PEER_NKEYS, PEER_PICKS)
            c_scr[t0 + t] = lax.dot_general(left, right, (((1,), (1,)), ((), ())),
                                            preferred_element_type=F32)

    def regroup(t0):
        sw = jnp.swapaxes(c_scr[pl.ds(t0, sub)], 0, 1)
        for k1 in range(PEER_NKEYS):
            p_ref[pl.ds(t0, sub), k1 * PEER_NKEYS:(k1 + 1) * PEER_NKEYS] = sw[k1].astype(p_ref.dtype)

    def block(blk, carry):
        base = pl.multiple_of(blk * COEF_UNROLL, COEF_UNROLL)
        for t0 in range(0, COEF_UNROLL, sub):
            products(base + t0)
            if t0:
                regroup(base + t0 - sub)
        regroup(base + COEF_UNROLL - sub)
        return carry

    lax.fori_loop(0, tp // COEF_UNROLL, block, 0)


COEF_UNROLL = 64


def _peer_coef(e, g, act, tp=256):
    tok = pl.BlockSpec((tp, PEER_PICKS), lambda i: (i, 0))
    return pl.pallas_call(
        functools.partial(_peer_coef_kernel, tp=tp),
        out_shape=jax.ShapeDtypeStruct((N_TOK, PEER_EXPERTS), BF16),
        grid=(N_TOK // tp,),
        in_specs=[tok, tok, tok],
        out_specs=pl.BlockSpec((tp, PEER_EXPERTS), lambda i: (i, 0)),
        scratch_shapes=[pltpu.VMEM((tp, PEER_PICKS), F32),
                        pltpu.VMEM((tp, PEER_PICKS), F32),
                        pltpu.VMEM((tp, PEER_PICKS), F32),
                        pltpu.VMEM((tp, PEER_NKEYS, PEER_NKEYS), F32)],
        compiler_params=_params(("parallel",), 44),
        name="peer_coef",
    )(e, g, act)


def _peer_down_kernel(p_ref, v_ref, x_ref, g_ref, o_ref):
    j = pl.program_id(1)

    @pl.when(j == 0)
    def _():
        o_ref[...] = jnp.zeros_like(o_ref)

    o_ref[...] += jnp.dot(p_ref[...], v_ref[...], preferred_element_type=F32)
    cols = pl.ds(pl.multiple_of(j * LANES, LANES), LANES)
    o_ref[:, cols] += x_ref[...]

    @pl.when(j == pl.num_programs(1) - 1)
    def _():
        o_ref[...] = _rms(o_ref[...], g_ref[...])


def _peer_down(coef, v, x1, g_final, tm=2048):
    tk = PEER_EXPERTS // (D_MODEL // LANES)
    return pl.pallas_call(
        _peer_down_kernel,
        out_shape=jax.ShapeDtypeStruct((N_TOK, D_MODEL), F32),
        grid=(N_TOK // tm, PEER_EXPERTS // tk),
        in_specs=[pl.BlockSpec((tm, tk), lambda i, j: (i, j)),
                  pl.BlockSpec((tk, D_MODEL), lambda i, j: (j, 0)),
                  pl.BlockSpec((tm, LANES), lambda i, j: (i, j)),
                  pl.BlockSpec((1, D_MODEL), lambda i, j: (0, 0))],
        out_specs=pl.BlockSpec((tm, D_MODEL), lambda i, j: (i, 0)),
        compiler_params=_params(("parallel", "arbitrary"), 60),
        name="peer_down",
    )(coef, v, x1, g_final)


def kernel(x, rel_bias, norm_mix_g, w_in, sink_a, w_oa, w_ob, w_out, norm_ffn_g,
           peer_wq, peer_keys, peer_u, peer_v, norm_final_g):
    assert x.shape == (BATCH, SEQ, D_MODEL) and w_in.shape[0] == 1
    x2 = x.reshape(N_TOK, D_MODEL)
    proj = _inproj(x2, norm_mix_g, w_in[0].astype(BF16))

    oa = _windowed_attention(proj, rel_bias, sink_a[0])
    ob = _dilated_attention(proj, rel_bias)
    x1, h2 = _outproj(oa, ob, proj, x2, w_oa[0].astype(BF16), w_ob[0].astype(BF16),
                      w_out[0].astype(BF16), norm_ffn_g)

    keys = peer_keys[0].reshape(PEER_HEADS * 2, PEER_NKEYS, PEER_QDIM // 2).astype(BF16)
    e, gate, v_bf16 = _route(h2, peer_wq[0].astype(BF16), keys, peer_v[0])
    act = _peer_up(h2, peer_u[0], e)
    coef = _peer_coef(e, gate, act)
    out = _peer_down(coef, v_bf16, x1, norm_final_g.reshape(1, D_MODEL))
    return out.reshape(BATCH, SEQ, D_MODEL)
```

```python
import functools
import math

import numpy as np
import jax
import jax.numpy as jnp
from jax import lax
from jax.experimental import pallas as pl
from jax.experimental.pallas import tpu as pltpu

F32 = jnp.float32
BF16 = jnp.bfloat16
I32 = jnp.int32

D_MODEL = 2048
BATCH = 2
SEQ = 4096
N_TOK = BATCH * SEQ
HEAD_DIM = 128
LANES = 128
SUBLANES = 8

A_Q_HEADS = 8
A_KV_HEADS = 2
A_GROUP = A_Q_HEADS // A_KV_HEADS
A_HALF_WINDOW = 128
B_DILATIONS = (1, 4, 16)
B_GROUPS = 3
B_HEADS_PER_GROUP = 4
B_HALF_SPAN = 64
N_BUCKETS = 32
MAX_DISTANCE = 1024
N_ATTN_HEADS = A_Q_HEADS + B_GROUPS * B_HEADS_PER_GROUP

PEER_HEADS = 8
PEER_NKEYS = 128
PEER_KEY_BITS = PEER_NKEYS.bit_length() - 1
assert 1 << PEER_KEY_BITS == PEER_NKEYS
PEER_EXPERTS = PEER_NKEYS * PEER_NKEYS
PEER_QDIM = 256
PEER_TOPK = 16
PEER_PICKS = PEER_HEADS * PEER_TOPK
EPS = 1e-6

A_Q_W = A_Q_HEADS * HEAD_DIM
A_KV_W = A_KV_HEADS * HEAD_DIM
B_W = B_GROUPS * B_HEADS_PER_GROUP * HEAD_DIM
B_OUT_W = B_HEADS_PER_GROUP * HEAD_DIM
IN_WIDTH = A_Q_W + 2 * A_KV_W + 3 * B_W + 2 * D_MODEL
QA_CB = 0
KA_CB = A_Q_W // LANES
VA_CB = (A_Q_W + A_KV_W) // LANES
QB_CB = (A_Q_W + 2 * A_KV_W) // LANES
KB_CB = QB_CB + B_W // LANES
VB_CB = KB_CB + B_W // LANES
GA_OFF = A_Q_W + 2 * A_KV_W + 3 * B_W
GB_OFF = GA_OFF + D_MODEL
IN_CB = IN_WIDTH // LANES

NEG = -1e30
ATT_SCALE = HEAD_DIM ** -0.5
ATT_QT = 128
ATT_GROUP = 8

MIB = 1024 * 1024


def _params(sem, vmem_mib):
    return pltpu.CompilerParams(dimension_semantics=sem, vmem_limit_bytes=vmem_mib * MIB)


def _rms(x, g):
    return x * lax.rsqrt(jnp.mean(x * x, axis=-1, keepdims=True) + EPS) * g


def _inproj_kernel(x_ref, g_ref, w_ref, o_ref, h_scr):
    @pl.when(pl.program_id(1) == 0)
    def _():
        h_scr[...] = _rms(x_ref[...], g_ref[...]).astype(BF16)

    res = jnp.dot(h_scr[...], w_ref[...], preferred_element_type=F32)
    for c in range(o_ref.shape[0]):
        o_ref[c] = res[:, c * LANES:(c + 1) * LANES].astype(o_ref.dtype)


def _inproj(x2, g, w, tm=1024, tn=2048):
    n = w.shape[1]
    return pl.pallas_call(
        _inproj_kernel,
        out_shape=jax.ShapeDtypeStruct((n // LANES, N_TOK, LANES), BF16),
        grid=(N_TOK // tm, n // tn),
        in_specs=[
            pl.BlockSpec((tm, D_MODEL), lambda i, j: (i, 0)),
            pl.BlockSpec((1, D_MODEL), lambda i, j: (0, 0)),
            pl.BlockSpec((D_MODEL, tn), lambda i, j: (0, j)),
        ],
        out_specs=pl.BlockSpec((tn // LANES, tm, LANES), lambda i, j: (j, i, 0)),
        scratch_shapes=[pltpu.VMEM((tm, D_MODEL), BF16)],
        compiler_params=_params(("parallel", "arbitrary"), 52),
        name="inproj",
    )(x2, g, w)


def _t5_bucket_np(rel):
    half = N_BUCKETS // 2
    max_exact = half // 2
    ret = np.where(rel > 0, half, 0)
    n = np.abs(rel)
    nf = np.maximum(n, 1).astype(np.float64)
    large = max_exact + (np.log(nf / max_exact) / math.log(MAX_DISTANCE / max_exact) * (half - max_exact)).astype(np.int64)
    large = np.minimum(large, half - 1)
    return (ret + np.where(n < max_exact, n, large)).astype(np.int32)


def _bucket_matrix(hw, dist_scale):
    w = ATT_QT + 2 * hw
    delta = (np.arange(w)[None, :] - hw) - np.arange(ATT_QT)[:, None]
    bkt = _t5_bucket_np(delta * dist_scale)
    return np.where(np.abs(delta) <= hw, bkt, -1).astype(np.int32)


def _residues_per_trip(dil):
    tiles = SEQ // dil // ATT_QT
    return max(1, min(dil, ATT_GROUP // tiles))


def _bias_tile(bkt_ref, tab_ref, head, bias_scr):
    bkt = bkt_ref[...]
    bias = jnp.full(bkt.shape, NEG, F32)
    for b in range(N_BUCKETS):
        bias = jnp.where(bkt == b, tab_ref[b, head], bias)
    bias_scr[...] = bias


def _banded_attention(q_ref, k_ref, v_ref, bias_scr, kpad, vpad, stage, emit, *, dil, hw, sink):
    seq = SEQ // dil
    qt = ATT_QT
    win = qt + 2 * hw
    n_res = _residues_per_trip(dil)

    zeros = jnp.zeros((hw, HEAD_DIM), BF16)
    for pad_ref in (kpad, vpad):
        for u in range(n_res):
            pad_ref[u, 0:hw, :] = zeros
            pad_ref[u, hw + seq:hw + seq + hw, :] = zeros
    if dil == 1:
        kpad[0, hw:hw + seq, :] = k_ref[...]
        vpad[0, hw:hw + seq, :] = v_ref[...]
    else:
        q32, k32, v32, qres = stage
        q32[...] = q_ref[...].astype(F32)
        k32[...] = k_ref[...].astype(F32)
        v32[...] = v_ref[...].astype(F32)

    def scores(q0, q, kw):
        s = lax.dot_general(q, kw, (((1,), (1,)), ((), ())), preferred_element_type=F32)
        s = s * ATT_SCALE + bias_scr[...]
        kpos = q0 - hw + lax.broadcasted_iota(I32, (qt, win), 1)
        return jnp.where((kpos >= 0) & (kpos < seq), s, NEG)

    def softmax(s):
        m = jnp.max(s, axis=-1, keepdims=True)
        if sink is not None:
            m = jnp.maximum(m, sink)
        p = jnp.exp(s - m)
        den = jnp.sum(p, axis=-1, keepdims=True)
        if sink is not None:
            den = den + jnp.exp(sink - m)
        return p.astype(BF16), m, den

    group = min(ATT_GROUP, seq // qt)

    def residues(g, carry):
        r0 = g * n_res
        if dil > 1:
            for u in range(n_res):
                kpad[u, hw:hw + seq, :] = k32[pl.ds(r0 + u, seq, stride=dil), :].astype(BF16)
                vpad[u, hw:hw + seq, :] = v32[pl.ds(r0 + u, seq, stride=dil), :].astype(BF16)
                qres[u] = q32[pl.ds(r0 + u, seq, stride=dil), :].astype(BF16)

        def body(t, c):
            work = [(u, pl.multiple_of((t * group + i) * qt, qt)) for u in range(n_res) for i in range(group)]
            q_src = (lambda u: qres.at[u]) if dil > 1 else (lambda u: q_ref)
            s_all = [scores(q0, q_src(u)[pl.ds(q0, qt), :], kpad[u, pl.ds(q0, win), :]) for u, q0 in work]
            p_all = [softmax(s) for s in s_all]
            o_all = [jnp.dot(p, vpad[u, pl.ds(q0, win), :], preferred_element_type=F32) / den
                     for (u, q0), (p, m, den) in zip(work, p_all)]
            for (u, q0), (p, m, den), o in zip(work, p_all, o_all):
                emit(q0 if dil == 1 else r0 + u + q0 * dil, dil, o, m, den)
            return c

        lax.fori_loop(0, seq // (qt * group), body, 0)
        return carry

    if dil > 1:
        lax.fori_loop(0, dil // n_res, residues, 0)
    else:
        residues(0, 0)


def _attn_scratch(dil, hw):
    n_res = _residues_per_trip(dil)
    seq = SEQ // dil
    return [pltpu.VMEM((n_res, seq + 2 * hw, HEAD_DIM), BF16),
            pltpu.VMEM((n_res, seq + 2 * hw, HEAD_DIM), BF16),
            pltpu.VMEM((ATT_QT, ATT_QT + 2 * hw), F32)]


def _windowed_kernel(tab_ref, sink_ref, q_ref, k_ref, v_ref, bkt_ref, o_ref, kpad, vpad, bias_scr):
    head = pl.program_id(1)
    _bias_tile(bkt_ref, tab_ref, head, bias_scr)

    def emit(start, stride, o, m, den):
        o_ref[0, 0, pl.ds(start, ATT_QT), :] = o.astype(o_ref.dtype)

    _banded_attention(q_ref.at[0, 0], k_ref.at[0, 0], v_ref.at[0, 0], bias_scr, kpad, vpad, None, emit,
                      dil=1, hw=A_HALF_WINDOW, sink=sink_ref[head])


def _windowed_attention(proj, rel_bias, sink):
    pv = proj.reshape(IN_CB, BATCH, SEQ, HEAD_DIM)
    hw = A_HALF_WINDOW
    blk = (1, 1, SEQ, HEAD_DIM)
    out = pl.pallas_call(
        _windowed_kernel,
        out_shape=jax.ShapeDtypeStruct((A_Q_HEADS, BATCH, SEQ, HEAD_DIM), BF16),
        grid=(BATCH, A_Q_HEADS),
        in_specs=[
            pl.BlockSpec(memory_space=pltpu.SMEM),
            pl.BlockSpec(memory_space=pltpu.SMEM),
            pl.BlockSpec(blk, lambda b, h: (QA_CB + h, b, 0, 0)),
            pl.BlockSpec(blk, lambda b, h: (KA_CB + h // A_GROUP, b, 0, 0)),
            pl.BlockSpec(blk, lambda b, h: (VA_CB + h // A_GROUP, b, 0, 0)),
            pl.BlockSpec((ATT_QT, ATT_QT + 2 * hw), lambda b, h: (0, 0)),
        ],
        out_specs=pl.BlockSpec(blk, lambda b, h: (h, b, 0, 0)),
        scratch_shapes=_attn_scratch(1, hw),
        compiler_params=_params(("parallel", "arbitrary"), 32),
        name="attn_windowed",
    )(rel_bias, sink, pv, pv, pv, jnp.asarray(_bucket_matrix(hw, 1)))
    return out.reshape(A_Q_HEADS, N_TOK, HEAD_DIM)


MERGE_ROWS = 512


def _dilated_kernel(tab_ref, *refs):
    n_g = B_GROUPS
    qkv = [refs[3 * g:3 * g + 3] for g in range(n_g)]
    bkts = refs[3 * n_g:4 * n_g]
    o_ref = refs[4 * n_g]
    scr = list(refs[4 * n_g + 1:])
    pads = [scr[3 * g:3 * g + 3] for g in range(n_g)]
    q32, k32, v32, o32, l32 = scr[3 * n_g:3 * n_g + 5]
    qres = scr[3 * n_g + 5:]
    slot = pl.program_id(1)

    for g, dil in enumerate(B_DILATIONS):
        kpad, vpad, bias_scr = pads[g]
        _bias_tile(bkts[g], tab_ref, A_Q_HEADS + g * B_HEADS_PER_GROUP + slot, bias_scr)

        def emit(start, stride, o, m, den, g=g):
            rows = pl.ds(start, ATT_QT) if stride == 1 else pl.ds(start, ATT_QT, stride=stride)
            o32[g, rows, :] = o
            l32[g, rows, :] = jnp.broadcast_to(m + jnp.log(den), (ATT_QT, HEAD_DIM))

        q_ref, k_ref, v_ref = qkv[g]
        stage = None if dil == 1 else (q32, k32, v32, qres[g])
        _banded_attention(q_ref.at[0, 0], k_ref.at[0, 0], v_ref.at[0, 0], bias_scr, kpad, vpad, stage, emit,
                          dil=dil, hw=B_HALF_SPAN, sink=None)

    def merge(c, carry):
        rows = pl.ds(pl.multiple_of(c * MERGE_ROWS, MERGE_ROWS), MERGE_ROWS)
        lses = [l32[g, rows, :] for g in range(n_g)]
        mx = functools.reduce(jnp.maximum, lses)
        es = [jnp.exp(l - mx) for l in lses]
        den = functools.reduce(jnp.add, es)
        o_ref[0, 0, rows, :] = functools.reduce(
            jnp.add, [(e / den) * o32[g, rows, :] for g, e in enumerate(es)]).astype(o_ref.dtype)
        return carry

    lax.fori_loop(0, SEQ // MERGE_ROWS, merge, 0)


def _dilated_attention(proj, rel_bias):
    pv = proj.reshape(IN_CB, BATCH, SEQ, HEAD_DIM)
    hw = B_HALF_SPAN
    blk = (1, 1, SEQ, HEAD_DIM)
    in_specs = [pl.BlockSpec(memory_space=pltpu.SMEM)]
    args = [rel_bias]
    for g in range(B_GROUPS):
        for cb in (QB_CB, KB_CB, VB_CB):
            in_specs.append(pl.BlockSpec(blk, lambda b, h, c=cb + g * B_HEADS_PER_GROUP: (c + h, b, 0, 0)))
            args.append(pv)
    for dil in B_DILATIONS:
        in_specs.append(pl.BlockSpec((ATT_QT, ATT_QT + 2 * hw), lambda b, h: (0, 0)))
        args.append(jnp.asarray(_bucket_matrix(hw, dil)))
    scratch = []
    for dil in B_DILATIONS:
        scratch += _attn_scratch(dil, hw)
    scratch += [pltpu.VMEM((SEQ, HEAD_DIM), F32)] * 3
    scratch += [pltpu.VMEM((B_GROUPS, SEQ, HEAD_DIM), F32)] * 2
    scratch += [pltpu.VMEM((_residues_per_trip(dil), SEQ // dil, HEAD_DIM), BF16) for dil in B_DILATIONS]
    out = pl.pallas_call(
        _dilated_kernel,
        out_shape=jax.ShapeDtypeStruct((B_HEADS_PER_GROUP, BATCH, SEQ, HEAD_DIM), BF16),
        grid=(BATCH, B_HEADS_PER_GROUP),
        in_specs=in_specs,
        out_specs=pl.BlockSpec(blk, lambda b, h: (h, b, 0, 0)),
        scratch_shapes=scratch,
        compiler_params=_params(("parallel", "arbitrary"), 52),
        name="attn_dilated",
    )(*args)
    return out.reshape(B_HEADS_PER_GROUP, N_TOK, HEAD_DIM)


def _outproj_kernel(oa_ref, ob_ref, ga_ref, gb_ref, x_ref, woa_ref, wob_ref, wout_ref, gn_ref,
                    x1_ref, h2_ref):
    wide = lambda ref: jnp.concatenate([ref[c] for c in range(ref.shape[0])], axis=-1)
    ya = jnp.dot(wide(oa_ref), woa_ref[...], preferred_element_type=F32)
    yb = jnp.dot(wide(ob_ref), wob_ref[...], preferred_element_type=F32)
    merged = (jax.nn.sigmoid(wide(ga_ref).astype(F32)) * ya
              + jax.nn.sigmoid(wide(gb_ref).astype(F32)) * yb)
    x1 = x_ref[...] + jnp.dot(merged.astype(BF16), wout_ref[...], preferred_element_type=F32)
    x1_ref[...] = x1
    h2_ref[...] = _rms(x1, gn_ref[...]).astype(BF16)


def _outproj(oa, ob, proj, x2, w_oa, w_ob, w_out, g_ffn, tm=512):
    row = lambda w: pl.BlockSpec((tm, w), lambda i: (i, 0))
    slabs = lambda n, first: pl.BlockSpec((n, tm, LANES), lambda i: (first // n, i, 0))
    gate_cb = D_MODEL // LANES
    const = lambda shape: pl.BlockSpec(shape, lambda i: (0, 0), pipeline_mode=pl.Buffered(1))
    return pl.pallas_call(
        _outproj_kernel,
        out_shape=[jax.ShapeDtypeStruct((N_TOK, D_MODEL), F32),
                   jax.ShapeDtypeStruct((N_TOK, D_MODEL), BF16)],
        grid=(N_TOK // tm,),
        in_specs=[slabs(A_Q_HEADS, 0), slabs(B_HEADS_PER_GROUP, 0),
                  slabs(gate_cb, GA_OFF // LANES), slabs(gate_cb, GB_OFF // LANES),
                  row(D_MODEL),
                  const((A_Q_W, D_MODEL)), const((B_OUT_W, D_MODEL)), const((D_MODEL, D_MODEL)),
                  const((1, D_MODEL))],
        out_specs=[row(D_MODEL), row(D_MODEL)],
        compiler_params=_params(("parallel",), 56),
        name="outproj",
    )(oa, ob, proj, proj, x2, w_oa, w_ob, w_out, g_ffn)


_FAR = 1024.0
RT_ROWS = 512
RT_SLICE = 256
RT_HEADS = 4
UP_CHUNK = 256
UP_ROWS = 256

_LANE_GATHER = lax.GatherDimensionNumbers(
    offset_dims=(), collapsed_slice_dims=(1,), start_index_map=(1,),
    operand_batching_dims=(0,), start_indices_batching_dims=(0,))


def _tree(op, xs):
    xs = list(xs)
    while len(xs) > 1:
        nxt = [op(xs[i], xs[i + 1]) for i in range(0, len(xs) - 1, 2)]
        if len(xs) % 2:
            nxt.append(xs[-1])
        xs = nxt
    return xs[0]


def _all_sublanes(op, x):
    for shift in (4, 2, 1):
        x = op(x, pltpu.roll(x, shift, axis=0))
    return x


def _pop_max(tiles, sub_iota, payload=None):
    m = _all_sublanes(jnp.maximum, _tree(jnp.maximum, tiles))
    first = _tree(jnp.minimum, [jnp.where(t == m, float(SUBLANES * v), _FAR) for v, t in enumerate(tiles)])
    row = _all_sublanes(jnp.minimum, first + sub_iota)
    base = row - sub_iota
    hits = [base == float(SUBLANES * v) for v in range(len(tiles))]
    if payload is None:
        val = row
    else:
        val = _all_sublanes(jnp.maximum, _tree(jnp.maximum, [jnp.where(h, p, -1.0)
                                                            for h, p in zip(hits, payload)]))
    return m, val, [jnp.where(h, -jnp.inf, t) for h, t in zip(hits, tiles)]


def _route_scores(h, q_ref, keys_ref, s_scr):
    for c in range(2):
        col = pl.multiple_of((2 * h + c) * LANES, LANES)
        s_scr[c] = lax.dot_general(keys_ref[2 * h + c], q_ref[:, pl.ds(col, LANES)],
                                   (((1,), (1,)), ((), ())), preferred_element_type=F32)


def _batcher_pairs(n):
    size = 16
    pairs = []
    p = 1
    while p < size:
        k = p
        while k >= 1:
            for j in range(k % p, size - k, 2 * k):
                for i in range(min(k, size - j - k)):
                    if (i + j) // (2 * p) == (i + j + k) // (2 * p) and i + j + k < n:
                        pairs.append((i + j, i + j + k))
            k //= 2
        p *= 2
    return pairs


_FAR_ID = 1e9


def _top_sorted(vals, ids, k):
    vals, ids = list(vals), list(ids)
    n = len(vals)
    for i, j in _batcher_pairs(n):
        up = vals[j] > vals[i]
        vals[i], vals[j] = jnp.where(up, vals[j], vals[i]), jnp.where(up, vals[i], vals[j])
        ids[i], ids[j] = jnp.where(up, ids[j], ids[i]), jnp.where(up, ids[i], ids[j])
    top_v, top_i = [], []
    for it in range(k):
        m = _all_sublanes(jnp.maximum, vals[0])
        first = _all_sublanes(jnp.minimum, jnp.where(vals[0] == m, ids[0], _FAR_ID))
        win = ids[0] == first
        top_v.append(m)
        top_i.append(first)
        for r in range(min(k - it, n)):
            if r + 1 < n:
                vals[r] = jnp.where(win, vals[r + 1], vals[r])
                ids[r] = jnp.where(win, ids[r + 1], ids[r])
            else:
                vals[r] = jnp.where(win, -jnp.inf, vals[r])
    runner_up = _all_sublanes(jnp.maximum, vals[0])
    tie = jnp.zeros_like(runner_up)
    for a, b in zip(top_v, top_v[1:] + [runner_up]):
        tie = jnp.where(a == b, 1.0, tie)
    return top_v, top_i, tie


def _route_topk(s_scr, ts_scr, ti_scr, bs_scr, et_ref, gt_ref, *, exact):
    k = PEER_TOPK
    nk = float(PEER_NKEYS)
    sub_iota = lax.broadcasted_iota(I32, (SUBLANES, LANES), 0).astype(F32)
    halves = [slice(i * LANES, (i + 1) * LANES) for i in range(RT_SLICE // LANES)]
    tie = None if exact else jnp.zeros((SUBLANES, LANES), F32)

    chains = []
    for c in range(2):
        for lanes in halves:
            tiles = [s_scr[c, SUBLANES * v:SUBLANES * (v + 1), lanes] for v in range(PEER_NKEYS // SUBLANES)]
            chains.append([c, lanes, tiles])
    if exact:
        for kk in range(k):
            for chain in chains:
                c, lanes, tiles = chain
                m, row, chain[2] = _pop_max(tiles, sub_iota)
                ts_scr[c, kk:kk + 1, lanes] = m[0:1]
                ti_scr[c, kk:kk + 1, lanes] = row[0:1]
    else:
        row_ids = [float(SUBLANES * v) + sub_iota for v in range(PEER_NKEYS // SUBLANES)]
        for c, lanes, tiles in chains:
            top_v, top_i, t = _top_sorted(tiles, row_ids, k)
            tie = jnp.maximum(tie, t)
            for kk in range(k):
                ts_scr[c, kk:kk + 1, lanes] = top_v[kk][0:1]
                ti_scr[c, kk:kk + 1, lanes] = top_i[kk][0:1]

    def candidates(lanes):
        s2 = [ts_scr[1, 0:8, lanes], ts_scr[1, 8:16, lanes]]
        i2 = [ti_scr[1, 0:8, lanes], ti_scr[1, 8:16, lanes]]
        s1 = lambda k1: ts_scr[0, k1:k1 + 1, lanes]
        i1 = lambda k1: ti_scr[0, k1:k1 + 1, lanes] * nk
        tiles = [s1(0) + s2[0], s1(0) + s2[1]]
        pay = [i1(0) + i2[0], i1(0) + i2[1]]
        for k1 in range(1, 8):
            allowed = k // (k1 + 1)
            t = s1(k1) + s2[0]
            tiles.append(t if allowed >= SUBLANES else jnp.where(sub_iota < float(allowed), t, -jnp.inf))
            pay.append(i1(k1) + i2[0])
        tiles.append(ts_scr[0, 8:16, lanes] + ts_scr[1, 0:1, lanes])
        pay.append(ti_scr[0, 8:16, lanes] * nk + ti_scr[1, 0:1, lanes])
        return tiles, pay

    cands = [candidates(lanes) for lanes in halves]
    if exact:
        tiles = [c[0] for c in cands]
        for kk in range(k):
            for i, lanes in enumerate(halves):
                m, expert, tiles[i] = _pop_max(tiles[i], sub_iota, payload=cands[i][1])
                bs_scr[kk:kk + 1, lanes] = m[0:1]
                et_ref[kk:kk + 1, lanes] = expert[0:1]
    else:
        for (tiles, pay), lanes in zip(cands, halves):
            top_v, top_i, t = _top_sorted(tiles, pay, k)
            tie = jnp.maximum(tie, t)
            for kk in range(k):
                bs_scr[kk:kk + 1, lanes] = top_v[kk][0:1]
                et_ref[kk:kk + 1, lanes] = top_i[kk][0:1]
    for lanes in halves:
        bs = bs_scr[:, lanes]
        ex = jnp.exp(bs - jnp.max(bs, axis=0, keepdims=True))
        gt_ref[:, lanes] = ex / jnp.sum(ex, axis=0, keepdims=True)
    return tie


def _expert_up(j, h_ref, u_ref, e_ref, act_ref, *, te):
    tb = h_ref.shape[0]
    u = [u_ref[c * UP_CHUNK:(c + 1) * UP_CHUNK, :].astype(BF16) for c in range(te // UP_CHUNK)]
    for m in range(tb // UP_ROWS):
        rows = slice(m * UP_ROWS, (m + 1) * UP_ROWS)
        e = e_ref[rows, :]
        row = e >> PEER_KEY_BITS
        col = (e & (PEER_NKEYS - 1))[..., None]
        acc = act_ref[rows, :]
        h = h_ref[rows, :]
        for c in range(te // UP_CHUNK):
            dense = lax.dot_general(h, u[c], (((1,), (1,)), ((), ())), preferred_element_type=F32)
            for q in range(UP_CHUNK // LANES):
                got = lax.gather(dense[:, q * LANES:(q + 1) * LANES], col, _LANE_GATHER, (1, 1),
                                 mode=lax.GatherScatterMode.PROMISE_IN_BOUNDS)
                acc = jnp.where(row == (j * te + c * UP_CHUNK) // LANES + q, got, acc)
        act_ref[rows, :] = acc


def _route_kernel(h_ref, wq_ref, keys_ref, v_ref, e_ref, g_ref, vb_ref,
                  q_scr, s_scr, ts_scr, ti_scr, bs_scr, et_scr, gt_scr):
    vb_ref[...] = v_ref[...].astype(BF16)
    q_scr[...] = jnp.dot(h_ref[...], wq_ref[...], preferred_element_type=F32).astype(BF16)

    head_groups = PEER_HEADS // RT_HEADS

    def heads(g, carry):
        sl = g // head_groups
        q_rows = q_scr.at[pl.ds(pl.multiple_of(sl * RT_SLICE, RT_SLICE), RT_SLICE), :]
        hs = [(g % head_groups) * RT_HEADS + u for u in range(RT_HEADS)]
        for u, h in enumerate(hs):
            _route_scores(h, q_rows, keys_ref, s_scr.at[u])
        args = [(s_scr.at[u], ts_scr.at[u], ti_scr.at[u], bs_scr.at[u], et_scr.at[sl, h], gt_scr.at[sl, h])
                for u, h in enumerate(hs)]
        ties = [_route_topk(*a, exact=False) for a in args]
        for a, tie in zip(args, ties):
            @pl.when(jnp.max(tie) > 0.0)
            def _():
                _route_topk(*a, exact=True)

        return carry

    lax.fori_loop(0, (RT_ROWS // RT_SLICE) * head_groups, heads, 0)

    for sl in range(RT_ROWS // RT_SLICE):
        for half in range(RT_SLICE // LANES):
            lanes = slice(half * LANES, (half + 1) * LANES)
            rows = slice(sl * RT_SLICE + half * LANES, sl * RT_SLICE + (half + 1) * LANES)
            et = jnp.concatenate([et_scr[sl, h, :, lanes] for h in range(PEER_HEADS)], axis=0)
            gt = jnp.concatenate([gt_scr[sl, h, :, lanes] for h in range(PEER_HEADS)], axis=0)
            e_ref[rows, :] = et.T.astype(I32)
            g_ref[rows, :] = gt.T


def _route(h2, wq, keys, v):
    k = PEER_TOPK
    steps = N_TOK // RT_ROWS
    tok = pl.BlockSpec((RT_ROWS, PEER_PICKS), lambda i: (i, 0))
    v_slab = pl.BlockSpec((PEER_EXPERTS // steps, D_MODEL), lambda i: (i, 0))
    return pl.pallas_call(
        _route_kernel,
        out_shape=[jax.ShapeDtypeStruct((N_TOK, PEER_PICKS), I32),
                   jax.ShapeDtypeStruct((N_TOK, PEER_PICKS), F32),
                   jax.ShapeDtypeStruct((PEER_EXPERTS, D_MODEL), BF16)],
        grid=(steps,),
        in_specs=[pl.BlockSpec((RT_ROWS, D_MODEL), lambda i: (i, 0)),
                  pl.BlockSpec((D_MODEL, PEER_HEADS * PEER_QDIM), lambda i: (0, 0),
                               pipeline_mode=pl.Buffered(1)),
                  pl.BlockSpec((PEER_HEADS * 2, PEER_NKEYS, PEER_QDIM // 2), lambda i: (0, 0, 0)),
                  v_slab],
        out_specs=[tok, tok, v_slab],
        scratch_shapes=[
            pltpu.VMEM((RT_ROWS, PEER_HEADS * PEER_QDIM), BF16),
            pltpu.VMEM((RT_HEADS, 2, PEER_NKEYS, RT_SLICE), F32),
            pltpu.VMEM((RT_HEADS, 2, k, RT_SLICE), F32),
            pltpu.VMEM((RT_HEADS, 2, k, RT_SLICE), F32),
            pltpu.VMEM((RT_HEADS, k, RT_SLICE), F32),
            pltpu.VMEM((RT_ROWS // RT_SLICE, PEER_HEADS, k, RT_SLICE), F32),
            pltpu.VMEM((RT_ROWS // RT_SLICE, PEER_HEADS, k, RT_SLICE), F32),
        ],
        compiler_params=_params(("parallel",), 52),
        name="peer_route",
    )(h2, wq, keys, v)


def _peer_up_kernel(h_ref, u_ref, e_ref, act_ref, *, te):
    j = pl.program_id(1)

    @pl.when(j == 0)
    def _():
        act_ref[...] = jnp.zeros_like(act_ref)

    _expert_up(j, h_ref, u_ref, e_ref, act_ref, te=te)


def _peer_up(h2, u, e, tb=2048, te=1024):
    return pl.pallas_call(
        functools.partial(_peer_up_kernel, te=te),
        out_shape=jax.ShapeDtypeStruct((N_TOK, PEER_PICKS), F32),
        grid=(N_TOK // tb, PEER_EXPERTS // te),
        in_specs=[pl.BlockSpec((tb, D_MODEL), lambda i, j: (i, 0)),
                  pl.BlockSpec((te, D_MODEL), lambda i, j: (j, 0)),
                  pl.BlockSpec((tb, PEER_PICKS), lambda i, j: (i, 0))],
        out_specs=pl.BlockSpec((tb, PEER_PICKS), lambda i, j: (i, 0)),
        compiler_params=_params(("parallel", "arbitrary"), 56),
        name="peer_up",
    )(h2, u, e)


def _peer_coef_kernel(e_ref, g_ref, act_ref, p_ref, row_scr, col_scr, w_scr, c_scr, *, tp):
    e = e_ref[...]
    row_scr[...] = (e >> PEER_KEY_BITS).astype(F32)
    col_scr[...] = (e & (PEER_NKEYS - 1)).astype(F32)
    act = act_ref[...]
    w_scr[...] = g_ref[...] * (0.5 * act * (1.0 + lax.erf(act * math.sqrt(0.5))))
    sub = 16
    iota = lax.broadcasted_iota(I32, (PEER_NKEYS // sub, sub, PEER_PICKS), 0) * sub \
        + lax.broadcasted_iota(I32, (PEER_NKEYS // sub, sub, PEER_PICKS), 1)
    iota = iota.astype(F32).astype(BF16)
    one = jnp.ones((), BF16)
    zero = jnp.zeros((), BF16)

    def bcast(ref, t):
        return jnp.broadcast_to(ref[pl.ds(t, 1), :], (sub, PEER_PICKS)).astype(BF16)[None]

    def products(t0):
        for t in range(sub):
            rows, cols, w = bcast(row_scr, t0 + t), bcast(col_scr, t0 + t), bcast(w_scr, t0 + t)
            left = jnp.where(iota == rows, one, zero).reshape(PEER_NKEYS, PEER_PICKS)
            right = jnp.where(iota == cols, w, zero).reshape(PEER_NKEYS, PEER_PICKS)
            c_scr[t0 + t] = lax.dot_general(left, right, (((1,), (1,)), ((), ())),
                                            preferred_element_type=F32)

    def regroup(t0):
        sw = jnp.swapaxes(c_scr[pl.ds(t0, sub)], 0, 1)
        for k1 in range(PEER_NKEYS):
            p_ref[pl.ds(t0, sub), k1 * PEER_NKEYS:(k1 + 1) * PEER_NKEYS] = sw[k1].astype(p_ref.dtype)

    def block(blk, carry):
        base = pl.multiple_of(blk * COEF_UNROLL, COEF_UNROLL)
        for t0 in range(0, COEF_UNROLL, sub):
            products(base + t0)
            if t0:
                regroup(base + t0 - sub)
        regroup(base + COEF_UNROLL - sub)
        return carry

    lax.fori_loop(0, tp // COEF_UNROLL, block, 0)


COEF_UNROLL = 64


def _peer_coef(e, g, act, tp=128):
    tok = pl.BlockSpec((tp, PEER_PICKS), lambda i: (i, 0))
    return pl.pallas_call(
        functools.partial(_peer_coef_kernel, tp=tp),
        out_shape=jax.ShapeDtypeStruct((N_TOK, PEER_EXPERTS), BF16),
        grid=(N_TOK // tp,),
        in_specs=[tok, tok, tok],
        out_specs=pl.BlockSpec((tp, PEER_EXPERTS), lambda i: (i, 0)),
        scratch_shapes=[pltpu.VMEM((tp, PEER_PICKS), F32),
                        pltpu.VMEM((tp, PEER_PICKS), F32),
                        pltpu.VMEM((tp, PEER_PICKS), F32),
                        pltpu.VMEM((tp, PEER_NKEYS, PEER_NKEYS), F32)],
        compiler_params=_params(("parallel",), 32),
        name="peer_coef",
    )(e, g, act)


def _peer_down_kernel(p_ref, v_ref, x_ref, g_ref, o_ref):
    j = pl.program_id(1)

    @pl.when(j == 0)
    def _():
        o_ref[...] = jnp.zeros_like(o_ref)

    o_ref[...] += jnp.dot(p_ref[...], v_ref[...], preferred_element_type=F32)
    cols = pl.ds(pl.multiple_of(j * LANES, LANES), LANES)
    o_ref[:, cols] += x_ref[...]

    @pl.when(j == pl.num_programs(1) - 1)
    def _():
        o_ref[...] = _rms(o_ref[...], g_ref[...])


def _peer_down(coef, v, x1, g_final, tm=2048):
    tk = PEER_EXPERTS // (D_MODEL // LANES)
    return pl.pallas_call(
        _peer_down_kernel,
        out_shape=jax.ShapeDtypeStruct((N_TOK, D_MODEL), F32),
        grid=(N_TOK // tm, PEER_EXPERTS // tk),
        in_specs=[pl.BlockSpec((tm, tk), lambda i, j: (i, j)),
                  pl.BlockSpec((tk, D_MODEL), lambda i, j: (j, 0)),
                  pl.BlockSpec((tm, LANES), lambda i, j: (i, j)),
                  pl.BlockSpec((1, D_MODEL), lambda i, j: (0, 0))],
        out_specs=pl.BlockSpec((tm, D_MODEL), lambda i, j: (i, 0)),
        compiler_params=_params(("parallel", "arbitrary"), 60),
        name="peer_down",
    )(coef, v, x1, g_final)


def kernel(x, rel_bias, norm_mix_g, w_in, sink_a, w_oa, w_ob, w_out, norm_ffn_g,
           peer_wq, peer_keys, peer_u, peer_v, norm_final_g):
    assert x.shape == (BATCH, SEQ, D_MODEL) and w_in.shape[0] == 1
    x2 = x.reshape(N_TOK, D_MODEL)
    proj = _inproj(x2, norm_mix_g, w_in[0].astype(BF16))

    oa = _windowed_attention(proj, rel_bias, sink_a[0])
    ob = _dilated_attention(proj, rel_bias)
    x1, h2 = _outproj(oa, ob, proj, x2, w_oa[0].astype(BF16), w_ob[0].astype(BF16),
                      w_out[0].astype(BF16), norm_ffn_g)

    keys = peer_keys[0].reshape(PEER_HEADS * 2, PEER_NKEYS, PEER_QDIM // 2).astype(BF16)
    e, gate, v_bf16 = _route(h2, peer_wq[0].astype(BF16), keys, peer_v[0])
    act = _peer_up(h2, peer_u[0], e)
    coef = _peer_coef(e, gate, act)
    out = _peer_down(coef, v_bf16, x1, norm_final_g.reshape(1, D_MODEL))
    return out.reshape(BATCH, SEQ, D_MODEL)
```

```python
import functools
import math

import numpy as np
import jax
import jax.numpy as jnp
from jax import lax
from jax.experimental import pallas as pl
from jax.experimental.pallas import tpu as pltpu

F32 = jnp.float32
BF16 = jnp.bfloat16
I32 = jnp.int32

D_MODEL = 2048
BATCH = 2
SEQ = 4096
N_TOK = BATCH * SEQ
HEAD_DIM = 128
LANES = 128
SUBLANES = 8

A_Q_HEADS = 8
A_KV_HEADS = 2
A_GROUP = A_Q_HEADS // A_KV_HEADS
A_HALF_WINDOW = 128
B_DILATIONS = (1, 4, 16)
B_GROUPS = 3
B_HEADS_PER_GROUP = 4
B_HALF_SPAN = 64
N_BUCKETS = 32
MAX_DISTANCE = 1024
N_ATTN_HEADS = A_Q_HEADS + B_GROUPS * B_HEADS_PER_GROUP

PEER_HEADS = 8
PEER_NKEYS = 128
PEER_KEY_BITS = PEER_NKEYS.bit_length() - 1
assert 1 << PEER_KEY_BITS == PEER_NKEYS
PEER_EXPERTS = PEER_NKEYS * PEER_NKEYS
PEER_QDIM = 256
PEER_TOPK = 16
PEER_PICKS = PEER_HEADS * PEER_TOPK
EPS = 1e-6

A_Q_W = A_Q_HEADS * HEAD_DIM
A_KV_W = A_KV_HEADS * HEAD_DIM
B_W = B_GROUPS * B_HEADS_PER_GROUP * HEAD_DIM
B_OUT_W = B_HEADS_PER_GROUP * HEAD_DIM
IN_WIDTH = A_Q_W + 2 * A_KV_W + 3 * B_W + 2 * D_MODEL
QA_CB = 0
KA_CB = A_Q_W // LANES
VA_CB = (A_Q_W + A_KV_W) // LANES
QB_CB = (A_Q_W + 2 * A_KV_W) // LANES
KB_CB = QB_CB + B_W // LANES
VB_CB = KB_CB + B_W // LANES
GA_OFF = A_Q_W + 2 * A_KV_W + 3 * B_W
GB_OFF = GA_OFF + D_MODEL
IN_CB = IN_WIDTH // LANES

NEG = -1e30
ATT_SCALE = HEAD_DIM ** -0.5
ATT_QT = 128
ATT_GROUP = 8

MIB = 1024 * 1024


def _params(sem, vmem_mib):
    return pltpu.CompilerParams(dimension_semantics=sem, vmem_limit_bytes=vmem_mib * MIB)


def _rms(x, g):
    return x * lax.rsqrt(jnp.mean(x * x, axis=-1, keepdims=True) + EPS) * g


def _inproj_kernel(x_ref, g_ref, w_ref, o_ref, h_scr):
    @pl.when(pl.program_id(1) == 0)
    def _():
        h_scr[...] = _rms(x_ref[...], g_ref[...]).astype(BF16)

    res = jnp.dot(h_scr[...], w_ref[...], preferred_element_type=F32)
    for c in range(o_ref.shape[0]):
        o_ref[c] = res[:, c * LANES:(c + 1) * LANES].astype(o_ref.dtype)


def _inproj(x2, g, w, tm=1024, tn=2048):
    n = w.shape[1]
    return pl.pallas_call(
        _inproj_kernel,
        out_shape=jax.ShapeDtypeStruct((n // LANES, N_TOK, LANES), BF16),
        grid=(N_TOK // tm, n // tn),
        in_specs=[
            pl.BlockSpec((tm, D_MODEL), lambda i, j: (i, 0)),
            pl.BlockSpec((1, D_MODEL), lambda i, j: (0, 0)),
            pl.BlockSpec((D_MODEL, tn), lambda i, j: (0, j)),
        ],
        out_specs=pl.BlockSpec((tn // LANES, tm, LANES), lambda i, j: (j, i, 0)),
        scratch_shapes=[pltpu.VMEM((tm, D_MODEL), BF16)],
        compiler_params=_params(("parallel", "arbitrary"), 52),
        name="inproj",
    )(x2, g, w)


def _t5_bucket_np(rel):
    half = N_BUCKETS // 2
    max_exact = half // 2
    ret = np.where(rel > 0, half, 0)
    n = np.abs(rel)
    nf = np.maximum(n, 1).astype(np.float64)
    large = max_exact + (np.log(nf / max_exact) / math.log(MAX_DISTANCE / max_exact) * (half - max_exact)).astype(np.int64)
    large = np.minimum(large, half - 1)
    return (ret + np.where(n < max_exact, n, large)).astype(np.int32)


def _bucket_matrix(hw, dist_scale):
    w = ATT_QT + 2 * hw
    delta = (np.arange(w)[None, :] - hw) - np.arange(ATT_QT)[:, None]
    bkt = _t5_bucket_np(delta * dist_scale)
    return np.where(np.abs(delta) <= hw, bkt, -1).astype(np.int32)


def _residues_per_trip(dil):
    tiles = SEQ // dil // ATT_QT
    return max(1, min(dil, ATT_GROUP // tiles))


def _bias_tile(bkt_ref, tab_ref, head, bias_scr):
    bkt = bkt_ref[...]
    bias = jnp.full(bkt.shape, NEG, F32)
    for b in range(N_BUCKETS):
        bias = jnp.where(bkt == b, tab_ref[b, head], bias)
    bias_scr[...] = bias


def _banded_attention(q_ref, k_ref, v_ref, bias_scr, kpad, vpad, stage, emit, *, dil, hw, sink):
    seq = SEQ // dil
    qt = ATT_QT
    win = qt + 2 * hw
    n_res = _residues_per_trip(dil)

    zeros = jnp.zeros((hw, HEAD_DIM), BF16)
    for pad_ref in (kpad, vpad):
        for u in range(n_res):
            pad_ref[u, 0:hw, :] = zeros
            pad_ref[u, hw + seq:hw + seq + hw, :] = zeros
    if dil == 1:
        kpad[0, hw:hw + seq, :] = k_ref[...]
        vpad[0, hw:hw + seq, :] = v_ref[...]
    else:
        q32, k32, v32, qres = stage
        q32[...] = q_ref[...].astype(F32)
        k32[...] = k_ref[...].astype(F32)
        v32[...] = v_ref[...].astype(F32)

    def scores(q0, q, kw):
        s = lax.dot_general(q, kw, (((1,), (1,)), ((), ())), preferred_element_type=F32)
        s = s * ATT_SCALE + bias_scr[...]
        kpos = q0 - hw + lax.broadcasted_iota(I32, (qt, win), 1)
        return jnp.where((kpos >= 0) & (kpos < seq), s, NEG)

    def softmax(s):
        m = jnp.max(s, axis=-1, keepdims=True)
        if sink is not None:
            m = jnp.maximum(m, sink)
        p = jnp.exp(s - m)
        den = jnp.sum(p, axis=-1, keepdims=True)
        if sink is not None:
            den = den + jnp.exp(sink - m)
        return p.astype(BF16), m, den

    group = min(ATT_GROUP, seq // qt)

    def residues(g, carry):
        r0 = g * n_res
        if dil > 1:
            for u in range(n_res):
                kpad[u, hw:hw + seq, :] = k32[pl.ds(r0 + u, seq, stride=dil), :].astype(BF16)
                vpad[u, hw:hw + seq, :] = v32[pl.ds(r0 + u, seq, stride=dil), :].astype(BF16)
                qres[u] = q32[pl.ds(r0 + u, seq, stride=dil), :].astype(BF16)

        def body(t, c):
            work = [(u, pl.multiple_of((t * group + i) * qt, qt)) for u in range(n_res) for i in range(group)]
            q_src = (lambda u: qres.at[u]) if dil > 1 else (lambda u: q_ref)
            s_all = [scores(q0, q_src(u)[pl.ds(q0, qt), :], kpad[u, pl.ds(q0, win), :]) for u, q0 in work]
            p_all = [softmax(s) for s in s_all]
            o_all = [jnp.dot(p, vpad[u, pl.ds(q0, win), :], preferred_element_type=F32) / den
                     for (u, q0), (p, m, den) in zip(work, p_all)]
            for (u, q0), (p, m, den), o in zip(work, p_all, o_all):
                emit(q0 if dil == 1 else r0 + u + q0 * dil, dil, o, m, den)
            return c

        lax.fori_loop(0, seq // (qt * group), body, 0)
        return carry

    if dil > 1:
        lax.fori_loop(0, dil // n_res, residues, 0)
    else:
        residues(0, 0)


def _attn_scratch(dil, hw):
    n_res = _residues_per_trip(dil)
    seq = SEQ // dil
    return [pltpu.VMEM((n_res, seq + 2 * hw, HEAD_DIM), BF16),
            pltpu.VMEM((n_res, seq + 2 * hw, HEAD_DIM), BF16),
            pltpu.VMEM((ATT_QT, ATT_QT + 2 * hw), F32)]


def _windowed_kernel(tab_ref, sink_ref, q_ref, k_ref, v_ref, bkt_ref, o_ref, kpad, vpad, bias_scr):
    head = pl.program_id(1)
    _bias_tile(bkt_ref, tab_ref, head, bias_scr)

    def emit(start, stride, o, m, den):
        o_ref[0, 0, pl.ds(start, ATT_QT), :] = o.astype(o_ref.dtype)

    _banded_attention(q_ref.at[0, 0], k_ref.at[0, 0], v_ref.at[0, 0], bias_scr, kpad, vpad, None, emit,
                      dil=1, hw=A_HALF_WINDOW, sink=sink_ref[head])


def _windowed_attention(proj, rel_bias, sink):
    pv = proj.reshape(IN_CB, BATCH, SEQ, HEAD_DIM)
    hw = A_HALF_WINDOW
    blk = (1, 1, SEQ, HEAD_DIM)
    out = pl.pallas_call(
        _windowed_kernel,
        out_shape=jax.ShapeDtypeStruct((A_Q_HEADS, BATCH, SEQ, HEAD_DIM), BF16),
        grid=(BATCH, A_Q_HEADS),
        in_specs=[
            pl.BlockSpec(memory_space=pltpu.SMEM),
            pl.BlockSpec(memory_space=pltpu.SMEM),
            pl.BlockSpec(blk, lambda b, h: (QA_CB + h, b, 0, 0)),
            pl.BlockSpec(blk, lambda b, h: (KA_CB + h // A_GROUP, b, 0, 0)),
            pl.BlockSpec(blk, lambda b, h: (VA_CB + h // A_GROUP, b, 0, 0)),
            pl.BlockSpec((ATT_QT, ATT_QT + 2 * hw), lambda b, h: (0, 0)),
        ],
        out_specs=pl.BlockSpec(blk, lambda b, h: (h, b, 0, 0)),
        scratch_shapes=_attn_scratch(1, hw),
        compiler_params=_params(("parallel", "arbitrary"), 32),
        name="attn_windowed",
    )(rel_bias, sink, pv, pv, pv, jnp.asarray(_bucket_matrix(hw, 1)))
    return out.reshape(A_Q_HEADS, N_TOK, HEAD_DIM)


MERGE_ROWS = 512


def _dilated_kernel(tab_ref, *refs):
    n_g = B_GROUPS
    qkv = [refs[3 * g:3 * g + 3] for g in range(n_g)]
    bkts = refs[3 * n_g:4 * n_g]
    o_ref = refs[4 * n_g]
    scr = list(refs[4 * n_g + 1:])
    pads = [scr[3 * g:3 * g + 3] for g in range(n_g)]
    q32, k32, v32, o32, l32 = scr[3 * n_g:3 * n_g + 5]
    qres = scr[3 * n_g + 5:]
    slot = pl.program_id(1)

    for g, dil in enumerate(B_DILATIONS):
        kpad, vpad, bias_scr = pads[g]
        _bias_tile(bkts[g], tab_ref, A_Q_HEADS + g * B_HEADS_PER_GROUP + slot, bias_scr)

        def emit(start, stride, o, m, den, g=g):
            rows = pl.ds(start, ATT_QT) if stride == 1 else pl.ds(start, ATT_QT, stride=stride)
            o32[g, rows, :] = o
            l32[g, rows, :] = jnp.broadcast_to(m + jnp.log(den), (ATT_QT, HEAD_DIM))

        q_ref, k_ref, v_ref = qkv[g]
        stage = None if dil == 1 else (q32, k32, v32, qres[g])
        _banded_attention(q_ref.at[0, 0], k_ref.at[0, 0], v_ref.at[0, 0], bias_scr, kpad, vpad, stage, emit,
                          dil=dil, hw=B_HALF_SPAN, sink=None)

    def merge(c, carry):
        rows = pl.ds(pl.multiple_of(c * MERGE_ROWS, MERGE_ROWS), MERGE_ROWS)
        lses = [l32[g, rows, :] for g in range(n_g)]
        mx = functools.reduce(jnp.maximum, lses)
        es = [jnp.exp(l - mx) for l in lses]
        den = functools.reduce(jnp.add, es)
        o_ref[0, 0, rows, :] = functools.reduce(
            jnp.add, [(e / den) * o32[g, rows, :] for g, e in enumerate(es)]).astype(o_ref.dtype)
        return carry

    lax.fori_loop(0, SEQ // MERGE_ROWS, merge, 0)


def _dilated_attention(proj, rel_bias):
    pv = proj.reshape(IN_CB, BATCH, SEQ, HEAD_DIM)
    hw = B_HALF_SPAN
    blk = (1, 1, SEQ, HEAD_DIM)
    in_specs = [pl.BlockSpec(memory_space=pltpu.SMEM)]
    args = [rel_bias]
    for g in range(B_GROUPS):
        for cb in (QB_CB, KB_CB, VB_CB):
            in_specs.append(pl.BlockSpec(blk, lambda b, h, c=cb + g * B_HEADS_PER_GROUP: (c + h, b, 0, 0)))
            args.append(pv)
    for dil in B_DILATIONS:
        in_specs.append(pl.BlockSpec((ATT_QT, ATT_QT + 2 * hw), lambda b, h: (0, 0)))
        args.append(jnp.asarray(_bucket_matrix(hw, dil)))
    scratch = []
    for dil in B_DILATIONS:
        scratch += _attn_scratch(dil, hw)
    scratch += [pltpu.VMEM((SEQ, HEAD_DIM), F32)] * 3
    scratch += [pltpu.VMEM((B_GROUPS, SEQ, HEAD_DIM), F32)] * 2
    scratch += [pltpu.VMEM((_residues_per_trip(dil), SEQ // dil, HEAD_DIM), BF16) for dil in B_DILATIONS]
    out = pl.pallas_call(
        _dilated_kernel,
        out_shape=jax.ShapeDtypeStruct((B_HEADS_PER_GROUP, BATCH, SEQ, HEAD_DIM), BF16),
        grid=(BATCH, B_HEADS_PER_GROUP),
        in_specs=in_specs,
        out_specs=pl.BlockSpec(blk, lambda b, h: (h, b, 0, 0)),
        scratch_shapes=scratch,
        compiler_params=_params(("parallel", "arbitrary"), 52),
        name="attn_dilated",
    )(*args)
    return out.reshape(B_HEADS_PER_GROUP, N_TOK, HEAD_DIM)


def _outproj_kernel(oa_ref, ob_ref, ga_ref, gb_ref, x_ref, woa_ref, wob_ref, wout_ref, gn_ref,
                    x1_ref, h2_ref):
    wide = lambda ref: jnp.concatenate([ref[c] for c in range(ref.shape[0])], axis=-1)
    ya = jnp.dot(wide(oa_ref), woa_ref[...], preferred_element_type=F32)
    yb = jnp.dot(wide(ob_ref), wob_ref[...], preferred_element_type=F32)
    merged = (jax.nn.sigmoid(wide(ga_ref).astype(F32)) * ya
              + jax.nn.sigmoid(wide(gb_ref).astype(F32)) * yb)
    x1 = x_ref[...] + jnp.dot(merged.astype(BF16), wout_ref[...], preferred_element_type=F32)
    x1_ref[...] = x1
    h2_ref[...] = _rms(x1, gn_ref[...]).astype(BF16)


def _outproj(oa, ob, proj, x2, w_oa, w_ob, w_out, g_ffn, tm=512):
    row = lambda w: pl.BlockSpec((tm, w), lambda i: (i, 0))
    slabs = lambda n, first: pl.BlockSpec((n, tm, LANES), lambda i: (first // n, i, 0))
    gate_cb = D_MODEL // LANES
    const = lambda shape: pl.BlockSpec(shape, lambda i: (0, 0), pipeline_mode=pl.Buffered(1))
    return pl.pallas_call(
        _outproj_kernel,
        out_shape=[jax.ShapeDtypeStruct((N_TOK, D_MODEL), F32),
                   jax.ShapeDtypeStruct((N_TOK, D_MODEL), BF16)],
        grid=(N_TOK // tm,),
        in_specs=[slabs(A_Q_HEADS, 0), slabs(B_HEADS_PER_GROUP, 0),
                  slabs(gate_cb, GA_OFF // LANES), slabs(gate_cb, GB_OFF // LANES),
                  row(D_MODEL),
                  const((A_Q_W, D_MODEL)), const((B_OUT_W, D_MODEL)), const((D_MODEL, D_MODEL)),
                  const((1, D_MODEL))],
        out_specs=[row(D_MODEL), row(D_MODEL)],
        compiler_params=_params(("parallel",), 56),
        name="outproj",
    )(oa, ob, proj, proj, x2, w_oa, w_ob, w_out, g_ffn)


_FAR = 1024.0
RT_ROWS = 512
RT_SLICE = 256
RT_HEADS = 4
UP_CHUNK = 256
UP_ROWS = 256

_LANE_GATHER = lax.GatherDimensionNumbers(
    offset_dims=(), collapsed_slice_dims=(1,), start_index_map=(1,),
    operand_batching_dims=(0,), start_indices_batching_dims=(0,))


def _tree(op, xs):
    xs = list(xs)
    while len(xs) > 1:
        nxt = [op(xs[i], xs[i + 1]) for i in range(0, len(xs) - 1, 2)]
        if len(xs) % 2:
            nxt.append(xs[-1])
        xs = nxt
    return xs[0]


def _all_sublanes(op, x):
    for shift in (4, 2, 1):
        x = op(x, pltpu.roll(x, shift, axis=0))
    return x


def _pop_max(tiles, sub_iota, payload=None):
    m = _all_sublanes(jnp.maximum, _tree(jnp.maximum, tiles))
    first = _tree(jnp.minimum, [jnp.where(t == m, float(SUBLANES * v), _FAR) for v, t in enumerate(tiles)])
    row = _all_sublanes(jnp.minimum, first + sub_iota)
    base = row - sub_iota
    hits = [base == float(SUBLANES * v) for v in range(len(tiles))]
    if payload is None:
        val = row
    else:
        val = _all_sublanes(jnp.maximum, _tree(jnp.maximum, [jnp.where(h, p, -1.0)
                                                            for h, p in zip(hits, payload)]))
    return m, val, [jnp.where(h, -jnp.inf, t) for h, t in zip(hits, tiles)]


def _route_scores(h, q_ref, keys_ref, s_scr):
    for c in range(2):
        col = pl.multiple_of((2 * h + c) * LANES, LANES)
        s_scr[c] = lax.dot_general(keys_ref[2 * h + c], q_ref[:, pl.ds(col, LANES)],
                                   (((1,), (1,)), ((), ())), preferred_element_type=F32)


def _batcher_pairs(n):
    size = 16
    pairs = []
    p = 1
    while p < size:
        k = p
        while k >= 1:
            for j in range(k % p, size - k, 2 * k):
                for i in range(min(k, size - j - k)):
                    if (i + j) // (2 * p) == (i + j + k) // (2 * p) and i + j + k < n:
                        pairs.append((i + j, i + j + k))
            k //= 2
        p *= 2
    return pairs


_FAR_ID = 1e9


def _top_sorted(vals, ids, k):
    vals, ids = list(vals), list(ids)
    n = len(vals)
    for i, j in _batcher_pairs(n):
        up = vals[j] > vals[i]
        vals[i], vals[j] = jnp.where(up, vals[j], vals[i]), jnp.where(up, vals[i], vals[j])
        ids[i], ids[j] = jnp.where(up, ids[j], ids[i]), jnp.where(up, ids[i], ids[j])
    top_v, top_i = [], []
    for it in range(k):
        m = _all_sublanes(jnp.maximum, vals[0])
        first = _all_sublanes(jnp.minimum, jnp.where(vals[0] == m, ids[0], _FAR_ID))
        win = ids[0] == first
        top_v.append(m)
        top_i.append(first)
        for r in range(min(k - it, n)):
            if r + 1 < n:
                vals[r] = jnp.where(win, vals[r + 1], vals[r])
                ids[r] = jnp.where(win, ids[r + 1], ids[r])
            else:
                vals[r] = jnp.where(win, -jnp.inf, vals[r])
    runner_up = _all_sublanes(jnp.maximum, vals[0])
    tie = jnp.zeros_like(runner_up)
    for a, b in zip(top_v, top_v[1:] + [runner_up]):
        tie = jnp.where(a == b, 1.0, tie)
    return top_v, top_i, tie


def _route_topk(s_scr, ts_scr, ti_scr, bs_scr, et_ref, gt_ref, *, exact):
    k = PEER_TOPK
    nk = float(PEER_NKEYS)
    sub_iota = lax.broadcasted_iota(I32, (SUBLANES, LANES), 0).astype(F32)
    halves = [slice(i * LANES, (i + 1) * LANES) for i in range(RT_SLICE // LANES)]
    tie = None if exact else jnp.zeros((SUBLANES, LANES), F32)

    chains = []
    for c in range(2):
        for lanes in halves:
            tiles = [s_scr[c, SUBLANES * v:SUBLANES * (v + 1), lanes] for v in range(PEER_NKEYS // SUBLANES)]
            chains.append([c, lanes, tiles])
    if exact:
        for kk in range(k):
            for chain in chains:
                c, lanes, tiles = chain
                m, row, chain[2] = _pop_max(tiles, sub_iota)
                ts_scr[c, kk:kk + 1, lanes] = m[0:1]
                ti_scr[c, kk:kk + 1, lanes] = row[0:1]
    else:
        row_ids = [float(SUBLANES * v) + sub_iota for v in range(PEER_NKEYS // SUBLANES)]
        for c, lanes, tiles in chains:
            top_v, top_i, t = _top_sorted(tiles, row_ids, k)
            tie = jnp.maximum(tie, t)
            for kk in range(k):
                ts_scr[c, kk:kk + 1, lanes] = top_v[kk][0:1]
                ti_scr[c, kk:kk + 1, lanes] = top_i[kk][0:1]

    def candidates(lanes):
        s2 = [ts_scr[1, 0:8, lanes], ts_scr[1, 8:16, lanes]]
        i2 = [ti_scr[1, 0:8, lanes], ti_scr[1, 8:16, lanes]]
        s1 = lambda k1: ts_scr[0, k1:k1 + 1, lanes]
        i1 = lambda k1: ti_scr[0, k1:k1 + 1, lanes] * nk
        tiles = [s1(0) + s2[0], s1(0) + s2[1]]
        pay = [i1(0) + i2[0], i1(0) + i2[1]]
        for k1 in range(1, 8):
            allowed = k // (k1 + 1)
            t = s1(k1) + s2[0]
            tiles.append(t if allowed >= SUBLANES else jnp.where(sub_iota < float(allowed), t, -jnp.inf))
            pay.append(i1(k1) + i2[0])
        tiles.append(ts_scr[0, 8:16, lanes] + ts_scr[1, 0:1, lanes])
        pay.append(ti_scr[0, 8:16, lanes] * nk + ti_scr[1, 0:1, lanes])
        return tiles, pay

    cands = [candidates(lanes) for lanes in halves]
    if exact:
        tiles = [c[0] for c in cands]
        for kk in range(k):
            for i, lanes in enumerate(halves):
                m, expert, tiles[i] = _pop_max(tiles[i], sub_iota, payload=cands[i][1])
                bs_scr[kk:kk + 1, lanes] = m[0:1]
                et_ref[kk:kk + 1, lanes] = expert[0:1]
    else:
        for (tiles, pay), lanes in zip(cands, halves):
            top_v, top_i, t = _top_sorted(tiles, pay, k)
            tie = jnp.maximum(tie, t)
            for kk in range(k):
                bs_scr[kk:kk + 1, lanes] = top_v[kk][0:1]
                et_ref[kk:kk + 1, lanes] = top_i[kk][0:1]
    for lanes in halves:
        bs = bs_scr[:, lanes]
        ex = jnp.exp(bs - jnp.max(bs, axis=0, keepdims=True))
        gt_ref[:, lanes] = ex / jnp.sum(ex, axis=0, keepdims=True)
    return tie


def _expert_up(j, h_ref, u_ref, e_ref, act_ref, *, te):
    tb = h_ref.shape[0]
    u = [u_ref[c * UP_CHUNK:(c + 1) * UP_CHUNK, :].astype(BF16) for c in range(te // UP_CHUNK)]
    for m in range(tb // UP_ROWS):
        rows = slice(m * UP_ROWS, (m + 1) * UP_ROWS)
        e = e_ref[rows, :]
        row = e >> PEER_KEY_BITS
        col = (e & (PEER_NKEYS - 1))[..., None]
        acc = act_ref[rows, :]
        h = h_ref[rows, :]
        for c in range(te // UP_CHUNK):
            dense = lax.dot_general(h, u[c], (((1,), (1,)), ((), ())), preferred_element_type=F32)
            for q in range(UP_CHUNK // LANES):
                got = lax.gather(dense[:, q * LANES:(q + 1) * LANES], col, _LANE_GATHER, (1, 1),
                                 mode=lax.GatherScatterMode.PROMISE_IN_BOUNDS)
                acc = jnp.where(row == (j * te + c * UP_CHUNK) // LANES + q, got, acc)
        act_ref[rows, :] = acc


def _route_kernel(h_ref, wq_ref, keys_ref, v_ref, e_ref, g_ref, vb_ref,
                  q_scr, s_scr, ts_scr, ti_scr, bs_scr, et_scr, gt_scr):
    vb_ref[...] = v_ref[...].astype(BF16)
    q_scr[...] = jnp.dot(h_ref[...], wq_ref[...], preferred_element_type=F32).astype(BF16)

    head_groups = PEER_HEADS // RT_HEADS

    def heads(g, carry):
        sl = g // head_groups
        q_rows = q_scr.at[pl.ds(pl.multiple_of(sl * RT_SLICE, RT_SLICE), RT_SLICE), :]
        hs = [(g % head_groups) * RT_HEADS + u for u in range(RT_HEADS)]
        for u, h in enumerate(hs):
            _route_scores(h, q_rows, keys_ref, s_scr.at[u])
        args = [(s_scr.at[u], ts_scr.at[u], ti_scr.at[u], bs_scr.at[u], et_scr.at[sl, h], gt_scr.at[sl, h])
                for u, h in enumerate(hs)]
        ties = [_route_topk(*a, exact=False) for a in args]
        for a, tie in zip(args, ties):
            @pl.when(jnp.max(tie) > 0.0)
            def _():
                _route_topk(*a, exact=True)

        return carry

    lax.fori_loop(0, (RT_ROWS // RT_SLICE) * head_groups, heads, 0)

    for sl in range(RT_ROWS // RT_SLICE):
        for half in range(RT_SLICE // LANES):
            lanes = slice(half * LANES, (half + 1) * LANES)
            rows = slice(sl * RT_SLICE + half * LANES, sl * RT_SLICE + (half + 1) * LANES)
            et = jnp.concatenate([et_scr[sl, h, :, lanes] for h in range(PEER_HEADS)], axis=0)
            gt = jnp.concatenate([gt_scr[sl, h, :, lanes] for h in range(PEER_HEADS)], axis=0)
            e_ref[rows, :] = et.T.astype(I32)
            g_ref[rows, :] = gt.T


def _route(h2, wq, keys, v):
    k = PEER_TOPK
    steps = N_TOK // RT_ROWS
    tok = pl.BlockSpec((RT_ROWS, PEER_PICKS), lambda i: (i, 0))
    v_slab = pl.BlockSpec((PEER_EXPERTS // steps, D_MODEL), lambda i: (i, 0))
    return pl.pallas_call(
        _route_kernel,
        out_shape=[jax.ShapeDtypeStruct((N_TOK, PEER_PICKS), I32),
                   jax.ShapeDtypeStruct((N_TOK, PEER_PICKS), F32),
                   jax.ShapeDtypeStruct((PEER_EXPERTS, D_MODEL), BF16)],
        grid=(steps,),
        in_specs=[pl.BlockSpec((RT_ROWS, D_MODEL), lambda i: (i, 0)),
                  pl.BlockSpec((D_MODEL, PEER_HEADS * PEER_QDIM), lambda i: (0, 0),
                               pipeline_mode=pl.Buffered(1)),
                  pl.BlockSpec((PEER_HEADS * 2, PEER_NKEYS, PEER_QDIM // 2), lambda i: (0, 0, 0)),
                  v_slab],
        out_specs=[tok, tok, v_slab],
        scratch_shapes=[
            pltpu.VMEM((RT_ROWS, PEER_HEADS * PEER_QDIM), BF16),
            pltpu.VMEM((RT_HEADS, 2, PEER_NKEYS, RT_SLICE), F32),
            pltpu.VMEM((RT_HEADS, 2, k, RT_SLICE), F32),
            pltpu.VMEM((RT_HEADS, 2, k, RT_SLICE), F32),
            pltpu.VMEM((RT_HEADS, k, RT_SLICE), F32),
            pltpu.VMEM((RT_ROWS // RT_SLICE, PEER_HEADS, k, RT_SLICE), F32),
            pltpu.VMEM((RT_ROWS // RT_SLICE, PEER_HEADS, k, RT_SLICE), F32),
        ],
        compiler_params=_params(("parallel",), 52),
        name="peer_route",
    )(h2, wq, keys, v)


def _peer_up_kernel(h_ref, u_ref, e_ref, act_ref, *, te):
    j = pl.program_id(1)

    @pl.when(j == 0)
    def _():
        act_ref[...] = jnp.zeros_like(act_ref)

    _expert_up(j, h_ref, u_ref, e_ref, act_ref, te=te)


def _peer_up(h2, u, e, tb=2048, te=1024):
    return pl.pallas_call(
        functools.partial(_peer_up_kernel, te=te),
        out_shape=jax.ShapeDtypeStruct((N_TOK, PEER_PICKS), F32),
        grid=(N_TOK // tb, PEER_EXPERTS // te),
        in_specs=[pl.BlockSpec((tb, D_MODEL), lambda i, j: (i, 0)),
                  pl.BlockSpec((te, D_MODEL), lambda i, j: (j, 0)),
                  pl.BlockSpec((tb, PEER_PICKS), lambda i, j: (i, 0))],
        out_specs=pl.BlockSpec((tb, PEER_PICKS), lambda i, j: (i, 0)),
        compiler_params=_params(("parallel", "arbitrary"), 56),
        name="peer_up",
    )(h2, u, e)


def _peer_coef_kernel(e_ref, g_ref, act_ref, p_ref, row_scr, col_scr, w_scr, c_scr, *, tp):
    e = e_ref[...]
    row_scr[...] = (e >> PEER_KEY_BITS).astype(F32)
    col_scr[...] = (e & (PEER_NKEYS - 1)).astype(F32)
    act = act_ref[...]
    w_scr[...] = g_ref[...] * (0.5 * act * (1.0 + lax.erf(act * math.sqrt(0.5))))
    sub = 16
    iota = lax.broadcasted_iota(I32, (PEER_NKEYS // sub, sub, PEER_PICKS), 0) * sub \
        + lax.broadcasted_iota(I32, (PEER_NKEYS // sub, sub, PEER_PICKS), 1)
    iota = iota.astype(F32).astype(BF16)
    one = jnp.ones((), BF16)
    zero = jnp.zeros((), BF16)

    def bcast(ref, t):
        return jnp.broadcast_to(ref[pl.ds(t, 1), :], (sub, PEER_PICKS)).astype(BF16)[None]

    def products(t0):
        for t in range(sub):
            rows, cols, w = bcast(row_scr, t0 + t), bcast(col_scr, t0 + t), bcast(w_scr, t0 + t)
            left = jnp.where(iota == rows, one, zero).reshape(PEER_NKEYS, PEER_PICKS)
            right = jnp.where(iota == cols, w, zero).reshape(PEER_NKEYS, PEER_PICKS)
            c_scr[t0 + t] = lax.dot_general(left, right, (((1,), (1,)), ((), ())),
                                            preferred_element_type=F32)

    def regroup(t0):
        pairs = jnp.stack([pltpu.pack_elementwise([c_scr[t0 + t], c_scr[t0 + t + 1]], packed_dtype=BF16)
                           for t in range(0, sub, 2)])
        sw = jnp.swapaxes(pairs, 0, 1)
        for k1 in range(PEER_NKEYS):
            p_ref[pl.ds(t0, sub), k1 * PEER_NKEYS:(k1 + 1) * PEER_NKEYS] = pltpu.bitcast(sw[k1], p_ref.dtype)

    def block(blk, carry):
        base = pl.multiple_of(blk * COEF_UNROLL, COEF_UNROLL)
        for t0 in range(0, COEF_UNROLL, sub):
            products(base + t0)
            if t0:
                regroup(base + t0 - sub)
        regroup(base + COEF_UNROLL - sub)
        return carry

    lax.fori_loop(0, tp // COEF_UNROLL, block, 0)


COEF_UNROLL = 64


def _peer_coef(e, g, act, tp=128):
    tok = pl.BlockSpec((tp, PEER_PICKS), lambda i: (i, 0))
    return pl.pallas_call(
        functools.partial(_peer_coef_kernel, tp=tp),
        out_shape=jax.ShapeDtypeStruct((N_TOK, PEER_EXPERTS), BF16),
        grid=(N_TOK // tp,),
        in_specs=[tok, tok, tok],
        out_specs=pl.BlockSpec((tp, PEER_EXPERTS), lambda i: (i, 0)),
        scratch_shapes=[pltpu.VMEM((tp, PEER_PICKS), F32),
                        pltpu.VMEM((tp, PEER_PICKS), F32),
                        pltpu.VMEM((tp, PEER_PICKS), F32),
                        pltpu.VMEM((tp, PEER_NKEYS, PEER_NKEYS), F32)],
        compiler_params=_params(("parallel",), 32),
        name="peer_coef",
    )(e, g, act)


def _peer_down_kernel(p_ref, v_ref, x_ref, g_ref, o_ref):
    j = pl.program_id(1)

    @pl.when(j == 0)
    def _():
        o_ref[...] = jnp.zeros_like(o_ref)

    o_ref[...] += jnp.dot(p_ref[...], v_ref[...], preferred_element_type=F32)
    cols = pl.ds(pl.multiple_of(j * LANES, LANES), LANES)
    o_ref[:, cols] += x_ref[...]

    @pl.when(j == pl.num_programs(1) - 1)
    def _():
        o_ref[...] = _rms(o_ref[...], g_ref[...])


def _peer_down(coef, v, x1, g_final, tm=2048):
    tk = PEER_EXPERTS // (D_MODEL // LANES)
    return pl.pallas_call(
        _peer_down_kernel,
        out_shape=jax.ShapeDtypeStruct((N_TOK, D_MODEL), F32),
        grid=(N_TOK // tm, PEER_EXPERTS // tk),
        in_specs=[pl.BlockSpec((tm, tk), lambda i, j: (i, j)),
                  pl.BlockSpec((tk, D_MODEL), lambda i, j: (j, 0)),
                  pl.BlockSpec((tm, LANES), lambda i, j: (i, j)),
                  pl.BlockSpec((1, D_MODEL), lambda i, j: (0, 0))],
        out_specs=pl.BlockSpec((tm, D_MODEL), lambda i, j: (i, 0)),
        compiler_params=_params(("parallel", "arbitrary"), 60),
        name="peer_down",
    )(coef, v, x1, g_final)


def kernel(x, rel_bias, norm_mix_g, w_in, sink_a, w_oa, w_ob, w_out, norm_ffn_g,
           peer_wq, peer_keys, peer_u, peer_v, norm_final_g):
    assert x.shape == (BATCH, SEQ, D_MODEL) and w_in.shape[0] == 1
    x2 = x.reshape(N_TOK, D_MODEL)
    proj = _inproj(x2, norm_mix_g, w_in[0].astype(BF16))

    oa = _windowed_attention(proj, rel_bias, sink_a[0])
    ob = _dilated_attention(proj, rel_bias)
    x1, h2 = _outproj(oa, ob, proj, x2, w_oa[0].astype(BF16), w_ob[0].astype(BF16),
                      w_out[0].astype(BF16), norm_ffn_g)

    keys = peer_keys[0].reshape(PEER_HEADS * 2, PEER_NKEYS, PEER_QDIM // 2).astype(BF16)
    e, gate, v_bf16 = _route(h2, peer_wq[0].astype(BF16), keys, peer_v[0])
    act = _peer_up(h2, peer_u[0], e)
    coef = _peer_coef(e, gate, act)
    out = _peer_down(coef, v_bf16, x1, norm_final_g.reshape(1, D_MODEL))
    return out.reshape(BATCH, SEQ, D_MODEL)
```

```python
import functools
import math

import numpy as np
import jax
import jax.numpy as jnp
from jax import lax
from jax.experimental import pallas as pl
from jax.experimental.pallas import tpu as pltpu

F32 = jnp.float32
BF16 = jnp.bfloat16
I32 = jnp.int32

D_MODEL = 2048
BATCH = 2
SEQ = 4096
N_TOK = BATCH * SEQ
HEAD_DIM = 128
LANES = 128
SUBLANES = 8

A_Q_HEADS = 8
A_KV_HEADS = 2
A_GROUP = A_Q_HEADS // A_KV_HEADS
A_HALF_WINDOW = 128
B_DILATIONS = (1, 4, 16)
B_GROUPS = 3
B_HEADS_PER_GROUP = 4
B_HALF_SPAN = 64
N_BUCKETS = 32
MAX_DISTANCE = 1024
N_ATTN_HEADS = A_Q_HEADS + B_GROUPS * B_HEADS_PER_GROUP

PEER_HEADS = 8
PEER_NKEYS = 128
PEER_KEY_BITS = PEER_NKEYS.bit_length() - 1
assert 1 << PEER_KEY_BITS == PEER_NKEYS
PEER_EXPERTS = PEER_NKEYS * PEER_NKEYS
PEER_QDIM = 256
PEER_TOPK = 16
PEER_PICKS = PEER_HEADS * PEER_TOPK
EPS = 1e-6

A_Q_W = A_Q_HEADS * HEAD_DIM
A_KV_W = A_KV_HEADS * HEAD_DIM
B_W = B_GROUPS * B_HEADS_PER_GROUP * HEAD_DIM
B_OUT_W = B_HEADS_PER_GROUP * HEAD_DIM
IN_WIDTH = A_Q_W + 2 * A_KV_W + 3 * B_W + 2 * D_MODEL
QA_CB = 0
KA_CB = A_Q_W // LANES
VA_CB = (A_Q_W + A_KV_W) // LANES
QB_CB = (A_Q_W + 2 * A_KV_W) // LANES
KB_CB = QB_CB + B_W // LANES
VB_CB = KB_CB + B_W // LANES
GA_OFF = A_Q_W + 2 * A_KV_W + 3 * B_W
GB_OFF = GA_OFF + D_MODEL
IN_CB = IN_WIDTH // LANES

NEG = -1e30
ATT_SCALE = HEAD_DIM ** -0.5
ATT_QT = 128
ATT_GROUP = 8

MIB = 1024 * 1024


def _params(sem, vmem_mib):
    return pltpu.CompilerParams(dimension_semantics=sem, vmem_limit_bytes=vmem_mib * MIB)


def _rms(x, g):
    return x * lax.rsqrt(jnp.mean(x * x, axis=-1, keepdims=True) + EPS) * g


def _inproj_kernel(x_ref, g_ref, w_ref, o_ref, h_scr):
    @pl.when(pl.program_id(1) == 0)
    def _():
        h_scr[...] = _rms(x_ref[...], g_ref[...]).astype(BF16)

    res = jnp.dot(h_scr[...], w_ref[...], preferred_element_type=F32)
    for c in range(o_ref.shape[0]):
        o_ref[c] = res[:, c * LANES:(c + 1) * LANES].astype(o_ref.dtype)


def _inproj(x2, g, w, tm=1024, tn=2048):
    n = w.shape[1]
    return pl.pallas_call(
        _inproj_kernel,
        out_shape=jax.ShapeDtypeStruct((n // LANES, N_TOK, LANES), BF16),
        grid=(N_TOK // tm, n // tn),
        in_specs=[
            pl.BlockSpec((tm, D_MODEL), lambda i, j: (i, 0)),
            pl.BlockSpec((1, D_MODEL), lambda i, j: (0, 0)),
            pl.BlockSpec((D_MODEL, tn), lambda i, j: (0, j)),
        ],
        out_specs=pl.BlockSpec((tn // LANES, tm, LANES), lambda i, j: (j, i, 0)),
        scratch_shapes=[pltpu.VMEM((tm, D_MODEL), BF16)],
        compiler_params=_params(("parallel", "arbitrary"), 52),
        name="inproj",
    )(x2, g, w)


def _t5_bucket_np(rel):
    half = N_BUCKETS // 2
    max_exact = half // 2
    ret = np.where(rel > 0, half, 0)
    n = np.abs(rel)
    nf = np.maximum(n, 1).astype(np.float64)
    large = max_exact + (np.log(nf / max_exact) / math.log(MAX_DISTANCE / max_exact) * (half - max_exact)).astype(np.int64)
    large = np.minimum(large, half - 1)
    return (ret + np.where(n < max_exact, n, large)).astype(np.int32)


def _bucket_matrix(hw, dist_scale):
    w = ATT_QT + 2 * hw
    delta = (np.arange(w)[None, :] - hw) - np.arange(ATT_QT)[:, None]
    bkt = _t5_bucket_np(delta * dist_scale)
    return np.where(np.abs(delta) <= hw, bkt, -1).astype(np.int32)


def _residues_per_trip(dil):
    tiles = SEQ // dil // ATT_QT
    return max(1, min(dil, ATT_GROUP // tiles))


def _bias_tile(bkt_ref, tab_ref, head, bias_scr):
    bkt = bkt_ref[...]
    bias = jnp.full(bkt.shape, NEG, F32)
    for b in range(N_BUCKETS):
        bias = jnp.where(bkt == b, tab_ref[b, head], bias)
    bias_scr[...] = bias


def _banded_attention(q_ref, k_ref, v_ref, bias_scr, kpad, vpad, stage, emit, *, dil, hw, sink):
    seq = SEQ // dil
    qt = ATT_QT
    win = qt + 2 * hw
    n_res = _residues_per_trip(dil)

    zeros = jnp.zeros((hw, HEAD_DIM), BF16)
    for pad_ref in (kpad, vpad):
        for u in range(n_res):
            pad_ref[u, 0:hw, :] = zeros
            pad_ref[u, hw + seq:hw + seq + hw, :] = zeros
    if dil == 1:
        kpad[0, hw:hw + seq, :] = k_ref[...]
        vpad[0, hw:hw + seq, :] = v_ref[...]
    else:
        q32, k32, v32, qres = stage
        q32[...] = q_ref[...].astype(F32)
        k32[...] = k_ref[...].astype(F32)
        v32[...] = v_ref[...].astype(F32)

    def scores(q0, q, kw):
        s = lax.dot_general(q, kw, (((1,), (1,)), ((), ())), preferred_element_type=F32)
        s = s * ATT_SCALE + bias_scr[...]
        kpos = q0 - hw + lax.broadcasted_iota(I32, (qt, win), 1)
        return jnp.where((kpos >= 0) & (kpos < seq), s, NEG)

    def softmax(s):
        m = jnp.max(s, axis=-1, keepdims=True)
        if sink is not None:
            m = jnp.maximum(m, sink)
        p = jnp.exp(s - m)
        den = jnp.sum(p, axis=-1, keepdims=True)
        if sink is not None:
            den = den + jnp.exp(sink - m)
        return p.astype(BF16), m, den

    group = min(ATT_GROUP, seq // qt)

    def residues(g, carry):
        r0 = g * n_res
        if dil > 1:
            for u in range(n_res):
                kpad[u, hw:hw + seq, :] = k32[pl.ds(r0 + u, seq, stride=dil), :].astype(BF16)
                vpad[u, hw:hw + seq, :] = v32[pl.ds(r0 + u, seq, stride=dil), :].astype(BF16)
                qres[u] = q32[pl.ds(r0 + u, seq, stride=dil), :].astype(BF16)

        def body(t, c):
            work = [(u, pl.multiple_of((t * group + i) * qt, qt)) for u in range(n_res) for i in range(group)]
            q_src = (lambda u: qres.at[u]) if dil > 1 else (lambda u: q_ref)
            s_all = [scores(q0, q_src(u)[pl.ds(q0, qt), :], kpad[u, pl.ds(q0, win), :]) for u, q0 in work]
            p_all = [softmax(s) for s in s_all]
            o_all = [jnp.dot(p, vpad[u, pl.ds(q0, win), :], preferred_element_type=F32) / den
                     for (u, q0), (p, m, den) in zip(work, p_all)]
            for (u, q0), (p, m, den), o in zip(work, p_all, o_all):
                emit(q0 if dil == 1 else r0 + u + q0 * dil, dil, o, m, den)
            return c

        lax.fori_loop(0, seq // (qt * group), body, 0)
        return carry

    if dil > 1:
        lax.fori_loop(0, dil // n_res, residues, 0)
    else:
        residues(0, 0)


def _attn_scratch(dil, hw):
    n_res = _residues_per_trip(dil)
    seq = SEQ // dil
    return [pltpu.VMEM((n_res, seq + 2 * hw, HEAD_DIM), BF16),
            pltpu.VMEM((n_res, seq + 2 * hw, HEAD_DIM), BF16),
            pltpu.VMEM((ATT_QT, ATT_QT + 2 * hw), F32)]


def _windowed_kernel(tab_ref, sink_ref, q_ref, k_ref, v_ref, bkt_ref, o_ref, kpad, vpad, bias_scr):
    head = pl.program_id(1)
    _bias_tile(bkt_ref, tab_ref, head, bias_scr)

    def emit(start, stride, o, m, den):
        o_ref[0, 0, pl.ds(start, ATT_QT), :] = o.astype(o_ref.dtype)

    _banded_attention(q_ref.at[0, 0], k_ref.at[0, 0], v_ref.at[0, 0], bias_scr, kpad, vpad, None, emit,
                      dil=1, hw=A_HALF_WINDOW, sink=sink_ref[head])


def _windowed_attention(proj, rel_bias, sink):
    pv = proj.reshape(IN_CB, BATCH, SEQ, HEAD_DIM)
    hw = A_HALF_WINDOW
    blk = (1, 1, SEQ, HEAD_DIM)
    out = pl.pallas_call(
        _windowed_kernel,
        out_shape=jax.ShapeDtypeStruct((A_Q_HEADS, BATCH, SEQ, HEAD_DIM), BF16),
        grid=(BATCH, A_Q_HEADS),
        in_specs=[
            pl.BlockSpec(memory_space=pltpu.SMEM),
            pl.BlockSpec(memory_space=pltpu.SMEM),
            pl.BlockSpec(blk, lambda b, h: (QA_CB + h, b, 0, 0)),
            pl.BlockSpec(blk, lambda b, h: (KA_CB + h // A_GROUP, b, 0, 0)),
            pl.BlockSpec(blk, lambda b, h: (VA_CB + h // A_GROUP, b, 0, 0)),
            pl.BlockSpec((ATT_QT, ATT_QT + 2 * hw), lambda b, h: (0, 0)),
        ],
        out_specs=pl.BlockSpec(blk, lambda b, h: (h, b, 0, 0)),
        scratch_shapes=_attn_scratch(1, hw),
        compiler_params=_params(("parallel", "arbitrary"), 32),
        name="attn_windowed",
    )(rel_bias, sink, pv, pv, pv, jnp.asarray(_bucket_matrix(hw, 1)))
    return out.reshape(A_Q_HEADS, N_TOK, HEAD_DIM)


MERGE_ROWS = 512


def _dilated_kernel(tab_ref, *refs):
    n_g = B_GROUPS
    qkv = [refs[3 * g:3 * g + 3] for g in range(n_g)]
    bkts = refs[3 * n_g:4 * n_g]
    o_ref = refs[4 * n_g]
    scr = list(refs[4 * n_g + 1:])
    pads = [scr[3 * g:3 * g + 3] for g in range(n_g)]
    q32, k32, v32, o32, l32 = scr[3 * n_g:3 * n_g + 5]
    qres = scr[3 * n_g + 5:]
    slot = pl.program_id(1)

    for g, dil in enumerate(B_DILATIONS):
        kpad, vpad, bias_scr = pads[g]
        _bias_tile(bkts[g], tab_ref, A_Q_HEADS + g * B_HEADS_PER_GROUP + slot, bias_scr)

        def emit(start, stride, o, m, den, g=g):
            rows = pl.ds(start, ATT_QT) if stride == 1 else pl.ds(start, ATT_QT, stride=stride)
            o32[g, rows, :] = o
            l32[g, rows, :] = jnp.broadcast_to(m + jnp.log(den), (ATT_QT, HEAD_DIM))

        q_ref, k_ref, v_ref = qkv[g]
        stage = None if dil == 1 else (q32, k32, v32, qres[g])
        _banded_attention(q_ref.at[0, 0], k_ref.at[0, 0], v_ref.at[0, 0], bias_scr, kpad, vpad, stage, emit,
                          dil=dil, hw=B_HALF_SPAN, sink=None)

    def merge(c, carry):
        rows = pl.ds(pl.multiple_of(c * MERGE_ROWS, MERGE_ROWS), MERGE_ROWS)
        lses = [l32[g, rows, :] for g in range(n_g)]
        mx = functools.reduce(jnp.maximum, lses)
        es = [jnp.exp(l - mx) for l in lses]
        den = functools.reduce(jnp.add, es)
        o_ref[0, 0, rows, :] = functools.reduce(
            jnp.add, [(e / den) * o32[g, rows, :] for g, e in enumerate(es)]).astype(o_ref.dtype)
        return carry

    lax.fori_loop(0, SEQ // MERGE_ROWS, merge, 0)


def _dilated_attention(proj, rel_bias):
    pv = proj.reshape(IN_CB, BATCH, SEQ, HEAD_DIM)
    hw = B_HALF_SPAN
    blk = (1, 1, SEQ, HEAD_DIM)
    in_specs = [pl.BlockSpec(memory_space=pltpu.SMEM)]
    args = [rel_bias]
    for g in range(B_GROUPS):
        for cb in (QB_CB, KB_CB, VB_CB):
            in_specs.append(pl.BlockSpec(blk, lambda b, h, c=cb + g * B_HEADS_PER_GROUP: (c + h, b, 0, 0)))
            args.append(pv)
    for dil in B_DILATIONS:
        in_specs.append(pl.BlockSpec((ATT_QT, ATT_QT + 2 * hw), lambda b, h: (0, 0)))
        args.append(jnp.asarray(_bucket_matrix(hw, dil)))
    scratch = []
    for dil in B_DILATIONS:
        scratch += _attn_scratch(dil, hw)
    scratch += [pltpu.VMEM((SEQ, HEAD_DIM), F32)] * 3
    scratch += [pltpu.VMEM((B_GROUPS, SEQ, HEAD_DIM), F32)] * 2
    scratch += [pltpu.VMEM((_residues_per_trip(dil), SEQ // dil, HEAD_DIM), BF16) for dil in B_DILATIONS]
    out = pl.pallas_call(
        _dilated_kernel,
        out_shape=jax.ShapeDtypeStruct((B_HEADS_PER_GROUP, BATCH, SEQ, HEAD_DIM), BF16),
        grid=(BATCH, B_HEADS_PER_GROUP),
        in_specs=in_specs,
        out_specs=pl.BlockSpec(blk, lambda b, h: (h, b, 0, 0)),
        scratch_shapes=scratch,
        compiler_params=_params(("parallel", "arbitrary"), 52),
        name="attn_dilated",
    )(*args)
    return out.reshape(B_HEADS_PER_GROUP, N_TOK, HEAD_DIM)


def _outproj_kernel(oa_ref, ob_ref, ga_ref, gb_ref, x_ref, woa_ref, wob_ref, wout_ref, gn_ref,
                    x1_ref, h2_ref):
    wide = lambda ref: jnp.concatenate([ref[c] for c in range(ref.shape[0])], axis=-1)
    ya = jnp.dot(wide(oa_ref), woa_ref[...], preferred_element_type=F32)
    yb = jnp.dot(wide(ob_ref), wob_ref[...], preferred_element_type=F32)
    merged = (jax.nn.sigmoid(wide(ga_ref).astype(F32)) * ya
              + jax.nn.sigmoid(wide(gb_ref).astype(F32)) * yb)
    x1 = x_ref[...] + jnp.dot(merged.astype(BF16), wout_ref[...], preferred_element_type=F32)
    x1_ref[...] = x1
    h2_ref[...] = _rms(x1, gn_ref[...]).astype(BF16)


def _outproj(oa, ob, proj, x2, w_oa, w_ob, w_out, g_ffn, tm=512):
    row = lambda w: pl.BlockSpec((tm, w), lambda i: (i, 0))
    slabs = lambda n, first: pl.BlockSpec((n, tm, LANES), lambda i: (first // n, i, 0))
    gate_cb = D_MODEL // LANES
    const = lambda shape: pl.BlockSpec(shape, lambda i: (0, 0), pipeline_mode=pl.Buffered(1))
    return pl.pallas_call(
        _outproj_kernel,
        out_shape=[jax.ShapeDtypeStruct((N_TOK, D_MODEL), F32),
                   jax.ShapeDtypeStruct((N_TOK, D_MODEL), BF16)],
        grid=(N_TOK // tm,),
        in_specs=[slabs(A_Q_HEADS, 0), slabs(B_HEADS_PER_GROUP, 0),
                  slabs(gate_cb, GA_OFF // LANES), slabs(gate_cb, GB_OFF // LANES),
                  row(D_MODEL),
                  const((A_Q_W, D_MODEL)), const((B_OUT_W, D_MODEL)), const((D_MODEL, D_MODEL)),
                  const((1, D_MODEL))],
        out_specs=[row(D_MODEL), row(D_MODEL)],
        compiler_params=_params(("parallel",), 56),
        name="outproj",
    )(oa, ob, proj, proj, x2, w_oa, w_ob, w_out, g_ffn)


_FAR = 1024.0
RT_ROWS = 512
RT_SLICE = 256
RT_HEADS = 4
UP_CHUNK = 256
UP_ROWS = 256

_LANE_GATHER = lax.GatherDimensionNumbers(
    offset_dims=(), collapsed_slice_dims=(1,), start_index_map=(1,),
    operand_batching_dims=(0,), start_indices_batching_dims=(0,))


def _tree(op, xs):
    xs = list(xs)
    while len(xs) > 1:
        nxt = [op(xs[i], xs[i + 1]) for i in range(0, len(xs) - 1, 2)]
        if len(xs) % 2:
            nxt.append(xs[-1])
        xs = nxt
    return xs[0]


def _all_sublanes(op, x):
    for shift in (4, 2, 1):
        x = op(x, pltpu.roll(x, shift, axis=0))
    return x


def _pop_max(tiles, sub_iota, payload=None):
    m = _all_sublanes(jnp.maximum, _tree(jnp.maximum, tiles))
    first = _tree(jnp.minimum, [jnp.where(t == m, float(SUBLANES * v), _FAR) for v, t in enumerate(tiles)])
    row = _all_sublanes(jnp.minimum, first + sub_iota)
    base = row - sub_iota
    hits = [base == float(SUBLANES * v) for v in range(len(tiles))]
    if payload is None:
        val = row
    else:
        val = _all_sublanes(jnp.maximum, _tree(jnp.maximum, [jnp.where(h, p, -1.0)
                                                            for h, p in zip(hits, payload)]))
    return m, val, [jnp.where(h, -jnp.inf, t) for h, t in zip(hits, tiles)]


def _route_scores(h, q_ref, keys_ref, s_scr):
    for c in range(2):
        col = pl.multiple_of((2 * h + c) * LANES, LANES)
        s_scr[c] = lax.dot_general(keys_ref[2 * h + c], q_ref[:, pl.ds(col, LANES)],
                                   (((1,), (1,)), ((), ())), preferred_element_type=F32)


def _batcher_pairs(n):
    size = 16
    pairs = []
    p = 1
    while p < size:
        k = p
        while k >= 1:
            for j in range(k % p, size - k, 2 * k):
                for i in range(min(k, size - j - k)):
                    if (i + j) // (2 * p) == (i + j + k) // (2 * p) and i + j + k < n:
                        pairs.append((i + j, i + j + k))
            k //= 2
        p *= 2
    return pairs


_FAR_ID = 1e9


def _top_sorted(vals, ids, k):
    vals, ids = list(vals), list(ids)
    n = len(vals)
    for i, j in _batcher_pairs(n):
        up = vals[j] > vals[i]
        vals[i], vals[j] = jnp.where(up, vals[j], vals[i]), jnp.where(up, vals[i], vals[j])
        ids[i], ids[j] = jnp.where(up, ids[j], ids[i]), jnp.where(up, ids[i], ids[j])
    top_v, top_i = [], []
    for it in range(k):
        m = _all_sublanes(jnp.maximum, vals[0])
        first = _all_sublanes(jnp.minimum, jnp.where(vals[0] == m, ids[0], _FAR_ID))
        win = ids[0] == first
        top_v.append(m)
        top_i.append(first)
        for r in range(min(k - it, n)):
            if r + 1 < n:
                vals[r] = jnp.where(win, vals[r + 1], vals[r])
                ids[r] = jnp.where(win, ids[r + 1], ids[r])
            else:
                vals[r] = jnp.where(win, -jnp.inf, vals[r])
    runner_up = _all_sublanes(jnp.maximum, vals[0])
    tie = jnp.zeros_like(runner_up)
    for a, b in zip(top_v, top_v[1:] + [runner_up]):
        tie = jnp.where(a == b, 1.0, tie)
    return top_v, top_i, tie


def _route_topk(s_scr, ts_scr, ti_scr, bs_scr, et_ref, gt_ref, *, exact):
    k = PEER_TOPK
    nk = float(PEER_NKEYS)
    sub_iota = lax.broadcasted_iota(I32, (SUBLANES, LANES), 0).astype(F32)
    halves = [slice(i * LANES, (i + 1) * LANES) for i in range(RT_SLICE // LANES)]
    tie = None if exact else jnp.zeros((SUBLANES, LANES), F32)

    chains = []
    for c in range(2):
        for lanes in halves:
            tiles = [s_scr[c, SUBLANES * v:SUBLANES * (v + 1), lanes] for v in range(PEER_NKEYS // SUBLANES)]
            chains.append([c, lanes, tiles])
    if exact:
        for kk in range(k):
            for chain in chains:
                c, lanes, tiles = chain
                m, row, chain[2] = _pop_max(tiles, sub_iota)
                ts_scr[c, kk:kk + 1, lanes] = m[0:1]
                ti_scr[c, kk:kk + 1, lanes] = row[0:1]
    else:
        row_ids = [float(SUBLANES * v) + sub_iota for v in range(PEER_NKEYS // SUBLANES)]
        for c, lanes, tiles in chains:
            top_v, top_i, t = _top_sorted(tiles, row_ids, k)
            tie = jnp.maximum(tie, t)
            for kk in range(k):
                ts_scr[c, kk:kk + 1, lanes] = top_v[kk][0:1]
                ti_scr[c, kk:kk + 1, lanes] = top_i[kk][0:1]

    def candidates(lanes):
        s2 = [ts_scr[1, 0:8, lanes], ts_scr[1, 8:16, lanes]]
        i2 = [ti_scr[1, 0:8, lanes], ti_scr[1, 8:16, lanes]]
        s1 = lambda k1: ts_scr[0, k1:k1 + 1, lanes]
        i1 = lambda k1: ti_scr[0, k1:k1 + 1, lanes] * nk
        tiles = [s1(0) + s2[0], s1(0) + s2[1]]
        pay = [i1(0) + i2[0], i1(0) + i2[1]]
        for k1 in range(1, 8):
            allowed = k // (k1 + 1)
            t = s1(k1) + s2[0]
            tiles.append(t if allowed >= SUBLANES else jnp.where(sub_iota < float(allowed), t, -jnp.inf))
            pay.append(i1(k1) + i2[0])
        tiles.append(ts_scr[0, 8:16, lanes] + ts_scr[1, 0:1, lanes])
        pay.append(ti_scr[0, 8:16, lanes] * nk + ti_scr[1, 0:1, lanes])
        return tiles, pay

    cands = [candidates(lanes) for lanes in halves]
    if exact:
        tiles = [c[0] for c in cands]
        for kk in range(k):
            for i, lanes in enumerate(halves):
                m, expert, tiles[i] = _pop_max(tiles[i], sub_iota, payload=cands[i][1])
                bs_scr[kk:kk + 1, lanes] = m[0:1]
                et_ref[kk:kk + 1, lanes] = expert[0:1]
    else:
        for (tiles, pay), lanes in zip(cands, halves):
            top_v, top_i, t = _top_sorted(tiles, pay, k)
            tie = jnp.maximum(tie, t)
            for kk in range(k):
                bs_scr[kk:kk + 1, lanes] = top_v[kk][0:1]
                et_ref[kk:kk + 1, lanes] = top_i[kk][0:1]
    for lanes in halves:
        bs = bs_scr[:, lanes]
        ex = jnp.exp(bs - jnp.max(bs, axis=0, keepdims=True))
        gt_ref[:, lanes] = ex / jnp.sum(ex, axis=0, keepdims=True)
    return tie


def _expert_up(j, h_ref, u_ref, e_ref, act_ref, *, te):
    tb = h_ref.shape[0]
    u = [u_ref[c * UP_CHUNK:(c + 1) * UP_CHUNK, :].astype(BF16) for c in range(te // UP_CHUNK)]
    for m in range(tb // UP_ROWS):
        rows = slice(m * UP_ROWS, (m + 1) * UP_ROWS)
        e = e_ref[rows, :]
        row = e >> PEER_KEY_BITS
        col = (e & (PEER_NKEYS - 1))[..., None]
        acc = act_ref[rows, :]
        h = h_ref[rows, :]
        for c in range(te // UP_CHUNK):
            dense = lax.dot_general(h, u[c], (((1,), (1,)), ((), ())), preferred_element_type=F32)
            for q in range(UP_CHUNK // LANES):
                got = lax.gather(dense[:, q * LANES:(q + 1) * LANES], col, _LANE_GATHER, (1, 1),
                                 mode=lax.GatherScatterMode.PROMISE_IN_BOUNDS)
                acc = jnp.where(row == (j * te + c * UP_CHUNK) // LANES + q, got, acc)
        act_ref[rows, :] = acc


def _route_kernel(h_ref, wq_ref, keys_ref, v_ref, e_ref, g_ref, vb_ref,
                  q_scr, s_scr, ts_scr, ti_scr, bs_scr, et_scr, gt_scr):
    vb_ref[...] = v_ref[...].astype(BF16)
    q_scr[...] = jnp.dot(h_ref[...], wq_ref[...], preferred_element_type=F32).astype(BF16)

    head_groups = PEER_HEADS // RT_HEADS

    def heads(g, carry):
        sl = g // head_groups
        q_rows = q_scr.at[pl.ds(pl.multiple_of(sl * RT_SLICE, RT_SLICE), RT_SLICE), :]
        hs = [(g % head_groups) * RT_HEADS + u for u in range(RT_HEADS)]
        for u, h in enumerate(hs):
            _route_scores(h, q_rows, keys_ref, s_scr.at[u])
        args = [(s_scr.at[u], ts_scr.at[u], ti_scr.at[u], bs_scr.at[u], et_scr.at[sl, h], gt_scr.at[sl, h])
                for u, h in enumerate(hs)]
        ties = [_route_topk(*a, exact=False) for a in args]
        for a, tie in zip(args, ties):
            @pl.when(jnp.max(tie) > 0.0)
            def _():
                _route_topk(*a, exact=True)

        return carry

    lax.fori_loop(0, (RT_ROWS // RT_SLICE) * head_groups, heads, 0)

    for sl in range(RT_ROWS // RT_SLICE):
        for half in range(RT_SLICE // LANES):
            lanes = slice(half * LANES, (half + 1) * LANES)
            rows = slice(sl * RT_SLICE + half * LANES, sl * RT_SLICE + (half + 1) * LANES)
            et = jnp.concatenate([et_scr[sl, h, :, lanes] for h in range(PEER_HEADS)], axis=0)
            gt = jnp.concatenate([gt_scr[sl, h, :, lanes] for h in range(PEER_HEADS)], axis=0)
            e_ref[rows, :] = et.T.astype(I32)
            g_ref[rows, :] = gt.T


def _route(h2, wq, keys, v):
    k = PEER_TOPK
    steps = N_TOK // RT_ROWS
    tok = pl.BlockSpec((RT_ROWS, PEER_PICKS), lambda i: (i, 0))
    v_slab = pl.BlockSpec((PEER_EXPERTS // steps, D_MODEL), lambda i: (i, 0))
    return pl.pallas_call(
        _route_kernel,
        out_shape=[jax.ShapeDtypeStruct((N_TOK, PEER_PICKS), I32),
                   jax.ShapeDtypeStruct((N_TOK, PEER_PICKS), F32),
                   jax.ShapeDtypeStruct((PEER_EXPERTS, D_MODEL), BF16)],
        grid=(steps,),
        in_specs=[pl.BlockSpec((RT_ROWS, D_MODEL), lambda i: (i, 0)),
                  pl.BlockSpec((D_MODEL, PEER_HEADS * PEER_QDIM), lambda i: (0, 0),
                               pipeline_mode=pl.Buffered(1)),
                  pl.BlockSpec((PEER_HEADS * 2, PEER_NKEYS, PEER_QDIM // 2), lambda i: (0, 0, 0)),
                  v_slab],
        out_specs=[tok, tok, v_slab],
        scratch_shapes=[
            pltpu.VMEM((RT_ROWS, PEER_HEADS * PEER_QDIM), BF16),
            pltpu.VMEM((RT_HEADS, 2, PEER_NKEYS, RT_SLICE), F32),
            pltpu.VMEM((RT_HEADS, 2, k, RT_SLICE), F32),
            pltpu.VMEM((RT_HEADS, 2, k, RT_SLICE), F32),
            pltpu.VMEM((RT_HEADS, k, RT_SLICE), F32),
            pltpu.VMEM((RT_ROWS // RT_SLICE, PEER_HEADS, k, RT_SLICE), F32),
            pltpu.VMEM((RT_ROWS // RT_SLICE, PEER_HEADS, k, RT_SLICE), F32),
        ],
        compiler_params=_params(("parallel",), 52),
        name="peer_route",
    )(h2, wq, keys, v)


def _peer_up_kernel(h_ref, u_ref, e_ref, act_ref, *, te):
    j = pl.program_id(1)

    @pl.when(j == 0)
    def _():
        act_ref[...] = jnp.zeros_like(act_ref)

    _expert_up(j, h_ref, u_ref, e_ref, act_ref, te=te)


def _peer_up(h2, u, e, tb=2048, te=1024):
    return pl.pallas_call(
        functools.partial(_peer_up_kernel, te=te),
        out_shape=jax.ShapeDtypeStruct((N_TOK, PEER_PICKS), F32),
        grid=(N_TOK // tb, PEER_EXPERTS // te),
        in_specs=[pl.BlockSpec((tb, D_MODEL), lambda i, j: (i, 0)),
                  pl.BlockSpec((te, D_MODEL), lambda i, j: (j, 0)),
                  pl.BlockSpec((tb, PEER_PICKS), lambda i, j: (i, 0))],
        out_specs=pl.BlockSpec((tb, PEER_PICKS), lambda i, j: (i, 0)),
        compiler_params=_params(("parallel", "arbitrary"), 56),
        name="peer_up",
    )(h2, u, e)


def _peer_coef_kernel(e_ref, g_ref, act_ref, p_ref, row_scr, col_scr, w_scr, c_scr, *, tp):
    e = e_ref[...]
    row_scr[...] = (e >> PEER_KEY_BITS).astype(F32)
    col_scr[...] = (e & (PEER_NKEYS - 1)).astype(F32)
    act = act_ref[...]
    w_scr[...] = g_ref[...] * (0.5 * act * (1.0 + lax.erf(act * math.sqrt(0.5))))
    sub = 16
    iota = lax.broadcasted_iota(I32, (PEER_NKEYS // sub, sub, PEER_PICKS), 0) * sub \
        + lax.broadcasted_iota(I32, (PEER_NKEYS // sub, sub, PEER_PICKS), 1)
    iota = iota.astype(F32).astype(BF16)
    one = jnp.ones((), BF16)
    zero = jnp.zeros((), BF16)

    def bcast(ref, t):
        return jnp.broadcast_to(ref[pl.ds(t, 1), :], (sub, PEER_PICKS)).astype(BF16)[None]

    def products(t0):
        for t in range(sub):
            rows, cols, w = bcast(row_scr, t0 + t), bcast(col_scr, t0 + t), bcast(w_scr, t0 + t)
            left = jnp.where(iota == rows, one, zero).reshape(PEER_NKEYS, PEER_PICKS)
            right = jnp.where(iota == cols, w, zero).reshape(PEER_NKEYS, PEER_PICKS)
            c_scr[t0 + t] = lax.dot_general(left, right, (((1,), (1,)), ((), ())),
                                            preferred_element_type=F32)

    def regroup(t0):
        pairs = jnp.stack([pltpu.pack_elementwise([c_scr[t0 + t], c_scr[t0 + t + 1]], packed_dtype=BF16)
                           for t in range(0, sub, 2)])
        sw = jnp.swapaxes(pairs, 0, 1)
        for k1 in range(PEER_NKEYS):
            p_ref[pl.ds(t0, sub), k1 * PEER_NKEYS:(k1 + 1) * PEER_NKEYS] = pltpu.bitcast(sw[k1], p_ref.dtype)

    def block(blk, carry):
        base = pl.multiple_of(blk * COEF_UNROLL, COEF_UNROLL)
        for t0 in range(0, COEF_UNROLL, sub):
            products(base + t0)
            if t0:
                regroup(base + t0 - sub)
        regroup(base + COEF_UNROLL - sub)
        return carry

    lax.fori_loop(0, tp // COEF_UNROLL, block, 0)


COEF_UNROLL = 128


def _peer_coef(e, g, act, tp=128):
    tok = pl.BlockSpec((tp, PEER_PICKS), lambda i: (i, 0))
    return pl.pallas_call(
        functools.partial(_peer_coef_kernel, tp=tp),
        out_shape=jax.ShapeDtypeStruct((N_TOK, PEER_EXPERTS), BF16),
        grid=(N_TOK // tp,),
        in_specs=[tok, tok, tok],
        out_specs=pl.BlockSpec((tp, PEER_EXPERTS), lambda i: (i, 0)),
        scratch_shapes=[pltpu.VMEM((tp, PEER_PICKS), F32),
                        pltpu.VMEM((tp, PEER_PICKS), F32),
                        pltpu.VMEM((tp, PEER_PICKS), F32),
                        pltpu.VMEM((tp, PEER_NKEYS, PEER_NKEYS), F32)],
        compiler_params=_params(("parallel",), 32),
        name="peer_coef",
    )(e, g, act)


def _peer_down_kernel(p_ref, v_ref, x_ref, g_ref, o_ref):
    j = pl.program_id(1)

    @pl.when(j == 0)
    def _():
        o_ref[...] = jnp.zeros_like(o_ref)

    o_ref[...] += jnp.dot(p_ref[...], v_ref[...], preferred_element_type=F32)
    cols = pl.ds(pl.multiple_of(j * LANES, LANES), LANES)
    o_ref[:, cols] += x_ref[...]

    @pl.when(j == pl.num_programs(1) - 1)
    def _():
        o_ref[...] = _rms(o_ref[...], g_ref[...])


def _peer_down(coef, v, x1, g_final, tm=2048):
    tk = PEER_EXPERTS // (D_MODEL // LANES)
    return pl.pallas_call(
        _peer_down_kernel,
        out_shape=jax.ShapeDtypeStruct((N_TOK, D_MODEL), F32),
        grid=(N_TOK // tm, PEER_EXPERTS // tk),
        in_specs=[pl.BlockSpec((tm, tk), lambda i, j: (i, j)),
                  pl.BlockSpec((tk, D_MODEL), lambda i, j: (j, 0)),
                  pl.BlockSpec((tm, LANES), lambda i, j: (i, j)),
                  pl.BlockSpec((1, D_MODEL), lambda i, j: (0, 0))],
        out_specs=pl.BlockSpec((tm, D_MODEL), lambda i, j: (i, 0)),
        compiler_params=_params(("parallel", "arbitrary"), 60),
        name="peer_down",
    )(coef, v, x1, g_final)


def kernel(x, rel_bias, norm_mix_g, w_in, sink_a, w_oa, w_ob, w_out, norm_ffn_g,
           peer_wq, peer_keys, peer_u, peer_v, norm_final_g):
    assert x.shape == (BATCH, SEQ, D_MODEL) and w_in.shape[0] == 1
    x2 = x.reshape(N_TOK, D_MODEL)
    proj = _inproj(x2, norm_mix_g, w_in[0].astype(BF16))

    oa = _windowed_attention(proj, rel_bias, sink_a[0])
    ob = _dilated_attention(proj, rel_bias)
    x1, h2 = _outproj(oa, ob, proj, x2, w_oa[0].astype(BF16), w_ob[0].astype(BF16),
                      w_out[0].astype(BF16), norm_ffn_g)

    keys = peer_keys[0].reshape(PEER_HEADS * 2, PEER_NKEYS, PEER_QDIM // 2).astype(BF16)
    e, gate, v_bf16 = _route(h2, peer_wq[0].astype(BF16), keys, peer_v[0])
    act = _peer_up(h2, peer_u[0], e)
    coef = _peer_coef(e, gate, act)
    out = _peer_down(coef, v_bf16, x1, norm_final_g.reshape(1, D_MODEL))
    return out.reshape(BATCH, SEQ, D_MODEL)
```
